```python
import math
import jax, jax.numpy as jnp
from jax import lax
import numpy as np

D_MODEL = 1024
BATCH = 2
SEQ = 8192
DEPTH = 4
DEC_BATCH = 32
DEC_SEQ = 1
PAST_LEN = 8192
PAGE_SIZE = 128

HEAD_DIM = 64
A_HEADS = 8
A_KV = 2
IDX_HEADS = 4
IDX_DIM = 64
IDX_TOPK = 256
B_HEADS = 8
B_KV = 2
CMP_BLOCK = 32
SEL_BLOCK = 64
SEL_TOPN = 16
WINDOW = 512
N_BUCKETS = 32
MAX_DISTANCE = 1024
C_CHUNK = 128
C_GROUPS = 8
C_WIDTH = 2 * D_MODEL
D_FF = ((8 * D_MODEL // 3 + 127) // 128) * 128
Q_BLOCK = 128
EPS = 1e-6
N_ATT = (DEPTH + 1) // 2
N_CHK = DEPTH // 2
MIX_WIDTH = (A_HEADS + B_HEADS) * HEAD_DIM
PROJ_SPLITS = (A_HEADS * HEAD_DIM, 2 * A_KV * HEAD_DIM, IDX_HEADS * IDX_DIM, IDX_DIM, IDX_HEADS,
               B_HEADS * HEAD_DIM, 2 * B_KV * HEAD_DIM, 2 * B_KV * HEAD_DIM, 2 * B_KV * HEAD_DIM, 3 * B_HEADS)
PROJ_WIDTH = sum(PROJ_SPLITS)

kernel_name = 'hybrid_dsa_nsa_gmlp_decoder_step'


def rmsnorm(x, g):
    xf = x.astype(jnp.float32)
    y = xf * lax.rsqrt(jnp.mean(xf * xf, -1, keepdims=True) + EPS) * g.astype(jnp.float32)
    return y.astype(x.dtype)


def layernorm(x, g):
    xf = x.astype(jnp.float32)
    mu = jnp.mean(xf, -1, keepdims=True)
    var = jnp.mean(jnp.square(xf - mu), -1, keepdims=True)
    return ((xf - mu) * lax.rsqrt(var + EPS) * g.astype(jnp.float32)).astype(x.dtype)


def swiglu(h, w_in, w_out):
    a, b = jnp.split(h @ w_in, 2, axis=-1)
    return (jax.nn.silu(a) * b) @ w_out


def t5_bucket(dist):
    n = jnp.maximum(dist, 0)
    max_exact = N_BUCKETS // 2
    nf = jnp.maximum(n, 1).astype(jnp.float32)
    large = max_exact + (jnp.log(nf / max_exact) / math.log(MAX_DISTANCE / max_exact)
                         * (N_BUCKETS - max_exact)).astype(jnp.int32)
    large = jnp.minimum(large, N_BUCKETS - 1)
    return jnp.where(n < max_exact, n, large)


def rel_pos_bias(tab, dist):
    G = tab.shape[1]
    dist = jnp.broadcast_to(dist, dist.shape[:-2] + (G, dist.shape[-1]))
    g_idx = jnp.broadcast_to(jnp.arange(G)[:, None], dist.shape)
    b = jnp.swapaxes(tab, 0, 1)[g_idx, t5_bucket(dist)]
    return jnp.swapaxes(b, -1, -2)


def attend(q, k, v, key_spec, bias, mask):
    s = jnp.einsum('btgrd,' + key_spec + '->btgrk', q, k).astype(jnp.float32) * HEAD_DIM ** -0.5
    s = jnp.where(mask, s + bias.astype(jnp.float32), -1e30)
    p = jax.nn.softmax(s, axis=-1) * mask
    o = jnp.einsum('btgrk,' + key_spec + '->btgrd', p.astype(v.dtype), v)
    return o, p


def rows_by_pos(rows, pos):
    return jax.vmap(lambda r, p: r[p])(rows, pos)


def kv_by_pos_head(rows, pos):
    head = jnp.broadcast_to(jnp.arange(rows.shape[3])[:, None], pos.shape[-2:])
    return jax.vmap(lambda r, p: (r[p, 0, head], r[p, 1, head]))(rows, pos)


def split_kv(rows):
    return rows[..., 0, :, :], rows[..., 1, :, :]


def paged_lookup(page_table, pos, past_len):
    pc = jnp.minimum(pos, past_len - 1)
    phys = jax.vmap(lambda pt, p: pt[p // PAGE_SIZE])(page_table, pc)
    return phys, pc % PAGE_SIZE, pos < past_len


def gather_past(pool, a, page_table):
    g = pool[a, page_table]
    return g.reshape((page_table.shape[0], page_table.shape[1] * PAGE_SIZE) + pool.shape[3:])


def split_proj(h, w):
    B, T = h.shape[:2]
    parts = jnp.split(h @ w, np.cumsum(PROJ_SPLITS)[:-1].tolist(), axis=-1)
    qa = parts[0].reshape(B, T, A_KV, A_HEADS // A_KV, HEAD_DIM)
    kva = parts[1].reshape(B, T, 2, A_KV, HEAD_DIM)
    qi = parts[2].reshape(B, T, IDX_HEADS, IDX_DIM)
    ki = parts[3]
    wi = parts[4]
    qb = parts[5].reshape(B, T, B_KV, B_HEADS // B_KV, HEAD_DIM)
    kvc, kvs, kvw = [p.reshape(B, T, 2, B_KV, HEAD_DIM) for p in parts[6:9]]
    gates = parts[9].reshape(B, T, B_KV, B_HEADS // B_KV, 3)
    return qa, kva, qi, ki, wi, qb, kvc, kvs, kvw, gates


def compress(kv_rows, pe, w):
    B, L = kv_rows.shape[:2]
    nc = L // CMP_BLOCK
    blocks = kv_rows[:, :nc * CMP_BLOCK].reshape((B, nc, CMP_BLOCK) + kv_rows.shape[2:])
    blocks = blocks + jnp.swapaxes(pe, 0, 1)[None, None, :, :, None, :]
    out = jnp.einsum('bncihd,icde->bnihe', blocks, w)
    return out[:, :, 0], out[:, :, 1]


def dsa_attend(q, qi, wi, kidx, qpos, fetch, topk, tab):
    L = kidx.shape[1]
    sc = jnp.einsum('bthd,bsd->bths', qi, kidx).astype(jnp.float32)
    score = jnp.einsum('bths,bth->bts', jax.nn.relu(sc), wi.astype(jnp.float32))
    causal = jnp.arange(L)[None, :] <= qpos[:, None]
    score = jnp.where(causal, score, -jnp.inf)
    _, idx = lax.top_k(score, topk)
    valid = idx <= qpos[None, :, None]
    k, v = fetch(idx)
    bias = rel_pos_bias(tab, (qpos[None, :, None] - idx)[:, :, None, :])
    o, _ = attend(q, k, v, 'btkgd', bias, valid[:, :, None, None, :])
    return o


def nsa_attend(q, gates, qpos, kc, vc, sel_fetch, n_sel_blocks, kw, vw, kwpos, tab):
    B, T, G, R, D = q.shape
    nc = kc.shape[1]
    cend = jnp.arange(nc) * CMP_BLOCK + CMP_BLOCK - 1
    cdist = qpos[:, None] - cend[None, :]
    o_c, p_c = attend(q, kc, vc, 'bkgd', rel_pos_bias(tab, cdist[None, :, None, :]),
                      (cdist >= 0)[None, :, None, None, :])
    ratio = SEL_BLOCK // CMP_BLOCK
    imp = jnp.pad(p_c.sum(3), ((0, 0), (0, 0), (0, 0), (0, ratio * n_sel_blocks - nc)))
    imp = imp.reshape(B, T, G, n_sel_blocks, ratio).sum(-1)
    j = jnp.arange(n_sel_blocks)[None, :]
    cur = (qpos // SEL_BLOCK)[:, None]
    forced = ((j == 0) | (j == cur) | (j == cur - 1))[None, :, None, :]
    admissible = (j <= cur)[None, :, None, :]
    imp = jnp.where(admissible, jnp.where(forced, jnp.inf, imp), -jnp.inf)
    _, blk = lax.top_k(imp, min(SEL_TOPN, n_sel_blocks))
    pos = (blk[..., None] * SEL_BLOCK + jnp.arange(SEL_BLOCK)).reshape(B, T, G, -1)
    smask = pos <= qpos[None, :, None, None]
    pos = jnp.minimum(pos, qpos[-1])
    ks, vs = sel_fetch(pos)
    o_s, _ = attend(q, ks, vs, 'btgkd', rel_pos_bias(tab, qpos[None, :, None, None] - pos),
                    smask[:, :, :, None, :])
    wdist = qpos[:, None] - kwpos[None, :]
    wmask = (wdist >= 0) & (wdist <= WINDOW) & (kwpos >= 0)[None, :]
    o_w, _ = attend(q, kw, vw, 'bkgd', rel_pos_bias(tab, wdist[None, :, None, :]),
                    wmask[None, :, None, None, :])
    g = jax.nn.sigmoid(gates.astype(jnp.float32)).astype(q.dtype)
    return g[..., 0:1] * o_c + g[..., 1:2] * o_s + g[..., 2:3] * o_w


def chunk_mixer(h, w_in, ln_g, w_s, b_s, w_out):
    B, T = h.shape[:2]
    u, v = jnp.split(jax.nn.gelu(h @ w_in), 2, axis=-1)
    v = layernorm(v, ln_g)
    tp = -(-T // C_CHUNK) * C_CHUNK
    vb = jnp.pad(v, ((0, 0), (0, tp - T), (0, 0))).reshape(B, tp // C_CHUNK, C_CHUNK, C_GROUPS, C_WIDTH // C_GROUPS)
    ws = w_s * jnp.tril(jnp.ones((C_CHUNK, C_CHUNK), w_s.dtype))
    sv = jnp.einsum('gij,bnjgc->bnigc', ws, vb) + jnp.swapaxes(b_s, 0, 1)[None, None, :, :, None]
    sv = sv.reshape(B, tp, C_WIDTH)[:, :T]
    return (u * sv) @ w_out, v


def to_blocks(a):
    B, S = a.shape[:2]
    return jnp.moveaxis(a.reshape((B, S // Q_BLOCK, Q_BLOCK) + a.shape[2:]), 1, 0)


def from_blocks(a):
    nb, B, qb = a.shape[:3]
    return jnp.moveaxis(a, 0, 1).reshape((B, nb * qb) + a.shape[3:])


def setup_inputs(seed: int = 0) -> dict:
    key = jax.random.key(seed)
    ks = jax.random.split(key, 32)
    n_pages = PAST_LEN // PAGE_SIZE
    used = DEC_BATCH * n_pages
    n_phys = used + max(1, used // 4)
    win = min(WINDOW, PAST_LEN)

    def nrm(k, shape, s=1.0):
        return jax.random.normal(k, shape, jnp.float32) * s

    page_table = jax.random.permutation(ks[7], n_phys)[:used].reshape(DEC_BATCH, n_pages).astype(jnp.int32)
    return {
        'x_prompt': nrm(ks[0], (BATCH, SEQ, D_MODEL)),
        'x_sample': nrm(ks[1], (DEC_BATCH, DEC_SEQ, D_MODEL)),
        'cache_a_kv': nrm(ks[2], (N_ATT, n_phys, PAGE_SIZE, 2, A_KV, HEAD_DIM)),
        'cache_a_idx': nrm(ks[3], (N_ATT, n_phys, PAGE_SIZE, IDX_DIM)),
        'cache_b_cmp_kv': nrm(ks[4], (N_ATT, n_phys, PAGE_SIZE, 2, B_KV, HEAD_DIM)),
        'cache_b_sel_kv': nrm(ks[5], (N_ATT, n_phys, PAGE_SIZE, 2, B_KV, HEAD_DIM)),
        'state_b_win_kv': nrm(ks[6], (N_ATT, DEC_BATCH, win, 2, B_KV, HEAD_DIM)),
        'page_table': page_table,
        'c_prompt': nrm(ks[8], (BATCH, D_MODEL)),
        'c_sample': nrm(ks[9], (DEC_BATCH, D_MODEL)),
        'rel_bias': nrm(ks[10], (N_BUCKETS, A_HEADS + B_HEADS), 0.1),
        'w_ada': nrm(ks[11], (DEPTH, D_MODEL, 9 * D_MODEL), 0.1 * D_MODEL ** -0.5),
        'b_ada': nrm(ks[12], (DEPTH, 9 * D_MODEL), 0.01),
        'norm_g': 1.0 + nrm(ks[13], (DEPTH, 3, D_MODEL), 0.02),
        'w_ffn_in': nrm(ks[14], (DEPTH, 2, D_MODEL, 2 * D_FF), D_MODEL ** -0.5),
        'w_ffn_out': nrm(ks[15], (DEPTH, 2, D_FF, D_MODEL), D_FF ** -0.5),
        'w_in_att': nrm(ks[16], (N_ATT, D_MODEL, PROJ_WIDTH), D_MODEL ** -0.5),
        'w_out_att': nrm(ks[17], (N_ATT, MIX_WIDTH, D_MODEL), MIX_WIDTH ** -0.5),
        'cmp_pe': nrm(ks[18], (N_ATT, 2, CMP_BLOCK, HEAD_DIM), 0.1),
        'w_cmp': nrm(ks[19], (N_ATT, 2, CMP_BLOCK, HEAD_DIM, HEAD_DIM), (CMP_BLOCK * HEAD_DIM) ** -0.5),
        'w_in_c': nrm(ks[20], (N_CHK, D_MODEL, 2 * C_WIDTH), D_MODEL ** -0.5),
        'ln_c_g': 1.0 + nrm(ks[21], (N_CHK, C_WIDTH), 0.02),
        'w_sp': nrm(ks[22], (N_CHK, C_GROUPS, C_CHUNK, C_CHUNK), 0.5 * C_CHUNK ** -0.5),
        'b_sp': 1.0 + nrm(ks[23], (N_CHK, C_GROUPS, C_CHUNK), 0.02),
        'w_out_c': nrm(ks[24], (N_CHK, C_WIDTH, D_MODEL), C_WIDTH ** -0.5),
        'final_g': 1.0 + nrm(ks[25], (D_MODEL,), 0.02),
    }


def reference(x_prompt, x_sample, cache_a_kv, cache_a_idx, cache_b_cmp_kv, cache_b_sel_kv, state_b_win_kv,
              page_table, c_prompt, c_sample, rel_bias, w_ada, b_ada, norm_g, w_ffn_in, w_ffn_out,
              w_in_att, w_out_att, cmp_pe, w_cmp, w_in_c, ln_c_g, w_sp, b_sp, w_out_c, final_g):
    tab_a = rel_bias[:, :A_HEADS].reshape(N_BUCKETS, A_KV, A_HEADS // A_KV)
    tab_b = rel_bias[:, A_HEADS:].reshape(N_BUCKETS, B_KV, B_HEADS // B_KV)
    past_len = page_table.shape[1] * PAGE_SIZE

    def merge(oa, ob, a):
        B, T = oa.shape[:2]
        return jnp.concatenate([oa.reshape(B, T, -1), ob.reshape(B, T, -1)], -1) @ w_out_att[a]

    def even_prompt(h, a):
        qa, kva, qi, ki, wi, qb, kvc, kvs, kvw, gates = split_proj(h, w_in_att[a])
        S = h.shape[1]
        kc, vc = compress(kvc, cmp_pe[a], w_cmp[a])
        kvw_pad = jnp.pad(kvw, ((0, 0), (WINDOW, 0), (0, 0), (0, 0), (0, 0)))
        topk = min(IDX_TOPK, S // 4)
        n_sel = -(-S // SEL_BLOCK)

        def fetch_a(idx):
            return split_kv(rows_by_pos(kva, idx))

        def fetch_s(pos):
            return kv_by_pos_head(kvs, pos)

        def body(xs):
            i, qa_i, qi_i, wi_i, qb_i, g_i = xs
            qpos = i * Q_BLOCK + jnp.arange(Q_BLOCK)
            oa = dsa_attend(qa_i, qi_i, wi_i, ki, qpos, fetch_a, topk, tab_a)
            wkv = lax.dynamic_slice_in_dim(kvw_pad, i * Q_BLOCK, WINDOW + Q_BLOCK, axis=1)
            kwpos = i * Q_BLOCK - WINDOW + jnp.arange(WINDOW + Q_BLOCK)
            ob = nsa_attend(qb_i, g_i, qpos, kc, vc, fetch_s, n_sel, wkv[:, :, 0], wkv[:, :, 1], kwpos, tab_b)
            return oa, ob

        nb = S // Q_BLOCK
        oa, ob = lax.map(body, (jnp.arange(nb), to_blocks(qa), to_blocks(qi), to_blocks(wi),
                                to_blocks(qb), to_blocks(gates)))
        out = merge(from_blocks(oa), from_blocks(ob), a)
        return out, (kva, ki, kvc, kvs, kvw[:, S - min(WINDOW, S):])

    def even_sample(h, a):
        qa, kva, qi, ki, wi, qb, kvc, kvs, kvw, gates = split_proj(h, w_in_att[a])
        T = h.shape[1]
        L = past_len + T
        qpos = past_len + jnp.arange(T)
        ki_all = jnp.concatenate([gather_past(cache_a_idx, a, page_table), ki], 1)

        def fetch_a(idx):
            phys, off, is_past = paged_lookup(page_table, idx, past_len)
            old = cache_a_kv[a, phys, off]
            new = rows_by_pos(kva, jnp.clip(idx - past_len, 0, T - 1))
            return split_kv(jnp.where(is_past[..., None, None, None], old, new))

        oa = dsa_attend(qa, qi, wi, ki_all, qpos, fetch_a, min(IDX_TOPK, L // 4), tab_a)
        kvc_all = jnp.concatenate([gather_past(cache_b_cmp_kv, a, page_table), kvc], 1)
        kc, vc = compress(kvc_all, cmp_pe[a], w_cmp[a])

        def fetch_s(pos):
            phys, off, is_past = paged_lookup(page_table, pos, past_len)
            head = jnp.broadcast_to(jnp.arange(B_KV)[:, None], pos.shape)
            ko = cache_b_sel_kv[a, phys, off, 0, head]
            vo = cache_b_sel_kv[a, phys, off, 1, head]
            kn, vn = kv_by_pos_head(kvs, jnp.clip(pos - past_len, 0, T - 1))
            m = is_past[..., None]
            return jnp.where(m, ko, kn), jnp.where(m, vo, vn)

        win = state_b_win_kv[a]
        nw = win.shape[1]
        kw_all = jnp.concatenate([win, kvw], 1)
        kwpos = jnp.concatenate([past_len - nw + jnp.arange(nw), qpos])
        ob = nsa_attend(qb, gates, qpos, kc, vc, fetch_s, -(-L // SEL_BLOCK),
                        kw_all[:, :, 0], kw_all[:, :, 1], kwpos, tab_b)
        return merge(oa, ob, a), (kva, ki, kvc, kvs, kw_all[:, T:])

    def odd(h, ci):
        return chunk_mixer(h, w_in_c[ci], ln_c_g[ci], w_sp[ci], b_sp[ci], w_out_c[ci])

    def layer(x, c, l, mixer):
        ada = (jax.nn.silu(c) @ w_ada[l] + b_ada[l]).reshape(c.shape[0], 3, 3, D_MODEL)
        shift, scale, gate = ada[:, :, 0, None], ada[:, :, 1, None], ada[:, :, 2, None]

        def mod(y, j):
            return rmsnorm(y, norm_g[l, j]) * (1.0 + scale[:, j]) + shift[:, j]

        x = x + 0.5 * (1.0 + gate[:, 0]) * swiglu(mod(x, 0), w_ffn_in[l, 0], w_ffn_out[l, 0])
        m, st = mixer(mod(x, 1))
        x = x + (1.0 + gate[:, 1]) * m
        x = x + 0.5 * (1.0 + gate[:, 2]) * swiglu(mod(x, 2), w_ffn_in[l, 1], w_ffn_out[l, 1])
        return x, st

    xp, xs = x_prompt, x_sample
    st_p, st_s, st_c = [], [], []
    for l in range(DEPTH):
        if l % 2 == 0:
            a = l // 2
            xp, sp = layer(xp, c_prompt, l, lambda h: even_prompt(h, a))
            xs, ss = layer(xs, c_sample, l, lambda h: even_sample(h, a))
            st_p.append(sp)
            st_s.append(ss)
        else:
            ci = l // 2
            xp, _ = layer(xp, c_prompt, l, lambda h: odd(h, ci))
            xs, vs = layer(xs, c_sample, l, lambda h: odd(h, ci))
            st_c.append(vs)

    def stk(sts, i):
        return jnp.stack([s[i] for s in sts])

    y_prompt = rmsnorm(xp, final_g)
    y_sample = rmsnorm(xs, final_g)
    return (y_prompt, y_sample,
            stk(st_p, 0), stk(st_p, 1), stk(st_p, 2), stk(st_p, 3), stk(st_p, 4),
            stk(st_s, 0), stk(st_s, 1), stk(st_s, 2), stk(st_s, 3), stk(st_s, 4),
            jnp.stack(st_c))
```

```python
import functools
import math

import numpy as np
import jax
import jax.numpy as jnp
from jax import lax
from jax.experimental import pallas as pl
from jax.experimental.pallas import tpu as pltpu

F32 = jnp.float32
BF16 = jnp.bfloat16
I32 = jnp.int32

HEAD_DIM = 64
A_HEADS = 8
A_KV = 2
IDX_HEADS = 4
IDX_DIM = 64
IDX_TOPK = 256
B_HEADS = 8
B_KV = 2
CMP_BLOCK = 32
SEL_BLOCK = 64
SEL_TOPN = 16
WINDOW = 512
N_BUCKETS = 32
MAX_DISTANCE = 1024
C_CHUNK = 128
C_GROUPS = 8
EPS = 1e-6
PAGE = 128
GROUP = A_HEADS // A_KV
KVW = 2 * A_KV * HEAD_DIM

LANE = 128
VMEM_LIMIT = 56 * 1024 * 1024

NEG = -1e30
KMIN = -2 ** 31

P_QA, P_QB, P_KVA, P_QI, P_KVC, P_KVS, P_KVW, P_MISC = 0, 512, 1024, 1280, 1536, 1792, 2048, 2304
P_WIDTH = 2432
M_KI, M_WI, M_GATE = 0, IDX_DIM, IDX_DIM + IDX_HEADS


def _cparams(sem):
    return pltpu.CompilerParams(dimension_semantics=sem, vmem_limit_bytes=VMEM_LIMIT)


def _mm(a, b):
    return jnp.dot(a.astype(BF16), b.astype(BF16), preferred_element_type=F32)


def _mm_nt(a, b):
    return lax.dot_general(a.astype(BF16), b.astype(BF16), (((1,), (1,)), ((), ())),
                           preferred_element_type=F32)


def _modulated_norm(x, g, mod_ref):
    y = x * lax.rsqrt(jnp.mean(x * x, axis=-1, keepdims=True) + EPS) * g
    return y * (1.0 + mod_ref[1]) + mod_ref[0]


def _ada_kernel(c_ref, w_ref, b_ref, o_ref):
    c = c_ref[...]
    o_ref[...] = _mm(c * jax.nn.sigmoid(c), w_ref[...]) + b_ref[...]


def _ada_call(c_all, w_ada, b_ada):
    depth, d, n = w_ada.shape
    r = c_all.shape[0]
    tn = n // 8
    return pl.pallas_call(
        _ada_kernel,
        grid=(depth, n // tn),
        in_specs=[pl.BlockSpec((r, d), lambda l, j: (0, 0)),
                  pl.BlockSpec((None, d, tn), lambda l, j: (l, 0, j)),
                  pl.BlockSpec((None, 1, tn), lambda l, j: (l, 0, j))],
        out_specs=pl.BlockSpec((None, r, tn), lambda l, j: (l, 0, j)),
        out_shape=jax.ShapeDtypeStruct((depth, r, n), F32),
        compiler_params=_cparams(("arbitrary", "arbitrary")),
        name="ada",
    )(c_all, w_ada, b_ada.reshape(depth, 1, n))


def _ffn_kernel(x_ref, mod_ref, g_ref, wa_ref, wb_ref, wo_ref, fg_ref, o_ref, h_scr, acc_scr, *,
                n_f, final_norm):
    f = pl.program_id(2)

    @pl.when(f == 0)
    def _():
        h_scr[...] = _modulated_norm(x_ref[...], g_ref[...], mod_ref).astype(BF16)
        acc_scr[...] = jnp.zeros_like(acc_scr)

    h = h_scr[...]
    a = _mm(h, wa_ref[...])
    b = _mm(h, wb_ref[...])
    acc_scr[...] += _mm(a * jax.nn.sigmoid(a) * b, wo_ref[...])

    @pl.when(f == n_f - 1)
    def _():
        y = x_ref[...] + 0.5 * (1.0 + mod_ref[2]) * acc_scr[...]
        if final_norm:
            y = y * lax.rsqrt(jnp.mean(y * y, axis=-1, keepdims=True) + EPS) * fg_ref[...]
        o_ref[...] = y


def _ffn_call(x, mod, g, w_in, w_out, l, j, final_g, tm):
    nb, t, d = x.shape
    dff = w_out.shape[2]
    tf = 256
    n_f = dff // tf
    tmod = mod.shape[2]
    mod_spec = (pl.BlockSpec((None, 3, 1, d), lambda b, i, f: (b, 0, 0, 0)) if tmod == 1 else
                pl.BlockSpec((None, 3, tm, d), lambda b, i, f: (b, 0, i, 0)))
    final_norm = final_g is not None
    fg = (final_g if final_norm else g).reshape(1, d)
    return pl.pallas_call(
        functools.partial(_ffn_kernel, n_f=n_f, final_norm=final_norm),
        grid=(nb, t // tm, n_f),
        in_specs=[pl.BlockSpec((None, tm, d), lambda b, i, f: (b, i, 0)),
                  mod_spec,
                  pl.BlockSpec((1, d), lambda b, i, f: (0, 0)),
                  pl.BlockSpec((None, None, d, tf), lambda b, i, f: (l, j, 0, f)),
                  pl.BlockSpec((None, None, d, tf), lambda b, i, f: (l, j, 0, f + n_f)),
                  pl.BlockSpec((None, None, tf, d), lambda b, i, f: (l, j, f, 0)),
                  pl.BlockSpec((1, d), lambda b, i, f: (0, 0))],
        out_specs=pl.BlockSpec((None, tm, d), lambda b, i, f: (b, i, 0)),
        out_shape=jax.ShapeDtypeStruct(x.shape, F32),
        scratch_shapes=[pltpu.VMEM((tm, d), BF16), pltpu.VMEM((tm, d), F32)],
        compiler_params=_cparams(("arbitrary", "arbitrary", "arbitrary")),
        name="ffn",
    )(x, mod, g.reshape(1, d), w_in, w_in, w_out, fg)


_PROJ_OUT = (("qa", P_QA, 512, True), ("qb", P_QB, 512, True), ("qi", P_QI, 256, True),
             ("kva", P_KVA, 256, False), ("kvc", P_KVC, 256, False), ("kvs", P_KVS, 256, False),
             ("kvw", P_KVW, 256, False), ("misc", P_MISC, 128, False),
             ("kva_h", P_KVA, 256, True), ("kvs_h", P_KVS, 256, True), ("kvw_h", P_KVW, 256, True),
             ("misc_h", P_MISC, 128, True))


def _proj_kernel(x_ref, mod_ref, g_ref, w_ref, *o_refs):
    h = _modulated_norm(x_ref[...], g_ref[...], mod_ref).astype(BF16)
    done = {}
    for (name, off, width, _), o_ref in zip(_PROJ_OUT, o_refs):
        if off not in done:
            y = _mm(h, w_ref[:, off:off + width])
            if name in ("qa", "qb"):
                y = y * (HEAD_DIM ** -0.5)
            done[off] = y
        o_ref[...] = done[off].astype(o_ref.dtype)


def _proj_call(x, mod, g, wp, tm):
    nb, t, d = x.shape
    tmod = mod.shape[2]
    mod_spec = (pl.BlockSpec((None, 3, 1, d), lambda b, i: (b, 0, 0, 0)) if tmod == 1 else
                pl.BlockSpec((None, 3, tm, d), lambda b, i: (b, 0, i, 0)))
    outs = pl.pallas_call(
        _proj_kernel,
        grid=(nb, t // tm),
        in_specs=[pl.BlockSpec((None, tm, d), lambda b, i: (b, i, 0)),
                  mod_spec,
                  pl.BlockSpec((1, d), lambda b, i: (0, 0)),
                  pl.BlockSpec((d, P_WIDTH), lambda b, i: (0, 0))],
        out_specs=[pl.BlockSpec((None, tm, w), lambda b, i: (b, i, 0)) for _, _, w, _ in _PROJ_OUT],
        out_shape=[jax.ShapeDtypeStruct((nb, t, w), BF16 if half else F32) for _, _, w, half in _PROJ_OUT],
        compiler_params=_cparams(("arbitrary", "arbitrary")),
        name="proj",
    )(x, mod, g.reshape(1, d), wp)
    return {name: o for (name, _, _, _), o in zip(_PROJ_OUT, outs)}


def _rearranged_proj_weight(w):
    d = w.shape[0]
    o = np.cumsum((0, 512, 256, 256, 64, 4, 512, 256, 256, 256, 24))
    seg = lambda i: w[:, o[i]:o[i + 1]]
    qa, kva, qi, ki, wi, qb, kvc, kvs, kvw, gates = [seg(i) for i in range(10)]
    pad = jnp.zeros((d, LANE - IDX_DIM - IDX_HEADS - 3 * B_HEADS), w.dtype)
    return jnp.concatenate([qa, qb, kva, qi, kvc, kvs, kvw, ki, wi, gates, pad], axis=1).astype(BF16)


def _t5_bucket(dist):
    n = jnp.maximum(dist, 0)
    max_exact = N_BUCKETS // 2
    nf = jnp.maximum(n, 1).astype(F32)
    large = max_exact + (jnp.log(nf / max_exact) / math.log(MAX_DISTANCE / max_exact)
                         * (N_BUCKETS - max_exact)).astype(I32)
    large = jnp.minimum(large, N_BUCKETS - 1)
    return jnp.where(n < max_exact, n, large)


def _near_count(tk):
    return -(-(MAX_DISTANCE + tk - 1) // tk)


def _bias_strips(tab, tq, tk):
    near = _near_count(tk)
    r = np.arange(tq)[None, :, None]
    c = np.arange(tk)[None, None, :]
    bd = np.arange(near + 1)[:, None, None]
    dist = r - c + tk * bd
    dist = np.where(bd == near, MAX_DISTANCE, dist)
    return jnp.transpose(tab[_t5_bucket(jnp.asarray(dist, I32))], (3, 0, 1, 2))


def _cmp_near_count(delta):
    return -(-(-(-(MAX_DISTANCE - delta) // CMP_BLOCK)) // 8) * 8


def _cmp_delta(tq):
    return -(CMP_BLOCK - 1) - CMP_BLOCK * ((tq - CMP_BLOCK) // CMP_BLOCK)


def _cmp_strips(tab, tq):
    delta = _cmp_delta(tq)
    ncn = _cmp_near_count(delta)
    dist = np.arange(tq)[:, None] + delta + CMP_BLOCK * np.arange(ncn)[None, :]
    v = jnp.transpose(tab[_t5_bucket(jnp.asarray(dist, I32))], (2, 0, 1))
    hi = v.astype(BF16)
    r1 = v - hi.astype(F32)
    mid = r1.astype(BF16)
    lo = (r1 - mid.astype(F32)).astype(BF16)
    return jnp.stack([hi, mid, lo], axis=1)


def _stack_heads(x, g):
    return jnp.concatenate(
        [x[:, (g * GROUP + r) * HEAD_DIM:(g * GROUP + r + 1) * HEAD_DIM] for r in range(GROUP)], axis=0)


def _flash_reset(m_scr, l_scr, acc_scr):
    m_scr[...] = jnp.full_like(m_scr, NEG)
    l_scr[...] = jnp.zeros_like(l_scr)
    acc_scr[...] = jnp.zeros_like(acc_scr)


def _flash_update(m_scr, l_scr, acc_scr, g, q_g, k_t, v_t, bias3, mask, tq):
    tk = k_t.shape[0]
    s = _mm_nt(q_g, k_t).reshape(GROUP, tq, tk) + bias3
    s = jnp.where(mask[None], s, NEG)
    m_old = m_scr[g]
    m_new = jnp.maximum(m_old, jnp.max(s, axis=-1, keepdims=True))
    alpha = jnp.exp(m_old - m_new)
    p = jnp.where(mask[None], jnp.exp(s - m_new), 0.0)
    l_scr[g] = alpha * l_scr[g] + jnp.sum(p, axis=-1, keepdims=True)
    acc_scr[g] = (alpha.reshape(GROUP * tq, 1) * acc_scr[g]
                  + _mm(p.reshape(GROUP * tq, tk), v_t))
    m_scr[g] = m_new


def _flash_result(l_scr, acc_scr, g, tq):
    l = l_scr[g].reshape(GROUP * tq, 1)
    return jnp.where(l > 0.0, acc_scr[g] / jnp.where(l > 0.0, l, 1.0), 0.0)


def _ordered_key(x):
    b = lax.bitcast_convert_type(x, I32)
    return jnp.where(x == 0.0, 0, b ^ ((b >> 31) & 0x7FFFFFFF))


def _dsa_core(qi, wi, qa, get_kidx, get_kva, strips_ref, kd, t0, tq, tk, topk,
              keys_scr, m_scr, l_scr, acc_scr, tri_scr):
    n_tiles = kd + 1
    near = _near_count(tk)
    row_pos = t0 + lax.broadcasted_iota(I32, (tq, tk), 0)
    col = lax.broadcasted_iota(I32, (tq, tk), 1)

    qi_st = jnp.concatenate([qi[:, h * IDX_DIM:(h + 1) * IDX_DIM] for h in range(IDX_HEADS)], axis=0)
    wi_st = jnp.concatenate([wi[:, h:h + 1] for h in range(IDX_HEADS)], axis=0)

    def score_tile(k, c):
        sc = _mm_nt(qi_st, get_kidx(k)[:, :IDX_DIM])
        sc = (jnp.maximum(sc, 0.0) * wi_st).reshape(IDX_HEADS, tq, tk)
        score = sc[0]
        for h in range(1, IDX_HEADS):
            score = score + sc[h]
        keys_scr[k] = jnp.where(k * tk + col <= row_pos, _ordered_key(score), KMIN)
        return c

    lax.fori_loop(0, n_tiles, score_tile, 0)

    tu = jnp.zeros((tq, 1), I32)
    for bit in range(31, -1, -1):
        cand_u = tu | np.int32(-2 ** 31 if bit == 31 else 2 ** bit)
        cand_s = cand_u ^ np.int32(KMIN)

        def count_tile(k, cnt, cand_s=cand_s):
            return cnt + jnp.where(keys_scr[k] >= cand_s, 1.0, 0.0)

        cnt = lax.fori_loop(0, n_tiles, count_tile, jnp.zeros((tq, tk), F32))
        tu = jnp.where(jnp.sum(cnt, axis=-1, keepdims=True) >= topk, cand_u, tu)
    thr = tu ^ np.int32(KMIN)

    def count_gt(k, cnt):
        return cnt + jnp.where(keys_scr[k] > thr, 1.0, 0.0)

    n_gt = jnp.sum(lax.fori_loop(0, n_tiles, count_gt, jnp.zeros((tq, tk), F32)), axis=-1, keepdims=True)
    need = topk - n_gt
    ties_ok = thr > np.int32(KMIN)

    tri_scr[...] = jnp.where(lax.broadcasted_iota(I32, (tk, tk), 0) <= lax.broadcasted_iota(I32, (tk, tk), 1),
                             1.0, 0.0).astype(BF16)
    q_g = [_stack_heads(qa, g) for g in range(A_KV)]
    _flash_reset(m_scr, l_scr, acc_scr)

    def attend_tile(k, eq_seen):
        key = keys_scr[k]
        eq = key == thr
        incl = _mm(jnp.where(eq, 1.0, 0.0), tri_scr[...])
        mask = (key > thr) | (eq & ties_ok & (eq_seen + incl <= need))
        kv = get_kva(k)
        bd = jnp.minimum(kd - k, near)
        for g in range(A_KV):
            _flash_update(m_scr, l_scr, acc_scr, g, q_g[g],
                          kv[:, g * HEAD_DIM:(g + 1) * HEAD_DIM],
                          kv[:, (A_KV + g) * HEAD_DIM:(A_KV + g + 1) * HEAD_DIM],
                          strips_ref[pl.ds(g * GROUP, GROUP), bd], mask, tq)
        return eq_seen + incl[:, tk - 1:tk]

    lax.fori_loop(0, n_tiles, attend_tile, jnp.zeros((tq, 1), F32))
    outs = []
    for g in range(A_KV):
        o = _flash_result(l_scr, acc_scr, g, tq)
        outs += [o[r * tq:(r + 1) * tq] for r in range(GROUP)]
    return jnp.concatenate(outs, axis=1)


def _pick_blocks(v, n_pick):
    lane = lax.broadcasted_iota(I32, v.shape, 1)
    sel = jnp.zeros(v.shape, jnp.bool_)
    for _ in range(n_pick):
        mx = jnp.max(v, axis=-1, keepdims=True)
        cand = (v == mx) & jnp.logical_not(sel)
        first = jnp.min(jnp.where(cand, lane, v.shape[1]), axis=-1, keepdims=True)
        pick = lane == first
        sel = sel | pick
        v = jnp.where(pick, -jnp.inf, v)
    return sel


def _nsa_core(qb, gates, kcp, get_kvs, get_kvw, strips_ref, cstrips_ref, far_ref, kd, t0, tq, tk, nj,
              n_pick, extra_block, m_scr, l_scr, acc_scr):
    near = _near_count(tk)
    n_tiles = kd + 1
    delta = _cmp_delta(tq)
    ncn = cstrips_ref.shape[-1]
    q_g = [_stack_heads(qb, g) for g in range(B_KV)]
    row_pos1 = t0 + lax.broadcasted_iota(I32, (tq, 1), 0)

    lam = lax.broadcasted_iota(I32, (tq, 2 * nj), 1)
    m_of = 2 * (lam % nj) + lam // nj
    cdist = row_pos1 - (CMP_BLOCK * m_of + CMP_BLOCK - 1)
    cmask = cdist >= 0
    m_hi = (t0 - (CMP_BLOCK - 1) - delta) // CMP_BLOCK
    lam_s = lax.broadcasted_iota(I32, (ncn, 2 * nj), 1)
    shift = jnp.where(2 * (lam_s % nj) + lam_s // nj == m_hi - lax.broadcasted_iota(I32, (ncn, 2 * nj), 0),
                      1.0, 0.0).astype(BF16)
    is_far = m_of <= m_hi - ncn
    o_c, imp = [], []
    for g in range(B_KV):
        bias = []
        for r in range(GROUP):
            h = g * GROUP + r
            b = (_mm(cstrips_ref[h, 0], shift) + _mm(cstrips_ref[h, 1], shift)) + _mm(cstrips_ref[h, 2], shift)
            bias.append(jnp.where(is_far, far_ref[h], b))
        s = _mm_nt(q_g[g], kcp[:, g * HEAD_DIM:(g + 1) * HEAD_DIM]).reshape(GROUP, tq, 2 * nj) + jnp.stack(bias)
        s = jnp.where(cmask[None], s, NEG)
        e = jnp.where(cmask[None], jnp.exp(s - jnp.max(s, axis=-1, keepdims=True)), 0.0)
        l = jnp.sum(e, axis=-1, keepdims=True)
        p = jnp.where(l > 0.0, e / jnp.where(l > 0.0, l, 1.0), 0.0)
        o_c.append(_mm(p.reshape(GROUP * tq, 2 * nj), kcp[:, (B_KV + g) * HEAD_DIM:(B_KV + g + 1) * HEAD_DIM]))
        ps = p[0]
        for r in range(1, GROUP):
            ps = ps + p[r]
        imp.append(ps[:, :nj] + ps[:, nj:])

    jl = lax.broadcasted_iota(I32, (tq, nj), 1)
    cur = row_pos1 // SEL_BLOCK
    forced = (jl == 0) | (jl == cur) | (jl == cur - 1)
    admissible = jl <= cur
    bmask = []
    for g in range(B_KV):
        v = jnp.where(admissible, jnp.where(forced, jnp.inf, imp[g]), -jnp.inf)
        bmask.append(jnp.where(_pick_blocks(v, n_pick) & admissible, 1.0, 0.0).astype(BF16))
    row_pos = t0 + lax.broadcasted_iota(I32, (tq, tk), 0)
    col = lax.broadcasted_iota(I32, (tq, tk), 1)
    ej = lax.broadcasted_iota(I32, (nj, tk), 0)
    ec = lax.broadcasted_iota(I32, (nj, tk), 1)
    _flash_reset(m_scr, l_scr, acc_scr)

    def sel_tile(k, c):
        kv = get_kvs(k)
        bd = jnp.minimum(kd - k, near)
        key_pos = k * tk + col
        causal = key_pos <= row_pos
        expand = jnp.where(ej == (k * tk + ec) // SEL_BLOCK, 1.0, 0.0).astype(BF16)
        for g in range(B_KV):
            mask = _mm(bmask[g], expand) > 0.5
            if extra_block:
                mask = mask | (key_pos >= nj * SEL_BLOCK)
            _flash_update(m_scr, l_scr, acc_scr, g, q_g[g],
                          kv[:, g * HEAD_DIM:(g + 1) * HEAD_DIM],
                          kv[:, (B_KV + g) * HEAD_DIM:(B_KV + g + 1) * HEAD_DIM],
                          strips_ref[pl.ds(g * GROUP, GROUP), bd], mask & causal, tq)
        return c

    lax.fori_loop(0, n_tiles, sel_tile, 0)
    o_s = [_flash_result(l_scr, acc_scr, g, tq) for g in range(B_KV)]

    _flash_reset(m_scr, l_scr, acc_scr)
    rr = lax.broadcasted_iota(I32, (tq, tk), 0)
    for bd in range(WINDOW // tk, -1, -1):
        dist = rr - col + tk * bd
        wmask = (dist >= 0) & (dist <= WINDOW)

        def win_tile(bd=bd, wmask=wmask):
            kv = get_kvw(bd)
            for g in range(B_KV):
                _flash_update(m_scr, l_scr, acc_scr, g, q_g[g],
                              kv[:, g * HEAD_DIM:(g + 1) * HEAD_DIM],
                              kv[:, (B_KV + g) * HEAD_DIM:(B_KV + g + 1) * HEAD_DIM],
                              strips_ref[pl.ds(g * GROUP, GROUP), bd], wmask, tq)

        if isinstance(kd, int):
            if kd - bd >= 0:
                win_tile()
        else:
            pl.when(kd - bd >= 0)(win_tile)
    o_w = [_flash_result(l_scr, acc_scr, g, tq) for g in range(B_KV)]

    gs = jax.nn.sigmoid(gates)
    outs = []
    for g in range(B_KV):
        gcol = lambda j: jnp.concatenate(
            [gs[:, (g * GROUP + r) * 3 + j:(g * GROUP + r) * 3 + j + 1] for r in range(GROUP)], axis=0)
        o = gcol(0) * o_c[g] + gcol(1) * o_s[g] + gcol(2) * o_w[g]
        outs += [o[r * tq:(r + 1) * tq] for r in range(GROUP)]
    return jnp.concatenate(outs, axis=1)


def _compress(load_rows, n_pairs, pe_ref, wbd_ref):
    acc = [jnp.zeros((n_pairs, KVW), F32), jnp.zeros((n_pairs, KVW), F32)]
    for c in range(2 * CMP_BLOCK):
        ci = c % CMP_BLOCK
        acc[c // CMP_BLOCK] = acc[c // CMP_BLOCK] + _mm(load_rows(c) + pe_ref[ci], wbd_ref[ci])
    return jnp.concatenate(acc, axis=0)


def _compress_operands(pe, w):
    pe_rows = jnp.transpose(jnp.broadcast_to(pe[:, None], (2, B_KV, CMP_BLOCK, HEAD_DIM)), (2, 0, 1, 3))
    pe_rows = pe_rows.reshape(CMP_BLOCK, 1, KVW)
    eye = jnp.eye(2 * B_KV, dtype=w.dtype).reshape(2, B_KV, 2, B_KV)
    wbd = jnp.einsum("icde,igjh->cigdjhe", w, eye).reshape(CMP_BLOCK, KVW, KVW)
    return pe_rows, wbd.astype(BF16)


TQ_P = 128


def _pdsa_kernel(qi_ref, misc_ref, qa_ref, kidx_ref, kva_ref, strips_ref, o_ref,
                 keys_scr, m_scr, l_scr, acc_scr, tri_scr, *, topk):
    i = pl.program_id(1)
    tq = tk = TQ_P
    o = _dsa_core(qi_ref[...], misc_ref[:, M_WI:M_WI + IDX_HEADS], qa_ref[...],
                  lambda k: kidx_ref[pl.ds(pl.multiple_of(k * tk, tk), tk), :],
                  lambda k: kva_ref[pl.ds(pl.multiple_of(k * tk, tk), tk), :],
                  strips_ref, i, i * tq, tq, tk, topk, keys_scr, m_scr, l_scr, acc_scr, tri_scr)
    o_ref[...] = o.astype(o_ref.dtype)


def _flash_scratch(tq):
    return [pltpu.VMEM((A_KV, GROUP, tq, 1), F32), pltpu.VMEM((A_KV, GROUP, tq, 1), F32),
            pltpu.VMEM((A_KV, GROUP * tq, HEAD_DIM), F32)]


def _pdsa_call(pr, strips):
    nb, s, _ = pr["qa"].shape
    tq = TQ_P
    near1 = strips.shape[1]
    row = lambda w: pl.BlockSpec((None, tq, w), lambda b, i: (b, i, 0))
    full = lambda w: pl.BlockSpec((None, s, w), lambda b, i: (b, 0, 0))
    return pl.pallas_call(
        functools.partial(_pdsa_kernel, topk=min(IDX_TOPK, s // 4)),
        grid=(nb, s // tq),
        in_specs=[row(256), row(128), row(512), full(128), full(256),
                  pl.BlockSpec((A_HEADS, near1, tq, tq), lambda b, i: (0, 0, 0, 0))],
        out_specs=row(512),
        out_shape=jax.ShapeDtypeStruct((nb, s, A_HEADS * HEAD_DIM), BF16),
        scratch_shapes=[pltpu.VMEM((s // tq, tq, tq), I32)] + _flash_scratch(tq)
        + [pltpu.VMEM((tq, tq), BF16)],
        compiler_params=_cparams(("arbitrary", "arbitrary")),
        name="dsa_prompt",
    )(pr["qi"], pr["misc"], pr["qa"], pr["misc_h"], pr["kva_h"], strips)


def _strided_rows(lo_ref, hi_ref, c, n):
    rows = pl.ds(c, n, stride=2 * CMP_BLOCK)
    return jnp.concatenate([lo_ref[rows, :], hi_ref[rows, :]], axis=1)


def _pcmp_kernel(lo_ref, hi_ref, pe_ref, wbd_ref, o_ref, *, n_pairs):
    o_ref[...] = _compress(lambda c: _strided_rows(lo_ref, hi_ref, c, n_pairs), n_pairs, pe_ref, wbd_ref)


def _pcmp_call(kvc, pe_rows, wbd):
    nb, s, _ = kvc.shape
    n_pairs = s // (2 * CMP_BLOCK)
    return pl.pallas_call(
        functools.partial(_pcmp_kernel, n_pairs=n_pairs),
        grid=(nb,),
        in_specs=[pl.BlockSpec((None, s, LANE), lambda b: (b, 0, 0)),
                  pl.BlockSpec((None, s, LANE), lambda b: (b, 0, 1)),
                  pl.BlockSpec((CMP_BLOCK, 1, KVW), lambda b: (0, 0, 0)),
                  pl.BlockSpec((CMP_BLOCK, KVW, KVW), lambda b: (0, 0, 0))],
        out_specs=pl.BlockSpec((None, 2 * n_pairs, KVW), lambda b: (b, 0, 0)),
        out_shape=jax.ShapeDtypeStruct((nb, 2 * n_pairs, KVW), F32),
        compiler_params=_cparams(("arbitrary",)),
        name="compress_prompt",
    )(kvc, kvc, pe_rows, wbd)


def _pnsa_kernel(qb_ref, misc_ref, kcp_ref, kvs_ref, kvw_ref, strips_ref, cstrips_ref, far_ref, o_ref,
                 m_scr, l_scr, acc_scr, *, nj, n_pick):
    i = pl.program_id(1)
    tq = tk = TQ_P
    o = _nsa_core(qb_ref[...], misc_ref[:, M_GATE:M_GATE + 3 * B_HEADS], kcp_ref[...],
                  lambda k: kvs_ref[pl.ds(pl.multiple_of(k * tk, tk), tk), :],
                  lambda bd: kvw_ref[pl.ds(pl.multiple_of((i - bd) * tk, tk), tk), :],
                  strips_ref, cstrips_ref, far_ref, i, i * tq, tq, tk, nj, n_pick, False,
                  m_scr, l_scr, acc_scr)
    o_ref[...] = o.astype(o_ref.dtype)


def _pnsa_call(pr, kcp, strips, cstrips, far):
    nb, s, _ = pr["qb"].shape
    tq = TQ_P
    nj = s // SEL_BLOCK
    row = lambda w: pl.BlockSpec((None, tq, w), lambda b, i: (b, i, 0))
    full = lambda n, w: pl.BlockSpec((None, n, w), lambda b, i: (b, 0, 0))
    const = lambda a: pl.BlockSpec(a.shape, lambda b, i: (0,) * a.ndim)
    return pl.pallas_call(
        functools.partial(_pnsa_kernel, nj=nj, n_pick=min(SEL_TOPN, nj)),
        grid=(nb, s // tq),
        in_specs=[row(512), row(128), full(2 * nj, KVW), full(s, KVW), full(s, KVW),
                  const(strips), const(cstrips),
                  pl.BlockSpec(memory_space=pltpu.SMEM)],
        out_specs=row(512),
        out_shape=jax.ShapeDtypeStruct((nb, s, B_HEADS * HEAD_DIM), BF16),
        scratch_shapes=_flash_scratch(tq),
        compiler_params=_cparams(("arbitrary", "arbitrary")),
        name="nsa_prompt",
    )(pr["qb"], pr["misc"], kcp, pr["kvs_h"], pr["kvw_h"], strips, cstrips, far)


TQ_S = 8
TK_S = 512


def _fill_tail(buf, past, new_row):
    first = lax.broadcasted_iota(I32, (TK_S, buf.shape[1]), 0) == 0
    buf[pl.ds(past, TK_S), :] = jnp.where(first, new_row, 0.0).astype(buf.dtype)


def _sdsa_kernel(pt_ref, idxp_ref, akvp_ref, qi_ref, wi_ref, qa_ref, new_idx_ref, new_kv_ref, strips_ref,
                 o_ref, idx_buf, akv_buf, keys_scr, m_scr, l_scr, acc_scr, tri_scr, *, n_pages, topk):
    p = pl.program_id(1)
    past = n_pages * PAGE
    idx_buf[pl.ds(pl.multiple_of(p * PAGE, PAGE), PAGE), :] = idxp_ref[...].astype(BF16)
    akv_buf[pl.ds(pl.multiple_of(p * PAGE, PAGE), PAGE), :] = akvp_ref[...].astype(BF16)

    @pl.when(p == n_pages - 1)
    def _():
        _fill_tail(idx_buf, past, new_idx_ref[...])
        _fill_tail(akv_buf, past, new_kv_ref[...])
        tk = TK_S
        o = _dsa_core(qi_ref[...], wi_ref[...], qa_ref[...],
                      lambda k: idx_buf[pl.ds(pl.multiple_of(k * tk, tk), tk), :],
                      lambda k: akv_buf[pl.ds(pl.multiple_of(k * tk, tk), tk), :],
                      strips_ref, past // tk, past, TQ_S, tk, topk,
                      keys_scr, m_scr, l_scr, acc_scr, tri_scr)
        o_ref[...] = o.astype(o_ref.dtype)


def _sample_rows(x):
    return jnp.broadcast_to(x[:, None, :], (x.shape[0], TQ_S, x.shape[1]))


def _sdsa_call(a, page_table, cache_idx, cache_kv, pr, strips):
    bd, n_pages = page_table.shape
    past = n_pages * PAGE
    lk = past + TK_S
    n_phys = cache_idx.shape[1]
    cache_kv = cache_kv.reshape(cache_kv.shape[0], n_phys, PAGE, KVW)
    misc = pr["misc"][0]
    rows = lambda w: pl.BlockSpec((None, TQ_S, w), lambda b, p, pt: (b, 0, 0))
    one = lambda w: pl.BlockSpec((None, 1, w), lambda b, p, pt: (b, 0, 0))
    page = lambda w: pl.BlockSpec((None, None, PAGE, w), lambda b, p, pt: (a, pt[b * n_pages + p], 0, 0))
    grid_spec = pltpu.PrefetchScalarGridSpec(
        num_scalar_prefetch=1,
        grid=(bd, n_pages),
        in_specs=[page(IDX_DIM), page(KVW), rows(256), rows(IDX_HEADS), rows(512), one(IDX_DIM), one(KVW),
                  pl.BlockSpec(strips.shape, lambda b, p, pt: (0, 0, 0, 0))],
        out_specs=rows(512),
        scratch_shapes=[pltpu.VMEM((lk, IDX_DIM), BF16), pltpu.VMEM((lk, KVW), BF16),
                        pltpu.VMEM((lk // TK_S, TQ_S, TK_S), I32)] + _flash_scratch(TQ_S)
        + [pltpu.VMEM((TK_S, TK_S), BF16)])
    out = pl.pallas_call(
        functools.partial(_sdsa_kernel, n_pages=n_pages, topk=min(IDX_TOPK, (past + 1) // 4)),
        grid_spec=grid_spec,
        out_shape=jax.ShapeDtypeStruct((bd, TQ_S, A_HEADS * HEAD_DIM), BF16),
        compiler_params=_cparams(("arbitrary", "arbitrary")),
        name="dsa_sample",
    )(page_table.reshape(-1), cache_idx, cache_kv,
      _sample_rows(pr["qi"][0]), _sample_rows(misc[:, M_WI:M_WI + IDX_HEADS]), _sample_rows(pr["qa"][0]),
      misc[:, None, M_KI:M_KI + IDX_DIM], pr["kva"][0][:, None, :], strips)
    return out[:, 0]


def _snsa_kernel(pt_ref, cmpp_ref, selp_ref, qb_ref, gates_ref, new_sel_ref, win_ref, new_win_ref,
                 pe_ref, wbd_ref, strips_ref, cstrips_ref, far_ref, o_ref,
                 cmp_lo, cmp_hi, sel_buf, win_buf, m_scr, l_scr, acc_scr, *, n_pages):
    p = pl.program_id(1)
    past = n_pages * PAGE
    cmp_lo[pl.ds(pl.multiple_of(p * PAGE, PAGE), PAGE), :] = cmpp_ref[:, :LANE]
    cmp_hi[pl.ds(pl.multiple_of(p * PAGE, PAGE), PAGE), :] = cmpp_ref[:, LANE:]
    sel_buf[pl.ds(pl.multiple_of(p * PAGE, PAGE), PAGE), :] = selp_ref[...].astype(BF16)

    @pl.when(p == n_pages - 1)
    def _():
        tk = TK_S
        _fill_tail(sel_buf, past, new_sel_ref[...])
        win_buf[pl.ds(0, WINDOW), :] = win_ref[...].astype(BF16)
        _fill_tail(win_buf, WINDOW, new_win_ref[...])
        nj = past // SEL_BLOCK
        kcp = _compress(lambda c: _strided_rows(cmp_lo, cmp_hi, c, nj), nj, pe_ref, wbd_ref)
        kd = past // tk
        o = _nsa_core(qb_ref[...], gates_ref[...], kcp,
                      lambda k: sel_buf[pl.ds(pl.multiple_of(k * tk, tk), tk), :],
                      lambda bd: win_buf[pl.ds((WINDOW // tk - bd) * tk, tk), :],
                      strips_ref, cstrips_ref, far_ref, kd, past, TQ_S, tk, nj,
                      min(SEL_TOPN, nj + 1) - 1, True, m_scr, l_scr, acc_scr)
        o_ref[...] = o.astype(o_ref.dtype)


def _snsa_call(a, page_table, cache_cmp, cache_sel, win, pr, pe_rows, wbd, strips, cstrips, far):
    bd, n_pages = page_table.shape
    past = n_pages * PAGE
    assert past % TK_S == 0 and win.shape[2] == WINDOW and WINDOW % TK_S == 0
    n_phys = cache_cmp.shape[1]
    cache_cmp = cache_cmp.reshape(cache_cmp.shape[0], n_phys, PAGE, KVW)
    cache_sel = cache_sel.reshape(cache_sel.shape[0], n_phys, PAGE, KVW)
    win = win.reshape(win.shape[0], bd, WINDOW, KVW)
    misc = pr["misc"][0]
    rows = lambda w: pl.BlockSpec((None, TQ_S, w), lambda b, p, pt: (b, 0, 0))
    one = lambda w: pl.BlockSpec((None, 1, w), lambda b, p, pt: (b, 0, 0))
    page = lambda w: pl.BlockSpec((None, None, PAGE, w), lambda b, p, pt: (a, pt[b * n_pages + p], 0, 0))
    const = lambda x: pl.BlockSpec(x.shape, lambda b, p, pt: (0,) * x.ndim)
    grid_spec = pltpu.PrefetchScalarGridSpec(
        num_scalar_prefetch=1,
        grid=(bd, n_pages),
        in_specs=[page(KVW), page(KVW), rows(512), rows(3 * B_HEADS), one(KVW),
                  pl.BlockSpec((None, None, WINDOW, KVW), lambda b, p, pt: (a, b, 0, 0)), one(KVW),
                  const(pe_rows), const(wbd), const(strips), const(cstrips),
                  pl.BlockSpec(memory_space=pltpu.SMEM)],
        out_specs=rows(512),
        scratch_shapes=[pltpu.VMEM((past, LANE), F32), pltpu.VMEM((past, LANE), F32),
                        pltpu.VMEM((past + TK_S, KVW), BF16),
                        pltpu.VMEM((WINDOW + TK_S, KVW), BF16)] + _flash_scratch(TQ_S))
    out = pl.pallas_call(
        functools.partial(_snsa_kernel, n_pages=n_pages),
        grid_spec=grid_spec,
        out_shape=jax.ShapeDtypeStruct((bd, TQ_S, B_HEADS * HEAD_DIM), BF16),
        compiler_params=_cparams(("arbitrary", "arbitrary")),
        name="nsa_sample",
    )(page_table.reshape(-1), cache_cmp, cache_sel,
      _sample_rows(pr["qb"][0]), _sample_rows(misc[:, M_GATE:M_GATE + 3 * B_HEADS]),
      pr["kvs"][0][:, None, :], win, pr["kvw"][0][:, None, :],
      pe_rows, wbd, strips, cstrips, far)
    return out[:, 0]


def _merge_kernel(x_ref, mod_ref, oa_ref, ob_ref, w_ref, o_ref):
    n = oa_ref.shape[-1]
    y = _mm(oa_ref[...], w_ref[:n]) + _mm(ob_ref[...], w_ref[n:])
    o_ref[...] = x_ref[...] + (1.0 + mod_ref[2]) * y


def _merge_call(x, mod, oa, ob, w, tm):
    nb, t, d = x.shape
    tmod = mod.shape[2]
    mod_spec = (pl.BlockSpec((None, 3, 1, d), lambda b, i: (b, 0, 0, 0)) if tmod == 1 else
                pl.BlockSpec((None, 3, tm, d), lambda b, i: (b, 0, i, 0)))
    row = lambda w_: pl.BlockSpec((None, tm, w_), lambda b, i: (b, i, 0))
    return pl.pallas_call(
        _merge_kernel,
        grid=(nb, t // tm),
        in_specs=[row(d), mod_spec, row(oa.shape[-1]), row(ob.shape[-1]),
                  pl.BlockSpec(w.shape, lambda b, i: (0, 0))],
        out_specs=row(d),
        out_shape=jax.ShapeDtypeStruct(x.shape, F32),
        compiler_params=_cparams(("arbitrary", "arbitrary")),
        name="merge",
    )(x, mod, oa, ob, w)


def _gelu_ln(h, w_in_ref, ln_ref, cw):
    uv = jax.nn.gelu(_mm(h, w_in_ref[...]))
    u, v = uv[:, :cw], uv[:, cw:]
    mu = jnp.mean(v, axis=-1, keepdims=True)
    var = jnp.mean(jnp.square(v - mu), axis=-1, keepdims=True)
    return u, (v - mu) * lax.rsqrt(var + EPS) * ln_ref[...]


def _pgmlp_kernel(x_ref, mod_ref, g_ref, w_in_ref, ln_ref, ws_ref, bs_ref, w_out_ref, o_ref, *, cw):
    x = x_ref[...]
    h = _modulated_norm(x, g_ref[...], mod_ref).astype(BF16)
    u, v = _gelu_ln(h, w_in_ref, ln_ref, cw)
    gw = cw // C_GROUPS
    rows = []
    for n in range(x.shape[0] // C_CHUNK):
        vb = v[n * C_CHUNK:(n + 1) * C_CHUNK].astype(BF16)
        sv = [_mm(ws_ref[g], vb[:, g * gw:(g + 1) * gw]) + bs_ref[:, g:g + 1] for g in range(C_GROUPS)]
        rows.append(jnp.concatenate(sv, axis=1))
    sv = rows[0] if len(rows) == 1 else jnp.concatenate(rows, axis=0)
    o_ref[...] = x + (1.0 + mod_ref[2]) * _mm(u * sv, w_out_ref[...])


def _pgmlp_call(x, mod, g, w_in, ln_g, ws, bs_t, w_out, tm):
    nb, t, d = x.shape
    cw = w_out.shape[0]
    const = lambda a: pl.BlockSpec(a.shape, lambda b, i: (0,) * a.ndim)
    return pl.pallas_call(
        functools.partial(_pgmlp_kernel, cw=cw),
        grid=(nb, t // tm),
        in_specs=[pl.BlockSpec((None, tm, d), lambda b, i: (b, i, 0)),
                  pl.BlockSpec((None, 3, 1, d), lambda b, i: (b, 0, 0, 0)),
                  pl.BlockSpec((1, d), lambda b, i: (0, 0)),
                  const(w_in), pl.BlockSpec((1, cw), lambda b, i: (0, 0)), const(ws), const(bs_t), const(w_out)],
        out_specs=pl.BlockSpec((None, tm, d), lambda b, i: (b, i, 0)),
        out_shape=jax.ShapeDtypeStruct(x.shape, F32),
        compiler_params=_cparams(("arbitrary", "arbitrary")),
        name="gmlp_prompt",
    )(x, mod, g.reshape(1, d), w_in, ln_g.reshape(1, cw), ws, bs_t, w_out)


def _sgmlp_kernel(x_ref, mod_ref, g_ref, w_in_ref, ln_ref, ws0_ref, bs0_ref, w_out_ref, o_ref, v_ref, *, cw):
    x = x_ref[...]
    h = _modulated_norm(x, g_ref[...], mod_ref).astype(BF16)
    u, v = _gelu_ln(h, w_in_ref, ln_ref, cw)
    v_ref[...] = v
    sv = v.astype(BF16).astype(F32) * ws0_ref[...].astype(F32) + bs0_ref[...]
    o_ref[...] = x + (1.0 + mod_ref[2]) * _mm(u * sv, w_out_ref[...])


def _sgmlp_call(x, mod, g, w_in, ln_g, ws, b_sp, w_out):
    nb, t, d = x.shape
    cw = w_out.shape[0]
    gw = cw // C_GROUPS
    ws0 = jnp.repeat(ws[:, 0, 0], gw).reshape(1, cw)
    bs0 = jnp.repeat(b_sp[:, 0], gw).reshape(1, cw)
    full = lambda a: pl.BlockSpec(a.shape, lambda: (0,) * a.ndim)
    x2, mod2 = x[0], mod[0]
    o, v = pl.pallas_call(
        functools.partial(_sgmlp_kernel, cw=cw),
        in_specs=[full(x2), full(mod2), pl.BlockSpec((1, d), lambda: (0, 0)), full(w_in),
                  pl.BlockSpec((1, cw), lambda: (0, 0)), full(ws0), full(bs0), full(w_out)],
        out_specs=[full(x2), pl.BlockSpec((t, cw), lambda: (0, 0))],
        out_shape=[jax.ShapeDtypeStruct(x2.shape, F32), jax.ShapeDtypeStruct((t, cw), F32)],
        compiler_params=pltpu.CompilerParams(vmem_limit_bytes=VMEM_LIMIT),
        name="gmlp_sample",
    )(x2, mod2, g.reshape(1, d), w_in, ln_g.reshape(1, cw), ws0, bs0, w_out)
    return o[None], v


def kernel(x_prompt, x_sample, cache_a_kv, cache_a_idx, cache_b_cmp_kv, cache_b_sel_kv, state_b_win_kv,
           page_table, c_prompt, c_sample, rel_bias, w_ada, b_ada, norm_g, w_ffn_in, w_ffn_out,
           w_in_att, w_out_att, cmp_pe, w_cmp, w_in_c, ln_c_g, w_sp, b_sp, w_out_c, final_g):
    nbp, s, d = x_prompt.shape
    nbs, tdec, _ = x_sample.shape
    depth = w_ada.shape[0]
    assert tdec == 1 and s % TQ_P == 0 and page_table.shape[1] * PAGE >= WINDOW
    past = page_table.shape[1] * PAGE

    ada = _ada_call(jnp.concatenate([c_prompt, c_sample], axis=0), w_ada, b_ada)
    ada = ada.reshape(depth, nbp + nbs, 3, 3, d)
    mod_p = lambda l, j: ada[l, :nbp, j][:, :, None, :]
    mod_s = lambda l, j: jnp.transpose(ada[l, nbp:, j], (1, 0, 2))[None]

    w_ffn_in_h = w_ffn_in.astype(BF16)
    w_ffn_out_h = w_ffn_out.astype(BF16)
    tab_a, tab_b = rel_bias[:, :A_HEADS], rel_bias[:, A_HEADS:]
    strips_pa = _bias_strips(tab_a, TQ_P, TQ_P)
    strips_pb = _bias_strips(tab_b, TQ_P, TQ_P)
    strips_sa = _bias_strips(tab_a, TQ_S, TK_S)
    strips_sb = _bias_strips(tab_b, TQ_S, TK_S)
    cstrips_p = _cmp_strips(tab_b, TQ_P)
    cstrips_s = _cmp_strips(tab_b, TQ_S)
    far_b = tab_b[N_BUCKETS - 1]

    xp = x_prompt
    xs = jnp.transpose(x_sample, (1, 0, 2))
    tm_p = 512 if s % 512 == 0 else TQ_P
    st_p, st_s, st_c = [], [], []
    for l in range(depth):
        last = l == depth - 1
        xp = _ffn_call(xp, mod_p(l, 0), norm_g[l, 0], w_ffn_in_h, w_ffn_out_h, l, 0, None, tm_p)
        xs = _ffn_call(xs, mod_s(l, 0), norm_g[l, 0], w_ffn_in_h, w_ffn_out_h, l, 0, None, nbs)
        if l % 2 == 0:
            a = l // 2
            wp = _rearranged_proj_weight(w_in_att[a])
            w_out = w_out_att[a].astype(BF16)
            pe_rows, wbd = _compress_operands(cmp_pe[a], w_cmp[a])

            pr = _proj_call(xp, mod_p(l, 1), norm_g[l, 1], wp, tm_p)
            oa = _pdsa_call(pr, strips_pa)
            kcp = _pcmp_call(pr["kvc"], pe_rows, wbd)
            ob = _pnsa_call(pr, kcp, strips_pb, cstrips_p, far_b)
            xp = _merge_call(xp, mod_p(l, 1), oa, ob, w_out, tm_p)
            kv5 = lambda t: t.reshape(t.shape[0], t.shape[1], 2, A_KV, HEAD_DIM)
            st_p.append((kv5(pr["kva"]), pr["misc"][:, :, M_KI:M_KI + IDX_DIM], kv5(pr["kvc"]),
                         kv5(pr["kvs"]), kv5(pr["kvw"][:, s - min(WINDOW, s):])))

            ps = _proj_call(xs, mod_s(l, 1), norm_g[l, 1], wp, nbs)
            oa = _sdsa_call(a, page_table, cache_a_idx, cache_a_kv, ps, strips_sa)
            ob = _snsa_call(a, page_table, cache_b_cmp_kv, cache_b_sel_kv, state_b_win_kv, ps,
                            pe_rows, wbd, strips_sb, cstrips_s, far_b)
            xs = _merge_call(xs, mod_s(l, 1), oa[None], ob[None], w_out, nbs)
            tok = lambda t: t[0].reshape(nbs, 1, 2, A_KV, HEAD_DIM)
            st_s.append((tok(ps["kva"]), ps["misc"][0][:, None, M_KI:M_KI + IDX_DIM], tok(ps["kvc"]),
                         tok(ps["kvs"]),
                         jnp.concatenate([state_b_win_kv[a][:, tdec:], tok(ps["kvw"])], axis=1)))
        else:
            ci = l // 2
            w_in = w_in_c[ci].astype(BF16)
            w_out = w_out_c[ci].astype(BF16)
            ws = (w_sp[ci] * jnp.tril(jnp.ones((C_CHUNK, C_CHUNK), w_sp.dtype))).astype(BF16)
            xp = _pgmlp_call(xp, mod_p(l, 1), norm_g[l, 1], w_in, ln_c_g[ci], ws,
                             jnp.transpose(b_sp[ci]), w_out, 256 if s % 256 == 0 else C_CHUNK)
            xs, v = _sgmlp_call(xs, mod_s(l, 1), norm_g[l, 1], w_in, ln_c_g[ci], ws, b_sp[ci], w_out)
            st_c.append(v[:, None, :])
        fg = final_g if last else None
        xp = _ffn_call(xp, mod_p(l, 2), norm_g[l, 2], w_ffn_in_h, w_ffn_out_h, l, 1, fg, tm_p)
        xs = _ffn_call(xs, mod_s(l, 2), norm_g[l, 2], w_ffn_in_h, w_ffn_out_h, l, 1, fg, nbs)

    stk = lambda sts, i: jnp.stack([st[i] for st in sts])
    return (xp, jnp.transpose(xs, (1, 0, 2)),
            stk(st_p, 0), stk(st_p, 1), stk(st_p, 2), stk(st_p, 3), stk(st_p, 4),
            stk(st_s, 0), stk(st_s, 1), stk(st_s, 2), stk(st_s, 3), stk(st_s, 4),
            jnp.stack(st_c))
```

```python
import functools
import math

import numpy as np
import jax
import jax.numpy as jnp
from jax import lax
from jax.experimental import pallas as pl
from jax.experimental.pallas import tpu as pltpu

F32 = jnp.float32
BF16 = jnp.bfloat16
I32 = jnp.int32

HEAD_DIM = 64
A_HEADS = 8
A_KV = 2
IDX_HEADS = 4
IDX_DIM = 64
IDX_TOPK = 256
B_HEADS = 8
B_KV = 2
CMP_BLOCK = 32
SEL_BLOCK = 64
SEL_TOPN = 16
WINDOW = 512
N_BUCKETS = 32
MAX_DISTANCE = 1024
C_CHUNK = 128
C_GROUPS = 8
EPS = 1e-6
PAGE = 128
GROUP = A_HEADS // A_KV
KVW = 2 * A_KV * HEAD_DIM

LANE = 128
VMEM_LIMIT = 56 * 1024 * 1024
KVX = 3 * LANE

NEG = -1e30
KMIN = -2 ** 31

P_QA, P_QB, P_KVA, P_QI, P_KVC, P_KVS, P_KVW, P_MISC = 0, 512, 1024, 1280, 1536, 1792, 2048, 2304
P_WIDTH = 2432
M_KI, M_WI, M_GATE = 0, IDX_DIM, IDX_DIM + IDX_HEADS

assert A_KV == 2 and B_KV == 2 and A_KV * HEAD_DIM == LANE and IDX_DIM == HEAD_DIM


def _cparams(sem):
    return pltpu.CompilerParams(dimension_semantics=sem, vmem_limit_bytes=VMEM_LIMIT)


def _mm(a, b):
    return jnp.dot(a.astype(BF16), b.astype(BF16), preferred_element_type=F32)


def _mm_nt(a, b):
    return lax.dot_general(a.astype(BF16), b.astype(BF16), (((1,), (1,)), ((), ())),
                           preferred_element_type=F32)


def _modulated_norm(x, g, mod_ref):
    y = x * lax.rsqrt(jnp.mean(x * x, axis=-1, keepdims=True) + EPS) * g
    return y * (1.0 + mod_ref[1]) + mod_ref[0]


def _ada_kernel(c_ref, w_ref, b_ref, o_ref):
    c = c_ref[...]
    o_ref[...] = _mm(c * jax.nn.sigmoid(c), w_ref[...]) + b_ref[...]


def _ada_call(c_all, w_ada, b_ada):
    depth, d, n = w_ada.shape
    r = c_all.shape[0]
    tn = n // 8
    return pl.pallas_call(
        _ada_kernel,
        grid=(depth, n // tn),
        in_specs=[pl.BlockSpec((r, d), lambda l, j: (0, 0)),
                  pl.BlockSpec((None, d, tn), lambda l, j: (l, 0, j)),
                  pl.BlockSpec((None, 1, tn), lambda l, j: (l, 0, j))],
        out_specs=pl.BlockSpec((None, r, tn), lambda l, j: (l, 0, j)),
        out_shape=jax.ShapeDtypeStruct((depth, r, n), F32),
        compiler_params=_cparams(("arbitrary", "arbitrary")),
        name="ada",
    )(c_all, w_ada, b_ada.reshape(depth, 1, n))


def _ffn_kernel(x_ref, mod_ref, g_ref, wa_ref, wb_ref, wo_ref, fg_ref, o_ref, h_scr, acc_scr, *,
                n_f, final_norm):
    f = pl.program_id(2)

    @pl.when(f == 0)
    def _():
        h_scr[...] = _modulated_norm(x_ref[...], g_ref[...], mod_ref).astype(BF16)
        acc_scr[...] = jnp.zeros_like(acc_scr)

    h = h_scr[...]
    a = _mm(h, wa_ref[...])
    b = _mm(h, wb_ref[...])
    acc_scr[...] += _mm(a * jax.nn.sigmoid(a) * b, wo_ref[...])

    @pl.when(f == n_f - 1)
    def _():
        y = x_ref[...] + 0.5 * (1.0 + mod_ref[2]) * acc_scr[...]
        if final_norm:
            y = y * lax.rsqrt(jnp.mean(y * y, axis=-1, keepdims=True) + EPS) * fg_ref[...]
        o_ref[...] = y


def _ffn_call(x, mod, g, w_in, w_out, l, j, final_g, tm):
    nb, t, d = x.shape
    dff = w_out.shape[2]
    tf = 256
    n_f = dff // tf
    tmod = mod.shape[2]
    mod_spec = (pl.BlockSpec((None, 3, 1, d), lambda b, i, f: (b, 0, 0, 0)) if tmod == 1 else
                pl.BlockSpec((None, 3, tm, d), lambda b, i, f: (b, 0, i, 0)))
    final_norm = final_g is not None
    fg = (final_g if final_norm else g).reshape(1, d)
    return pl.pallas_call(
        functools.partial(_ffn_kernel, n_f=n_f, final_norm=final_norm),
        grid=(nb, t // tm, n_f),
        in_specs=[pl.BlockSpec((None, tm, d), lambda b, i, f: (b, i, 0)),
                  mod_spec,
                  pl.BlockSpec((1, d), lambda b, i, f: (0, 0)),
                  pl.BlockSpec((None, None, d, tf), lambda b, i, f: (l, j, 0, f)),
                  pl.BlockSpec((None, None, d, tf), lambda b, i, f: (l, j, 0, f + n_f)),
                  pl.BlockSpec((None, None, tf, d), lambda b, i, f: (l, j, f, 0)),
                  pl.BlockSpec((1, d), lambda b, i, f: (0, 0))],
        out_specs=pl.BlockSpec((None, tm, d), lambda b, i, f: (b, i, 0)),
        out_shape=jax.ShapeDtypeStruct(x.shape, F32),
        scratch_shapes=[pltpu.VMEM((tm, d), BF16), pltpu.VMEM((tm, d), F32)],
        compiler_params=_cparams(("arbitrary", "arbitrary", "arbitrary")),
        name="ffn",
    )(x, mod, g.reshape(1, d), w_in, w_in, w_out, fg)


def _kv_ext(y):
    e = jnp.where(lax.broadcasted_iota(I32, (y.shape[0], HEAD_DIM), 1) == 0, 1.0, 0.0).astype(BF16)
    y = y.astype(BF16)
    return jnp.concatenate([y[:, :LANE], y[:, LANE:LANE + HEAD_DIM], e, y[:, LANE + HEAD_DIM:], e], axis=1)


def _kv_ext_t(y):
    e = jnp.where(lax.broadcasted_iota(I32, (HEAD_DIM, y.shape[1]), 0) == 0, 1.0, 0.0).astype(BF16)
    y = y.astype(BF16)
    return jnp.concatenate([y[:LANE], y[LANE:LANE + HEAD_DIM], e, y[LANE + HEAD_DIM:], e], axis=0)


_PROJ_OUT = (("qa", P_QA, 512, "h"), ("qb", P_QB, 512, "h"), ("qi", P_QI, 256, "h"),
             ("kva", P_KVA, 256, "f"), ("kvc", P_KVC, 256, "f"), ("kvs", P_KVS, 256, "f"),
             ("kvw", P_KVW, 256, "f"), ("misc", P_MISC, 128, "f"),
             ("kva_x", P_KVA, KVX, "x"), ("kvs_x", P_KVS, KVX, "x"), ("kvw_x", P_KVW, KVX, "x"),
             ("misc_h", P_MISC, 128, "h"))


def _proj_kernel(x_ref, mod_ref, g_ref, w_ref, *o_refs):
    h = _modulated_norm(x_ref[...], g_ref[...], mod_ref).astype(BF16)
    done = {}
    for (name, off, width, kind), o_ref in zip(_PROJ_OUT, o_refs):
        if off not in done:
            y = _mm(h, w_ref[:, off:off + (KVW if kind == "x" else width)])
            if name in ("qa", "qb"):
                y = y * (HEAD_DIM ** -0.5)
            done[off] = y
        o_ref[...] = _kv_ext(done[off]) if kind == "x" else done[off].astype(o_ref.dtype)


def _proj_call(x, mod, g, wp, tm):
    nb, t, d = x.shape
    tmod = mod.shape[2]
    mod_spec = (pl.BlockSpec((None, 3, 1, d), lambda b, i: (b, 0, 0, 0)) if tmod == 1 else
                pl.BlockSpec((None, 3, tm, d), lambda b, i: (b, 0, i, 0)))
    outs = pl.pallas_call(
        _proj_kernel,
        grid=(nb, t // tm),
        in_specs=[pl.BlockSpec((None, tm, d), lambda b, i: (b, i, 0)),
                  mod_spec,
                  pl.BlockSpec((1, d), lambda b, i: (0, 0)),
                  pl.BlockSpec((d, P_WIDTH), lambda b, i: (0, 0))],
        out_specs=[pl.BlockSpec((None, tm, w), lambda b, i: (b, i, 0)) for _, _, w, _ in _PROJ_OUT],
        out_shape=[jax.ShapeDtypeStruct((nb, t, w), F32 if kind == "f" else BF16)
                   for _, _, w, kind in _PROJ_OUT],
        compiler_params=_cparams(("arbitrary", "arbitrary")),
        name="proj",
    )(x, mod, g.reshape(1, d), wp)
    return {name: o for (name, _, _, _), o in zip(_PROJ_OUT, outs)}


def _rearranged_proj_weight(w):
    d = w.shape[0]
    o = np.cumsum((0, 512, 256, 256, 64, 4, 512, 256, 256, 256, 24))
    seg = lambda i: w[:, o[i]:o[i + 1]]
    qa, kva, qi, ki, wi, qb, kvc, kvs, kvw, gates = [seg(i) for i in range(10)]
    pad = jnp.zeros((d, LANE - IDX_DIM - IDX_HEADS - 3 * B_HEADS), w.dtype)
    return jnp.concatenate([qa, qb, kva, qi, kvc, kvs, kvw, ki, wi, gates, pad], axis=1).astype(BF16)


def _t5_bucket(dist):
    n = jnp.maximum(dist, 0)
    max_exact = N_BUCKETS // 2
    nf = jnp.maximum(n, 1).astype(F32)
    large = max_exact + (jnp.log(nf / max_exact) / math.log(MAX_DISTANCE / max_exact)
                         * (N_BUCKETS - max_exact)).astype(I32)
    large = jnp.minimum(large, N_BUCKETS - 1)
    return jnp.where(n < max_exact, n, large)


def _bias_lookup(tab, dist):
    b = _t5_bucket(jnp.asarray(dist.reshape(-1), I32))
    onehot = (b[:, None] == jnp.arange(N_BUCKETS, dtype=I32)[None, :]).astype(F32)
    out = jnp.dot(onehot, tab, precision=lax.Precision.HIGHEST)
    return jnp.transpose(out).reshape((tab.shape[1],) + dist.shape)


NEAR = -(-(MAX_DISTANCE + LANE - 1) // LANE)


def _bias_strips(tab, tq):
    r = np.arange(tq)[None, :, None]
    c = np.arange(LANE)[None, None, :]
    bd = np.arange(NEAR + 1)[:, None, None]
    dist = np.where(bd == NEAR, MAX_DISTANCE, r - c + LANE * bd)
    return _bias_lookup(tab, dist)


def _cmp_near_count(delta):
    return -(-(-(-(MAX_DISTANCE - delta) // CMP_BLOCK)) // 8) * 8


def _cmp_delta(tq):
    return -(CMP_BLOCK - 1) - CMP_BLOCK * ((tq - CMP_BLOCK) // CMP_BLOCK)


def _cmp_strips(tab, tq):
    delta = _cmp_delta(tq)
    ncn = _cmp_near_count(delta)
    dist = np.arange(tq)[:, None] + delta + CMP_BLOCK * np.arange(ncn)[None, :]
    v = _bias_lookup(tab, dist)
    hi = v.astype(BF16)
    r1 = v - hi.astype(F32)
    mid = r1.astype(BF16)
    lo = (r1 - mid.astype(F32)).astype(BF16)
    return jnp.stack([hi, mid, lo], axis=1)


def _padded_heads(x, g):
    z = jnp.zeros((x.shape[0], HEAD_DIM), x.dtype)
    rows = []
    for r in range(GROUP):
        h = g * GROUP + r
        piece = x[:, h * HEAD_DIM:(h + 1) * HEAD_DIM]
        rows.append(jnp.concatenate([piece, z] if g == 0 else [z, piece], axis=1))
    return jnp.concatenate(rows, axis=0)


def _k_part(kv, kv_t):
    return kv[:LANE] if kv_t else kv[:, :LANE]


def _v_part(kv, g, kv_t):
    return kv[(1 + g) * LANE:(2 + g) * LANE] if kv_t else kv[:, (1 + g) * LANE:(2 + g) * LANE]


def _flash_reset(m_scr, acc_scr):
    m_scr[...] = jnp.full_like(m_scr, NEG)
    acc_scr[...] = jnp.zeros_like(acc_scr)


def _flash_update(m_scr, acc_scr, g, qp_g, kv, strips_ref, bds, mask, tq, kv_t):
    kb = _k_part(kv, kv_t)
    s_all = _mm(qp_g, kb) if kv_t else _mm_nt(qp_g, kb)
    ps, alphas = [], []
    for r in range(GROUP):
        h = g * GROUP + r
        s = s_all[r * tq:(r + 1) * tq]
        s = jnp.concatenate([s[:, j * LANE:(j + 1) * LANE] + strips_ref[h, bd] for j, bd in enumerate(bds)],
                            axis=1)
        s = jnp.where(mask, s, NEG)
        m_old = m_scr[g, r]
        m_new = jnp.maximum(m_old, jnp.max(s, axis=-1, keepdims=True))
        m_scr[g, r] = m_new
        alphas.append(jnp.exp(m_old - m_new))
        ps.append(jnp.exp(s - m_new).astype(BF16))
    p = jnp.concatenate(ps, axis=0)
    vx = _v_part(kv, g, kv_t)
    pv = _mm_nt(p, vx) if kv_t else _mm(p, vx)
    acc_scr[g] = jnp.concatenate(alphas, axis=0) * acc_scr[g] + pv


def _flash_result(acc_scr, g):
    acc = acc_scr[g]
    l = acc[:, HEAD_DIM:HEAD_DIM + 1]
    return jnp.where(l > 0.0, acc[:, :HEAD_DIM] / jnp.where(l > 0.0, l, 1.0), 0.0)


def _tile_bds(qb, k, tk):
    nblk = tk // LANE
    return [jnp.clip(qb - (k * nblk + j), 0, NEAR) for j in range(nblk)]


def _ordered_key(x):
    b = lax.bitcast_convert_type(x, I32)
    return jnp.where(x == 0.0, 0, b ^ ((b >> 31) & 0x7FFFFFFF))


def _dsa_core(qi, wi, qa, get_kidx, get_kva, strips_ref, t0, tq, tk, topk, kv_t,
              keys_scr, wib_scr, m_scr, acc_scr, tri_scr):
    n_tiles = (t0 + tq - 1) // tk + 1
    qb = t0 // LANE
    row_pos = t0 + lax.broadcasted_iota(I32, (tq, tk), 0)
    col = lax.broadcasted_iota(I32, (tq, tk), 1)

    if kv_t:
        qi_st = jnp.concatenate([qi[:, h * IDX_DIM:(h + 1) * IDX_DIM] for h in range(IDX_HEADS)], axis=0)
    else:
        z = jnp.zeros((tq, LANE - IDX_DIM), qi.dtype)
        qi_st = jnp.concatenate(
            [jnp.concatenate([qi[:, h * IDX_DIM:(h + 1) * IDX_DIM], z], axis=1) for h in range(IDX_HEADS)], axis=0)
    for h in range(IDX_HEADS):
        wib_scr[h] = jnp.broadcast_to(wi[:, h:h + 1], (tq, tk))

    def score_tile(k, c):
        kt = get_kidx(k)
        sc = _mm(qi_st, kt) if kv_t else _mm_nt(qi_st, kt)
        score = jnp.maximum(sc[:tq], 0.0) * wib_scr[0]
        for h in range(1, IDX_HEADS):
            score = score + jnp.maximum(sc[h * tq:(h + 1) * tq], 0.0) * wib_scr[h]
        keys_scr[k] = jnp.where(k * tk + col <= row_pos, _ordered_key(score), KMIN)
        return c

    lax.fori_loop(0, n_tiles, score_tile, 0)

    tu = jnp.zeros((tq, 1), I32)
    n_ge = jnp.zeros((tq, 1), F32)
    for bit in range(31, -1, -1):
        cand_u = tu | np.int32(-2 ** 31 if bit == 31 else 2 ** bit)
        cand_s = cand_u ^ np.int32(KMIN)

        def count_tile(k, cnt, cand_s=cand_s):
            return cnt + jnp.where(keys_scr[k] >= cand_s, 1.0, 0.0)

        cnt = lax.fori_loop(0, n_tiles, count_tile, jnp.zeros((tq, tk), F32))
        tot = jnp.sum(cnt, axis=-1, keepdims=True)
        tu = jnp.where(tot >= topk, cand_u, tu)
        n_ge = jnp.where(tot >= topk, tot, n_ge)
    thr = tu ^ np.int32(KMIN)
    found = thr > np.int32(KMIN)
    has_ties = jnp.max(jnp.where(found & (n_ge > topk), 1.0, 0.0)) > 0.5

    qp_g = [_padded_heads(qa, g) for g in range(A_KV)]
    _flash_reset(m_scr, acc_scr)

    def attend(k, mask):
        kv = get_kva(k)
        bds = _tile_bds(qb, k, tk)
        for g in range(A_KV):
            _flash_update(m_scr, acc_scr, g, qp_g[g], kv, strips_ref, bds, mask, tq, kv_t)

    @pl.when(jnp.logical_not(has_ties))
    def _():
        thr_ge = jnp.where(found, thr, np.int32(KMIN + 1))

        def tile(k, c):
            attend(k, keys_scr[k] >= thr_ge)
            return c

        lax.fori_loop(0, n_tiles, tile, 0)

    @pl.when(has_ties)
    def _():
        def count_gt(k, cnt):
            return cnt + jnp.where(keys_scr[k] > thr, 1.0, 0.0)

        n_gt = jnp.sum(lax.fori_loop(0, n_tiles, count_gt, jnp.zeros((tq, tk), F32)), axis=-1, keepdims=True)
        need = topk - n_gt
        tri_scr[...] = jnp.where(
            lax.broadcasted_iota(I32, (tk, tk), 0) <= lax.broadcasted_iota(I32, (tk, tk), 1), 1.0, 0.0).astype(BF16)

        def tile(k, eq_seen):
            key = keys_scr[k]
            eq = key == thr
            incl = _mm(jnp.where(eq, 1.0, 0.0), tri_scr[...])
            attend(k, (key > thr) | (eq & found & (eq_seen + incl <= need)))
            return eq_seen + incl[:, tk - 1:tk]

        lax.fori_loop(0, n_tiles, tile, jnp.zeros((tq, 1), F32))

    outs = []
    for g in range(A_KV):
        o = _flash_result(acc_scr, g)
        outs += [o[r * tq:(r + 1) * tq] for r in range(GROUP)]
    return jnp.concatenate(outs, axis=1)


def _pick_blocks(v, n_pick):
    lane = lax.broadcasted_iota(I32, v.shape, 1).astype(F32)
    sel = jnp.zeros(v.shape, jnp.bool_)
    for _ in range(n_pick):
        mx = jnp.max(v, axis=-1, keepdims=True)
        cand = (v == mx) & jnp.logical_not(sel)
        first = jnp.min(jnp.where(cand, lane, float(v.shape[1])), axis=-1, keepdims=True)
        pick = lane == first
        sel = sel | pick
        v = jnp.where(pick, -jnp.inf, v)
    return sel


def _nsa_core(qb, gates, kcp, get_kvs, win_tiles, strips_ref, cstrips_ref, far_ref, t0, tq, tk, nj,
              n_pick, extra_block, kv_t, m_scr, acc_scr):
    n_tiles = (t0 + tq - 1) // tk + 1
    qblk = t0 // LANE
    delta = _cmp_delta(tq)
    ncn = cstrips_ref.shape[-1]
    qp_g = [_padded_heads(qb, g) for g in range(B_KV)]
    row_pos1 = t0 + lax.broadcasted_iota(I32, (tq, 1), 0)

    lam = lax.broadcasted_iota(I32, (tq, 2 * nj), 1)
    m_of = 2 * (lam % nj) + lam // nj
    cdist = row_pos1 - (CMP_BLOCK * m_of + CMP_BLOCK - 1)
    cmask = cdist >= 0
    m_hi = (t0 - (CMP_BLOCK - 1) - delta) // CMP_BLOCK
    lam_s = lax.broadcasted_iota(I32, (ncn, 2 * nj), 1)
    shift = jnp.where(2 * (lam_s % nj) + lam_s // nj == m_hi - lax.broadcasted_iota(I32, (ncn, 2 * nj), 0),
                      1.0, 0.0).astype(BF16)
    is_far = m_of <= m_hi - ncn
    kc_both = kcp[:, :LANE]
    o_c, imp = [], []
    for g in range(B_KV):
        bias = []
        for r in range(GROUP):
            h = g * GROUP + r
            b = (_mm(cstrips_ref[h, 0], shift) + _mm(cstrips_ref[h, 1], shift)) + _mm(cstrips_ref[h, 2], shift)
            bias.append(jnp.where(is_far, far_ref[h], b))
        s = _mm_nt(qp_g[g], kc_both).reshape(GROUP, tq, 2 * nj) + jnp.stack(bias)
        s = jnp.where(cmask[None], s, NEG)
        e = jnp.where(cmask[None], jnp.exp(s - jnp.max(s, axis=-1, keepdims=True)), 0.0)
        l = jnp.sum(e, axis=-1, keepdims=True)
        p = jnp.where(l > 0.0, e / jnp.where(l > 0.0, l, 1.0), 0.0)
        o_c.append(_mm(p.reshape(GROUP * tq, 2 * nj), kcp[:, (B_KV + g) * HEAD_DIM:(B_KV + g + 1) * HEAD_DIM]))
        ps = p[0]
        for r in range(1, GROUP):
            ps = ps + p[r]
        imp.append(ps[:, :nj] + ps[:, nj:])

    jl = lax.broadcasted_iota(I32, (tq, nj), 1)
    cur = row_pos1 // SEL_BLOCK
    forced = (jl == 0) | (jl == cur) | (jl == cur - 1)
    admissible = jl <= cur
    bmask = []
    for g in range(B_KV):
        v = jnp.where(admissible, jnp.where(forced, jnp.inf, imp[g]), -jnp.inf)
        bmask.append(jnp.where(_pick_blocks(v, n_pick) & admissible, 1.0, 0.0).astype(BF16))
    row_pos = t0 + lax.broadcasted_iota(I32, (tq, tk), 0)
    col = lax.broadcasted_iota(I32, (tq, tk), 1)
    ej = lax.broadcasted_iota(I32, (nj, tk), 0)
    ec = lax.broadcasted_iota(I32, (nj, tk), 1)
    _flash_reset(m_scr, acc_scr)

    def sel_tile(k, c):
        kv = get_kvs(k)
        bds = _tile_bds(qblk, k, tk)
        key_pos = k * tk + col
        causal = key_pos <= row_pos
        expand = jnp.where(ej == (k * tk + ec) // SEL_BLOCK, 1.0, 0.0).astype(BF16)
        for g in range(B_KV):
            mask = _mm(bmask[g], expand) > 0.5
            if extra_block:
                mask = mask | (key_pos >= nj * SEL_BLOCK)
            _flash_update(m_scr, acc_scr, g, qp_g[g], kv, strips_ref, bds, mask & causal, tq, kv_t)
        return c

    lax.fori_loop(0, n_tiles, sel_tile, 0)
    o_s = [_flash_result(acc_scr, g) for g in range(B_KV)]

    _flash_reset(m_scr, acc_scr)
    for get_tile, off, guard in win_tiles:
        def win_tile(get_tile=get_tile, off=off):
            kv = get_tile()
            wt = kv.shape[1] if kv_t else kv.shape[0]
            dist = (lax.broadcasted_iota(I32, (tq, wt), 0) - lax.broadcasted_iota(I32, (tq, wt), 1)
                    - LANE * off)
            wmask = (dist >= 0) & (dist <= WINDOW)
            bds = [min(max(-off - j, 0), NEAR) for j in range(wt // LANE)]
            for g in range(B_KV):
                _flash_update(m_scr, acc_scr, g, qp_g[g], kv, strips_ref, bds, wmask, tq, kv_t)

        if guard is None:
            win_tile()
        else:
            pl.when(guard)(win_tile)
    o_w = [_flash_result(acc_scr, g) for g in range(B_KV)]

    gs = jax.nn.sigmoid(gates)
    outs = []
    for g in range(B_KV):
        gcol = lambda j: jnp.concatenate(
            [gs[:, (g * GROUP + r) * 3 + j:(g * GROUP + r) * 3 + j + 1] for r in range(GROUP)], axis=0)
        o = gcol(0) * o_c[g] + gcol(1) * o_s[g] + gcol(2) * o_w[g]
        outs += [o[r * tq:(r + 1) * tq] for r in range(GROUP)]
    return jnp.concatenate(outs, axis=1)


def _compress(load_rows, n_pairs, pe_ref, wbd_ref):
    acc = [jnp.zeros((n_pairs, KVW), F32), jnp.zeros((n_pairs, KVW), F32)]
    for c in range(2 * CMP_BLOCK):
        ci = c % CMP_BLOCK
        acc[c // CMP_BLOCK] = acc[c // CMP_BLOCK] + _mm(load_rows(c) + pe_ref[ci], wbd_ref[ci])
    return jnp.concatenate(acc, axis=0)


def _compress_operands(pe, w):
    pe_rows = jnp.transpose(jnp.broadcast_to(pe[:, None], (2, B_KV, CMP_BLOCK, HEAD_DIM)), (2, 0, 1, 3))
    pe_rows = pe_rows.reshape(CMP_BLOCK, 1, KVW)
    eye = jnp.eye(2 * B_KV, dtype=w.dtype).reshape(2, B_KV, 2, B_KV)
    wbd = jnp.einsum("icde,igjh->cigdjhe", w, eye).reshape(CMP_BLOCK, KVW, KVW)
    return pe_rows, wbd.astype(BF16)


def _strided_rows(lo_ref, hi_ref, c, n):
    rows = pl.ds(c, n, stride=2 * CMP_BLOCK)
    return jnp.concatenate([lo_ref[rows, :], hi_ref[rows, :]], axis=1)


def _flash_scratch(tq):
    return [pltpu.VMEM((A_KV, GROUP, tq, 1), F32), pltpu.VMEM((A_KV, GROUP * tq, LANE), F32)]


TQ_P = 128
TK_P = 512


def _prompt_tile(ref, k, tk):
    return ref[pl.ds(pl.multiple_of(k * tk, tk), tk), :]


def _pdsa_kernel(qi_ref, misc_ref, qa_ref, kidx_ref, kva_ref, strips_ref, o_ref,
                 keys_scr, wib_scr, m_scr, acc_scr, tri_scr, *, topk, tk):
    i = pl.program_id(1)
    o = _dsa_core(qi_ref[...], misc_ref[:, M_WI:M_WI + IDX_HEADS], qa_ref[...],
                  lambda k: _prompt_tile(kidx_ref, k, tk), lambda k: _prompt_tile(kva_ref, k, tk),
                  strips_ref, i * TQ_P, TQ_P, tk, topk, False,
                  keys_scr, wib_scr, m_scr, acc_scr, tri_scr)
    o_ref[...] = o.astype(o_ref.dtype)


def _pdsa_call(pr, strips, tk):
    nb, s, _ = pr["qa"].shape
    tq = TQ_P
    row = lambda w: pl.BlockSpec((None, tq, w), lambda b, i: (b, i, 0))
    full = lambda w: pl.BlockSpec((None, s, w), lambda b, i: (b, 0, 0))
    return pl.pallas_call(
        functools.partial(_pdsa_kernel, topk=min(IDX_TOPK, s // 4), tk=tk),
        grid=(nb, s // tq),
        in_specs=[row(256), row(128), row(512), full(128), full(KVX),
                  pl.BlockSpec(strips.shape, lambda b, i: (0, 0, 0, 0))],
        out_specs=row(512),
        out_shape=jax.ShapeDtypeStruct((nb, s, A_HEADS * HEAD_DIM), BF16),
        scratch_shapes=[pltpu.VMEM((s // tk, tq, tk), I32), pltpu.VMEM((IDX_HEADS, tq, tk), F32)]
        + _flash_scratch(tq) + [pltpu.VMEM((tk, tk), BF16)],
        compiler_params=_cparams(("arbitrary", "arbitrary")),
        name="dsa_prompt",
    )(pr["qi"], pr["misc"], pr["qa"], pr["misc_h"], pr["kva_x"], strips)


def _pcmp_kernel(lo_ref, hi_ref, pe_ref, wbd_ref, o_ref, *, n_pairs):
    o_ref[...] = _compress(lambda c: _strided_rows(lo_ref, hi_ref, c, n_pairs), n_pairs, pe_ref, wbd_ref)


def _pcmp_call(kvc, pe_rows, wbd):
    nb, s, _ = kvc.shape
    n_pairs = s // (2 * CMP_BLOCK)
    return pl.pallas_call(
        functools.partial(_pcmp_kernel, n_pairs=n_pairs),
        grid=(nb,),
        in_specs=[pl.BlockSpec((None, s, LANE), lambda b: (b, 0, 0)),
                  pl.BlockSpec((None, s, LANE), lambda b: (b, 0, 1)),
                  pl.BlockSpec((CMP_BLOCK, 1, KVW), lambda b: (0, 0, 0)),
                  pl.BlockSpec((CMP_BLOCK, KVW, KVW), lambda b: (0, 0, 0))],
        out_specs=pl.BlockSpec((None, 2 * n_pairs, KVW), lambda b: (b, 0, 0)),
        out_shape=jax.ShapeDtypeStruct((nb, 2 * n_pairs, KVW), F32),
        compiler_params=_cparams(("arbitrary",)),
        name="compress_prompt",
    )(kvc, kvc, pe_rows, wbd)


def _pnsa_kernel(qb_ref, misc_ref, kcp_ref, kvs_ref, kvw_ref, strips_ref, cstrips_ref, far_ref, o_ref,
                 m_scr, acc_scr, *, nj, n_pick, tk):
    i = pl.program_id(1)
    win_tiles = [(lambda wd=wd: _prompt_tile(kvw_ref, i - wd, LANE), -wd, i - wd >= 0)
                 for wd in range(WINDOW // LANE, -1, -1)]
    o = _nsa_core(qb_ref[...], misc_ref[:, M_GATE:M_GATE + 3 * B_HEADS], kcp_ref[...],
                  lambda k: _prompt_tile(kvs_ref, k, tk), win_tiles,
                  strips_ref, cstrips_ref, far_ref, i * TQ_P, TQ_P, tk, nj, n_pick, False, False,
                  m_scr, acc_scr)
    o_ref[...] = o.astype(o_ref.dtype)


def _pnsa_call(pr, kcp, strips, cstrips, far, tk):
    nb, s, _ = pr["qb"].shape
    tq = TQ_P
    nj = s // SEL_BLOCK
    row = lambda w: pl.BlockSpec((None, tq, w), lambda b, i: (b, i, 0))
    full = lambda n, w: pl.BlockSpec((None, n, w), lambda b, i: (b, 0, 0))
    const = lambda a: pl.BlockSpec(a.shape, lambda b, i: (0,) * a.ndim)
    return pl.pallas_call(
        functools.partial(_pnsa_kernel, nj=nj, n_pick=min(SEL_TOPN, nj), tk=tk),
        grid=(nb, s // tq),
        in_specs=[row(512), row(128), full(2 * nj, KVW), full(s, KVX), full(s, KVX),
                  const(strips), const(cstrips),
                  pl.BlockSpec(memory_space=pltpu.SMEM)],
        out_specs=row(512),
        out_shape=jax.ShapeDtypeStruct((nb, s, B_HEADS * HEAD_DIM), BF16),
        scratch_shapes=_flash_scratch(tq),
        compiler_params=_cparams(("arbitrary", "arbitrary")),
        name="nsa_prompt",
    )(pr["qb"], pr["misc"], kcp, pr["kvs_x"], pr["kvw_x"], strips, cstrips, far)


TQ_S = 8
TK_S = 512
PPT = TK_S // PAGE


def _pages_per_step(n_pages):
    return max(g for g in (8, 4, 2, 1) if n_pages % g == 0)


def _new_token_block(col):
    return jnp.where(lax.broadcasted_iota(I32, (col.shape[0], PAGE), 1) == 0, col, 0.0)


def _slot_tile(buf, k):
    blk = buf[pl.ds(k * PPT, PPT)]
    return jnp.concatenate([blk[j] for j in range(PPT)], axis=1)


def _sdsa_kernel(*refs, n_pages, pps, topk):
    idx_refs = refs[1:1 + pps]
    akv_refs = refs[1 + pps:1 + 2 * pps]
    (qi_ref, wi_ref, qa_ref, new_idx_ref, new_kv_ref, strips_ref, o_ref,
     idx_buf, akv_buf, keys_scr, wib_scr, m_scr, acc_scr, tri_scr) = refs[1 + 2 * pps:]
    p = pl.program_id(1)
    for j in range(pps):
        idx_buf[p * pps + j] = idx_refs[j][...].astype(BF16)
        akv_buf[p * pps + j] = _kv_ext_t(akv_refs[j][...])

    @pl.when(p == n_pages // pps - 1)
    def _():
        idx_buf[n_pages] = _new_token_block(new_idx_ref[...]).astype(BF16)
        akv_buf[n_pages] = _kv_ext_t(_new_token_block(new_kv_ref[...]))
        for j in range(1, PPT):
            idx_buf[n_pages + j] = jnp.zeros((IDX_DIM, PAGE), BF16)
            akv_buf[n_pages + j] = _kv_ext_t(jnp.zeros((KVW, PAGE), BF16))
        o = _dsa_core(qi_ref[...], wi_ref[...], qa_ref[...],
                      lambda k: _slot_tile(idx_buf, k), lambda k: _slot_tile(akv_buf, k),
                      strips_ref, n_pages * PAGE, TQ_S, TK_S, topk, True,
                      keys_scr, wib_scr, m_scr, acc_scr, tri_scr)
        o_ref[...] = o.astype(o_ref.dtype)


def _sample_rows(x):
    return jnp.broadcast_to(x[:, None, :], (x.shape[0], TQ_S, x.shape[1]))


def _pages_t(cache):
    n, p = cache.shape[:2]
    nd = cache.ndim
    return jnp.transpose(cache, (0, 1) + tuple(range(3, nd)) + (2,)).reshape(n, p, -1, cache.shape[2])


def _sdsa_call(a, page_table, idx_t, akv_t, pr, strips):
    bd, n_pages = page_table.shape
    assert idx_t.shape[3] == PAGE and (n_pages * PAGE) % TK_S == 0
    pps = _pages_per_step(n_pages)
    slots = n_pages + PPT
    misc = pr["misc"][0]
    rows = lambda w: pl.BlockSpec((None, TQ_S, w), lambda b, p, pt: (b, 0, 0))
    col = lambda w: pl.BlockSpec((None, w, 1), lambda b, p, pt: (b, 0, 0))
    page = lambda w, j: pl.BlockSpec((None, None, w, PAGE),
                                     lambda b, p, pt: (a, pt[b * n_pages + p * pps + j], 0, 0))
    grid_spec = pltpu.PrefetchScalarGridSpec(
        num_scalar_prefetch=1,
        grid=(bd, n_pages // pps),
        in_specs=[page(IDX_DIM, j) for j in range(pps)] + [page(KVW, j) for j in range(pps)]
        + [rows(256), rows(IDX_HEADS), rows(512), col(IDX_DIM), col(KVW),
           pl.BlockSpec(strips.shape, lambda b, p, pt: (0, 0, 0, 0))],
        out_specs=rows(512),
        scratch_shapes=[pltpu.VMEM((slots, IDX_DIM, PAGE), BF16), pltpu.VMEM((slots, KVX, PAGE), BF16),
                        pltpu.VMEM((slots // PPT, TQ_S, TK_S), I32), pltpu.VMEM((IDX_HEADS, TQ_S, TK_S), F32)]
        + _flash_scratch(TQ_S) + [pltpu.VMEM((TK_S, TK_S), BF16)])
    out = pl.pallas_call(
        functools.partial(_sdsa_kernel, n_pages=n_pages, pps=pps, topk=min(IDX_TOPK, (n_pages * PAGE + 1) // 4)),
        grid_spec=grid_spec,
        out_shape=jax.ShapeDtypeStruct((bd, TQ_S, A_HEADS * HEAD_DIM), BF16),
        compiler_params=_cparams(("arbitrary", "arbitrary")),
        name="dsa_sample",
    )(page_table.reshape(-1), *([idx_t] * pps), *([akv_t] * pps),
      _sample_rows(pr["qi"][0]), _sample_rows(misc[:, M_WI:M_WI + IDX_HEADS]), _sample_rows(pr["qa"][0]),
      misc[:, M_KI:M_KI + IDX_DIM, None], pr["kva"][0][:, :, None], strips)
    return out[:, 0]


def _snsa_kernel(*refs, n_pages, pps):
    cmp_refs = refs[1:1 + pps]
    sel_refs = refs[1 + pps:1 + 2 * pps]
    (qb_ref, gates_ref, new_sel_ref, win_ref, new_win_ref, pe_ref, wbd_ref, strips_ref, cstrips_ref, far_ref,
     o_ref, cmp_lo, cmp_hi, sel_buf, m_scr, acc_scr) = refs[1 + 2 * pps:]
    p = pl.program_id(1)
    past = n_pages * PAGE
    for j in range(pps):
        rows = pl.ds(pl.multiple_of((p * pps + j) * PAGE, PAGE), PAGE)
        cmp_lo[rows, :] = jnp.transpose(cmp_refs[j][:LANE])
        cmp_hi[rows, :] = jnp.transpose(cmp_refs[j][LANE:])
        sel_buf[p * pps + j] = _kv_ext_t(sel_refs[j][...])

    @pl.when(p == n_pages // pps - 1)
    def _():
        sel_buf[n_pages] = _kv_ext_t(_new_token_block(new_sel_ref[...]))
        for j in range(1, PPT):
            sel_buf[n_pages + j] = _kv_ext_t(jnp.zeros((KVW, PAGE), BF16))
        nj = past // SEL_BLOCK
        kcp = _compress(lambda c: _strided_rows(cmp_lo, cmp_hi, c, nj), nj, pe_ref, wbd_ref)
        new_win = jnp.concatenate([_new_token_block(new_win_ref[...])]
                                  + [jnp.zeros((KVW, PAGE), F32)] * (PPT - 1), axis=1)
        win_tiles = [(lambda: _kv_ext_t(win_ref[...]), -(WINDOW // LANE), None),
                     (lambda: _kv_ext_t(new_win), 0, None)]
        o = _nsa_core(qb_ref[...], gates_ref[...], kcp, lambda k: _slot_tile(sel_buf, k), win_tiles,
                      strips_ref, cstrips_ref, far_ref, past, TQ_S, TK_S, nj,
                      min(SEL_TOPN, nj + 1) - 1, True, True, m_scr, acc_scr)
        o_ref[...] = o.astype(o_ref.dtype)


def _snsa_call(a, page_table, cmp_t, sel_t, win_t, pr, pe_rows, wbd, strips, cstrips, far):
    bd, n_pages = page_table.shape
    past = n_pages * PAGE
    assert past % TK_S == 0 and win_t.shape[3] == WINDOW == TK_S
    pps = _pages_per_step(n_pages)
    misc = pr["misc"][0]
    rows = lambda w: pl.BlockSpec((None, TQ_S, w), lambda b, p, pt: (b, 0, 0))
    col = lambda w: pl.BlockSpec((None, w, 1), lambda b, p, pt: (b, 0, 0))
    page = lambda j: pl.BlockSpec((None, None, KVW, PAGE),
                                  lambda b, p, pt: (a, pt[b * n_pages + p * pps + j], 0, 0))
    const = lambda x: pl.BlockSpec(x.shape, lambda b, p, pt: (0,) * x.ndim)
    grid_spec = pltpu.PrefetchScalarGridSpec(
        num_scalar_prefetch=1,
        grid=(bd, n_pages // pps),
        in_specs=[page(j) for j in range(pps)] + [page(j) for j in range(pps)]
        + [rows(512), rows(3 * B_HEADS), col(KVW),
           pl.BlockSpec((None, None, KVW, WINDOW), lambda b, p, pt: (a, b, 0, 0)), col(KVW),
           const(pe_rows), const(wbd), const(strips), const(cstrips),
           pl.BlockSpec(memory_space=pltpu.SMEM)],
        out_specs=rows(512),
        scratch_shapes=[pltpu.VMEM((past, LANE), F32), pltpu.VMEM((past, LANE), F32),
                        pltpu.VMEM((n_pages + PPT, KVX, PAGE), BF16)] + _flash_scratch(TQ_S))
    out = pl.pallas_call(
        functools.partial(_snsa_kernel, n_pages=n_pages, pps=pps),
        grid_spec=grid_spec,
        out_shape=jax.ShapeDtypeStruct((bd, TQ_S, B_HEADS * HEAD_DIM), BF16),
        compiler_params=_cparams(("arbitrary", "arbitrary")),
        name="nsa_sample",
    )(page_table.reshape(-1), *([cmp_t] * pps), *([sel_t] * pps),
      _sample_rows(pr["qb"][0]), _sample_rows(misc[:, M_GATE:M_GATE + 3 * B_HEADS]),
      pr["kvs"][0][:, :, None], win_t, pr["kvw"][0][:, :, None],
      pe_rows, wbd, strips, cstrips, far)
    return out[:, 0]


def _merge_kernel(x_ref, mod_ref, oa_ref, ob_ref, w_ref, o_ref):
    n = oa_ref.shape[-1]
    y = _mm(oa_ref[...], w_ref[:n]) + _mm(ob_ref[...], w_ref[n:])
    o_ref[...] = x_ref[...] + (1.0 + mod_ref[2]) * y


def _merge_call(x, mod, oa, ob, w, tm):
    nb, t, d = x.shape
    tmod = mod.shape[2]
    mod_spec = (pl.BlockSpec((None, 3, 1, d), lambda b, i: (b, 0, 0, 0)) if tmod == 1 else
                pl.BlockSpec((None, 3, tm, d), lambda b, i: (b, 0, i, 0)))
    row = lambda w_: pl.BlockSpec((None, tm, w_), lambda b, i: (b, i, 0))
    return pl.pallas_call(
        _merge_kernel,
        grid=(nb, t // tm),
        in_specs=[row(d), mod_spec, row(oa.shape[-1]), row(ob.shape[-1]),
                  pl.BlockSpec(w.shape, lambda b, i: (0, 0))],
        out_specs=row(d),
        out_shape=jax.ShapeDtypeStruct(x.shape, F32),
        compiler_params=_cparams(("arbitrary", "arbitrary")),
        name="merge",
    )(x, mod, oa, ob, w)


def _gelu_ln(h, w_in_ref, ln_ref, cw):
    uv = jax.nn.gelu(_mm(h, w_in_ref[...]))
    u, v = uv[:, :cw], uv[:, cw:]
    mu = jnp.mean(v, axis=-1, keepdims=True)
    var = jnp.mean(jnp.square(v - mu), axis=-1, keepdims=True)
    return u, (v - mu) * lax.rsqrt(var + EPS) * ln_ref[...]


def _pgmlp_kernel(x_ref, mod_ref, g_ref, w_in_ref, ln_ref, ws_ref, bs_ref, w_out_ref, o_ref, *, cw):
    x = x_ref[...]
    h = _modulated_norm(x, g_ref[...], mod_ref).astype(BF16)
    u, v = _gelu_ln(h, w_in_ref, ln_ref, cw)
    gw = cw // C_GROUPS
    rows = []
    for n in range(x.shape[0] // C_CHUNK):
        vb = v[n * C_CHUNK:(n + 1) * C_CHUNK].astype(BF16)
        sv = [_mm(ws_ref[g], vb[:, g * gw:(g + 1) * gw]) + bs_ref[:, g:g + 1] for g in range(C_GROUPS)]
        rows.append(jnp.concatenate(sv, axis=1))
    sv = rows[0] if len(rows) == 1 else jnp.concatenate(rows, axis=0)
    o_ref[...] = x + (1.0 + mod_ref[2]) * _mm(u * sv, w_out_ref[...])


def _pgmlp_call(x, mod, g, w_in, ln_g, ws, bs_t, w_out, tm):
    nb, t, d = x.shape
    cw = w_out.shape[0]
    const = lambda a: pl.BlockSpec(a.shape, lambda b, i: (0,) * a.ndim)
    return pl.pallas_call(
        functools.partial(_pgmlp_kernel, cw=cw),
        grid=(nb, t // tm),
        in_specs=[pl.BlockSpec((None, tm, d), lambda b, i: (b, i, 0)),
                  pl.BlockSpec((None, 3, 1, d), lambda b, i: (b, 0, 0, 0)),
                  pl.BlockSpec((1, d), lambda b, i: (0, 0)),
                  const(w_in), pl.BlockSpec((1, cw), lambda b, i: (0, 0)), const(ws), const(bs_t), const(w_out)],
        out_specs=pl.BlockSpec((None, tm, d), lambda b, i: (b, i, 0)),
        out_shape=jax.ShapeDtypeStruct(x.shape, F32),
        compiler_params=_cparams(("arbitrary", "arbitrary")),
        name="gmlp_prompt",
    )(x, mod, g.reshape(1, d), w_in, ln_g.reshape(1, cw), ws, bs_t, w_out)


def _sgmlp_kernel(x_ref, mod_ref, g_ref, w_in_ref, ln_ref, ws0_ref, bs0_ref, w_out_ref, o_ref, v_ref, *, cw):
    x = x_ref[...]
    h = _modulated_norm(x, g_ref[...], mod_ref).astype(BF16)
    u, v = _gelu_ln(h, w_in_ref, ln_ref, cw)
    v_ref[...] = v
    sv = v.astype(BF16).astype(F32) * ws0_ref[...].astype(F32) + bs0_ref[...]
    o_ref[...] = x + (1.0 + mod_ref[2]) * _mm(u * sv, w_out_ref[...])


def _sgmlp_call(x, mod, g, w_in, ln_g, ws, b_sp, w_out):
    nb, t, d = x.shape
    cw = w_out.shape[0]
    gw = cw // C_GROUPS
    ws0 = jnp.repeat(ws[:, 0, 0], gw).reshape(1, cw)
    bs0 = jnp.repeat(b_sp[:, 0], gw).reshape(1, cw)
    full = lambda a: pl.BlockSpec(a.shape, lambda: (0,) * a.ndim)
    x2, mod2 = x[0], mod[0]
    o, v = pl.pallas_call(
        functools.partial(_sgmlp_kernel, cw=cw),
        in_specs=[full(x2), full(mod2), pl.BlockSpec((1, d), lambda: (0, 0)), full(w_in),
                  pl.BlockSpec((1, cw), lambda: (0, 0)), full(ws0), full(bs0), full(w_out)],
        out_specs=[full(x2), pl.BlockSpec((t, cw), lambda: (0, 0))],
        out_shape=[jax.ShapeDtypeStruct(x2.shape, F32), jax.ShapeDtypeStruct((t, cw), F32)],
        compiler_params=pltpu.CompilerParams(vmem_limit_bytes=VMEM_LIMIT),
        name="gmlp_sample",
    )(x2, mod2, g.reshape(1, d), w_in, ln_g.reshape(1, cw), ws0, bs0, w_out)
    return o[None], v


def kernel(x_prompt, x_sample, cache_a_kv, cache_a_idx, cache_b_cmp_kv, cache_b_sel_kv, state_b_win_kv,
           page_table, c_prompt, c_sample, rel_bias, w_ada, b_ada, norm_g, w_ffn_in, w_ffn_out,
           w_in_att, w_out_att, cmp_pe, w_cmp, w_in_c, ln_c_g, w_sp, b_sp, w_out_c, final_g):
    nbp, s, d = x_prompt.shape
    nbs, tdec, _ = x_sample.shape
    depth = w_ada.shape[0]
    assert tdec == 1 and s % TQ_P == 0 and cache_a_kv.shape[2] == PAGE

    ada = _ada_call(jnp.concatenate([c_prompt, c_sample], axis=0), w_ada, b_ada)
    ada = ada.reshape(depth, nbp + nbs, 3, 3, d)
    mod_p = lambda l, j: ada[l, :nbp, j][:, :, None, :]
    mod_s = lambda l, j: jnp.transpose(ada[l, nbp:, j], (1, 0, 2))[None]

    w_ffn_in_h = w_ffn_in.astype(BF16)
    w_ffn_out_h = w_ffn_out.astype(BF16)
    tab_a, tab_b = rel_bias[:, :A_HEADS], rel_bias[:, A_HEADS:]
    strips_pa, strips_pb = _bias_strips(tab_a, TQ_P), _bias_strips(tab_b, TQ_P)
    strips_sa, strips_sb = _bias_strips(tab_a, TQ_S), _bias_strips(tab_b, TQ_S)
    cstrips_p, cstrips_s = _cmp_strips(tab_b, TQ_P), _cmp_strips(tab_b, TQ_S)
    far_b = tab_b[N_BUCKETS - 1]
    idx_t, akv_t = _pages_t(cache_a_idx), _pages_t(cache_a_kv)
    cmp_t, sel_t, win_t = _pages_t(cache_b_cmp_kv), _pages_t(cache_b_sel_kv), _pages_t(state_b_win_kv)

    xp = x_prompt
    xs = jnp.transpose(x_sample, (1, 0, 2))
    tm_p = 512 if s % 512 == 0 else TQ_P
    tk_p = TK_P if s % TK_P == 0 else TQ_P
    st_p, st_s, st_c = [], [], []
    for l in range(depth):
        last = l == depth - 1
        xp = _ffn_call(xp, mod_p(l, 0), norm_g[l, 0], w_ffn_in_h, w_ffn_out_h, l, 0, None, tm_p)
        xs = _ffn_call(xs, mod_s(l, 0), norm_g[l, 0], w_ffn_in_h, w_ffn_out_h, l, 0, None, nbs)
        if l % 2 == 0:
            a = l // 2
            wp = _rearranged_proj_weight(w_in_att[a])
            w_out = w_out_att[a].astype(BF16)
            pe_rows, wbd = _compress_operands(cmp_pe[a], w_cmp[a])

            pr = _proj_call(xp, mod_p(l, 1), norm_g[l, 1], wp, tm_p)
            oa = _pdsa_call(pr, strips_pa, tk_p)
            kcp = _pcmp_call(pr["kvc"], pe_rows, wbd)
            ob = _pnsa_call(pr, kcp, strips_pb, cstrips_p, far_b, tk_p)
            xp = _merge_call(xp, mod_p(l, 1), oa, ob, w_out, tm_p)
            kv5 = lambda t: t.reshape(t.shape[0], t.shape[1], 2, A_KV, HEAD_DIM)
            st_p.append((kv5(pr["kva"]), pr["misc"][:, :, M_KI:M_KI + IDX_DIM], kv5(pr["kvc"]),
                         kv5(pr["kvs"]), kv5(pr["kvw"][:, s - min(WINDOW, s):])))

            ps = _proj_call(xs, mod_s(l, 1), norm_g[l, 1], wp, nbs)
            oa = _sdsa_call(a, page_table, idx_t, akv_t, ps, strips_sa)
            ob = _snsa_call(a, page_table, cmp_t, sel_t, win_t, ps, pe_rows, wbd, strips_sb, cstrips_s, far_b)
            xs = _merge_call(xs, mod_s(l, 1), oa[None], ob[None], w_out, nbs)
            tok = lambda t: t[0].reshape(nbs, 1, 2, A_KV, HEAD_DIM)
            st_s.append((tok(ps["kva"]), ps["misc"][0][:, None, M_KI:M_KI + IDX_DIM], tok(ps["kvc"]),
                         tok(ps["kvs"]),
                         jnp.concatenate([state_b_win_kv[a][:, tdec:], tok(ps["kvw"])], axis=1)))
        else:
            ci = l // 2
            w_in = w_in_c[ci].astype(BF16)
            w_out = w_out_c[ci].astype(BF16)
            ws = (w_sp[ci] * jnp.tril(jnp.ones((C_CHUNK, C_CHUNK), w_sp.dtype))).astype(BF16)
            xp = _pgmlp_call(xp, mod_p(l, 1), norm_g[l, 1], w_in, ln_c_g[ci], ws,
                             jnp.transpose(b_sp[ci]), w_out, 256 if s % 256 == 0 else C_CHUNK)
            xs, v = _sgmlp_call(xs, mod_s(l, 1), norm_g[l, 1], w_in, ln_c_g[ci], ws, b_sp[ci], w_out)
            st_c.append(v[:, None, :])
        fg = final_g if last else None
        xp = _ffn_call(xp, mod_p(l, 2), norm_g[l, 2], w_ffn_in_h, w_ffn_out_h, l, 1, fg, tm_p)
        xs = _ffn_call(xs, mod_s(l, 2), norm_g[l, 2], w_ffn_in_h, w_ffn_out_h, l, 1, fg, nbs)

    stk = lambda sts, i: jnp.stack([st[i] for st in sts])
    return (xp, jnp.transpose(xs, (1, 0, 2)),
            stk(st_p, 0), stk(st_p, 1), stk(st_p, 2), stk(st_p, 3), stk(st_p, 4),
            stk(st_s, 0), stk(st_s, 1), stk(st_s, 2), stk(st_s, 3), stk(st_s, 4),
            jnp.stack(st_c))
```

```python
import functools
import math

import numpy as np
import jax
import jax.numpy as jnp
from jax import lax
from jax.experimental import pallas as pl
from jax.experimental.pallas import tpu as pltpu

F32 = jnp.float32
BF16 = jnp.bfloat16
I32 = jnp.int32

HEAD_DIM = 64
A_HEADS = 8
A_KV = 2
IDX_HEADS = 4
IDX_DIM = 64
IDX_TOPK = 256
B_HEADS = 8
B_KV = 2
CMP_BLOCK = 32
SEL_BLOCK = 64
SEL_TOPN = 16
WINDOW = 512
N_BUCKETS = 32
MAX_DISTANCE = 1024
C_CHUNK = 128
C_GROUPS = 8
EPS = 1e-6
PAGE = 128
GROUP = A_HEADS // A_KV
KVW = 2 * A_KV * HEAD_DIM

LANE = 128
VMEM_LIMIT = 56 * 1024 * 1024
KVX = 3 * LANE

NEG = -1e30
KMIN = -2 ** 31

P_QA, P_QB, P_KVA, P_QI, P_KVC, P_KVS, P_KVW, P_MISC = 0, 512, 1024, 1280, 1536, 1792, 2048, 2304
P_WIDTH = 2432
M_KI, M_WI, M_GATE = 0, IDX_DIM, IDX_DIM + IDX_HEADS

assert A_KV == 2 and B_KV == 2 and A_KV * HEAD_DIM == LANE and IDX_DIM == HEAD_DIM


def _cparams(sem):
    return pltpu.CompilerParams(dimension_semantics=sem, vmem_limit_bytes=VMEM_LIMIT)


def _mm(a, b):
    return jnp.dot(a.astype(BF16), b.astype(BF16), preferred_element_type=F32)


def _mm_nt(a, b):
    return lax.dot_general(a.astype(BF16), b.astype(BF16), (((1,), (1,)), ((), ())),
                           preferred_element_type=F32)


def _modulated_norm(x, g, mod_ref):
    y = x * lax.rsqrt(jnp.mean(x * x, axis=-1, keepdims=True) + EPS) * g
    return y * (1.0 + mod_ref[1]) + mod_ref[0]


def _ada_kernel(c_ref, w_ref, b_ref, o_ref):
    c = c_ref[...]
    o_ref[...] = _mm(c * jax.nn.sigmoid(c), w_ref[...]) + b_ref[...]


def _ada_call(c_all, w_ada, b_ada):
    depth, d, n = w_ada.shape
    r = c_all.shape[0]
    tn = n // 8
    return pl.pallas_call(
        _ada_kernel,
        grid=(depth, n // tn),
        in_specs=[pl.BlockSpec((r, d), lambda l, j: (0, 0)),
                  pl.BlockSpec((None, d, tn), lambda l, j: (l, 0, j)),
                  pl.BlockSpec((None, 1, tn), lambda l, j: (l, 0, j))],
        out_specs=pl.BlockSpec((None, r, tn), lambda l, j: (l, 0, j)),
        out_shape=jax.ShapeDtypeStruct((depth, r, n), F32),
        compiler_params=_cparams(("arbitrary", "arbitrary")),
        name="ada",
    )(c_all, w_ada, b_ada.reshape(depth, 1, n))


def _ffn_kernel(x_ref, mod_ref, g_ref, wa_ref, wb_ref, wo_ref, fg_ref, o_ref, h_scr, acc_scr, *,
                n_f, final_norm):
    f = pl.program_id(2)

    @pl.when(f == 0)
    def _():
        h_scr[...] = _modulated_norm(x_ref[...], g_ref[...], mod_ref).astype(BF16)
        acc_scr[...] = jnp.zeros_like(acc_scr)

    h = h_scr[...]
    a = _mm(h, wa_ref[...])
    b = _mm(h, wb_ref[...])
    acc_scr[...] += _mm(a * jax.nn.sigmoid(a) * b, wo_ref[...])

    @pl.when(f == n_f - 1)
    def _():
        y = x_ref[...] + 0.5 * (1.0 + mod_ref[2]) * acc_scr[...]
        if final_norm:
            y = y * lax.rsqrt(jnp.mean(y * y, axis=-1, keepdims=True) + EPS) * fg_ref[...]
        o_ref[...] = y


def _ffn_call(x, mod, g, w_in, w_out, l, j, final_g, tm):
    nb, t, d = x.shape
    dff = w_out.shape[2]
    tf = 256
    n_f = dff // tf
    tmod = mod.shape[2]
    mod_spec = (pl.BlockSpec((None, 3, 1, d), lambda b, i, f: (b, 0, 0, 0)) if tmod == 1 else
                pl.BlockSpec((None, 3, tm, d), lambda b, i, f: (b, 0, i, 0)))
    final_norm = final_g is not None
    fg = (final_g if final_norm else g).reshape(1, d)
    return pl.pallas_call(
        functools.partial(_ffn_kernel, n_f=n_f, final_norm=final_norm),
        grid=(nb, t // tm, n_f),
        in_specs=[pl.BlockSpec((None, tm, d), lambda b, i, f: (b, i, 0)),
                  mod_spec,
                  pl.BlockSpec((1, d), lambda b, i, f: (0, 0)),
                  pl.BlockSpec((None, None, d, tf), lambda b, i, f: (l, j, 0, f)),
                  pl.BlockSpec((None, None, d, tf), lambda b, i, f: (l, j, 0, f + n_f)),
                  pl.BlockSpec((None, None, tf, d), lambda b, i, f: (l, j, f, 0)),
                  pl.BlockSpec((1, d), lambda b, i, f: (0, 0))],
        out_specs=pl.BlockSpec((None, tm, d), lambda b, i, f: (b, i, 0)),
        out_shape=jax.ShapeDtypeStruct(x.shape, F32),
        scratch_shapes=[pltpu.VMEM((tm, d), BF16), pltpu.VMEM((tm, d), F32)],
        compiler_params=_cparams(("arbitrary", "arbitrary", "arbitrary")),
        name="ffn",
    )(x, mod, g.reshape(1, d), w_in, w_in, w_out, fg)


def _kv_ext(y):
    e = jnp.where(lax.broadcasted_iota(I32, (y.shape[0], HEAD_DIM), 1) == 0, 1.0, 0.0).astype(BF16)
    y = y.astype(BF16)
    return jnp.concatenate([y[:, :LANE], y[:, LANE:LANE + HEAD_DIM], e, y[:, LANE + HEAD_DIM:], e], axis=1)


def _kv_ext_t(y):
    e = jnp.where(lax.broadcasted_iota(I32, (HEAD_DIM, y.shape[1]), 0) == 0, 1.0, 0.0).astype(BF16)
    y = y.astype(BF16)
    return jnp.concatenate([y[:LANE], y[LANE:LANE + HEAD_DIM], e, y[LANE + HEAD_DIM:], e], axis=0)


_PROJ_OUT = (("qa", P_QA, 512, "h"), ("qb", P_QB, 512, "h"), ("qi", P_QI, 256, "h"),
             ("kva", P_KVA, 256, "f"), ("kvc", P_KVC, 256, "f"), ("kvs", P_KVS, 256, "f"),
             ("kvw", P_KVW, 256, "f"), ("misc", P_MISC, 128, "f"),
             ("kva_x", P_KVA, KVX, "x"), ("kvs_x", P_KVS, KVX, "x"), ("kvw_x", P_KVW, KVX, "x"),
             ("misc_h", P_MISC, 128, "h"))


def _proj_kernel(x_ref, mod_ref, g_ref, w_ref, *o_refs):
    h = _modulated_norm(x_ref[...], g_ref[...], mod_ref).astype(BF16)
    done = {}
    for (name, off, width, kind), o_ref in zip(_PROJ_OUT, o_refs):
        if off not in done:
            y = _mm(h, w_ref[:, off:off + (KVW if kind == "x" else width)])
            if name in ("qa", "qb"):
                y = y * (HEAD_DIM ** -0.5)
            done[off] = y
        o_ref[...] = _kv_ext(done[off]) if kind == "x" else done[off].astype(o_ref.dtype)


def _proj_call(x, mod, g, wp, tm):
    nb, t, d = x.shape
    tmod = mod.shape[2]
    mod_spec = (pl.BlockSpec((None, 3, 1, d), lambda b, i: (b, 0, 0, 0)) if tmod == 1 else
                pl.BlockSpec((None, 3, tm, d), lambda b, i: (b, 0, i, 0)))
    outs = pl.pallas_call(
        _proj_kernel,
        grid=(nb, t // tm),
        in_specs=[pl.BlockSpec((None, tm, d), lambda b, i: (b, i, 0)),
                  mod_spec,
                  pl.BlockSpec((1, d), lambda b, i: (0, 0)),
                  pl.BlockSpec((d, P_WIDTH), lambda b, i: (0, 0))],
        out_specs=[pl.BlockSpec((None, tm, w), lambda b, i: (b, i, 0)) for _, _, w, _ in _PROJ_OUT],
        out_shape=[jax.ShapeDtypeStruct((nb, t, w), F32 if kind == "f" else BF16)
                   for _, _, w, kind in _PROJ_OUT],
        compiler_params=_cparams(("arbitrary", "arbitrary")),
        name="proj",
    )(x, mod, g.reshape(1, d), wp)
    return {name: o for (name, _, _, _), o in zip(_PROJ_OUT, outs)}


def _rearranged_proj_weight(w):
    d = w.shape[0]
    o = np.cumsum((0, 512, 256, 256, 64, 4, 512, 256, 256, 256, 24))
    seg = lambda i: w[:, o[i]:o[i + 1]]
    qa, kva, qi, ki, wi, qb, kvc, kvs, kvw, gates = [seg(i) for i in range(10)]
    pad = jnp.zeros((d, LANE - IDX_DIM - IDX_HEADS - 3 * B_HEADS), w.dtype)
    return jnp.concatenate([qa, qb, kva, qi, kvc, kvs, kvw, ki, wi, gates, pad], axis=1).astype(BF16)


def _t5_bucket(dist):
    n = jnp.maximum(dist, 0)
    max_exact = N_BUCKETS // 2
    nf = jnp.maximum(n, 1).astype(F32)
    large = max_exact + (jnp.log(nf / max_exact) / math.log(MAX_DISTANCE / max_exact)
                         * (N_BUCKETS - max_exact)).astype(I32)
    large = jnp.minimum(large, N_BUCKETS - 1)
    return jnp.where(n < max_exact, n, large)


def _bias_lookup(tab, dist):
    b = _t5_bucket(jnp.asarray(dist.reshape(-1), I32))
    onehot = (b[:, None] == jnp.arange(N_BUCKETS, dtype=I32)[None, :]).astype(F32)
    out = jnp.dot(onehot, tab, precision=lax.Precision.HIGHEST)
    return jnp.transpose(out).reshape((tab.shape[1],) + dist.shape)


NEAR = -(-(MAX_DISTANCE + LANE - 1) // LANE)


def _bias_strips(tab, tq):
    r = np.arange(tq)[None, :, None]
    c = np.arange(LANE)[None, None, :]
    bd = np.arange(NEAR + 1)[:, None, None]
    dist = np.where(bd == NEAR, MAX_DISTANCE, r - c + LANE * bd)
    return _bias_lookup(tab, dist)


def _cmp_near_count(delta):
    return -(-(-(-(MAX_DISTANCE - delta) // CMP_BLOCK)) // 8) * 8


def _cmp_delta(tq):
    return -(CMP_BLOCK - 1) - CMP_BLOCK * ((tq - CMP_BLOCK) // CMP_BLOCK)


def _cmp_strips(tab, tq):
    delta = _cmp_delta(tq)
    ncn = _cmp_near_count(delta)
    dist = np.arange(tq)[:, None] + delta + CMP_BLOCK * np.arange(ncn)[None, :]
    v = _bias_lookup(tab, dist)
    hi = v.astype(BF16)
    r1 = v - hi.astype(F32)
    mid = r1.astype(BF16)
    lo = (r1 - mid.astype(F32)).astype(BF16)
    return jnp.stack([hi, mid, lo], axis=1)


def _padded_heads(x, g):
    z = jnp.zeros((x.shape[0], HEAD_DIM), x.dtype)
    rows = []
    for r in range(GROUP):
        h = g * GROUP + r
        piece = x[:, h * HEAD_DIM:(h + 1) * HEAD_DIM]
        rows.append(jnp.concatenate([piece, z] if g == 0 else [z, piece], axis=1))
    return jnp.concatenate(rows, axis=0)


def _k_part(kv, kv_t):
    return kv[:LANE] if kv_t else kv[:, :LANE]


def _v_part(kv, g, kv_t):
    return kv[(1 + g) * LANE:(2 + g) * LANE] if kv_t else kv[:, (1 + g) * LANE:(2 + g) * LANE]


def _flash_reset(m_scr, acc_scr):
    m_scr[...] = jnp.full_like(m_scr, NEG)
    acc_scr[...] = jnp.zeros_like(acc_scr)


def _mask_add(mask):
    return jnp.where(mask, 0.0, NEG)


def _flash_update(m_scr, acc_scr, g, qp_g, kv, strips_ref, far_ref, bds, madd, tq, kv_t):
    kb = _k_part(kv, kv_t)
    s_all = _mm(qp_g, kb) if kv_t else _mm_nt(qp_g, kb)
    nblk = madd.shape[1] // LANE
    ps, alphas = [], []
    for r in range(GROUP):
        h = g * GROUP + r
        blocks = []
        for j in range(nblk):
            sj = s_all[r * tq:(r + 1) * tq, j * LANE:(j + 1) * LANE] + madd[:, j * LANE:(j + 1) * LANE]
            blocks.append(sj if bds is None else sj + strips_ref[h, bds[j]])
        bmax = blocks[0]
        for sj in blocks[1:]:
            bmax = jnp.maximum(bmax, sj)
        row_max = jnp.max(bmax, axis=-1, keepdims=True)
        m_old = m_scr[g, r]
        if bds is None:
            m_new = jnp.maximum(m_old, row_max + far_ref[h])
            shift = m_new - far_ref[h]
        else:
            m_new = jnp.maximum(m_old, row_max)
            shift = m_new
        m_scr[g, r] = m_new
        alphas.append(jnp.exp(m_old - m_new))
        ps.append(jnp.concatenate([jnp.exp(sj - shift).astype(BF16) for sj in blocks], axis=1))
    p = jnp.concatenate(ps, axis=0)
    vx = _v_part(kv, g, kv_t)
    pv = _mm_nt(p, vx) if kv_t else _mm(p, vx)
    acc_scr[g] = jnp.concatenate(alphas, axis=0) * acc_scr[g] + pv


def _flash_result(acc_scr, g):
    acc = acc_scr[g]
    l = acc[:, HEAD_DIM:HEAD_DIM + 1]
    return jnp.where(l > 0.0, acc[:, :HEAD_DIM] / jnp.where(l > 0.0, l, 1.0), 0.0)


def _tile_bds(qb, k, tk):
    nblk = tk // LANE
    return [jnp.clip(qb - (k * nblk + j), 0, NEAR) for j in range(nblk)]


def _ordered_key(x):
    b = lax.bitcast_convert_type(x, I32)
    return jnp.where(x == 0.0, 0, b ^ ((b >> 31) & 0x7FFFFFFF))


def _far_tiles(qb, tk):
    n = (qb - NEAR + 1) // (tk // LANE)
    return max(n, 0) if isinstance(n, int) else jnp.maximum(n, 0)


def _dsa_core(qi, wi, qa, get_kidx, get_kva, strips_ref, far_ref, t0, tq, tk, topk, kv_t, unroll,
              keys_scr, wib_scr, m_scr, acc_scr, tri_scr):
    n_tiles = (t0 + tq - 1) // tk + 1
    qb = t0 // LANE
    n_far = _far_tiles(qb, tk)
    row_pos = t0 + lax.broadcasted_iota(I32, (tq, tk), 0)
    col = lax.broadcasted_iota(I32, (tq, tk), 1)

    if kv_t:
        qi_st = jnp.concatenate([qi[:, h * IDX_DIM:(h + 1) * IDX_DIM] for h in range(IDX_HEADS)], axis=0)
    else:
        z = jnp.zeros((tq, LANE - IDX_DIM), qi.dtype)
        qi_st = jnp.concatenate(
            [jnp.concatenate([qi[:, h * IDX_DIM:(h + 1) * IDX_DIM], z], axis=1) for h in range(IDX_HEADS)], axis=0)
    for h in range(IDX_HEADS):
        wib_scr[h] = jnp.broadcast_to(wi[:, h:h + 1], (tq, tk))

    def score_tile(k, c):
        kt = get_kidx(k)
        sc = _mm(qi_st, kt) if kv_t else _mm_nt(qi_st, kt)
        score = jnp.maximum(sc[:tq], 0.0) * wib_scr[0]
        for h in range(1, IDX_HEADS):
            score = score + jnp.maximum(sc[h * tq:(h + 1) * tq], 0.0) * wib_scr[h]
        keys_scr[k] = jnp.where(k * tk + col <= row_pos, _ordered_key(score), KMIN)
        return c

    lax.fori_loop(0, n_tiles, score_tile, 0, unroll=unroll)

    def count(pred):
        def count_tile(k, cnt):
            key = keys_scr[k]
            for j in range(tk // LANE):
                cnt = cnt + jnp.where(pred(key[:, j * LANE:(j + 1) * LANE]), 1.0, 0.0)
            return cnt

        cnt = lax.fori_loop(0, n_tiles, count_tile, jnp.zeros((tq, LANE), F32))
        return jnp.sum(cnt, axis=-1, keepdims=True)

    tu = jnp.zeros((tq, 1), I32)
    n_ge = jnp.zeros((tq, 1), F32)
    for bit in range(31, -1, -1):
        cand_u = tu | np.int32(-2 ** 31 if bit == 31 else 2 ** bit)
        cand_s = jnp.broadcast_to(cand_u ^ np.int32(KMIN), (tq, LANE))
        tot = count(lambda key, cand_s=cand_s: key >= cand_s)
        tu = jnp.where(tot >= topk, cand_u, tu)
        n_ge = jnp.where(tot >= topk, tot, n_ge)
    thr = tu ^ np.int32(KMIN)
    found = thr > np.int32(KMIN)
    has_ties = jnp.max(jnp.where(found & (n_ge > topk), 1.0, 0.0)) > 0.5

    qp_g = [_padded_heads(qa, g) for g in range(A_KV)]
    _flash_reset(m_scr, acc_scr)

    def attend(k, mask, far):
        kv = get_kva(k)
        bds = None if far else _tile_bds(qb, k, tk)
        madd = _mask_add(mask)
        for g in range(A_KV):
            _flash_update(m_scr, acc_scr, g, qp_g[g], kv, strips_ref, far_ref, bds, madd, tq, kv_t)

    @pl.when(jnp.logical_not(has_ties))
    def _():
        thr_ge = jnp.where(found, thr, np.int32(KMIN + 1))

        def tile(k, c, far):
            attend(k, keys_scr[k] >= thr_ge, far)
            return c

        lax.fori_loop(0, n_far, functools.partial(tile, far=True), 0, unroll=unroll)
        lax.fori_loop(n_far, n_tiles, functools.partial(tile, far=False), 0, unroll=unroll)

    @pl.when(has_ties)
    def _():
        need = topk - count(lambda key: key > jnp.broadcast_to(thr, (tq, LANE)))
        tri_scr[...] = jnp.where(
            lax.broadcasted_iota(I32, (tk, tk), 0) <= lax.broadcasted_iota(I32, (tk, tk), 1), 1.0, 0.0).astype(BF16)

        def tile(k, eq_seen):
            key = keys_scr[k]
            eq = key == thr
            incl = _mm(jnp.where(eq, 1.0, 0.0), tri_scr[...])
            attend(k, (key > thr) | (eq & found & (eq_seen + incl <= need)), False)
            return eq_seen + incl[:, tk - 1:tk]

        lax.fori_loop(0, n_tiles, tile, jnp.zeros((tq, 1), F32))

    outs = []
    for g in range(A_KV):
        o = _flash_result(acc_scr, g)
        outs += [o[r * tq:(r + 1) * tq] for r in range(GROUP)]
    return jnp.concatenate(outs, axis=1)


def _pick_blocks(v, n_pick):
    lane = lax.broadcasted_iota(I32, v.shape, 1).astype(F32)
    sel = jnp.zeros(v.shape, jnp.bool_)
    for _ in range(n_pick):
        mx = jnp.max(v, axis=-1, keepdims=True)
        cand = (v == mx) & jnp.logical_not(sel)
        first = jnp.min(jnp.where(cand, lane, float(v.shape[1])), axis=-1, keepdims=True)
        pick = lane == first
        sel = sel | pick
        v = jnp.where(pick, -jnp.inf, v)
    return sel


def _nsa_core(qb, gates, kcp, get_kvs, win_tiles, strips_ref, cstrips_ref, far_ref, t0, tq, tk, nj,
              n_pick, extra_block, kv_t, unroll, m_scr, acc_scr):
    n_tiles = (t0 + tq - 1) // tk + 1
    qblk = t0 // LANE
    delta = _cmp_delta(tq)
    ncn = cstrips_ref.shape[-1]
    qp_g = [_padded_heads(qb, g) for g in range(B_KV)]
    row_pos1 = t0 + lax.broadcasted_iota(I32, (tq, 1), 0)

    lam = lax.broadcasted_iota(I32, (tq, 2 * nj), 1)
    m_of = 2 * (lam % nj) + lam // nj
    cdist = row_pos1 - (CMP_BLOCK * m_of + CMP_BLOCK - 1)
    cmask = cdist >= 0
    m_hi = (t0 - (CMP_BLOCK - 1) - delta) // CMP_BLOCK
    lam_s = lax.broadcasted_iota(I32, (ncn, 2 * nj), 1)
    shift = jnp.where(2 * (lam_s % nj) + lam_s // nj == m_hi - lax.broadcasted_iota(I32, (ncn, 2 * nj), 0),
                      1.0, 0.0).astype(BF16)
    is_far = m_of <= m_hi - ncn
    kc_both = kcp[:, :LANE]
    o_c, imp = [], []
    for g in range(B_KV):
        bias = []
        for r in range(GROUP):
            h = g * GROUP + r
            b = (_mm(cstrips_ref[h, 0], shift) + _mm(cstrips_ref[h, 1], shift)) + _mm(cstrips_ref[h, 2], shift)
            bias.append(jnp.where(is_far, far_ref[h], b))
        s = _mm_nt(qp_g[g], kc_both).reshape(GROUP, tq, 2 * nj) + jnp.stack(bias)
        s = jnp.where(cmask[None], s, NEG)
        e = jnp.where(cmask[None], jnp.exp(s - jnp.max(s, axis=-1, keepdims=True)), 0.0)
        l = jnp.sum(e, axis=-1, keepdims=True)
        p = jnp.where(l > 0.0, e / jnp.where(l > 0.0, l, 1.0), 0.0)
        o_c.append(_mm(p.reshape(GROUP * tq, 2 * nj), kcp[:, (B_KV + g) * HEAD_DIM:(B_KV + g + 1) * HEAD_DIM]))
        ps = p[0]
        for r in range(1, GROUP):
            ps = ps + p[r]
        imp.append(ps[:, :nj] + ps[:, nj:])

    jl = lax.broadcasted_iota(I32, (tq, nj), 1)
    cur = row_pos1 // SEL_BLOCK
    forced = (jl == 0) | (jl == cur) | (jl == cur - 1)
    admissible = jl <= cur
    bmask = []
    for g in range(B_KV):
        v = jnp.where(admissible, jnp.where(forced, jnp.inf, imp[g]), -jnp.inf)
        bmask.append(jnp.where(_pick_blocks(v, n_pick) & admissible, 1.0, 0.0).astype(BF16))
    row_pos = t0 + lax.broadcasted_iota(I32, (tq, tk), 0)
    col = lax.broadcasted_iota(I32, (tq, tk), 1)
    ej = lax.broadcasted_iota(I32, (nj, tk), 0)
    ec = lax.broadcasted_iota(I32, (nj, tk), 1)
    _flash_reset(m_scr, acc_scr)

    def sel_tile(k, c, far):
        kv = get_kvs(k)
        expand = jnp.where(ej == (k * tk + ec) // SEL_BLOCK, 1.0, 0.0).astype(BF16)
        for g in range(B_KV):
            mask = _mm(bmask[g], expand) > 0.5
            if far:
                bds = None
            else:
                bds = _tile_bds(qblk, k, tk)
                key_pos = k * tk + col
                if extra_block:
                    mask = mask | (key_pos >= nj * SEL_BLOCK)
                mask = mask & (key_pos <= row_pos)
            _flash_update(m_scr, acc_scr, g, qp_g[g], kv, strips_ref, far_ref, bds, _mask_add(mask), tq, kv_t)
        return c

    n_far = _far_tiles(qblk, tk)
    lax.fori_loop(0, n_far, functools.partial(sel_tile, far=True), 0, unroll=unroll)
    lax.fori_loop(n_far, n_tiles, functools.partial(sel_tile, far=False), 0, unroll=unroll)
    o_s = [_flash_result(acc_scr, g) for g in range(B_KV)]

    _flash_reset(m_scr, acc_scr)
    for get_tile, off, guard in win_tiles:
        def win_tile(get_tile=get_tile, off=off):
            kv = get_tile()
            wt = kv.shape[1] if kv_t else kv.shape[0]
            dist = (lax.broadcasted_iota(I32, (tq, wt), 0) - lax.broadcasted_iota(I32, (tq, wt), 1)
                    - LANE * off)
            wadd = _mask_add((dist >= 0) & (dist <= WINDOW))
            bds = [min(max(-off - j, 0), NEAR) for j in range(wt // LANE)]
            for g in range(B_KV):
                _flash_update(m_scr, acc_scr, g, qp_g[g], kv, strips_ref, far_ref, bds, wadd, tq, kv_t)

        if guard is None:
            win_tile()
        else:
            pl.when(guard)(win_tile)
    o_w = [_flash_result(acc_scr, g) for g in range(B_KV)]

    gs = jax.nn.sigmoid(gates)
    outs = []
    for g in range(B_KV):
        gcol = lambda j: jnp.concatenate(
            [gs[:, (g * GROUP + r) * 3 + j:(g * GROUP + r) * 3 + j + 1] for r in range(GROUP)], axis=0)
        o = gcol(0) * o_c[g] + gcol(1) * o_s[g] + gcol(2) * o_w[g]
        outs += [o[r * tq:(r + 1) * tq] for r in range(GROUP)]
    return jnp.concatenate(outs, axis=1)


def _compress(load_rows, n_pairs, pe_ref, wbd_ref):
    acc = [jnp.zeros((n_pairs, KVW), F32), jnp.zeros((n_pairs, KVW), F32)]
    for c in range(2 * CMP_BLOCK):
        ci = c % CMP_BLOCK
        acc[c // CMP_BLOCK] = acc[c // CMP_BLOCK] + _mm(load_rows(c) + pe_ref[ci], wbd_ref[ci])
    return jnp.concatenate(acc, axis=0)


def _compress_operands(pe, w):
    pe_rows = jnp.transpose(jnp.broadcast_to(pe[:, None], (2, B_KV, CMP_BLOCK, HEAD_DIM)), (2, 0, 1, 3))
    pe_rows = pe_rows.reshape(CMP_BLOCK, 1, KVW)
    eye = jnp.eye(2 * B_KV, dtype=w.dtype).reshape(2, B_KV, 2, B_KV)
    wbd = jnp.einsum("icde,igjh->cigdjhe", w, eye).reshape(CMP_BLOCK, KVW, KVW)
    return pe_rows, wbd.astype(BF16)


def _strided_rows(lo_ref, hi_ref, c, n):
    rows = pl.ds(c, n, stride=2 * CMP_BLOCK)
    return jnp.concatenate([lo_ref[rows, :], hi_ref[rows, :]], axis=1)


def _flash_scratch(tq):
    return [pltpu.VMEM((A_KV, GROUP, tq, LANE), F32), pltpu.VMEM((A_KV, GROUP * tq, LANE), F32)]


TQ_P = 128
TK_P = 512


def _prompt_tile(ref, k, tk):
    return ref[pl.ds(pl.multiple_of(k * tk, tk), tk), :]


def _pdsa_kernel(qi_ref, misc_ref, qa_ref, kidx_ref, kva_ref, strips_ref, far_ref, o_ref,
                 keys_scr, wib_scr, m_scr, acc_scr, tri_scr, *, topk, tk):
    i = pl.program_id(1)
    o = _dsa_core(qi_ref[...], misc_ref[:, M_WI:M_WI + IDX_HEADS], qa_ref[...],
                  lambda k: _prompt_tile(kidx_ref, k, tk), lambda k: _prompt_tile(kva_ref, k, tk),
                  strips_ref, far_ref, i * TQ_P, TQ_P, tk, topk, False, None,
                  keys_scr, wib_scr, m_scr, acc_scr, tri_scr)
    o_ref[...] = o.astype(o_ref.dtype)


def _pdsa_call(pr, strips, far, tk):
    nb, s, _ = pr["qa"].shape
    tq = TQ_P
    row = lambda w: pl.BlockSpec((None, tq, w), lambda b, i: (b, i, 0))
    full = lambda w: pl.BlockSpec((None, s, w), lambda b, i: (b, 0, 0))
    return pl.pallas_call(
        functools.partial(_pdsa_kernel, topk=min(IDX_TOPK, s // 4), tk=tk),
        grid=(nb, s // tq),
        in_specs=[row(256), row(128), row(512), full(128), full(KVX),
                  pl.BlockSpec(strips.shape, lambda b, i: (0, 0, 0, 0)),
                  pl.BlockSpec(memory_space=pltpu.SMEM)],
        out_specs=row(512),
        out_shape=jax.ShapeDtypeStruct((nb, s, A_HEADS * HEAD_DIM), BF16),
        scratch_shapes=[pltpu.VMEM((s // tk, tq, tk), I32), pltpu.VMEM((IDX_HEADS, tq, tk), F32)]
        + _flash_scratch(tq) + [pltpu.VMEM((tk, tk), BF16)],
        compiler_params=_cparams(("arbitrary", "arbitrary")),
        name="dsa_prompt",
    )(pr["qi"], pr["misc"], pr["qa"], pr["misc_h"], pr["kva_x"], strips, far)


def _pcmp_kernel(lo_ref, hi_ref, pe_ref, wbd_ref, o_ref, *, n_pairs):
    o_ref[...] = _compress(lambda c: _strided_rows(lo_ref, hi_ref, c, n_pairs), n_pairs, pe_ref, wbd_ref)


def _pcmp_call(kvc, pe_rows, wbd):
    nb, s, _ = kvc.shape
    n_pairs = s // (2 * CMP_BLOCK)
    return pl.pallas_call(
        functools.partial(_pcmp_kernel, n_pairs=n_pairs),
        grid=(nb,),
        in_specs=[pl.BlockSpec((None, s, LANE), lambda b: (b, 0, 0)),
                  pl.BlockSpec((None, s, LANE), lambda b: (b, 0, 1)),
                  pl.BlockSpec((CMP_BLOCK, 1, KVW), lambda b: (0, 0, 0)),
                  pl.BlockSpec((CMP_BLOCK, KVW, KVW), lambda b: (0, 0, 0))],
        out_specs=pl.BlockSpec((None, 2 * n_pairs, KVW), lambda b: (b, 0, 0)),
        out_shape=jax.ShapeDtypeStruct((nb, 2 * n_pairs, KVW), F32),
        compiler_params=_cparams(("arbitrary",)),
        name="compress_prompt",
    )(kvc, kvc, pe_rows, wbd)


def _pnsa_kernel(qb_ref, misc_ref, kcp_ref, kvs_ref, kvw_ref, strips_ref, cstrips_ref, far_ref, o_ref,
                 m_scr, acc_scr, *, nj, n_pick, tk):
    i = pl.program_id(1)
    nw = WINDOW // LANE
    win_tiles = [(lambda: kvw_ref[pl.ds(pl.multiple_of((i - nw) * LANE, LANE), WINDOW + TQ_P), :], -nw, i >= nw)]
    win_tiles += [(lambda wd=wd: _prompt_tile(kvw_ref, i - wd, LANE), -wd, (i < nw) & (i - wd >= 0))
                  for wd in range(nw - 1, -1, -1)]
    o = _nsa_core(qb_ref[...], misc_ref[:, M_GATE:M_GATE + 3 * B_HEADS], kcp_ref[...],
                  lambda k: _prompt_tile(kvs_ref, k, tk), win_tiles,
                  strips_ref, cstrips_ref, far_ref, i * TQ_P, TQ_P, tk, nj, n_pick, False, False, None,
                  m_scr, acc_scr)
    o_ref[...] = o.astype(o_ref.dtype)


def _pnsa_call(pr, kcp, strips, cstrips, far, tk):
    nb, s, _ = pr["qb"].shape
    tq = TQ_P
    nj = s // SEL_BLOCK
    row = lambda w: pl.BlockSpec((None, tq, w), lambda b, i: (b, i, 0))
    full = lambda n, w: pl.BlockSpec((None, n, w), lambda b, i: (b, 0, 0))
    const = lambda a: pl.BlockSpec(a.shape, lambda b, i: (0,) * a.ndim)
    return pl.pallas_call(
        functools.partial(_pnsa_kernel, nj=nj, n_pick=min(SEL_TOPN, nj), tk=tk),
        grid=(nb, s // tq),
        in_specs=[row(512), row(128), full(2 * nj, KVW), full(s, KVX), full(s, KVX),
                  const(strips), const(cstrips),
                  pl.BlockSpec(memory_space=pltpu.SMEM)],
        out_specs=row(512),
        out_shape=jax.ShapeDtypeStruct((nb, s, B_HEADS * HEAD_DIM), BF16),
        scratch_shapes=_flash_scratch(tq),
        compiler_params=_cparams(("arbitrary", "arbitrary")),
        name="nsa_prompt",
    )(pr["qb"], pr["misc"], kcp, pr["kvs_x"], pr["kvw_x"], strips, cstrips, far)


TQ_S = 8
TK_S = 512
PPT = TK_S // PAGE


def _pages_per_step(n_pages):
    return max(g for g in (8, 4, 2, 1) if n_pages % g == 0)


def _new_token_block(col):
    return jnp.where(lax.broadcasted_iota(I32, (col.shape[0], PAGE), 1) == 0, col, 0.0)


def _slot_tile(buf, k):
    blk = buf[pl.ds(k * PPT, PPT)]
    return jnp.concatenate([blk[j] for j in range(PPT)], axis=1)


def _sdsa_kernel(*refs, n_pages, pps, topk):
    idx_refs = refs[1:1 + pps]
    akv_refs = refs[1 + pps:1 + 2 * pps]
    (qi_ref, wi_ref, qa_ref, new_idx_ref, new_kv_ref, strips_ref, far_ref, o_ref,
     idx_buf, akv_buf, keys_scr, wib_scr, m_scr, acc_scr, tri_scr) = refs[1 + 2 * pps:]
    p = pl.program_id(1)
    for j in range(pps):
        idx_buf[p * pps + j] = idx_refs[j][...].astype(BF16)
        akv_buf[p * pps + j] = _kv_ext_t(akv_refs[j][...])

    @pl.when(p == n_pages // pps - 1)
    def _():
        idx_buf[n_pages] = _new_token_block(new_idx_ref[...]).astype(BF16)
        akv_buf[n_pages] = _kv_ext_t(_new_token_block(new_kv_ref[...]))
        for j in range(1, PPT):
            idx_buf[n_pages + j] = jnp.zeros((IDX_DIM, PAGE), BF16)
            akv_buf[n_pages + j] = _kv_ext_t(jnp.zeros((KVW, PAGE), BF16))
        o = _dsa_core(qi_ref[...], wi_ref[...], qa_ref[...],
                      lambda k: _slot_tile(idx_buf, k), lambda k: _slot_tile(akv_buf, k),
                      strips_ref, far_ref, n_pages * PAGE, TQ_S, TK_S, topk, True, True,
                      keys_scr, wib_scr, m_scr, acc_scr, tri_scr)
        o_ref[...] = o.astype(o_ref.dtype)


def _sample_rows(x):
    return jnp.broadcast_to(x[:, None, :], (x.shape[0], TQ_S, x.shape[1]))


def _pages_t(cache):
    n, p = cache.shape[:2]
    nd = cache.ndim
    return jnp.transpose(cache, (0, 1) + tuple(range(3, nd)) + (2,)).reshape(n, p, -1, cache.shape[2])


def _sdsa_call(a, page_table, idx_t, akv_t, pr, strips, far):
    bd, n_pages = page_table.shape
    assert idx_t.shape[3] == PAGE and (n_pages * PAGE) % TK_S == 0
    pps = _pages_per_step(n_pages)
    slots = n_pages + PPT
    misc = pr["misc"][0]
    rows = lambda w: pl.BlockSpec((None, TQ_S, w), lambda b, p, pt: (b, 0, 0))
    col = lambda w: pl.BlockSpec((None, w, 1), lambda b, p, pt: (b, 0, 0))
    page = lambda w, j: pl.BlockSpec((None, None, w, PAGE),
                                     lambda b, p, pt: (a, pt[b * n_pages + p * pps + j], 0, 0))
    grid_spec = pltpu.PrefetchScalarGridSpec(
        num_scalar_prefetch=1,
        grid=(bd, n_pages // pps),
        in_specs=[page(IDX_DIM, j) for j in range(pps)] + [page(KVW, j) for j in range(pps)]
        + [rows(256), rows(IDX_HEADS), rows(512), col(IDX_DIM), col(KVW),
           pl.BlockSpec(strips.shape, lambda b, p, pt: (0, 0, 0, 0)),
           pl.BlockSpec(memory_space=pltpu.SMEM)],
        out_specs=rows(512),
        scratch_shapes=[pltpu.VMEM((slots, IDX_DIM, PAGE), BF16), pltpu.VMEM((slots, KVX, PAGE), BF16),
                        pltpu.VMEM((slots // PPT, TQ_S, TK_S), I32), pltpu.VMEM((IDX_HEADS, TQ_S, TK_S), F32)]
        + _flash_scratch(TQ_S) + [pltpu.VMEM((TK_S, TK_S), BF16)])
    out = pl.pallas_call(
        functools.partial(_sdsa_kernel, n_pages=n_pages, pps=pps, topk=min(IDX_TOPK, (n_pages * PAGE + 1) // 4)),
        grid_spec=grid_spec,
        out_shape=jax.ShapeDtypeStruct((bd, TQ_S, A_HEADS * HEAD_DIM), BF16),
        compiler_params=_cparams(("arbitrary", "arbitrary")),
        name="dsa_sample",
    )(page_table.reshape(-1), *([idx_t] * pps), *([akv_t] * pps),
      _sample_rows(pr["qi"][0]), _sample_rows(misc[:, M_WI:M_WI + IDX_HEADS]), _sample_rows(pr["qa"][0]),
      misc[:, M_KI:M_KI + IDX_DIM, None], pr["kva"][0][:, :, None], strips, far)
    return out[:, 0]


def _snsa_kernel(*refs, n_pages, pps):
    cmp_refs = refs[1:1 + pps]
    sel_refs = refs[1 + pps:1 + 2 * pps]
    (qb_ref, gates_ref, new_sel_ref, win_ref, new_win_ref, pe_ref, wbd_ref, strips_ref, cstrips_ref, far_ref,
     o_ref, cmp_lo, cmp_hi, sel_buf, m_scr, acc_scr) = refs[1 + 2 * pps:]
    p = pl.program_id(1)
    past = n_pages * PAGE
    for j in range(pps):
        rows = pl.ds(pl.multiple_of((p * pps + j) * PAGE, PAGE), PAGE)
        cmp_lo[rows, :] = jnp.transpose(cmp_refs[j][:LANE])
        cmp_hi[rows, :] = jnp.transpose(cmp_refs[j][LANE:])
        sel_buf[p * pps + j] = _kv_ext_t(sel_refs[j][...])

    @pl.when(p == n_pages // pps - 1)
    def _():
        sel_buf[n_pages] = _kv_ext_t(_new_token_block(new_sel_ref[...]))
        for j in range(1, PPT):
            sel_buf[n_pages + j] = _kv_ext_t(jnp.zeros((KVW, PAGE), BF16))
        nj = past // SEL_BLOCK
        kcp = _compress(lambda c: _strided_rows(cmp_lo, cmp_hi, c, nj), nj, pe_ref, wbd_ref)
        new_win = jnp.concatenate([_new_token_block(new_win_ref[...])]
                                  + [jnp.zeros((KVW, PAGE), F32)] * (PPT - 1), axis=1)
        win_tiles = [(lambda: _kv_ext_t(win_ref[...]), -(WINDOW // LANE), None),
                     (lambda: _kv_ext_t(new_win), 0, None)]
        o = _nsa_core(qb_ref[...], gates_ref[...], kcp, lambda k: _slot_tile(sel_buf, k), win_tiles,
                      strips_ref, cstrips_ref, far_ref, past, TQ_S, TK_S, nj,
                      min(SEL_TOPN, nj + 1) - 1, True, True, True, m_scr, acc_scr)
        o_ref[...] = o.astype(o_ref.dtype)


def _snsa_call(a, page_table, cmp_t, sel_t, win_t, pr, pe_rows, wbd, strips, cstrips, far):
    bd, n_pages = page_table.shape
    past = n_pages * PAGE
    assert past % TK_S == 0 and win_t.shape[3] == WINDOW == TK_S
    pps = _pages_per_step(n_pages)
    misc = pr["misc"][0]
    rows = lambda w: pl.BlockSpec((None, TQ_S, w), lambda b, p, pt: (b, 0, 0))
    col = lambda w: pl.BlockSpec((None, w, 1), lambda b, p, pt: (b, 0, 0))
    page = lambda j: pl.BlockSpec((None, None, KVW, PAGE),
                                  lambda b, p, pt: (a, pt[b * n_pages + p * pps + j], 0, 0))
    const = lambda x: pl.BlockSpec(x.shape, lambda b, p, pt: (0,) * x.ndim)
    grid_spec = pltpu.PrefetchScalarGridSpec(
        num_scalar_prefetch=1,
        grid=(bd, n_pages // pps),
        in_specs=[page(j) for j in range(pps)] + [page(j) for j in range(pps)]
        + [rows(512), rows(3 * B_HEADS), col(KVW),
           pl.BlockSpec((None, None, KVW, WINDOW), lambda b, p, pt: (a, b, 0, 0)), col(KVW),
           const(pe_rows), const(wbd), const(strips), const(cstrips),
           pl.BlockSpec(memory_space=pltpu.SMEM)],
        out_specs=rows(512),
        scratch_shapes=[pltpu.VMEM((past, LANE), F32), pltpu.VMEM((past, LANE), F32),
                        pltpu.VMEM((n_pages + PPT, KVX, PAGE), BF16)] + _flash_scratch(TQ_S))
    out = pl.pallas_call(
        functools.partial(_snsa_kernel, n_pages=n_pages, pps=pps),
        grid_spec=grid_spec,
        out_shape=jax.ShapeDtypeStruct((bd, TQ_S, B_HEADS * HEAD_DIM), BF16),
        compiler_params=_cparams(("arbitrary", "arbitrary")),
        name="nsa_sample",
    )(page_table.reshape(-1), *([cmp_t] * pps), *([sel_t] * pps),
      _sample_rows(pr["qb"][0]), _sample_rows(misc[:, M_GATE:M_GATE + 3 * B_HEADS]),
      pr["kvs"][0][:, :, None], win_t, pr["kvw"][0][:, :, None],
      pe_rows, wbd, strips, cstrips, far)
    return out[:, 0]


def _merge_kernel(x_ref, mod_ref, oa_ref, ob_ref, w_ref, o_ref):
    n = oa_ref.shape[-1]
    y = _mm(oa_ref[...], w_ref[:n]) + _mm(ob_ref[...], w_ref[n:])
    o_ref[...] = x_ref[...] + (1.0 + mod_ref[2]) * y


def _merge_call(x, mod, oa, ob, w, tm):
    nb, t, d = x.shape
    tmod = mod.shape[2]
    mod_spec = (pl.BlockSpec((None, 3, 1, d), lambda b, i: (b, 0, 0, 0)) if tmod == 1 else
                pl.BlockSpec((None, 3, tm, d), lambda b, i: (b, 0, i, 0)))
    row = lambda w_: pl.BlockSpec((None, tm, w_), lambda b, i: (b, i, 0))
    return pl.pallas_call(
        _merge_kernel,
        grid=(nb, t // tm),
        in_specs=[row(d), mod_spec, row(oa.shape[-1]), row(ob.shape[-1]),
                  pl.BlockSpec(w.shape, lambda b, i: (0, 0))],
        out_specs=row(d),
        out_shape=jax.ShapeDtypeStruct(x.shape, F32),
        compiler_params=_cparams(("arbitrary", "arbitrary")),
        name="merge",
    )(x, mod, oa, ob, w)


def _gelu_ln(h, w_in_ref, ln_ref, cw):
    uv = jax.nn.gelu(_mm(h, w_in_ref[...]))
    u, v = uv[:, :cw], uv[:, cw:]
    mu = jnp.mean(v, axis=-1, keepdims=True)
    var = jnp.mean(jnp.square(v - mu), axis=-1, keepdims=True)
    return u, (v - mu) * lax.rsqrt(var + EPS) * ln_ref[...]


def _pgmlp_kernel(x_ref, mod_ref, g_ref, w_in_ref, ln_ref, ws_ref, bs_ref, w_out_ref, o_ref, *, cw):
    x = x_ref[...]
    h = _modulated_norm(x, g_ref[...], mod_ref).astype(BF16)
    u, v = _gelu_ln(h, w_in_ref, ln_ref, cw)
    gw = cw // C_GROUPS
    rows = []
    for n in range(x.shape[0] // C_CHUNK):
        vb = v[n * C_CHUNK:(n + 1) * C_CHUNK].astype(BF16)
        sv = [_mm(ws_ref[g], vb[:, g * gw:(g + 1) * gw]) + bs_ref[:, g:g + 1] for g in range(C_GROUPS)]
        rows.append(jnp.concatenate(sv, axis=1))
    sv = rows[0] if len(rows) == 1 else jnp.concatenate(rows, axis=0)
    o_ref[...] = x + (1.0 + mod_ref[2]) * _mm(u * sv, w_out_ref[...])


def _pgmlp_call(x, mod, g, w_in, ln_g, ws, bs_t, w_out, tm):
    nb, t, d = x.shape
    cw = w_out.shape[0]
    const = lambda a: pl.BlockSpec(a.shape, lambda b, i: (0,) * a.ndim)
    return pl.pallas_call(
        functools.partial(_pgmlp_kernel, cw=cw),
        grid=(nb, t // tm),
        in_specs=[pl.BlockSpec((None, tm, d), lambda b, i: (b, i, 0)),
                  pl.BlockSpec((None, 3, 1, d), lambda b, i: (b, 0, 0, 0)),
                  pl.BlockSpec((1, d), lambda b, i: (0, 0)),
                  const(w_in), pl.BlockSpec((1, cw), lambda b, i: (0, 0)), const(ws), const(bs_t), const(w_out)],
        out_specs=pl.BlockSpec((None, tm, d), lambda b, i: (b, i, 0)),
        out_shape=jax.ShapeDtypeStruct(x.shape, F32),
        compiler_params=_cparams(("arbitrary", "arbitrary")),
        name="gmlp_prompt",
    )(x, mod, g.reshape(1, d), w_in, ln_g.reshape(1, cw), ws, bs_t, w_out)


def _sgmlp_kernel(x_ref, mod_ref, g_ref, w_in_ref, ln_ref, ws0_ref, bs0_ref, w_out_ref, o_ref, v_ref, *, cw):
    x = x_ref[...]
    h = _modulated_norm(x, g_ref[...], mod_ref).astype(BF16)
    u, v = _gelu_ln(h, w_in_ref, ln_ref, cw)
    v_ref[...] = v
    sv = v.astype(BF16).astype(F32) * ws0_ref[...].astype(F32) + bs0_ref[...]
    o_ref[...] = x + (1.0 + mod_ref[2]) * _mm(u * sv, w_out_ref[...])


def _sgmlp_call(x, mod, g, w_in, ln_g, ws, b_sp, w_out):
    nb, t, d = x.shape
    cw = w_out.shape[0]
    gw = cw // C_GROUPS
    ws0 = jnp.repeat(ws[:, 0, 0], gw).reshape(1, cw)
    bs0 = jnp.repeat(b_sp[:, 0], gw).reshape(1, cw)
    full = lambda a: pl.BlockSpec(a.shape, lambda: (0,) * a.ndim)
    x2, mod2 = x[0], mod[0]
    o, v = pl.pallas_call(
        functools.partial(_sgmlp_kernel, cw=cw),
        in_specs=[full(x2), full(mod2), pl.BlockSpec((1, d), lambda: (0, 0)), full(w_in),
                  pl.BlockSpec((1, cw), lambda: (0, 0)), full(ws0), full(bs0), full(w_out)],
        out_specs=[full(x2), pl.BlockSpec((t, cw), lambda: (0, 0))],
        out_shape=[jax.ShapeDtypeStruct(x2.shape, F32), jax.ShapeDtypeStruct((t, cw), F32)],
        compiler_params=pltpu.CompilerParams(vmem_limit_bytes=VMEM_LIMIT),
        name="gmlp_sample",
    )(x2, mod2, g.reshape(1, d), w_in, ln_g.reshape(1, cw), ws0, bs0, w_out)
    return o[None], v


def kernel(x_prompt, x_sample, cache_a_kv, cache_a_idx, cache_b_cmp_kv, cache_b_sel_kv, state_b_win_kv,
           page_table, c_prompt, c_sample, rel_bias, w_ada, b_ada, norm_g, w_ffn_in, w_ffn_out,
           w_in_att, w_out_att, cmp_pe, w_cmp, w_in_c, ln_c_g, w_sp, b_sp, w_out_c, final_g):
    nbp, s, d = x_prompt.shape
    nbs, tdec, _ = x_sample.shape
    depth = w_ada.shape[0]
    assert tdec == 1 and s % TQ_P == 0 and cache_a_kv.shape[2] == PAGE

    ada = _ada_call(jnp.concatenate([c_prompt, c_sample], axis=0), w_ada, b_ada)
    ada = ada.reshape(depth, nbp + nbs, 3, 3, d)
    mod_p = lambda l, j: ada[l, :nbp, j][:, :, None, :]
    mod_s = lambda l, j: jnp.transpose(ada[l, nbp:, j], (1, 0, 2))[None]

    w_ffn_in_h = w_ffn_in.astype(BF16)
    w_ffn_out_h = w_ffn_out.astype(BF16)
    tab_a, tab_b = rel_bias[:, :A_HEADS], rel_bias[:, A_HEADS:]
    strips_pa, strips_pb = _bias_strips(tab_a, TQ_P), _bias_strips(tab_b, TQ_P)
    strips_sa, strips_sb = _bias_strips(tab_a, TQ_S), _bias_strips(tab_b, TQ_S)
    cstrips_p, cstrips_s = _cmp_strips(tab_b, TQ_P), _cmp_strips(tab_b, TQ_S)
    far_a, far_b = tab_a[N_BUCKETS - 1], tab_b[N_BUCKETS - 1]
    idx_t, akv_t = _pages_t(cache_a_idx), _pages_t(cache_a_kv)
    cmp_t, sel_t, win_t = _pages_t(cache_b_cmp_kv), _pages_t(cache_b_sel_kv), _pages_t(state_b_win_kv)

    xp = x_prompt
    xs = jnp.transpose(x_sample, (1, 0, 2))
    tm_p = 512 if s % 512 == 0 else TQ_P
    tk_p = TK_P if s % TK_P == 0 else TQ_P
    st_p, st_s, st_c = [], [], []
    for l in range(depth):
        last = l == depth - 1
        xp = _ffn_call(xp, mod_p(l, 0), norm_g[l, 0], w_ffn_in_h, w_ffn_out_h, l, 0, None, tm_p)
        xs = _ffn_call(xs, mod_s(l, 0), norm_g[l, 0], w_ffn_in_h, w_ffn_out_h, l, 0, None, nbs)
        if l % 2 == 0:
            a = l // 2
            wp = _rearranged_proj_weight(w_in_att[a])
            w_out = w_out_att[a].astype(BF16)
            pe_rows, wbd = _compress_operands(cmp_pe[a], w_cmp[a])

            pr = _proj_call(xp, mod_p(l, 1), norm_g[l, 1], wp, tm_p)
            oa = _pdsa_call(pr, strips_pa, far_a, tk_p)
            kcp = _pcmp_call(pr["kvc"], pe_rows, wbd)
            ob = _pnsa_call(pr, kcp, strips_pb, cstrips_p, far_b, tk_p)
            xp = _merge_call(xp, mod_p(l, 1), oa, ob, w_out, tm_p)
            kv5 = lambda t: t.reshape(t.shape[0], t.shape[1], 2, A_KV, HEAD_DIM)
            st_p.append((kv5(pr["kva"]), pr["misc"][:, :, M_KI:M_KI + IDX_DIM], kv5(pr["kvc"]),
                         kv5(pr["kvs"]), kv5(pr["kvw"][:, s - min(WINDOW, s):])))

            ps = _proj_call(xs, mod_s(l, 1), norm_g[l, 1], wp, nbs)
            oa = _sdsa_call(a, page_table, idx_t, akv_t, ps, strips_sa, far_a)
            ob = _snsa_call(a, page_table, cmp_t, sel_t, win_t, ps, pe_rows, wbd, strips_sb, cstrips_s, far_b)
            xs = _merge_call(xs, mod_s(l, 1), oa[None], ob[None], w_out, nbs)
            tok = lambda t: t[0].reshape(nbs, 1, 2, A_KV, HEAD_DIM)
            st_s.append((tok(ps["kva"]), ps["misc"][0][:, None, M_KI:M_KI + IDX_DIM], tok(ps["kvc"]),
                         tok(ps["kvs"]),
                         jnp.concatenate([state_b_win_kv[a][:, tdec:], tok(ps["kvw"])], axis=1)))
        else:
            ci = l // 2
            w_in = w_in_c[ci].astype(BF16)
            w_out = w_out_c[ci].astype(BF16)
            ws = (w_sp[ci] * jnp.tril(jnp.ones((C_CHUNK, C_CHUNK), w_sp.dtype))).astype(BF16)
            xp = _pgmlp_call(xp, mod_p(l, 1), norm_g[l, 1], w_in, ln_c_g[ci], ws,
                             jnp.transpose(b_sp[ci]), w_out, 256 if s % 256 == 0 else C_CHUNK)
            xs, v = _sgmlp_call(xs, mod_s(l, 1), norm_g[l, 1], w_in, ln_c_g[ci], ws, b_sp[ci], w_out)
            st_c.append(v[:, None, :])
        fg = final_g if last else None
        xp = _ffn_call(xp, mod_p(l, 2), norm_g[l, 2], w_ffn_in_h, w_ffn_out_h, l, 1, fg, tm_p)
        xs = _ffn_call(xs, mod_s(l, 2), norm_g[l, 2], w_ffn_in_h, w_ffn_out_h, l, 1, fg, nbs)

    stk = lambda sts, i: jnp.stack([st[i] for st in sts])
    return (xp, jnp.transpose(xs, (1, 0, 2)),
            stk(st_p, 0), stk(st_p, 1), stk(st_p, 2), stk(st_p, 3), stk(st_p, 4),
            stk(st_s, 0), stk(st_s, 1), stk(st_s, 2), stk(st_s, 3), stk(st_s, 4),
            jnp.stack(st_c))
```

```python
import functools
import math

import numpy as np
import jax
import jax.numpy as jnp
from jax import lax
from jax.experimental import pallas as pl
from jax.experimental.pallas import tpu as pltpu

F32 = jnp.float32
BF16 = jnp.bfloat16
I32 = jnp.int32

HEAD_DIM = 64
A_HEADS = 8
A_KV = 2
IDX_HEADS = 4
IDX_DIM = 64
IDX_TOPK = 256
B_HEADS = 8
B_KV = 2
CMP_BLOCK = 32
SEL_BLOCK = 64
SEL_TOPN = 16
WINDOW = 512
N_BUCKETS = 32
MAX_DISTANCE = 1024
C_CHUNK = 128
C_GROUPS = 8
EPS = 1e-6
PAGE = 128
GROUP = A_HEADS // A_KV
KVW = 2 * A_KV * HEAD_DIM

LANE = 128
VMEM_LIMIT = 56 * 1024 * 1024
KVX = 3 * LANE

NEG = -1e30
KMIN = -2 ** 31

P_QA, P_QB, P_KVA, P_QI, P_KVC, P_KVS, P_KVW, P_MISC = 0, 512, 1024, 1280, 1536, 1792, 2048, 2304
P_WIDTH = 2432
M_KI, M_WI, M_GATE = 0, IDX_DIM, IDX_DIM + IDX_HEADS

assert A_KV == 2 and B_KV == 2 and A_KV * HEAD_DIM == LANE and IDX_DIM == HEAD_DIM


def _cparams(sem):
    return pltpu.CompilerParams(dimension_semantics=sem, vmem_limit_bytes=VMEM_LIMIT)


def _mm(a, b):
    return jnp.dot(a.astype(BF16), b.astype(BF16), preferred_element_type=F32)


def _mm_nt(a, b):
    return lax.dot_general(a.astype(BF16), b.astype(BF16), (((1,), (1,)), ((), ())),
                           preferred_element_type=F32)


def _modulated_norm(x, g, mod_ref):
    y = x * lax.rsqrt(jnp.mean(x * x, axis=-1, keepdims=True) + EPS) * g
    return y * (1.0 + mod_ref[1]) + mod_ref[0]


def _ada_kernel(c_ref, w_ref, b_ref, o_ref):
    c = c_ref[...]
    o_ref[...] = _mm(c * jax.nn.sigmoid(c), w_ref[...]) + b_ref[...]


def _ada_call(c_all, w_ada, b_ada):
    depth, d, n = w_ada.shape
    r = c_all.shape[0]
    tn = n // 8
    return pl.pallas_call(
        _ada_kernel,
        grid=(depth, n // tn),
        in_specs=[pl.BlockSpec((r, d), lambda l, j: (0, 0)),
                  pl.BlockSpec((None, d, tn), lambda l, j: (l, 0, j)),
                  pl.BlockSpec((None, 1, tn), lambda l, j: (l, 0, j))],
        out_specs=pl.BlockSpec((None, r, tn), lambda l, j: (l, 0, j)),
        out_shape=jax.ShapeDtypeStruct((depth, r, n), F32),
        compiler_params=_cparams(("arbitrary", "arbitrary")),
        name="ada",
    )(c_all, w_ada, b_ada.reshape(depth, 1, n))


def _ffn_kernel(x_ref, mod_ref, g_ref, wa_ref, wb_ref, wo_ref, fg_ref, o_ref, h_scr, acc_scr, *,
                n_f, final_norm):
    f = pl.program_id(2)

    @pl.when(f == 0)
    def _():
        h_scr[...] = _modulated_norm(x_ref[...], g_ref[...], mod_ref).astype(BF16)
        acc_scr[...] = jnp.zeros_like(acc_scr)

    h = h_scr[...]
    a = _mm(h, wa_ref[...])
    b = _mm(h, wb_ref[...])
    acc_scr[...] += _mm(a * jax.nn.sigmoid(a) * b, wo_ref[...])

    @pl.when(f == n_f - 1)
    def _():
        y = x_ref[...] + 0.5 * (1.0 + mod_ref[2]) * acc_scr[...]
        if final_norm:
            y = y * lax.rsqrt(jnp.mean(y * y, axis=-1, keepdims=True) + EPS) * fg_ref[...]
        o_ref[...] = y


def _ffn_call(x, mod, g, w_in, w_out, l, j, final_g, tm):
    nb, t, d = x.shape
    dff = w_out.shape[2]
    tf = dff // 2 if dff % (2 * LANE) == 0 else LANE
    n_f = dff // tf
    tmod = mod.shape[2]
    mod_spec = (pl.BlockSpec((None, 3, 1, d), lambda b, i, f: (b, 0, 0, 0)) if tmod == 1 else
                pl.BlockSpec((None, 3, tm, d), lambda b, i, f: (b, 0, i, 0)))
    final_norm = final_g is not None
    fg = (final_g if final_norm else g).reshape(1, d)
    return pl.pallas_call(
        functools.partial(_ffn_kernel, n_f=n_f, final_norm=final_norm),
        grid=(nb, t // tm, n_f),
        in_specs=[pl.BlockSpec((None, tm, d), lambda b, i, f: (b, i, 0)),
                  mod_spec,
                  pl.BlockSpec((1, d), lambda b, i, f: (0, 0)),
                  pl.BlockSpec((None, None, d, tf), lambda b, i, f: (l, j, 0, f)),
                  pl.BlockSpec((None, None, d, tf), lambda b, i, f: (l, j, 0, f + n_f)),
                  pl.BlockSpec((None, None, tf, d), lambda b, i, f: (l, j, f, 0)),
                  pl.BlockSpec((1, d), lambda b, i, f: (0, 0))],
        out_specs=pl.BlockSpec((None, tm, d), lambda b, i, f: (b, i, 0)),
        out_shape=jax.ShapeDtypeStruct(x.shape, F32),
        scratch_shapes=[pltpu.VMEM((tm, d), BF16), pltpu.VMEM((tm, d), F32)],
        compiler_params=_cparams(("arbitrary", "arbitrary", "arbitrary")),
        name="ffn",
    )(x, mod, g.reshape(1, d), w_in, w_in, w_out, fg)


def _kv_ext(y):
    e = jnp.where(lax.broadcasted_iota(I32, (y.shape[0], HEAD_DIM), 1) == 0, 1.0, 0.0).astype(BF16)
    y = y.astype(BF16)
    return jnp.concatenate([y[:, :LANE], y[:, LANE:LANE + HEAD_DIM], e, y[:, LANE + HEAD_DIM:], e], axis=1)


def _kv_ext_t(y):
    e = jnp.where(lax.broadcasted_iota(I32, (HEAD_DIM, y.shape[1]), 0) == 0, 1.0, 0.0).astype(BF16)
    y = y.astype(BF16)
    return jnp.concatenate([y[:LANE], y[LANE:LANE + HEAD_DIM], e, y[LANE + HEAD_DIM:], e], axis=0)


_PROJ_OUT = (("qa", P_QA, 512, "h"), ("qb", P_QB, 512, "h"), ("qi", P_QI, 256, "h"),
             ("kva", P_KVA, 256, "f"), ("kvc", P_KVC, 256, "f"), ("kvs", P_KVS, 256, "f"),
             ("kvw", P_KVW, 256, "f"), ("misc", P_MISC, 128, "f"),
             ("kva_x", P_KVA, KVX, "x"), ("kvs_x", P_KVS, KVX, "x"), ("kvw_x", P_KVW, KVX, "x"),
             ("misc_h", P_MISC, 128, "h"))


def _proj_kernel(x_ref, mod_ref, g_ref, w_ref, *o_refs):
    h = _modulated_norm(x_ref[...], g_ref[...], mod_ref).astype(BF16)
    done = {}
    for (name, off, width, kind), o_ref in zip(_PROJ_OUT, o_refs):
        if off not in done:
            y = _mm(h, w_ref[:, off:off + (KVW if kind == "x" else width)])
            if name in ("qa", "qb"):
                y = y * (HEAD_DIM ** -0.5)
            done[off] = y
        o_ref[...] = _kv_ext(done[off]) if kind == "x" else done[off].astype(o_ref.dtype)


def _proj_call(x, mod, g, wp, tm):
    nb, t, d = x.shape
    tmod = mod.shape[2]
    mod_spec = (pl.BlockSpec((None, 3, 1, d), lambda b, i: (b, 0, 0, 0)) if tmod == 1 else
                pl.BlockSpec((None, 3, tm, d), lambda b, i: (b, 0, i, 0)))
    outs = pl.pallas_call(
        _proj_kernel,
        grid=(nb, t // tm),
        in_specs=[pl.BlockSpec((None, tm, d), lambda b, i: (b, i, 0)),
                  mod_spec,
                  pl.BlockSpec((1, d), lambda b, i: (0, 0)),
                  pl.BlockSpec((d, P_WIDTH), lambda b, i: (0, 0))],
        out_specs=[pl.BlockSpec((None, tm, w), lambda b, i: (b, i, 0)) for _, _, w, _ in _PROJ_OUT],
        out_shape=[jax.ShapeDtypeStruct((nb, t, w), F32 if kind == "f" else BF16)
                   for _, _, w, kind in _PROJ_OUT],
        compiler_params=_cparams(("arbitrary", "arbitrary")),
        name="proj",
    )(x, mod, g.reshape(1, d), wp)
    return {name: o for (name, _, _, _), o in zip(_PROJ_OUT, outs)}


def _rearranged_proj_weight(w):
    d = w.shape[0]
    o = np.cumsum((0, 512, 256, 256, 64, 4, 512, 256, 256, 256, 24))
    seg = lambda i: w[:, o[i]:o[i + 1]]
    qa, kva, qi, ki, wi, qb, kvc, kvs, kvw, gates = [seg(i) for i in range(10)]
    pad = jnp.zeros((d, LANE - IDX_DIM - IDX_HEADS - 3 * B_HEADS), w.dtype)
    return jnp.concatenate([qa, qb, kva, qi, kvc, kvs, kvw, ki, wi, gates, pad], axis=1).astype(BF16)


def _t5_bucket(dist):
    n = jnp.maximum(dist, 0)
    max_exact = N_BUCKETS // 2
    nf = jnp.maximum(n, 1).astype(F32)
    large = max_exact + (jnp.log(nf / max_exact) / math.log(MAX_DISTANCE / max_exact)
                         * (N_BUCKETS - max_exact)).astype(I32)
    large = jnp.minimum(large, N_BUCKETS - 1)
    return jnp.where(n < max_exact, n, large)


def _bias_lookup(tab, dist):
    b = _t5_bucket(jnp.asarray(dist.reshape(-1), I32))
    onehot = (b[:, None] == jnp.arange(N_BUCKETS, dtype=I32)[None, :]).astype(F32)
    out = jnp.dot(onehot, tab, precision=lax.Precision.HIGHEST)
    return jnp.transpose(out).reshape((tab.shape[1],) + dist.shape)


NEAR = -(-(MAX_DISTANCE + LANE - 1) // LANE)


def _bias_strips(tab, tq):
    r = np.arange(tq)[None, :, None]
    c = np.arange(LANE)[None, None, :]
    bd = np.arange(NEAR + 1)[:, None, None]
    dist = np.where(bd == NEAR, MAX_DISTANCE, r - c + LANE * bd)
    return _bias_lookup(tab, dist)


def _cmp_near_count(delta):
    return -(-(-(-(MAX_DISTANCE - delta) // CMP_BLOCK)) // 8) * 8


def _cmp_delta(tq):
    return -(CMP_BLOCK - 1) - CMP_BLOCK * ((tq - CMP_BLOCK) // CMP_BLOCK)


def _cmp_strips(tab, tq):
    delta = _cmp_delta(tq)
    ncn = _cmp_near_count(delta)
    dist = np.arange(tq)[:, None] + delta + CMP_BLOCK * np.arange(ncn)[None, :]
    v = _bias_lookup(tab, dist)
    hi = v.astype(BF16)
    r1 = v - hi.astype(F32)
    mid = r1.astype(BF16)
    lo = (r1 - mid.astype(F32)).astype(BF16)
    return jnp.stack([hi, mid, lo], axis=1)


def _padded_heads(x, g):
    z = jnp.zeros((x.shape[0], HEAD_DIM), x.dtype)
    rows = []
    for r in range(GROUP):
        h = g * GROUP + r
        piece = x[:, h * HEAD_DIM:(h + 1) * HEAD_DIM]
        rows.append(jnp.concatenate([piece, z] if g == 0 else [z, piece], axis=1))
    return jnp.concatenate(rows, axis=0)


def _k_part(kv, kv_t):
    return kv[:LANE] if kv_t else kv[:, :LANE]


def _v_part(kv, g, kv_t):
    return kv[(1 + g) * LANE:(2 + g) * LANE] if kv_t else kv[:, (1 + g) * LANE:(2 + g) * LANE]


def _flash_init(scr, tq, in_values):
    if in_values:
        return tuple((jnp.full((GROUP, tq, LANE), NEG, F32), jnp.zeros((GROUP * tq, LANE), F32))
                     for _ in range(A_KV))
    scr[0][...] = jnp.full_like(scr[0], NEG)
    scr[1][...] = jnp.zeros_like(scr[1])
    return 0


def _flash_store(st, scr):
    if isinstance(st, tuple):
        for g, (m, acc) in enumerate(st):
            scr[0][g] = m
            scr[1][g] = acc


def _mask_add(mask):
    return jnp.where(mask, 0.0, NEG)


def _flash_update(st, scr, g, qp_g, kv, strips_ref, far_ref, bds, madd, tq, kv_t):
    m_all, acc = st[g] if isinstance(st, tuple) else (scr[0][g], scr[1][g])
    kb = _k_part(kv, kv_t)
    s_all = _mm(qp_g, kb) if kv_t else _mm_nt(qp_g, kb)
    nblk = madd.shape[1] // LANE
    ps, alphas, m_news = [], [], []
    for r in range(GROUP):
        h = g * GROUP + r
        blocks = []
        for j in range(nblk):
            sj = s_all[r * tq:(r + 1) * tq, j * LANE:(j + 1) * LANE] + madd[:, j * LANE:(j + 1) * LANE]
            blocks.append(sj if bds is None else sj + strips_ref[h, bds[j]])
        bmax = blocks[0]
        for sj in blocks[1:]:
            bmax = jnp.maximum(bmax, sj)
        row_max = jnp.max(bmax, axis=-1, keepdims=True)
        m_old = m_all[r]
        if bds is None:
            m_new = jnp.maximum(m_old, row_max + far_ref[h])
            shift = m_new - far_ref[h]
        else:
            m_new = jnp.maximum(m_old, row_max)
            shift = m_new
        m_news.append(m_new)
        alphas.append(jnp.exp(m_old - m_new))
        ps.append(jnp.concatenate([jnp.exp(sj - shift).astype(BF16) for sj in blocks], axis=1))
    p = jnp.concatenate(ps, axis=0)
    vx = _v_part(kv, g, kv_t)
    pv = _mm_nt(p, vx) if kv_t else _mm(p, vx)
    m_all = jnp.stack(m_news)
    acc = jnp.concatenate(alphas, axis=0) * acc + pv
    if isinstance(st, tuple):
        return st[:g] + ((m_all, acc),) + st[g + 1:]
    scr[0][g] = m_all
    scr[1][g] = acc
    return st


def _flash_result(scr, g):
    acc = scr[1][g]
    l = acc[:, HEAD_DIM:HEAD_DIM + 1]
    return jnp.where(l > 0.0, acc[:, :HEAD_DIM] / jnp.where(l > 0.0, l, 1.0), 0.0)


def _tile_bds(qb, k, tk):
    nblk = tk // LANE
    return [jnp.clip(qb - (k * nblk + j), 0, NEAR) for j in range(nblk)]


def _ordered_key(x):
    b = lax.bitcast_convert_type(x, I32)
    return jnp.where(x == 0.0, 0, b ^ ((b >> 31) & 0x7FFFFFFF))


def _far_tiles(qb, tk):
    n = (qb - NEAR + 1) // (tk // LANE)
    return max(n, 0) if isinstance(n, int) else jnp.maximum(n, 0)


def _dsa_core(qi, wi, qa, get_kidx, get_kva, strips_ref, far_ref, t0, tq, tk, topk, kv_t, unroll,
              keys_scr, wib_scr, m_scr, acc_scr, tri_scr):
    n_tiles = (t0 + tq - 1) // tk + 1
    qb = t0 // LANE
    n_far = _far_tiles(qb, tk)
    row_pos = t0 + lax.broadcasted_iota(I32, (tq, tk), 0)
    col = lax.broadcasted_iota(I32, (tq, tk), 1)

    if kv_t:
        qi_st = jnp.concatenate([qi[:, h * IDX_DIM:(h + 1) * IDX_DIM] for h in range(IDX_HEADS)], axis=0)
    else:
        z = jnp.zeros((tq, LANE - IDX_DIM), qi.dtype)
        qi_st = jnp.concatenate(
            [jnp.concatenate([qi[:, h * IDX_DIM:(h + 1) * IDX_DIM], z], axis=1) for h in range(IDX_HEADS)], axis=0)
    for h in range(IDX_HEADS):
        wib_scr[h] = jnp.broadcast_to(wi[:, h:h + 1], (tq, tk))

    def score_tile(k, c):
        kt = get_kidx(k)
        sc = _mm(qi_st, kt) if kv_t else _mm_nt(qi_st, kt)
        score = jnp.maximum(sc[:tq], 0.0) * wib_scr[0]
        for h in range(1, IDX_HEADS):
            score = score + jnp.maximum(sc[h * tq:(h + 1) * tq], 0.0) * wib_scr[h]
        keys_scr[k] = jnp.where(k * tk + col <= row_pos, _ordered_key(score), KMIN)
        return c

    lax.fori_loop(0, n_tiles, score_tile, 0, unroll=unroll)

    def count(pred):
        def count_tile(k, cnt):
            key = keys_scr[k]
            for j in range(tk // LANE):
                cnt = cnt + jnp.where(pred(key[:, j * LANE:(j + 1) * LANE]), 1.0, 0.0)
            return cnt

        cnt = lax.fori_loop(0, n_tiles, count_tile, jnp.zeros((tq, LANE), F32))
        return jnp.sum(cnt, axis=-1, keepdims=True)

    tu = jnp.zeros((tq, 1), I32)
    n_ge = jnp.zeros((tq, 1), F32)
    for bit in range(31, -1, -1):
        cand_u = tu | np.int32(-2 ** 31 if bit == 31 else 2 ** bit)
        cand_s = jnp.broadcast_to(cand_u ^ np.int32(KMIN), (tq, LANE))
        tot = count(lambda key, cand_s=cand_s: key >= cand_s)
        tu = jnp.where(tot >= topk, cand_u, tu)
        n_ge = jnp.where(tot >= topk, tot, n_ge)
    thr = tu ^ np.int32(KMIN)
    found = thr > np.int32(KMIN)
    has_ties = jnp.max(jnp.where(found & (n_ge > topk), 1.0, 0.0)) > 0.5

    qp_g = [_padded_heads(qa, g) for g in range(A_KV)]
    scr = (m_scr, acc_scr)
    in_values = bool(unroll)

    def attend(k, st, mask, far):
        kv = get_kva(k)
        bds = None if far else _tile_bds(qb, k, tk)
        madd = _mask_add(mask)
        for g in range(A_KV):
            st = _flash_update(st, scr, g, qp_g[g], kv, strips_ref, far_ref, bds, madd, tq, kv_t)
        return st

    @pl.when(jnp.logical_not(has_ties))
    def _():
        thr_ge = jnp.where(found, thr, np.int32(KMIN + 1))

        def tile(k, st, far):
            return attend(k, st, keys_scr[k] >= thr_ge, far)

        st = _flash_init(scr, tq, in_values)
        st = lax.fori_loop(0, n_far, functools.partial(tile, far=True), st, unroll=unroll)
        st = lax.fori_loop(n_far, n_tiles, functools.partial(tile, far=False), st, unroll=unroll)
        _flash_store(st, scr)

    @pl.when(has_ties)
    def _():
        need = topk - count(lambda key: key > jnp.broadcast_to(thr, (tq, LANE)))
        tri_scr[...] = jnp.where(
            lax.broadcasted_iota(I32, (tk, tk), 0) <= lax.broadcasted_iota(I32, (tk, tk), 1), 1.0, 0.0).astype(BF16)

        def tile(k, c):
            eq_seen, st = c
            key = keys_scr[k]
            eq = key == thr
            incl = _mm(jnp.where(eq, 1.0, 0.0), tri_scr[...])
            st = attend(k, st, (key > thr) | (eq & found & (eq_seen + incl <= need)), False)
            return eq_seen + incl[:, tk - 1:tk], st

        _, st = lax.fori_loop(0, n_tiles, tile, (jnp.zeros((tq, 1), F32), _flash_init(scr, tq, in_values)))
        _flash_store(st, scr)

    outs = []
    for g in range(A_KV):
        o = _flash_result(scr, g)
        outs += [o[r * tq:(r + 1) * tq] for r in range(GROUP)]
    return jnp.concatenate(outs, axis=1)


def _pick_blocks(v, n_pick):
    lane = lax.broadcasted_iota(I32, v.shape, 1).astype(F32)
    sel = jnp.zeros(v.shape, jnp.bool_)
    for _ in range(n_pick):
        mx = jnp.max(v, axis=-1, keepdims=True)
        cand = (v == mx) & jnp.logical_not(sel)
        first = jnp.min(jnp.where(cand, lane, float(v.shape[1])), axis=-1, keepdims=True)
        pick = lane == first
        sel = sel | pick
        v = jnp.where(pick, -jnp.inf, v)
    return sel


def _nsa_core(qb, gates, kcp, get_kvs, win_tiles, strips_ref, cstrips_ref, far_ref, t0, tq, tk, nj,
              n_pick, extra_block, kv_t, unroll, m_scr, acc_scr):
    n_tiles = (t0 + tq - 1) // tk + 1
    qblk = t0 // LANE
    delta = _cmp_delta(tq)
    ncn = cstrips_ref.shape[-1]
    qp_g = [_padded_heads(qb, g) for g in range(B_KV)]
    row_pos1 = t0 + lax.broadcasted_iota(I32, (tq, 1), 0)

    lam = lax.broadcasted_iota(I32, (tq, 2 * nj), 1)
    m_of = 2 * (lam % nj) + lam // nj
    cdist = row_pos1 - (CMP_BLOCK * m_of + CMP_BLOCK - 1)
    cmask = cdist >= 0
    m_hi = (t0 - (CMP_BLOCK - 1) - delta) // CMP_BLOCK
    lam_s = lax.broadcasted_iota(I32, (ncn, 2 * nj), 1)
    shift = jnp.where(2 * (lam_s % nj) + lam_s // nj == m_hi - lax.broadcasted_iota(I32, (ncn, 2 * nj), 0),
                      1.0, 0.0).astype(BF16)
    is_far = m_of <= m_hi - ncn
    kc_both = kcp[:, :LANE]
    o_c, imp = [], []
    for g in range(B_KV):
        bias = []
        for r in range(GROUP):
            h = g * GROUP + r
            b = (_mm(cstrips_ref[h, 0], shift) + _mm(cstrips_ref[h, 1], shift)) + _mm(cstrips_ref[h, 2], shift)
            bias.append(jnp.where(is_far, far_ref[h], b))
        s = _mm_nt(qp_g[g], kc_both).reshape(GROUP, tq, 2 * nj) + jnp.stack(bias)
        s = jnp.where(cmask[None], s, NEG)
        e = jnp.where(cmask[None], jnp.exp(s - jnp.max(s, axis=-1, keepdims=True)), 0.0)
        l = jnp.sum(e, axis=-1, keepdims=True)
        p = jnp.where(l > 0.0, e / jnp.where(l > 0.0, l, 1.0), 0.0)
        o_c.append(_mm(p.reshape(GROUP * tq, 2 * nj), kcp[:, (B_KV + g) * HEAD_DIM:(B_KV + g + 1) * HEAD_DIM]))
        ps = p[0]
        for r in range(1, GROUP):
            ps = ps + p[r]
        imp.append(ps[:, :nj] + ps[:, nj:])

    jl = lax.broadcasted_iota(I32, (tq, nj), 1)
    cur = row_pos1 // SEL_BLOCK
    forced = (jl == 0) | (jl == cur) | (jl == cur - 1)
    admissible = jl <= cur
    bmask = []
    for g in range(B_KV):
        v = jnp.where(admissible, jnp.where(forced, jnp.inf, imp[g]), -jnp.inf)
        bmask.append(jnp.where(_pick_blocks(v, n_pick) & admissible, 1.0, 0.0).astype(BF16))
    row_pos = t0 + lax.broadcasted_iota(I32, (tq, tk), 0)
    col = lax.broadcasted_iota(I32, (tq, tk), 1)
    ej = lax.broadcasted_iota(I32, (nj, tk), 0)
    ec = lax.broadcasted_iota(I32, (nj, tk), 1)
    scr = (m_scr, acc_scr)
    in_values = bool(unroll)

    def sel_tile(k, st, far):
        kv = get_kvs(k)
        expand = jnp.where(ej == (k * tk + ec) // SEL_BLOCK, 1.0, 0.0).astype(BF16)
        for g in range(B_KV):
            mask = _mm(bmask[g], expand) > 0.5
            if far:
                bds = None
            else:
                bds = _tile_bds(qblk, k, tk)
                key_pos = k * tk + col
                if extra_block:
                    mask = mask | (key_pos >= nj * SEL_BLOCK)
                mask = mask & (key_pos <= row_pos)
            st = _flash_update(st, scr, g, qp_g[g], kv, strips_ref, far_ref, bds, _mask_add(mask), tq, kv_t)
        return st

    n_far = _far_tiles(qblk, tk)
    st = _flash_init(scr, tq, in_values)
    st = lax.fori_loop(0, n_far, functools.partial(sel_tile, far=True), st, unroll=unroll)
    st = lax.fori_loop(n_far, n_tiles, functools.partial(sel_tile, far=False), st, unroll=unroll)
    _flash_store(st, scr)
    o_s = [_flash_result(scr, g) for g in range(B_KV)]

    st = _flash_init(scr, tq, in_values)
    for get_tile, off, guard in win_tiles:
        def win_tile(st, get_tile=get_tile, off=off):
            kv = get_tile()
            wt = kv.shape[1] if kv_t else kv.shape[0]
            dist = (lax.broadcasted_iota(I32, (tq, wt), 0) - lax.broadcasted_iota(I32, (tq, wt), 1)
                    - LANE * off)
            wadd = _mask_add((dist >= 0) & (dist <= WINDOW))
            bds = [min(max(-off - j, 0), NEAR) for j in range(wt // LANE)]
            for g in range(B_KV):
                st = _flash_update(st, scr, g, qp_g[g], kv, strips_ref, far_ref, bds, wadd, tq, kv_t)
            return st

        if guard is None:
            st = win_tile(st)
        else:
            assert not in_values

            def guarded(win_tile=win_tile):
                win_tile(0)

            pl.when(guard)(guarded)
    _flash_store(st, scr)
    o_w = [_flash_result(scr, g) for g in range(B_KV)]

    gs = jax.nn.sigmoid(gates)
    outs = []
    for g in range(B_KV):
        gcol = lambda j: jnp.concatenate(
            [gs[:, (g * GROUP + r) * 3 + j:(g * GROUP + r) * 3 + j + 1] for r in range(GROUP)], axis=0)
        o = gcol(0) * o_c[g] + gcol(1) * o_s[g] + gcol(2) * o_w[g]
        outs += [o[r * tq:(r + 1) * tq] for r in range(GROUP)]
    return jnp.concatenate(outs, axis=1)


def _compress(load_rows, n_pairs, pe_ref, wbd_ref):
    acc = [jnp.zeros((n_pairs, KVW), F32), jnp.zeros((n_pairs, KVW), F32)]
    for c in range(2 * CMP_BLOCK):
        ci = c % CMP_BLOCK
        acc[c // CMP_BLOCK] = acc[c // CMP_BLOCK] + _mm(load_rows(c) + pe_ref[ci], wbd_ref[ci])
    return jnp.concatenate(acc, axis=0)


def _compress_operands(pe, w):
    pe_rows = jnp.transpose(jnp.broadcast_to(pe[:, None], (2, B_KV, CMP_BLOCK, HEAD_DIM)), (2, 0, 1, 3))
    pe_rows = pe_rows.reshape(CMP_BLOCK, 1, KVW)
    eye = jnp.eye(2 * B_KV, dtype=w.dtype).reshape(2, B_KV, 2, B_KV)
    wbd = jnp.einsum("icde,igjh->cigdjhe", w, eye).reshape(CMP_BLOCK, KVW, KVW)
    return pe_rows, wbd.astype(BF16)


def _strided_rows(lo_ref, hi_ref, c, n):
    rows = pl.ds(c, n, stride=2 * CMP_BLOCK)
    return jnp.concatenate([lo_ref[rows, :], hi_ref[rows, :]], axis=1)


def _flash_scratch(tq):
    return [pltpu.VMEM((A_KV, GROUP, tq, LANE), F32), pltpu.VMEM((A_KV, GROUP * tq, LANE), F32)]


TQ_P = 128
TK_P = 512


def _prompt_tile(ref, k, tk):
    return ref[pl.ds(pl.multiple_of(k * tk, tk), tk), :]


def _pdsa_kernel(qi_ref, misc_ref, qa_ref, kidx_ref, kva_ref, strips_ref, far_ref, o_ref,
                 keys_scr, wib_scr, m_scr, acc_scr, tri_scr, *, topk, tk):
    i = pl.program_id(1)
    o = _dsa_core(qi_ref[...], misc_ref[:, M_WI:M_WI + IDX_HEADS], qa_ref[...],
                  lambda k: _prompt_tile(kidx_ref, k, tk), lambda k: _prompt_tile(kva_ref, k, tk),
                  strips_ref, far_ref, i * TQ_P, TQ_P, tk, topk, False, None,
                  keys_scr, wib_scr, m_scr, acc_scr, tri_scr)
    o_ref[...] = o.astype(o_ref.dtype)


def _pdsa_call(pr, strips, far, tk):
    nb, s, _ = pr["qa"].shape
    tq = TQ_P
    row = lambda w: pl.BlockSpec((None, tq, w), lambda b, i: (b, i, 0))
    full = lambda w: pl.BlockSpec((None, s, w), lambda b, i: (b, 0, 0))
    return pl.pallas_call(
        functools.partial(_pdsa_kernel, topk=min(IDX_TOPK, s // 4), tk=tk),
        grid=(nb, s // tq),
        in_specs=[row(256), row(128), row(512), full(128), full(KVX),
                  pl.BlockSpec(strips.shape, lambda b, i: (0, 0, 0, 0)),
                  pl.BlockSpec(memory_space=pltpu.SMEM)],
        out_specs=row(512),
        out_shape=jax.ShapeDtypeStruct((nb, s, A_HEADS * HEAD_DIM), BF16),
        scratch_shapes=[pltpu.VMEM((s // tk, tq, tk), I32), pltpu.VMEM((IDX_HEADS, tq, tk), F32)]
        + _flash_scratch(tq) + [pltpu.VMEM((tk, tk), BF16)],
        compiler_params=_cparams(("arbitrary", "arbitrary")),
        name="dsa_prompt",
    )(pr["qi"], pr["misc"], pr["qa"], pr["misc_h"], pr["kva_x"], strips, far)


def _pcmp_kernel(lo_ref, hi_ref, pe_ref, wbd_ref, o_ref, *, n_pairs):
    o_ref[...] = _compress(lambda c: _strided_rows(lo_ref, hi_ref, c, n_pairs), n_pairs, pe_ref, wbd_ref)


def _pcmp_call(kvc, pe_rows, wbd):
    nb, s, _ = kvc.shape
    n_pairs = s // (2 * CMP_BLOCK)
    return pl.pallas_call(
        functools.partial(_pcmp_kernel, n_pairs=n_pairs),
        grid=(nb,),
        in_specs=[pl.BlockSpec((None, s, LANE), lambda b: (b, 0, 0)),
                  pl.BlockSpec((None, s, LANE), lambda b: (b, 0, 1)),
                  pl.BlockSpec((CMP_BLOCK, 1, KVW), lambda b: (0, 0, 0)),
                  pl.BlockSpec((CMP_BLOCK, KVW, KVW), lambda b: (0, 0, 0))],
        out_specs=pl.BlockSpec((None, 2 * n_pairs, KVW), lambda b: (b, 0, 0)),
        out_shape=jax.ShapeDtypeStruct((nb, 2 * n_pairs, KVW), F32),
        compiler_params=_cparams(("arbitrary",)),
        name="compress_prompt",
    )(kvc, kvc, pe_rows, wbd)


def _pnsa_kernel(qb_ref, misc_ref, kcp_ref, kvs_ref, kvw_ref, strips_ref, cstrips_ref, far_ref, o_ref,
                 m_scr, acc_scr, *, nj, n_pick, tk):
    i = pl.program_id(1)
    nw = WINDOW // LANE
    win_tiles = [(lambda: kvw_ref[pl.ds(pl.multiple_of((i - nw) * LANE, LANE), WINDOW + TQ_P), :], -nw, i >= nw)]
    win_tiles += [(lambda wd=wd: _prompt_tile(kvw_ref, i - wd, LANE), -wd, (i < nw) & (i - wd >= 0))
                  for wd in range(nw - 1, -1, -1)]
    o = _nsa_core(qb_ref[...], misc_ref[:, M_GATE:M_GATE + 3 * B_HEADS], kcp_ref[...],
                  lambda k: _prompt_tile(kvs_ref, k, tk), win_tiles,
                  strips_ref, cstrips_ref, far_ref, i * TQ_P, TQ_P, tk, nj, n_pick, False, False, None,
                  m_scr, acc_scr)
    o_ref[...] = o.astype(o_ref.dtype)


def _pnsa_call(pr, kcp, strips, cstrips, far, tk):
    nb, s, _ = pr["qb"].shape
    tq = TQ_P
    nj = s // SEL_BLOCK
    row = lambda w: pl.BlockSpec((None, tq, w), lambda b, i: (b, i, 0))
    full = lambda n, w: pl.BlockSpec((None, n, w), lambda b, i: (b, 0, 0))
    const = lambda a: pl.BlockSpec(a.shape, lambda b, i: (0,) * a.ndim)
    return pl.pallas_call(
        functools.partial(_pnsa_kernel, nj=nj, n_pick=min(SEL_TOPN, nj), tk=tk),
        grid=(nb, s // tq),
        in_specs=[row(512), row(128), full(2 * nj, KVW), full(s, KVX), full(s, KVX),
                  const(strips), const(cstrips),
                  pl.BlockSpec(memory_space=pltpu.SMEM)],
        out_specs=row(512),
        out_shape=jax.ShapeDtypeStruct((nb, s, B_HEADS * HEAD_DIM), BF16),
        scratch_shapes=_flash_scratch(tq),
        compiler_params=_cparams(("arbitrary", "arbitrary")),
        name="nsa_prompt",
    )(pr["qb"], pr["misc"], kcp, pr["kvs_x"], pr["kvw_x"], strips, cstrips, far)


TQ_S = 8
TK_S = 512
PPT = TK_S // PAGE


def _pages_per_step(n_pages):
    return max(g for g in (8, 4, 2, 1) if n_pages % g == 0)


def _new_token_block(col):
    return jnp.where(lax.broadcasted_iota(I32, (col.shape[0], PAGE), 1) == 0, col, 0.0)


def _slot_tile(buf, k):
    blk = buf[pl.ds(k * PPT, PPT)]
    return jnp.concatenate([blk[j] for j in range(PPT)], axis=1)


def _sdsa_kernel(*refs, n_pages, pps, topk):
    idx_refs = refs[1:1 + pps]
    akv_refs = refs[1 + pps:1 + 2 * pps]
    (qi_ref, wi_ref, qa_ref, new_idx_ref, new_kv_ref, strips_ref, far_ref, o_ref,
     idx_buf, akv_buf, keys_scr, wib_scr, m_scr, acc_scr, tri_scr) = refs[1 + 2 * pps:]
    p = pl.program_id(1)
    for j in range(pps):
        idx_buf[p * pps + j] = idx_refs[j][...].astype(BF16)
        akv_buf[p * pps + j] = _kv_ext_t(akv_refs[j][...])

    @pl.when(p == n_pages // pps - 1)
    def _():
        idx_buf[n_pages] = _new_token_block(new_idx_ref[...]).astype(BF16)
        akv_buf[n_pages] = _kv_ext_t(_new_token_block(new_kv_ref[...]))
        for j in range(1, PPT):
            idx_buf[n_pages + j] = jnp.zeros((IDX_DIM, PAGE), BF16)
            akv_buf[n_pages + j] = _kv_ext_t(jnp.zeros((KVW, PAGE), BF16))
        o = _dsa_core(qi_ref[...], wi_ref[...], qa_ref[...],
                      lambda k: _slot_tile(idx_buf, k), lambda k: _slot_tile(akv_buf, k),
                      strips_ref, far_ref, n_pages * PAGE, TQ_S, TK_S, topk, True, True,
                      keys_scr, wib_scr, m_scr, acc_scr, tri_scr)
        o_ref[...] = o.astype(o_ref.dtype)


def _sample_rows(x):
    return jnp.broadcast_to(x[:, None, :], (x.shape[0], TQ_S, x.shape[1]))


def _pages_t(cache):
    n, p = cache.shape[:2]
    nd = cache.ndim
    return jnp.transpose(cache, (0, 1) + tuple(range(3, nd)) + (2,)).reshape(n, p, -1, cache.shape[2])


def _sdsa_call(a, page_table, idx_t, akv_t, pr, strips, far):
    bd, n_pages = page_table.shape
    assert idx_t.shape[3] == PAGE and (n_pages * PAGE) % TK_S == 0
    pps = _pages_per_step(n_pages)
    slots = n_pages + PPT
    misc = pr["misc"][0]
    rows = lambda w: pl.BlockSpec((None, TQ_S, w), lambda b, p, pt: (b, 0, 0))
    col = lambda w: pl.BlockSpec((None, w, 1), lambda b, p, pt: (b, 0, 0))
    page = lambda w, j: pl.BlockSpec((None, None, w, PAGE),
                                     lambda b, p, pt: (a, pt[b * n_pages + p * pps + j], 0, 0))
    grid_spec = pltpu.PrefetchScalarGridSpec(
        num_scalar_prefetch=1,
        grid=(bd, n_pages // pps),
        in_specs=[page(IDX_DIM, j) for j in range(pps)] + [page(KVW, j) for j in range(pps)]
        + [rows(256), rows(IDX_HEADS), rows(512), col(IDX_DIM), col(KVW),
           pl.BlockSpec(strips.shape, lambda b, p, pt: (0, 0, 0, 0)),
           pl.BlockSpec(memory_space=pltpu.SMEM)],
        out_specs=rows(512),
        scratch_shapes=[pltpu.VMEM((slots, IDX_DIM, PAGE), BF16), pltpu.VMEM((slots, KVX, PAGE), BF16),
                        pltpu.VMEM((slots // PPT, TQ_S, TK_S), I32), pltpu.VMEM((IDX_HEADS, TQ_S, TK_S), F32)]
        + _flash_scratch(TQ_S) + [pltpu.VMEM((TK_S, TK_S), BF16)])
    out = pl.pallas_call(
        functools.partial(_sdsa_kernel, n_pages=n_pages, pps=pps, topk=min(IDX_TOPK, (n_pages * PAGE + 1) // 4)),
        grid_spec=grid_spec,
        out_shape=jax.ShapeDtypeStruct((bd, TQ_S, A_HEADS * HEAD_DIM), BF16),
        compiler_params=_cparams(("arbitrary", "arbitrary")),
        name="dsa_sample",
    )(page_table.reshape(-1), *([idx_t] * pps), *([akv_t] * pps),
      _sample_rows(pr["qi"][0]), _sample_rows(misc[:, M_WI:M_WI + IDX_HEADS]), _sample_rows(pr["qa"][0]),
      misc[:, M_KI:M_KI + IDX_DIM, None], pr["kva"][0][:, :, None], strips, far)
    return out[:, 0]


def _snsa_kernel(*refs, n_pages, pps):
    cmp_refs = refs[1:1 + pps]
    sel_refs = refs[1 + pps:1 + 2 * pps]
    (qb_ref, gates_ref, new_sel_ref, win_ref, new_win_ref, pe_ref, wbd_ref, strips_ref, cstrips_ref, far_ref,
     o_ref, cmp_lo, cmp_hi, sel_buf, m_scr, acc_scr) = refs[1 + 2 * pps:]
    p = pl.program_id(1)
    past = n_pages * PAGE
    for j in range(pps):
        rows = pl.ds(pl.multiple_of((p * pps + j) * PAGE, PAGE), PAGE)
        cmp_lo[rows, :] = jnp.transpose(cmp_refs[j][:LANE])
        cmp_hi[rows, :] = jnp.transpose(cmp_refs[j][LANE:])
        sel_buf[p * pps + j] = _kv_ext_t(sel_refs[j][...])

    @pl.when(p == n_pages // pps - 1)
    def _():
        sel_buf[n_pages] = _kv_ext_t(_new_token_block(new_sel_ref[...]))
        for j in range(1, PPT):
            sel_buf[n_pages + j] = _kv_ext_t(jnp.zeros((KVW, PAGE), BF16))
        nj = past // SEL_BLOCK
        kcp = _compress(lambda c: _strided_rows(cmp_lo, cmp_hi, c, nj), nj, pe_ref, wbd_ref)
        new_win = jnp.concatenate([_new_token_block(new_win_ref[...])]
                                  + [jnp.zeros((KVW, PAGE), F32)] * (PPT - 1), axis=1)
        win_tiles = [(lambda: _kv_ext_t(win_ref[...]), -(WINDOW // LANE), None),
                     (lambda: _kv_ext_t(new_win), 0, None)]
        o = _nsa_core(qb_ref[...], gates_ref[...], kcp, lambda k: _slot_tile(sel_buf, k), win_tiles,
                      strips_ref, cstrips_ref, far_ref, past, TQ_S, TK_S, nj,
                      min(SEL_TOPN, nj + 1) - 1, True, True, True, m_scr, acc_scr)
        o_ref[...] = o.astype(o_ref.dtype)


def _snsa_call(a, page_table, cmp_t, sel_t, win_t, pr, pe_rows, wbd, strips, cstrips, far):
    bd, n_pages = page_table.shape
    past = n_pages * PAGE
    assert past % TK_S == 0 and win_t.shape[3] == WINDOW == TK_S
    pps = _pages_per_step(n_pages)
    misc = pr["misc"][0]
    rows = lambda w: pl.BlockSpec((None, TQ_S, w), lambda b, p, pt: (b, 0, 0))
    col = lambda w: pl.BlockSpec((None, w, 1), lambda b, p, pt: (b, 0, 0))
    page = lambda j: pl.BlockSpec((None, None, KVW, PAGE),
                                  lambda b, p, pt: (a, pt[b * n_pages + p * pps + j], 0, 0))
    const = lambda x: pl.BlockSpec(x.shape, lambda b, p, pt: (0,) * x.ndim)
    grid_spec = pltpu.PrefetchScalarGridSpec(
        num_scalar_prefetch=1,
        grid=(bd, n_pages // pps),
        in_specs=[page(j) for j in range(pps)] + [page(j) for j in range(pps)]
        + [rows(512), rows(3 * B_HEADS), col(KVW),
           pl.BlockSpec((None, None, KVW, WINDOW), lambda b, p, pt: (a, b, 0, 0)), col(KVW),
           const(pe_rows), const(wbd), const(strips), const(cstrips),
           pl.BlockSpec(memory_space=pltpu.SMEM)],
        out_specs=rows(512),
        scratch_shapes=[pltpu.VMEM((past, LANE), F32), pltpu.VMEM((past, LANE), F32),
                        pltpu.VMEM((n_pages + PPT, KVX, PAGE), BF16)] + _flash_scratch(TQ_S))
    out = pl.pallas_call(
        functools.partial(_snsa_kernel, n_pages=n_pages, pps=pps),
        grid_spec=grid_spec,
        out_shape=jax.ShapeDtypeStruct((bd, TQ_S, B_HEADS * HEAD_DIM), BF16),
        compiler_params=_cparams(("arbitrary", "arbitrary")),
        name="nsa_sample",
    )(page_table.reshape(-1), *([cmp_t] * pps), *([sel_t] * pps),
      _sample_rows(pr["qb"][0]), _sample_rows(misc[:, M_GATE:M_GATE + 3 * B_HEADS]),
      pr["kvs"][0][:, :, None], win_t, pr["kvw"][0][:, :, None],
      pe_rows, wbd, strips, cstrips, far)
    return out[:, 0]


def _merge_kernel(x_ref, mod_ref, oa_ref, ob_ref, w_ref, o_ref):
    n = oa_ref.shape[-1]
    y = _mm(oa_ref[...], w_ref[:n]) + _mm(ob_ref[...], w_ref[n:])
    o_ref[...] = x_ref[...] + (1.0 + mod_ref[2]) * y


def _merge_call(x, mod, oa, ob, w, tm):
    nb, t, d = x.shape
    tmod = mod.shape[2]
    mod_spec = (pl.BlockSpec((None, 3, 1, d), lambda b, i: (b, 0, 0, 0)) if tmod == 1 else
                pl.BlockSpec((None, 3, tm, d), lambda b, i: (b, 0, i, 0)))
    row = lambda w_: pl.BlockSpec((None, tm, w_), lambda b, i: (b, i, 0))
    return pl.pallas_call(
        _merge_kernel,
        grid=(nb, t // tm),
        in_specs=[row(d), mod_spec, row(oa.shape[-1]), row(ob.shape[-1]),
                  pl.BlockSpec(w.shape, lambda b, i: (0, 0))],
        out_specs=row(d),
        out_shape=jax.ShapeDtypeStruct(x.shape, F32),
        compiler_params=_cparams(("arbitrary", "arbitrary")),
        name="merge",
    )(x, mod, oa, ob, w)


def _gelu_ln(h, w_in_ref, ln_ref, cw):
    uv = jax.nn.gelu(_mm(h, w_in_ref[...]))
    u, v = uv[:, :cw], uv[:, cw:]
    mu = jnp.mean(v, axis=-1, keepdims=True)
    var = jnp.mean(jnp.square(v - mu), axis=-1, keepdims=True)
    return u, (v - mu) * lax.rsqrt(var + EPS) * ln_ref[...]


def _pgmlp_kernel(x_ref, mod_ref, g_ref, w_in_ref, ln_ref, ws_ref, bs_ref, w_out_ref, o_ref, *, cw):
    x = x_ref[...]
    h = _modulated_norm(x, g_ref[...], mod_ref).astype(BF16)
    u, v = _gelu_ln(h, w_in_ref, ln_ref, cw)
    gw = cw // C_GROUPS
    rows = []
    for n in range(x.shape[0] // C_CHUNK):
        vb = v[n * C_CHUNK:(n + 1) * C_CHUNK].astype(BF16)
        sv = [_mm(ws_ref[g], vb[:, g * gw:(g + 1) * gw]) + bs_ref[:, g:g + 1] for g in range(C_GROUPS)]
        rows.append(jnp.concatenate(sv, axis=1))
    sv = rows[0] if len(rows) == 1 else jnp.concatenate(rows, axis=0)
    o_ref[...] = x + (1.0 + mod_ref[2]) * _mm(u * sv, w_out_ref[...])


def _pgmlp_call(x, mod, g, w_in, ln_g, ws, bs_t, w_out, tm):
    nb, t, d = x.shape
    cw = w_out.shape[0]
    const = lambda a: pl.BlockSpec(a.shape, lambda b, i: (0,) * a.ndim)
    return pl.pallas_call(
        functools.partial(_pgmlp_kernel, cw=cw),
        grid=(nb, t // tm),
        in_specs=[pl.BlockSpec((None, tm, d), lambda b, i: (b, i, 0)),
                  pl.BlockSpec((None, 3, 1, d), lambda b, i: (b, 0, 0, 0)),
                  pl.BlockSpec((1, d), lambda b, i: (0, 0)),
                  const(w_in), pl.BlockSpec((1, cw), lambda b, i: (0, 0)), const(ws), const(bs_t), const(w_out)],
        out_specs=pl.BlockSpec((None, tm, d), lambda b, i: (b, i, 0)),
        out_shape=jax.ShapeDtypeStruct(x.shape, F32),
        compiler_params=_cparams(("arbitrary", "arbitrary")),
        name="gmlp_prompt",
    )(x, mod, g.reshape(1, d), w_in, ln_g.reshape(1, cw), ws, bs_t, w_out)


def _sgmlp_kernel(x_ref, mod_ref, g_ref, w_in_ref, ln_ref, ws0_ref, bs0_ref, w_out_ref, o_ref, v_ref, *, cw):
    x = x_ref[...]
    h = _modulated_norm(x, g_ref[...], mod_ref).astype(BF16)
    u, v = _gelu_ln(h, w_in_ref, ln_ref, cw)
    v_ref[...] = v
    sv = v.astype(BF16).astype(F32) * ws0_ref[...].astype(F32) + bs0_ref[...]
    o_ref[...] = x + (1.0 + mod_ref[2]) * _mm(u * sv, w_out_ref[...])


def _sgmlp_call(x, mod, g, w_in, ln_g, ws, b_sp, w_out):
    nb, t, d = x.shape
    cw = w_out.shape[0]
    gw = cw // C_GROUPS
    ws0 = jnp.repeat(ws[:, 0, 0], gw).reshape(1, cw)
    bs0 = jnp.repeat(b_sp[:, 0], gw).reshape(1, cw)
    full = lambda a: pl.BlockSpec(a.shape, lambda: (0,) * a.ndim)
    x2, mod2 = x[0], mod[0]
    o, v = pl.pallas_call(
        functools.partial(_sgmlp_kernel, cw=cw),
        in_specs=[full(x2), full(mod2), pl.BlockSpec((1, d), lambda: (0, 0)), full(w_in),
                  pl.BlockSpec((1, cw), lambda: (0, 0)), full(ws0), full(bs0), full(w_out)],
        out_specs=[full(x2), pl.BlockSpec((t, cw), lambda: (0, 0))],
        out_shape=[jax.ShapeDtypeStruct(x2.shape, F32), jax.ShapeDtypeStruct((t, cw), F32)],
        compiler_params=pltpu.CompilerParams(vmem_limit_bytes=VMEM_LIMIT),
        name="gmlp_sample",
    )(x2, mod2, g.reshape(1, d), w_in, ln_g.reshape(1, cw), ws0, bs0, w_out)
    return o[None], v


def kernel(x_prompt, x_sample, cache_a_kv, cache_a_idx, cache_b_cmp_kv, cache_b_sel_kv, state_b_win_kv,
           page_table, c_prompt, c_sample, rel_bias, w_ada, b_ada, norm_g, w_ffn_in, w_ffn_out,
           w_in_att, w_out_att, cmp_pe, w_cmp, w_in_c, ln_c_g, w_sp, b_sp, w_out_c, final_g):
    nbp, s, d = x_prompt.shape
    nbs, tdec, _ = x_sample.shape
    depth = w_ada.shape[0]
    assert tdec == 1 and s % TQ_P == 0 and cache_a_kv.shape[2] == PAGE

    ada = _ada_call(jnp.concatenate([c_prompt, c_sample], axis=0), w_ada, b_ada)
    ada = ada.reshape(depth, nbp + nbs, 3, 3, d)
    mod_p = lambda l, j: ada[l, :nbp, j][:, :, None, :]
    mod_s = lambda l, j: jnp.transpose(ada[l, nbp:, j], (1, 0, 2))[None]

    w_ffn_in_h = w_ffn_in.astype(BF16)
    w_ffn_out_h = w_ffn_out.astype(BF16)
    tab_a, tab_b = rel_bias[:, :A_HEADS], rel_bias[:, A_HEADS:]
    strips_pa, strips_pb = _bias_strips(tab_a, TQ_P), _bias_strips(tab_b, TQ_P)
    strips_sa, strips_sb = _bias_strips(tab_a, TQ_S), _bias_strips(tab_b, TQ_S)
    cstrips_p, cstrips_s = _cmp_strips(tab_b, TQ_P), _cmp_strips(tab_b, TQ_S)
    far_a, far_b = tab_a[N_BUCKETS - 1], tab_b[N_BUCKETS - 1]
    idx_t, akv_t = _pages_t(cache_a_idx), _pages_t(cache_a_kv)
    cmp_t, sel_t, win_t = _pages_t(cache_b_cmp_kv), _pages_t(cache_b_sel_kv), _pages_t(state_b_win_kv)

    xp = x_prompt
    xs = jnp.transpose(x_sample, (1, 0, 2))
    tm_p = 512 if s % 512 == 0 else TQ_P
    tk_p = TK_P if s % TK_P == 0 else TQ_P
    st_p, st_s, st_c = [], [], []
    for l in range(depth):
        last = l == depth - 1
        xp = _ffn_call(xp, mod_p(l, 0), norm_g[l, 0], w_ffn_in_h, w_ffn_out_h, l, 0, None, tm_p)
        xs = _ffn_call(xs, mod_s(l, 0), norm_g[l, 0], w_ffn_in_h, w_ffn_out_h, l, 0, None, nbs)
        if l % 2 == 0:
            a = l // 2
            wp = _rearranged_proj_weight(w_in_att[a])
            w_out = w_out_att[a].astype(BF16)
            pe_rows, wbd = _compress_operands(cmp_pe[a], w_cmp[a])

            pr = _proj_call(xp, mod_p(l, 1), norm_g[l, 1], wp, tm_p)
            oa = _pdsa_call(pr, strips_pa, far_a, tk_p)
            kcp = _pcmp_call(pr["kvc"], pe_rows, wbd)
            ob = _pnsa_call(pr, kcp, strips_pb, cstrips_p, far_b, tk_p)
            xp = _merge_call(xp, mod_p(l, 1), oa, ob, w_out, tm_p)
            kv5 = lambda t: t.reshape(t.shape[0], t.shape[1], 2, A_KV, HEAD_DIM)
            st_p.append((kv5(pr["kva"]), pr["misc"][:, :, M_KI:M_KI + IDX_DIM], kv5(pr["kvc"]),
                         kv5(pr["kvs"]), kv5(pr["kvw"][:, s - min(WINDOW, s):])))

            ps = _proj_call(xs, mod_s(l, 1), norm_g[l, 1], wp, nbs)
            oa = _sdsa_call(a, page_table, idx_t, akv_t, ps, strips_sa, far_a)
            ob = _snsa_call(a, page_table, cmp_t, sel_t, win_t, ps, pe_rows, wbd, strips_sb, cstrips_s, far_b)
            xs = _merge_call(xs, mod_s(l, 1), oa[None], ob[None], w_out, nbs)
            tok = lambda t: t[0].reshape(nbs, 1, 2, A_KV, HEAD_DIM)
            st_s.append((tok(ps["kva"]), ps["misc"][0][:, None, M_KI:M_KI + IDX_DIM], tok(ps["kvc"]),
                         tok(ps["kvs"]),
                         jnp.concatenate([state_b_win_kv[a][:, tdec:], tok(ps["kvw"])], axis=1)))
        else:
            ci = l // 2
            w_in = w_in_c[ci].astype(BF16)
            w_out = w_out_c[ci].astype(BF16)
            ws = (w_sp[ci] * jnp.tril(jnp.ones((C_CHUNK, C_CHUNK), w_sp.dtype))).astype(BF16)
            xp = _pgmlp_call(xp, mod_p(l, 1), norm_g[l, 1], w_in, ln_c_g[ci], ws,
                             jnp.transpose(b_sp[ci]), w_out, 256 if s % 256 == 0 else C_CHUNK)
            xs, v = _sgmlp_call(xs, mod_s(l, 1), norm_g[l, 1], w_in, ln_c_g[ci], ws, b_sp[ci], w_out)
            st_c.append(v[:, None, :])
        fg = final_g if last else None
        xp = _ffn_call(xp, mod_p(l, 2), norm_g[l, 2], w_ffn_in_h, w_ffn_out_h, l, 1, fg, tm_p)
        xs = _ffn_call(xs, mod_s(l, 2), norm_g[l, 2], w_ffn_in_h, w_ffn_out_h, l, 1, fg, nbs)

    stk = lambda sts, i: jnp.stack([st[i] for st in sts])
    return (xp, jnp.transpose(xs, (1, 0, 2)),
            stk(st_p, 0), stk(st_p, 1), stk(st_p, 2), stk(st_p, 3), stk(st_p, 4),
            stk(st_s, 0), stk(st_s, 1), stk(st_s, 2), stk(st_s, 3), stk(st_s, 4),
            jnp.stack(st_c))
```

```python
import functools
import math

import numpy as np
import jax
import jax.numpy as jnp
from jax import lax
from jax.experimental import pallas as pl
from jax.experimental.pallas import tpu as pltpu

F32 = jnp.float32
BF16 = jnp.bfloat16
I32 = jnp.int32

HEAD_DIM = 64
A_HEADS = 8
A_KV = 2
IDX_HEADS = 4
IDX_DIM = 64
IDX_TOPK = 256
B_HEADS = 8
B_KV = 2
CMP_BLOCK = 32
SEL_BLOCK = 64
SEL_TOPN = 16
WINDOW = 512
N_BUCKETS = 32
MAX_DISTANCE = 1024
C_CHUNK = 128
C_GROUPS = 8
EPS = 1e-6
PAGE = 128
GROUP = A_HEADS // A_KV
KVW = 2 * A_KV * HEAD_DIM

LANE = 128
VMEM_LIMIT = 56 * 1024 * 1024
KVX = 3 * LANE

NEG = -1e30
KMIN = -2 ** 31

P_QA, P_QB, P_KVA, P_QI, P_KVC, P_KVS, P_KVW, P_MISC = 0, 512, 1024, 1280, 1536, 1792, 2048, 2304
P_WIDTH = 2432
M_KI, M_WI, M_GATE = 0, IDX_DIM, IDX_DIM + IDX_HEADS

assert A_KV == 2 and B_KV == 2 and A_KV * HEAD_DIM == LANE and IDX_DIM == HEAD_DIM


def _cparams(sem):
    return pltpu.CompilerParams(dimension_semantics=sem, vmem_limit_bytes=VMEM_LIMIT)


def _mm(a, b):
    return jnp.dot(a.astype(BF16), b.astype(BF16), preferred_element_type=F32)


def _mm_nt(a, b):
    return lax.dot_general(a.astype(BF16), b.astype(BF16), (((1,), (1,)), ((), ())),
                           preferred_element_type=F32)


def _modulated_norm(x, g, mod_ref):
    y = x * lax.rsqrt(jnp.mean(x * x, axis=-1, keepdims=True) + EPS) * g
    return y * (1.0 + mod_ref[1]) + mod_ref[0]


def _ada_kernel(c_ref, w_ref, b_ref, o_ref):
    c = c_ref[...]
    o_ref[...] = _mm(c * jax.nn.sigmoid(c), w_ref[...]) + b_ref[...]


def _ada_call(c_all, w_ada, b_ada):
    depth, d, n = w_ada.shape
    r = c_all.shape[0]
    tn = n // 8
    return pl.pallas_call(
        _ada_kernel,
        grid=(depth, n // tn),
        in_specs=[pl.BlockSpec((r, d), lambda l, j: (0, 0)),
                  pl.BlockSpec((None, d, tn), lambda l, j: (l, 0, j)),
                  pl.BlockSpec((None, 1, tn), lambda l, j: (l, 0, j))],
        out_specs=pl.BlockSpec((None, r, tn), lambda l, j: (l, 0, j)),
        out_shape=jax.ShapeDtypeStruct((depth, r, n), F32),
        compiler_params=_cparams(("arbitrary", "arbitrary")),
        name="ada",
    )(c_all, w_ada, b_ada.reshape(depth, 1, n))


def _ffn_kernel(x_ref, mod_ref, g_ref, wa_ref, wb_ref, wo_ref, fg_ref, o_ref, h_scr, acc_scr, *,
                n_f, final_norm):
    f = pl.program_id(2)

    @pl.when(f == 0)
    def _():
        h_scr[...] = _modulated_norm(x_ref[...], g_ref[...], mod_ref).astype(BF16)
        acc_scr[...] = jnp.zeros_like(acc_scr)

    h = h_scr[...]
    a = _mm(h, wa_ref[...])
    b = _mm(h, wb_ref[...])
    acc_scr[...] += _mm(a * jax.nn.sigmoid(a) * b, wo_ref[...])

    @pl.when(f == n_f - 1)
    def _():
        y = x_ref[...] + 0.5 * (1.0 + mod_ref[2]) * acc_scr[...]
        if final_norm:
            y = y * lax.rsqrt(jnp.mean(y * y, axis=-1, keepdims=True) + EPS) * fg_ref[...]
        o_ref[...] = y


def _ffn_call(x, mod, g, w_in, w_out, l, j, final_g, tm):
    nb, t, d = x.shape
    dff = w_out.shape[2]
    tf = dff // 2 if dff % (2 * LANE) == 0 else LANE
    n_f = dff // tf
    tmod = mod.shape[2]
    mod_spec = (pl.BlockSpec((None, 3, 1, d), lambda b, i, f: (b, 0, 0, 0)) if tmod == 1 else
                pl.BlockSpec((None, 3, tm, d), lambda b, i, f: (b, 0, i, 0)))
    final_norm = final_g is not None
    fg = (final_g if final_norm else g).reshape(1, d)
    return pl.pallas_call(
        functools.partial(_ffn_kernel, n_f=n_f, final_norm=final_norm),
        grid=(nb, t // tm, n_f),
        in_specs=[pl.BlockSpec((None, tm, d), lambda b, i, f: (b, i, 0)),
                  mod_spec,
                  pl.BlockSpec((1, d), lambda b, i, f: (0, 0)),
                  pl.BlockSpec((None, None, d, tf), lambda b, i, f: (l, j, 0, f)),
                  pl.BlockSpec((None, None, d, tf), lambda b, i, f: (l, j, 0, f + n_f)),
                  pl.BlockSpec((None, None, tf, d), lambda b, i, f: (l, j, f, 0)),
                  pl.BlockSpec((1, d), lambda b, i, f: (0, 0))],
        out_specs=pl.BlockSpec((None, tm, d), lambda b, i, f: (b, i, 0)),
        out_shape=jax.ShapeDtypeStruct(x.shape, F32),
        scratch_shapes=[pltpu.VMEM((tm, d), BF16), pltpu.VMEM((tm, d), F32)],
        compiler_params=_cparams(("arbitrary", "arbitrary", "arbitrary")),
        name="ffn",
    )(x, mod, g.reshape(1, d), w_in, w_in, w_out, fg)


def _kv_ext(y):
    e = jnp.where(lax.broadcasted_iota(I32, (y.shape[0], HEAD_DIM), 1) == 0, 1.0, 0.0).astype(BF16)
    y = y.astype(BF16)
    return jnp.concatenate([y[:, :LANE], y[:, LANE:LANE + HEAD_DIM], e, y[:, LANE + HEAD_DIM:], e], axis=1)


def _kv_ext_t(y):
    e = jnp.where(lax.broadcasted_iota(I32, (HEAD_DIM, y.shape[1]), 0) == 0, 1.0, 0.0).astype(BF16)
    y = y.astype(BF16)
    return jnp.concatenate([y[:LANE], y[LANE:LANE + HEAD_DIM], e, y[LANE + HEAD_DIM:], e], axis=0)


_PROJ_OUT = (("qa", P_QA, 512, "h"), ("qb", P_QB, 512, "h"), ("qi", P_QI, 256, "h"),
             ("kva", P_KVA, 256, "f"), ("kvc", P_KVC, 256, "f"), ("kvs", P_KVS, 256, "f"),
             ("kvw", P_KVW, 256, "f"), ("misc", P_MISC, 128, "f"),
             ("kva_x", P_KVA, KVX, "x"), ("kvs_x", P_KVS, KVX, "x"), ("kvw_x", P_KVW, KVX, "x"),
             ("misc_h", P_MISC, 128, "h"))


def _proj_kernel(x_ref, mod_ref, g_ref, w_ref, *o_refs):
    h = _modulated_norm(x_ref[...], g_ref[...], mod_ref).astype(BF16)
    done = {}
    for (name, off, width, kind), o_ref in zip(_PROJ_OUT, o_refs):
        if off not in done:
            y = _mm(h, w_ref[:, off:off + (KVW if kind == "x" else width)])
            if name in ("qa", "qb"):
                y = y * (HEAD_DIM ** -0.5)
            done[off] = y
        o_ref[...] = _kv_ext(done[off]) if kind == "x" else done[off].astype(o_ref.dtype)


def _proj_call(x, mod, g, wp, tm):
    nb, t, d = x.shape
    tmod = mod.shape[2]
    mod_spec = (pl.BlockSpec((None, 3, 1, d), lambda b, i: (b, 0, 0, 0)) if tmod == 1 else
                pl.BlockSpec((None, 3, tm, d), lambda b, i: (b, 0, i, 0)))
    outs = pl.pallas_call(
        _proj_kernel,
        grid=(nb, t // tm),
        in_specs=[pl.BlockSpec((None, tm, d), lambda b, i: (b, i, 0)),
                  mod_spec,
                  pl.BlockSpec((1, d), lambda b, i: (0, 0)),
                  pl.BlockSpec((d, P_WIDTH), lambda b, i: (0, 0))],
        out_specs=[pl.BlockSpec((None, tm, w), lambda b, i: (b, i, 0)) for _, _, w, _ in _PROJ_OUT],
        out_shape=[jax.ShapeDtypeStruct((nb, t, w), F32 if kind == "f" else BF16)
                   for _, _, w, kind in _PROJ_OUT],
        compiler_params=_cparams(("arbitrary", "arbitrary")),
        name="proj",
    )(x, mod, g.reshape(1, d), wp)
    return {name: o for (name, _, _, _), o in zip(_PROJ_OUT, outs)}


def _rearranged_proj_weight(w):
    d = w.shape[0]
    o = np.cumsum((0, 512, 256, 256, 64, 4, 512, 256, 256, 256, 24))
    seg = lambda i: w[:, o[i]:o[i + 1]]
    qa, kva, qi, ki, wi, qb, kvc, kvs, kvw, gates = [seg(i) for i in range(10)]
    pad = jnp.zeros((d, LANE - IDX_DIM - IDX_HEADS - 3 * B_HEADS), w.dtype)
    return jnp.concatenate([qa, qb, kva, qi, kvc, kvs, kvw, ki, wi, gates, pad], axis=1).astype(BF16)


def _t5_bucket(dist):
    n = jnp.maximum(dist, 0)
    max_exact = N_BUCKETS // 2
    nf = jnp.maximum(n, 1).astype(F32)
    large = max_exact + (jnp.log(nf / max_exact) / math.log(MAX_DISTANCE / max_exact)
                         * (N_BUCKETS - max_exact)).astype(I32)
    large = jnp.minimum(large, N_BUCKETS - 1)
    return jnp.where(n < max_exact, n, large)


def _bias_lookup(tab, dist):
    b = _t5_bucket(jnp.asarray(dist.reshape(-1), I32))
    onehot = (b[:, None] == jnp.arange(N_BUCKETS, dtype=I32)[None, :]).astype(F32)
    out = jnp.dot(onehot, tab, precision=lax.Precision.HIGHEST)
    return jnp.transpose(out).reshape((tab.shape[1],) + dist.shape)


NEAR = -(-(MAX_DISTANCE + LANE - 1) // LANE)


def _bias_strips(tab, tq):
    r = np.arange(tq)[None, :, None]
    c = np.arange(LANE)[None, None, :]
    bd = np.arange(NEAR + 1)[:, None, None]
    dist = np.where(bd == NEAR, MAX_DISTANCE, r - c + LANE * bd)
    return _bias_lookup(tab, dist)


def _cmp_near_count(delta):
    return -(-(-(-(MAX_DISTANCE - delta) // CMP_BLOCK)) // 8) * 8


def _cmp_delta(tq):
    return -(CMP_BLOCK - 1) - CMP_BLOCK * ((tq - CMP_BLOCK) // CMP_BLOCK)


def _cmp_strips(tab, tq):
    delta = _cmp_delta(tq)
    ncn = _cmp_near_count(delta)
    dist = np.arange(tq)[:, None] + delta + CMP_BLOCK * np.arange(ncn)[None, :]
    v = _bias_lookup(tab, dist)
    hi = v.astype(BF16)
    r1 = v - hi.astype(F32)
    mid = r1.astype(BF16)
    lo = (r1 - mid.astype(F32)).astype(BF16)
    return jnp.stack([hi, mid, lo], axis=1)


def _padded_heads(x, g):
    z = jnp.zeros((x.shape[0], HEAD_DIM), x.dtype)
    rows = []
    for r in range(GROUP):
        h = g * GROUP + r
        piece = x[:, h * HEAD_DIM:(h + 1) * HEAD_DIM]
        rows.append(jnp.concatenate([piece, z] if g == 0 else [z, piece], axis=1))
    return jnp.concatenate(rows, axis=0)


def _k_part(kv, kv_t):
    return kv[:LANE] if kv_t else kv[:, :LANE]


def _v_part(kv, g, kv_t):
    return kv[(1 + g) * LANE:(2 + g) * LANE] if kv_t else kv[:, (1 + g) * LANE:(2 + g) * LANE]


def _flash_init(scr, tq, in_values):
    if in_values:
        return tuple((jnp.full((GROUP, tq, LANE), NEG, F32), jnp.zeros((GROUP * tq, LANE), F32))
                     for _ in range(A_KV))
    scr[0][...] = jnp.full_like(scr[0], NEG)
    scr[1][...] = jnp.zeros_like(scr[1])
    return 0


def _flash_store(st, scr):
    if isinstance(st, tuple):
        for g, (m, acc) in enumerate(st):
            scr[0][g] = m
            scr[1][g] = acc


def _mask_add(mask):
    return jnp.where(mask, 0.0, NEG)


def _flash_update(st, scr, g, qp_g, kv, strips_ref, far_ref, bds, madd, tq, kv_t):
    m_all, acc = st[g] if isinstance(st, tuple) else (scr[0][g], scr[1][g])
    kb = _k_part(kv, kv_t)
    s_all = _mm(qp_g, kb) if kv_t else _mm_nt(qp_g, kb)
    nblk = madd.shape[1] // LANE
    ps, alphas, m_news = [], [], []
    for r in range(GROUP):
        h = g * GROUP + r
        blocks = []
        for j in range(nblk):
            sj = s_all[r * tq:(r + 1) * tq, j * LANE:(j + 1) * LANE] + madd[:, j * LANE:(j + 1) * LANE]
            blocks.append(sj if bds is None else sj + strips_ref[h, bds[j]])
        bmax = blocks[0]
        for sj in blocks[1:]:
            bmax = jnp.maximum(bmax, sj)
        row_max = jnp.max(bmax, axis=-1, keepdims=True)
        m_old = m_all[r]
        if bds is None:
            m_new = jnp.maximum(m_old, row_max + far_ref[h])
            shift = m_new - far_ref[h]
        else:
            m_new = jnp.maximum(m_old, row_max)
            shift = m_new
        m_news.append(m_new)
        alphas.append(jnp.exp(m_old - m_new))
        ps.append(jnp.concatenate([jnp.exp(sj - shift).astype(BF16) for sj in blocks], axis=1))
    p = jnp.concatenate(ps, axis=0)
    vx = _v_part(kv, g, kv_t)
    pv = _mm_nt(p, vx) if kv_t else _mm(p, vx)
    m_all = jnp.stack(m_news)
    acc = jnp.concatenate(alphas, axis=0) * acc + pv
    if isinstance(st, tuple):
        return st[:g] + ((m_all, acc),) + st[g + 1:]
    scr[0][g] = m_all
    scr[1][g] = acc
    return st


def _flash_result(scr, g):
    acc = scr[1][g]
    l = acc[:, HEAD_DIM:HEAD_DIM + 1]
    return jnp.where(l > 0.0, acc[:, :HEAD_DIM] / jnp.where(l > 0.0, l, 1.0), 0.0)


def _tile_bds(qb, k, tk):
    nblk = tk // LANE
    return [jnp.clip(qb - (k * nblk + j), 0, NEAR) for j in range(nblk)]


def _ordered_key(x):
    b = lax.bitcast_convert_type(x, I32)
    return jnp.where(x == 0.0, 0, b ^ ((b >> 31) & 0x7FFFFFFF))


def _far_tiles(qb, tk):
    n = (qb - NEAR + 1) // (tk // LANE)
    return max(n, 0) if isinstance(n, int) else jnp.maximum(n, 0)


def _dsa_core(qi, wi, qa, get_kidx, get_kva, strips_ref, far_ref, t0, tq, tk, topk, kv_t, unroll,
              keys_scr, wib_scr, m_scr, acc_scr, tri_scr):
    n_tiles = (t0 + tq - 1) // tk + 1
    qb = t0 // LANE
    n_far = _far_tiles(qb, tk)
    row_pos = t0 + lax.broadcasted_iota(I32, (tq, tk), 0)
    col = lax.broadcasted_iota(I32, (tq, tk), 1)

    if kv_t:
        qi_st = jnp.concatenate([qi[:, h * IDX_DIM:(h + 1) * IDX_DIM] for h in range(IDX_HEADS)], axis=0)
    else:
        z = jnp.zeros((tq, LANE - IDX_DIM), qi.dtype)
        qi_st = jnp.concatenate(
            [jnp.concatenate([qi[:, h * IDX_DIM:(h + 1) * IDX_DIM], z], axis=1) for h in range(IDX_HEADS)], axis=0)
    for h in range(IDX_HEADS):
        wib_scr[h] = jnp.broadcast_to(wi[:, h:h + 1], (tq, tk))

    def score_tile(k, c):
        kt = get_kidx(k)
        sc = _mm(qi_st, kt) if kv_t else _mm_nt(qi_st, kt)
        score = jnp.maximum(sc[:tq], 0.0) * wib_scr[0]
        for h in range(1, IDX_HEADS):
            score = score + jnp.maximum(sc[h * tq:(h + 1) * tq], 0.0) * wib_scr[h]
        keys_scr[k] = jnp.where(k * tk + col <= row_pos, _ordered_key(score), KMIN)
        return c

    lax.fori_loop(0, n_tiles, score_tile, 0, unroll=unroll)

    def count(pred):
        def count_tile(k, cnt):
            key = keys_scr[k]
            for j in range(tk // LANE):
                cnt = cnt + jnp.where(pred(key[:, j * LANE:(j + 1) * LANE]), 1.0, 0.0)
            return cnt

        cnt = lax.fori_loop(0, n_tiles, count_tile, jnp.zeros((tq, LANE), F32))
        return jnp.sum(cnt, axis=-1, keepdims=True)

    tu = jnp.zeros((tq, 1), I32)
    n_ge = jnp.zeros((tq, 1), F32)
    for bit in range(31, -1, -1):
        cand_u = tu | np.int32(-2 ** 31 if bit == 31 else 2 ** bit)
        cand_s = jnp.broadcast_to(cand_u ^ np.int32(KMIN), (tq, LANE))
        tot = count(lambda key, cand_s=cand_s: key >= cand_s)
        tu = jnp.where(tot >= topk, cand_u, tu)
        n_ge = jnp.where(tot >= topk, tot, n_ge)
    thr = tu ^ np.int32(KMIN)
    found = thr > np.int32(KMIN)
    has_ties = jnp.max(jnp.where(found & (n_ge > topk), 1.0, 0.0)) > 0.5

    qp_g = [_padded_heads(qa, g) for g in range(A_KV)]
    scr = (m_scr, acc_scr)
    in_values = bool(unroll)

    def attend(k, st, mask, far):
        kv = get_kva(k)
        bds = None if far else _tile_bds(qb, k, tk)
        madd = _mask_add(mask)
        for g in range(A_KV):
            st = _flash_update(st, scr, g, qp_g[g], kv, strips_ref, far_ref, bds, madd, tq, kv_t)
        return st

    @pl.when(jnp.logical_not(has_ties))
    def _():
        thr_ge = jnp.where(found, thr, np.int32(KMIN + 1))

        def tile(k, st, far):
            return attend(k, st, keys_scr[k] >= thr_ge, far)

        st = _flash_init(scr, tq, in_values)
        st = lax.fori_loop(0, n_far, functools.partial(tile, far=True), st, unroll=unroll)
        st = lax.fori_loop(n_far, n_tiles, functools.partial(tile, far=False), st, unroll=unroll)
        _flash_store(st, scr)

    @pl.when(has_ties)
    def _():
        need = topk - count(lambda key: key > jnp.broadcast_to(thr, (tq, LANE)))
        tri_scr[...] = jnp.where(
            lax.broadcasted_iota(I32, (LANE, LANE), 0) <= lax.broadcasted_iota(I32, (LANE, LANE), 1),
            1.0, 0.0).astype(BF16)

        def tile(k, c):
            eq_seen, st = c
            key = keys_scr[k]
            eq = key == thr
            takes = []
            for j in range(tk // LANE):
                eq_j = eq[:, j * LANE:(j + 1) * LANE]
                rank = eq_seen + _mm(jnp.where(eq_j, 1.0, 0.0), tri_scr[...])
                takes.append(eq_j & (rank <= need))
                eq_seen = rank[:, LANE - 1:LANE]
            take = jnp.concatenate(takes, axis=1)
            st = attend(k, st, (key > thr) | (take & found), False)
            return eq_seen, st

        _, st = lax.fori_loop(0, n_tiles, tile, (jnp.zeros((tq, 1), F32), _flash_init(scr, tq, in_values)))
        _flash_store(st, scr)

    outs = []
    for g in range(A_KV):
        o = _flash_result(scr, g)
        outs += [o[r * tq:(r + 1) * tq] for r in range(GROUP)]
    return jnp.concatenate(outs, axis=1)


def _pick_blocks(v, n_pick, axis):
    lane = lax.broadcasted_iota(I32, v.shape, axis).astype(F32)
    sel = jnp.zeros(v.shape, jnp.bool_)
    for _ in range(n_pick):
        mx = jnp.max(v, axis=axis, keepdims=True)
        cand = (v == mx) & jnp.logical_not(sel)
        first = jnp.min(jnp.where(cand, lane, float(v.shape[axis])), axis=axis, keepdims=True)
        pick = lane == first
        sel = sel | pick
        v = jnp.where(pick, -jnp.inf, v)
    return sel


def _nsa_core(qb, gates, kcp, get_kvs, win_tiles, strips_ref, cstrips_ref, far_ref, t0, tq, tk, nj,
              n_pick, extra_block, kv_t, unroll, m_scr, acc_scr):
    n_tiles = (t0 + tq - 1) // tk + 1
    qblk = t0 // LANE
    delta = _cmp_delta(tq)
    ncn = cstrips_ref.shape[-1]
    qp_g = [_padded_heads(qb, g) for g in range(B_KV)]
    row_pos1 = t0 + lax.broadcasted_iota(I32, (tq, 1), 0)

    lam = lax.broadcasted_iota(I32, (tq, 2 * nj), 1)
    m_of = 2 * (lam % nj) + lam // nj
    cdist = row_pos1 - (CMP_BLOCK * m_of + CMP_BLOCK - 1)
    cmask = cdist >= 0
    m_hi = (t0 - (CMP_BLOCK - 1) - delta) // CMP_BLOCK
    lam_s = lax.broadcasted_iota(I32, (ncn, 2 * nj), 1)
    shift = jnp.where(2 * (lam_s % nj) + lam_s // nj == m_hi - lax.broadcasted_iota(I32, (ncn, 2 * nj), 0),
                      1.0, 0.0).astype(BF16)
    is_far = m_of <= m_hi - ncn
    kc_both = kcp[:, :LANE]
    o_c, imp = [], []
    for g in range(B_KV):
        bias = []
        for r in range(GROUP):
            h = g * GROUP + r
            b = (_mm(cstrips_ref[h, 0], shift) + _mm(cstrips_ref[h, 1], shift)) + _mm(cstrips_ref[h, 2], shift)
            bias.append(jnp.where(is_far, far_ref[h], b))
        s = _mm_nt(qp_g[g], kc_both).reshape(GROUP, tq, 2 * nj) + jnp.stack(bias)
        s = jnp.where(cmask[None], s, NEG)
        e = jnp.where(cmask[None], jnp.exp(s - jnp.max(s, axis=-1, keepdims=True)), 0.0)
        l = jnp.sum(e, axis=-1, keepdims=True)
        p = jnp.where(l > 0.0, e / jnp.where(l > 0.0, l, 1.0), 0.0)
        o_c.append(_mm(p.reshape(GROUP * tq, 2 * nj), kcp[:, (B_KV + g) * HEAD_DIM:(B_KV + g + 1) * HEAD_DIM]))
        ps = p[0]
        for r in range(1, GROUP):
            ps = ps + p[r]
        imp.append(ps[:, :nj] + ps[:, nj:])

    jl = lax.broadcasted_iota(I32, (tq, nj), 1)
    cur = row_pos1 // SEL_BLOCK
    forced = (jl == 0) | (jl == cur) | (jl == cur - 1)
    admissible = jl <= cur
    vs = [jnp.where(admissible, jnp.where(forced, jnp.inf, imp[g]), -jnp.inf) for g in range(B_KV)]
    if tq % LANE == 0:
        picked_t = _pick_blocks(jnp.concatenate([jnp.transpose(v) for v in vs], axis=1), n_pick, 0)
        picked = [jnp.transpose(jnp.where(picked_t[:, g * tq:(g + 1) * tq], 1.0, 0.0)) > 0.5
                  for g in range(B_KV)]
    else:
        picked = [_pick_blocks(v, n_pick, 1) for v in vs]
    bmask = [jnp.where(picked[g] & admissible, 1.0, 0.0).astype(BF16) for g in range(B_KV)]
    row_pos = t0 + lax.broadcasted_iota(I32, (tq, tk), 0)
    col = lax.broadcasted_iota(I32, (tq, tk), 1)
    ej = lax.broadcasted_iota(I32, (nj, tk), 0)
    ec = lax.broadcasted_iota(I32, (nj, tk), 1)
    scr = (m_scr, acc_scr)
    in_values = bool(unroll)

    def sel_tile(k, st, far):
        kv = get_kvs(k)
        expand = jnp.where(ej == (k * tk + ec) // SEL_BLOCK, 1.0, 0.0).astype(BF16)
        for g in range(B_KV):
            mask = _mm(bmask[g], expand) > 0.5
            if far:
                bds = None
            else:
                bds = _tile_bds(qblk, k, tk)
                key_pos = k * tk + col
                if extra_block:
                    mask = mask | (key_pos >= nj * SEL_BLOCK)
                mask = mask & (key_pos <= row_pos)
            st = _flash_update(st, scr, g, qp_g[g], kv, strips_ref, far_ref, bds, _mask_add(mask), tq, kv_t)
        return st

    n_far = _far_tiles(qblk, tk)
    st = _flash_init(scr, tq, in_values)
    st = lax.fori_loop(0, n_far, functools.partial(sel_tile, far=True), st, unroll=unroll)
    st = lax.fori_loop(n_far, n_tiles, functools.partial(sel_tile, far=False), st, unroll=unroll)
    _flash_store(st, scr)
    o_s = [_flash_result(scr, g) for g in range(B_KV)]

    st = _flash_init(scr, tq, in_values)
    for get_tile, off, guard in win_tiles:
        def win_tile(st, get_tile=get_tile, off=off):
            kv = get_tile()
            wt = kv.shape[1] if kv_t else kv.shape[0]
            dist = (lax.broadcasted_iota(I32, (tq, wt), 0) - lax.broadcasted_iota(I32, (tq, wt), 1)
                    - LANE * off)
            wadd = _mask_add((dist >= 0) & (dist <= WINDOW))
            bds = [min(max(-off - j, 0), NEAR) for j in range(wt // LANE)]
            for g in range(B_KV):
                st = _flash_update(st, scr, g, qp_g[g], kv, strips_ref, far_ref, bds, wadd, tq, kv_t)
            return st

        if guard is None:
            st = win_tile(st)
        else:
            assert not in_values

            def guarded(win_tile=win_tile):
                win_tile(0)

            pl.when(guard)(guarded)
    _flash_store(st, scr)
    o_w = [_flash_result(scr, g) for g in range(B_KV)]

    gs = jax.nn.sigmoid(gates)
    outs = []
    for g in range(B_KV):
        gcol = lambda j: jnp.concatenate(
            [gs[:, (g * GROUP + r) * 3 + j:(g * GROUP + r) * 3 + j + 1] for r in range(GROUP)], axis=0)
        o = gcol(0) * o_c[g] + gcol(1) * o_s[g] + gcol(2) * o_w[g]
        outs += [o[r * tq:(r + 1) * tq] for r in range(GROUP)]
    return jnp.concatenate(outs, axis=1)


def _compress(load_rows, n_pairs, pe_ref, wbd_ref):
    acc = [jnp.zeros((n_pairs, KVW), F32), jnp.zeros((n_pairs, KVW), F32)]
    for c in range(2 * CMP_BLOCK):
        ci = c % CMP_BLOCK
        acc[c // CMP_BLOCK] = acc[c // CMP_BLOCK] + _mm(load_rows(c) + pe_ref[ci], wbd_ref[ci])
    return jnp.concatenate(acc, axis=0)


def _compress_operands(pe, w):
    pe_rows = jnp.transpose(jnp.broadcast_to(pe[:, None], (2, B_KV, CMP_BLOCK, HEAD_DIM)), (2, 0, 1, 3))
    pe_rows = pe_rows.reshape(CMP_BLOCK, 1, KVW)
    eye = jnp.eye(2 * B_KV, dtype=w.dtype).reshape(2, B_KV, 2, B_KV)
    wbd = jnp.einsum("icde,igjh->cigdjhe", w, eye).reshape(CMP_BLOCK, KVW, KVW)
    return pe_rows, wbd.astype(BF16)


def _strided_rows(lo_ref, hi_ref, c, n):
    rows = pl.ds(c, n, stride=2 * CMP_BLOCK)
    return jnp.concatenate([lo_ref[rows, :], hi_ref[rows, :]], axis=1)


def _flash_scratch(tq):
    return [pltpu.VMEM((A_KV, GROUP, tq, LANE), F32), pltpu.VMEM((A_KV, GROUP * tq, LANE), F32)]


TQ_P = 128
TK_P = 512


def _prompt_tile(ref, k, tk):
    return ref[pl.ds(pl.multiple_of(k * tk, tk), tk), :]


def _pdsa_kernel(qi_ref, misc_ref, qa_ref, kidx_ref, kva_ref, strips_ref, far_ref, o_ref,
                 keys_scr, wib_scr, m_scr, acc_scr, tri_scr, *, topk, tk):
    i = pl.program_id(1)
    o = _dsa_core(qi_ref[...], misc_ref[:, M_WI:M_WI + IDX_HEADS], qa_ref[...],
                  lambda k: _prompt_tile(kidx_ref, k, tk), lambda k: _prompt_tile(kva_ref, k, tk),
                  strips_ref, far_ref, i * TQ_P, TQ_P, tk, topk, False, None,
                  keys_scr, wib_scr, m_scr, acc_scr, tri_scr)
    o_ref[...] = o.astype(o_ref.dtype)


def _pdsa_call(pr, strips, far, tk):
    nb, s, _ = pr["qa"].shape
    tq = TQ_P
    row = lambda w: pl.BlockSpec((None, tq, w), lambda b, i: (b, i, 0))
    full = lambda w: pl.BlockSpec((None, s, w), lambda b, i: (b, 0, 0))
    return pl.pallas_call(
        functools.partial(_pdsa_kernel, topk=min(IDX_TOPK, s // 4), tk=tk),
        grid=(nb, s // tq),
        in_specs=[row(256), row(128), row(512), full(128), full(KVX),
                  pl.BlockSpec(strips.shape, lambda b, i: (0, 0, 0, 0)),
                  pl.BlockSpec(memory_space=pltpu.SMEM)],
        out_specs=row(512),
        out_shape=jax.ShapeDtypeStruct((nb, s, A_HEADS * HEAD_DIM), BF16),
        scratch_shapes=[pltpu.VMEM((s // tk, tq, tk), I32), pltpu.VMEM((IDX_HEADS, tq, tk), F32)]
        + _flash_scratch(tq) + [pltpu.VMEM((LANE, LANE), BF16)],
        compiler_params=_cparams(("arbitrary", "arbitrary")),
        name="dsa_prompt",
    )(pr["qi"], pr["misc"], pr["qa"], pr["misc_h"], pr["kva_x"], strips, far)


def _pcmp_kernel(lo_ref, hi_ref, pe_ref, wbd_ref, o_ref, *, n_pairs):
    o_ref[...] = _compress(lambda c: _strided_rows(lo_ref, hi_ref, c, n_pairs), n_pairs, pe_ref, wbd_ref)


def _pcmp_call(kvc, pe_rows, wbd):
    nb, s, _ = kvc.shape
    n_pairs = s // (2 * CMP_BLOCK)
    return pl.pallas_call(
        functools.partial(_pcmp_kernel, n_pairs=n_pairs),
        grid=(nb,),
        in_specs=[pl.BlockSpec((None, s, LANE), lambda b: (b, 0, 0)),
                  pl.BlockSpec((None, s, LANE), lambda b: (b, 0, 1)),
                  pl.BlockSpec((CMP_BLOCK, 1, KVW), lambda b: (0, 0, 0)),
                  pl.BlockSpec((CMP_BLOCK, KVW, KVW), lambda b: (0, 0, 0))],
        out_specs=pl.BlockSpec((None, 2 * n_pairs, KVW), lambda b: (b, 0, 0)),
        out_shape=jax.ShapeDtypeStruct((nb, 2 * n_pairs, KVW), F32),
        compiler_params=_cparams(("arbitrary",)),
        name="compress_prompt",
    )(kvc, kvc, pe_rows, wbd)


def _pnsa_kernel(qb_ref, misc_ref, kcp_ref, kvs_ref, kvw_ref, strips_ref, cstrips_ref, far_ref, o_ref,
                 m_scr, acc_scr, *, nj, n_pick, tk):
    i = pl.program_id(1)
    nw = WINDOW // LANE
    win_tiles = [(lambda: kvw_ref[pl.ds(pl.multiple_of((i - nw) * LANE, LANE), WINDOW + TQ_P), :], -nw, i >= nw)]
    win_tiles += [(lambda wd=wd: _prompt_tile(kvw_ref, i - wd, LANE), -wd, (i < nw) & (i - wd >= 0))
                  for wd in range(nw - 1, -1, -1)]
    o = _nsa_core(qb_ref[...], misc_ref[:, M_GATE:M_GATE + 3 * B_HEADS], kcp_ref[...],
                  lambda k: _prompt_tile(kvs_ref, k, tk), win_tiles,
                  strips_ref, cstrips_ref, far_ref, i * TQ_P, TQ_P, tk, nj, n_pick, False, False, None,
                  m_scr, acc_scr)
    o_ref[...] = o.astype(o_ref.dtype)


def _pnsa_call(pr, kcp, strips, cstrips, far, tk):
    nb, s, _ = pr["qb"].shape
    tq = TQ_P
    nj = s // SEL_BLOCK
    row = lambda w: pl.BlockSpec((None, tq, w), lambda b, i: (b, i, 0))
    full = lambda n, w: pl.BlockSpec((None, n, w), lambda b, i: (b, 0, 0))
    const = lambda a: pl.BlockSpec(a.shape, lambda b, i: (0,) * a.ndim)
    return pl.pallas_call(
        functools.partial(_pnsa_kernel, nj=nj, n_pick=min(SEL_TOPN, nj), tk=tk),
        grid=(nb, s // tq),
        in_specs=[row(512), row(128), full(2 * nj, KVW), full(s, KVX), full(s, KVX),
                  const(strips), const(cstrips),
                  pl.BlockSpec(memory_space=pltpu.SMEM)],
        out_specs=row(512),
        out_shape=jax.ShapeDtypeStruct((nb, s, B_HEADS * HEAD_DIM), BF16),
        scratch_shapes=_flash_scratch(tq),
        compiler_params=_cparams(("arbitrary", "arbitrary")),
        name="nsa_prompt",
    )(pr["qb"], pr["misc"], kcp, pr["kvs_x"], pr["kvw_x"], strips, cstrips, far)


TQ_S = 8
TK_S = 2048
PPT = TK_S // PAGE


def _pages_per_step(n_pages):
    return max(g for g in (8, 4, 2, 1) if n_pages % g == 0)


def _new_token_block(col):
    return jnp.where(lax.broadcasted_iota(I32, (col.shape[0], PAGE), 1) == 0, col, 0.0)


def _slot_tile(buf, k):
    blk = buf[pl.ds(k * PPT, PPT)]
    return jnp.concatenate([blk[j] for j in range(PPT)], axis=1)


def _sdsa_kernel(*refs, n_pages, pps, topk):
    idx_refs = refs[1:1 + pps]
    akv_refs = refs[1 + pps:1 + 2 * pps]
    (qi_ref, wi_ref, qa_ref, new_idx_ref, new_kv_ref, strips_ref, far_ref, o_ref,
     idx_buf, akv_buf, keys_scr, wib_scr, m_scr, acc_scr, tri_scr) = refs[1 + 2 * pps:]
    p = pl.program_id(1)
    for j in range(pps):
        idx_buf[p * pps + j] = idx_refs[j][...].astype(BF16)
        akv_buf[p * pps + j] = _kv_ext_t(akv_refs[j][...])

    @pl.when(p == n_pages // pps - 1)
    def _():
        idx_buf[n_pages] = _new_token_block(new_idx_ref[...]).astype(BF16)
        akv_buf[n_pages] = _kv_ext_t(_new_token_block(new_kv_ref[...]))
        for j in range(1, PPT):
            idx_buf[n_pages + j] = jnp.zeros((IDX_DIM, PAGE), BF16)
            akv_buf[n_pages + j] = _kv_ext_t(jnp.zeros((KVW, PAGE), BF16))
        o = _dsa_core(qi_ref[...], wi_ref[...], qa_ref[...],
                      lambda k: _slot_tile(idx_buf, k), lambda k: _slot_tile(akv_buf, k),
                      strips_ref, far_ref, n_pages * PAGE, TQ_S, TK_S, topk, True, True,
                      keys_scr, wib_scr, m_scr, acc_scr, tri_scr)
        o_ref[...] = o.astype(o_ref.dtype)


def _sample_rows(x):
    return jnp.broadcast_to(x[:, None, :], (x.shape[0], TQ_S, x.shape[1]))


def _pages_t(cache):
    n, p = cache.shape[:2]
    nd = cache.ndim
    return jnp.transpose(cache, (0, 1) + tuple(range(3, nd)) + (2,)).reshape(n, p, -1, cache.shape[2])


def _sdsa_call(a, page_table, idx_t, akv_t, pr, strips, far):
    bd, n_pages = page_table.shape
    assert idx_t.shape[3] == PAGE and (n_pages * PAGE) % TK_S == 0
    pps = _pages_per_step(n_pages)
    slots = n_pages + PPT
    misc = pr["misc"][0]
    rows = lambda w: pl.BlockSpec((None, TQ_S, w), lambda b, p, pt: (b, 0, 0))
    col = lambda w: pl.BlockSpec((None, w, 1), lambda b, p, pt: (b, 0, 0))
    page = lambda w, j: pl.BlockSpec((None, None, w, PAGE),
                                     lambda b, p, pt: (a, pt[b * n_pages + p * pps + j], 0, 0))
    grid_spec = pltpu.PrefetchScalarGridSpec(
        num_scalar_prefetch=1,
        grid=(bd, n_pages // pps),
        in_specs=[page(IDX_DIM, j) for j in range(pps)] + [page(KVW, j) for j in range(pps)]
        + [rows(256), rows(IDX_HEADS), rows(512), col(IDX_DIM), col(KVW),
           pl.BlockSpec(strips.shape, lambda b, p, pt: (0, 0, 0, 0)),
           pl.BlockSpec(memory_space=pltpu.SMEM)],
        out_specs=rows(512),
        scratch_shapes=[pltpu.VMEM((slots, IDX_DIM, PAGE), BF16), pltpu.VMEM((slots, KVX, PAGE), BF16),
                        pltpu.VMEM((slots // PPT, TQ_S, TK_S), I32), pltpu.VMEM((IDX_HEADS, TQ_S, TK_S), F32)]
        + _flash_scratch(TQ_S) + [pltpu.VMEM((LANE, LANE), BF16)])
    out = pl.pallas_call(
        functools.partial(_sdsa_kernel, n_pages=n_pages, pps=pps, topk=min(IDX_TOPK, (n_pages * PAGE + 1) // 4)),
        grid_spec=grid_spec,
        out_shape=jax.ShapeDtypeStruct((bd, TQ_S, A_HEADS * HEAD_DIM), BF16),
        compiler_params=_cparams(("arbitrary", "arbitrary")),
        name="dsa_sample",
    )(page_table.reshape(-1), *([idx_t] * pps), *([akv_t] * pps),
      _sample_rows(pr["qi"][0]), _sample_rows(misc[:, M_WI:M_WI + IDX_HEADS]), _sample_rows(pr["qa"][0]),
      misc[:, M_KI:M_KI + IDX_DIM, None], pr["kva"][0][:, :, None], strips, far)
    return out[:, 0]


def _snsa_kernel(*refs, n_pages, pps):
    cmp_refs = refs[1:1 + pps]
    sel_refs = refs[1 + pps:1 + 2 * pps]
    (qb_ref, gates_ref, new_sel_ref, win_ref, new_win_ref, pe_ref, wbd_ref, strips_ref, cstrips_ref, far_ref,
     o_ref, cmp_lo, cmp_hi, sel_buf, m_scr, acc_scr) = refs[1 + 2 * pps:]
    p = pl.program_id(1)
    past = n_pages * PAGE
    for j in range(pps):
        rows = pl.ds(pl.multiple_of((p * pps + j) * PAGE, PAGE), PAGE)
        cmp_lo[rows, :] = jnp.transpose(cmp_refs[j][:LANE])
        cmp_hi[rows, :] = jnp.transpose(cmp_refs[j][LANE:])
        sel_buf[p * pps + j] = _kv_ext_t(sel_refs[j][...])

    @pl.when(p == n_pages // pps - 1)
    def _():
        sel_buf[n_pages] = _kv_ext_t(_new_token_block(new_sel_ref[...]))
        for j in range(1, PPT):
            sel_buf[n_pages + j] = _kv_ext_t(jnp.zeros((KVW, PAGE), BF16))
        nj = past // SEL_BLOCK
        kcp = _compress(lambda c: _strided_rows(cmp_lo, cmp_hi, c, nj), nj, pe_ref, wbd_ref)
        win_tiles = [(lambda: _kv_ext_t(win_ref[...]), -(WINDOW // LANE), None),
                     (lambda: _kv_ext_t(_new_token_block(new_win_ref[...])), 0, None)]
        o = _nsa_core(qb_ref[...], gates_ref[...], kcp, lambda k: _slot_tile(sel_buf, k), win_tiles,
                      strips_ref, cstrips_ref, far_ref, past, TQ_S, TK_S, nj,
                      min(SEL_TOPN, nj + 1) - 1, True, True, True, m_scr, acc_scr)
        o_ref[...] = o.astype(o_ref.dtype)


def _snsa_call(a, page_table, cmp_t, sel_t, win_t, pr, pe_rows, wbd, strips, cstrips, far):
    bd, n_pages = page_table.shape
    past = n_pages * PAGE
    assert past % TK_S == 0 and win_t.shape[3] == WINDOW
    pps = _pages_per_step(n_pages)
    misc = pr["misc"][0]
    rows = lambda w: pl.BlockSpec((None, TQ_S, w), lambda b, p, pt: (b, 0, 0))
    col = lambda w: pl.BlockSpec((None, w, 1), lambda b, p, pt: (b, 0, 0))
    page = lambda j: pl.BlockSpec((None, None, KVW, PAGE),
                                  lambda b, p, pt: (a, pt[b * n_pages + p * pps + j], 0, 0))
    const = lambda x: pl.BlockSpec(x.shape, lambda b, p, pt: (0,) * x.ndim)
    grid_spec = pltpu.PrefetchScalarGridSpec(
        num_scalar_prefetch=1,
        grid=(bd, n_pages // pps),
        in_specs=[page(j) for j in range(pps)] + [page(j) for j in range(pps)]
        + [rows(512), rows(3 * B_HEADS), col(KVW),
           pl.BlockSpec((None, None, KVW, WINDOW), lambda b, p, pt: (a, b, 0, 0)), col(KVW),
           const(pe_rows), const(wbd), const(strips), const(cstrips),
           pl.BlockSpec(memory_space=pltpu.SMEM)],
        out_specs=rows(512),
        scratch_shapes=[pltpu.VMEM((past, LANE), F32), pltpu.VMEM((past, LANE), F32),
                        pltpu.VMEM((n_pages + PPT, KVX, PAGE), BF16)] + _flash_scratch(TQ_S))
    out = pl.pallas_call(
        functools.partial(_snsa_kernel, n_pages=n_pages, pps=pps),
        grid_spec=grid_spec,
        out_shape=jax.ShapeDtypeStruct((bd, TQ_S, B_HEADS * HEAD_DIM), BF16),
        compiler_params=_cparams(("arbitrary", "arbitrary")),
        name="nsa_sample",
    )(page_table.reshape(-1), *([cmp_t] * pps), *([sel_t] * pps),
      _sample_rows(pr["qb"][0]), _sample_rows(misc[:, M_GATE:M_GATE + 3 * B_HEADS]),
      pr["kvs"][0][:, :, None], win_t, pr["kvw"][0][:, :, None],
      pe_rows, wbd, strips, cstrips, far)
    return out[:, 0]


def _merge_kernel(x_ref, mod_ref, oa_ref, ob_ref, w_ref, o_ref):
    n = oa_ref.shape[-1]
    y = _mm(oa_ref[...], w_ref[:n]) + _mm(ob_ref[...], w_ref[n:])
    o_ref[...] = x_ref[...] + (1.0 + mod_ref[2]) * y


def _merge_call(x, mod, oa, ob, w, tm):
    nb, t, d = x.shape
    tmod = mod.shape[2]
    mod_spec = (pl.BlockSpec((None, 3, 1, d), lambda b, i: (b, 0, 0, 0)) if tmod == 1 else
                pl.BlockSpec((None, 3, tm, d), lambda b, i: (b, 0, i, 0)))
    row = lambda w_: pl.BlockSpec((None, tm, w_), lambda b, i: (b, i, 0))
    return pl.pallas_call(
        _merge_kernel,
        grid=(nb, t // tm),
        in_specs=[row(d), mod_spec, row(oa.shape[-1]), row(ob.shape[-1]),
                  pl.BlockSpec(w.shape, lambda b, i: (0, 0))],
        out_specs=row(d),
        out_shape=jax.ShapeDtypeStruct(x.shape, F32),
        compiler_params=_cparams(("arbitrary", "arbitrary")),
        name="merge",
    )(x, mod, oa, ob, w)


def _gelu_ln(h, w_in_ref, ln_ref, cw):
    uv = jax.nn.gelu(_mm(h, w_in_ref[...]))
    u, v = uv[:, :cw], uv[:, cw:]
    mu = jnp.mean(v, axis=-1, keepdims=True)
    var = jnp.mean(jnp.square(v - mu), axis=-1, keepdims=True)
    return u, (v - mu) * lax.rsqrt(var + EPS) * ln_ref[...]


def _pgmlp_kernel(x_ref, mod_ref, g_ref, w_in_ref, ln_ref, ws_ref, bs_ref, w_out_ref, o_ref, *, cw):
    x = x_ref[...]
    h = _modulated_norm(x, g_ref[...], mod_ref).astype(BF16)
    u, v = _gelu_ln(h, w_in_ref, ln_ref, cw)
    gw = cw // C_GROUPS
    rows = []
    for n in range(x.shape[0] // C_CHUNK):
        vb = v[n * C_CHUNK:(n + 1) * C_CHUNK].astype(BF16)
        sv = [_mm(ws_ref[g], vb[:, g * gw:(g + 1) * gw]) + bs_ref[:, g:g + 1] for g in range(C_GROUPS)]
        rows.append(jnp.concatenate(sv, axis=1))
    sv = rows[0] if len(rows) == 1 else jnp.concatenate(rows, axis=0)
    o_ref[...] = x + (1.0 + mod_ref[2]) * _mm(u * sv, w_out_ref[...])


def _pgmlp_call(x, mod, g, w_in, ln_g, ws, bs_t, w_out, tm):
    nb, t, d = x.shape
    cw = w_out.shape[0]
    const = lambda a: pl.BlockSpec(a.shape, lambda b, i: (0,) * a.ndim)
    return pl.pallas_call(
        functools.partial(_pgmlp_kernel, cw=cw),
        grid=(nb, t // tm),
        in_specs=[pl.BlockSpec((None, tm, d), lambda b, i: (b, i, 0)),
                  pl.BlockSpec((None, 3, 1, d), lambda b, i: (b, 0, 0, 0)),
                  pl.BlockSpec((1, d), lambda b, i: (0, 0)),
                  const(w_in), pl.BlockSpec((1, cw), lambda b, i: (0, 0)), const(ws), const(bs_t), const(w_out)],
        out_specs=pl.BlockSpec((None, tm, d), lambda b, i: (b, i, 0)),
        out_shape=jax.ShapeDtypeStruct(x.shape, F32),
        compiler_params=_cparams(("arbitrary", "arbitrary")),
        name="gmlp_prompt",
    )(x, mod, g.reshape(1, d), w_in, ln_g.reshape(1, cw), ws, bs_t, w_out)


def _sgmlp_kernel(x_ref, mod_ref, g_ref, w_in_ref, ln_ref, ws0_ref, bs0_ref, w_out_ref, o_ref, v_ref, *, cw):
    x = x_ref[...]
    h = _modulated_norm(x, g_ref[...], mod_ref).astype(BF16)
    u, v = _gelu_ln(h, w_in_ref, ln_ref, cw)
    v_ref[...] = v
    sv = v.astype(BF16).astype(F32) * ws0_ref[...].astype(F32) + bs0_ref[...]
    o_ref[...] = x + (1.0 + mod_ref[2]) * _mm(u * sv, w_out_ref[...])


def _sgmlp_call(x, mod, g, w_in, ln_g, ws, b_sp, w_out):
    nb, t, d = x.shape
    cw = w_out.shape[0]
    gw = cw // C_GROUPS
    ws0 = jnp.repeat(ws[:, 0, 0], gw).reshape(1, cw)
    bs0 = jnp.repeat(b_sp[:, 0], gw).reshape(1, cw)
    full = lambda a: pl.BlockSpec(a.shape, lambda: (0,) * a.ndim)
    x2, mod2 = x[0], mod[0]
    o, v = pl.pallas_call(
        functools.partial(_sgmlp_kernel, cw=cw),
        in_specs=[full(x2), full(mod2), pl.BlockSpec((1, d), lambda: (0, 0)), full(w_in),
                  pl.BlockSpec((1, cw), lambda: (0, 0)), full(ws0), full(bs0), full(w_out)],
        out_specs=[full(x2), pl.BlockSpec((t, cw), lambda: (0, 0))],
        out_shape=[jax.ShapeDtypeStruct(x2.shape, F32), jax.ShapeDtypeStruct((t, cw), F32)],
        compiler_params=pltpu.CompilerParams(vmem_limit_bytes=VMEM_LIMIT),
        name="gmlp_sample",
    )(x2, mod2, g.reshape(1, d), w_in, ln_g.reshape(1, cw), ws0, bs0, w_out)
    return o[None], v


def kernel(x_prompt, x_sample, cache_a_kv, cache_a_idx, cache_b_cmp_kv, cache_b_sel_kv, state_b_win_kv,
           page_table, c_prompt, c_sample, rel_bias, w_ada, b_ada, norm_g, w_ffn_in, w_ffn_out,
           w_in_att, w_out_att, cmp_pe, w_cmp, w_in_c, ln_c_g, w_sp, b_sp, w_out_c, final_g):
    nbp, s, d = x_prompt.shape
    nbs, tdec, _ = x_sample.shape
    depth = w_ada.shape[0]
    assert tdec == 1 and s % TQ_P == 0 and cache_a_kv.shape[2] == PAGE

    ada = _ada_call(jnp.concatenate([c_prompt, c_sample], axis=0), w_ada, b_ada)
    ada = ada.reshape(depth, nbp + nbs, 3, 3, d)
    mod_p = lambda l, j: ada[l, :nbp, j][:, :, None, :]
    mod_s = lambda l, j: jnp.transpose(ada[l, nbp:, j], (1, 0, 2))[None]

    w_ffn_in_h = w_ffn_in.astype(BF16)
    w_ffn_out_h = w_ffn_out.astype(BF16)
    tab_a, tab_b = rel_bias[:, :A_HEADS], rel_bias[:, A_HEADS:]
    strips_pa, strips_pb = _bias_strips(tab_a, TQ_P), _bias_strips(tab_b, TQ_P)
    strips_sa, strips_sb = _bias_strips(tab_a, TQ_S), _bias_strips(tab_b, TQ_S)
    cstrips_p, cstrips_s = _cmp_strips(tab_b, TQ_P), _cmp_strips(tab_b, TQ_S)
    far_a, far_b = tab_a[N_BUCKETS - 1], tab_b[N_BUCKETS - 1]
    idx_t, akv_t = _pages_t(cache_a_idx), _pages_t(cache_a_kv)
    cmp_t, sel_t, win_t = _pages_t(cache_b_cmp_kv), _pages_t(cache_b_sel_kv), _pages_t(state_b_win_kv)

    xp = x_prompt
    xs = jnp.transpose(x_sample, (1, 0, 2))
    tm_p = 512 if s % 512 == 0 else TQ_P
    tk_p = TK_P if s % TK_P == 0 else TQ_P
    st_p, st_s, st_c = [], [], []
    for l in range(depth):
        last = l == depth - 1
        xp = _ffn_call(xp, mod_p(l, 0), norm_g[l, 0], w_ffn_in_h, w_ffn_out_h, l, 0, None, tm_p)
        xs = _ffn_call(xs, mod_s(l, 0), norm_g[l, 0], w_ffn_in_h, w_ffn_out_h, l, 0, None, nbs)
        if l % 2 == 0:
            a = l // 2
            wp = _rearranged_proj_weight(w_in_att[a])
            w_out = w_out_att[a].astype(BF16)
            pe_rows, wbd = _compress_operands(cmp_pe[a], w_cmp[a])

            pr = _proj_call(xp, mod_p(l, 1), norm_g[l, 1], wp, tm_p)
            oa = _pdsa_call(pr, strips_pa, far_a, tk_p)
            kcp = _pcmp_call(pr["kvc"], pe_rows, wbd)
            ob = _pnsa_call(pr, kcp, strips_pb, cstrips_p, far_b, tk_p)
            xp = _merge_call(xp, mod_p(l, 1), oa, ob, w_out, tm_p)
            kv5 = lambda t: t.reshape(t.shape[0], t.shape[1], 2, A_KV, HEAD_DIM)
            st_p.append((kv5(pr["kva"]), pr["misc"][:, :, M_KI:M_KI + IDX_DIM], kv5(pr["kvc"]),
                         kv5(pr["kvs"]), kv5(pr["kvw"][:, s - min(WINDOW, s):])))

            ps = _proj_call(xs, mod_s(l, 1), norm_g[l, 1], wp, nbs)
            oa = _sdsa_call(a, page_table, idx_t, akv_t, ps, strips_sa, far_a)
            ob = _snsa_call(a, page_table, cmp_t, sel_t, win_t, ps, pe_rows, wbd, strips_sb, cstrips_s, far_b)
            xs = _merge_call(xs, mod_s(l, 1), oa[None], ob[None], w_out, nbs)
            tok = lambda t: t[0].reshape(nbs, 1, 2, A_KV, HEAD_DIM)
            st_s.append((tok(ps["kva"]), ps["misc"][0][:, None, M_KI:M_KI + IDX_DIM], tok(ps["kvc"]),
                         tok(ps["kvs"]),
                         jnp.concatenate([state_b_win_kv[a][:, tdec:], tok(ps["kvw"])], axis=1)))
        else:
            ci = l // 2
            w_in = w_in_c[ci].astype(BF16)
            w_out = w_out_c[ci].astype(BF16)
            ws = (w_sp[ci] * jnp.tril(jnp.ones((C_CHUNK, C_CHUNK), w_sp.dtype))).astype(BF16)
            xp = _pgmlp_call(xp, mod_p(l, 1), norm_g[l, 1], w_in, ln_c_g[ci], ws,
                             jnp.transpose(b_sp[ci]), w_out, 256 if s % 256 == 0 else C_CHUNK)
            xs, v = _sgmlp_call(xs, mod_s(l, 1), norm_g[l, 1], w_in, ln_c_g[ci], ws, b_sp[ci], w_out)
            st_c.append(v[:, None, :])
        fg = final_g if last else None
        xp = _ffn_call(xp, mod_p(l, 2), norm_g[l, 2], w_ffn_in_h, w_ffn_out_h, l, 1, fg, tm_p)
        xs = _ffn_call(xs, mod_s(l, 2), norm_g[l, 2], w_ffn_in_h, w_ffn_out_h, l, 1, fg, nbs)

    stk = lambda sts, i: jnp.stack([st[i] for st in sts])
    return (xp, jnp.transpose(xs, (1, 0, 2)),
            stk(st_p, 0), stk(st_p, 1), stk(st_p, 2), stk(st_p, 3), stk(st_p, 4),
            stk(st_s, 0), stk(st_s, 1), stk(st_s, 2), stk(st_s, 3), stk(st_s, 4),
            jnp.stack(st_c))
```

```python
import functools
import math

import numpy as np
import jax
import jax.numpy as jnp
from jax import lax
from jax.experimental import pallas as pl
from jax.experimental.pallas import tpu as pltpu

F32 = jnp.float32
BF16 = jnp.bfloat16
I32 = jnp.int32

HEAD_DIM = 64
A_HEADS = 8
A_KV = 2
IDX_HEADS = 4
IDX_DIM = 64
IDX_TOPK = 256
B_HEADS = 8
B_KV = 2
CMP_BLOCK = 32
SEL_BLOCK = 64
SEL_TOPN = 16
WINDOW = 512
N_BUCKETS = 32
MAX_DISTANCE = 1024
C_CHUNK = 128
C_GROUPS = 8
EPS = 1e-6
PAGE = 128
GROUP = A_HEADS // A_KV
KVW = 2 * A_KV * HEAD_DIM

LANE = 128
VMEM_LIMIT = 56 * 1024 * 1024
KVX = 3 * LANE

NEG = -1e30
KMIN = -2 ** 31

P_QA, P_QB, P_KVA, P_QI, P_KVC, P_KVS, P_KVW, P_MISC = 0, 512, 1024, 1280, 1536, 1792, 2048, 2304
P_WIDTH = 2432
M_KI, M_WI, M_GATE = 0, IDX_DIM, IDX_DIM + IDX_HEADS

assert A_KV == 2 and B_KV == 2 and A_KV * HEAD_DIM == LANE and IDX_DIM == HEAD_DIM


def _cparams(sem):
    return pltpu.CompilerParams(dimension_semantics=sem, vmem_limit_bytes=VMEM_LIMIT)


def _mm(a, b):
    return jnp.dot(a.astype(BF16), b.astype(BF16), preferred_element_type=F32)


def _mm_nt(a, b):
    return lax.dot_general(a.astype(BF16), b.astype(BF16), (((1,), (1,)), ((), ())),
                           preferred_element_type=F32)


def _modulated_norm(x, g, mod_ref):
    y = x * lax.rsqrt(jnp.mean(x * x, axis=-1, keepdims=True) + EPS) * g
    return y * (1.0 + mod_ref[1]) + mod_ref[0]


def _ada_kernel(c_ref, w_ref, b_ref, o_ref):
    c = c_ref[...]
    o_ref[...] = _mm(c * jax.nn.sigmoid(c), w_ref[...]) + b_ref[...]


def _ada_call(c_all, w_ada, b_ada):
    depth, d, n = w_ada.shape
    r = c_all.shape[0]
    tn = n // 8
    return pl.pallas_call(
        _ada_kernel,
        grid=(depth, n // tn),
        in_specs=[pl.BlockSpec((r, d), lambda l, j: (0, 0)),
                  pl.BlockSpec((None, d, tn), lambda l, j: (l, 0, j)),
                  pl.BlockSpec((None, 1, tn), lambda l, j: (l, 0, j))],
        out_specs=pl.BlockSpec((None, r, tn), lambda l, j: (l, 0, j)),
        out_shape=jax.ShapeDtypeStruct((depth, r, n), F32),
        compiler_params=_cparams(("arbitrary", "arbitrary")),
        name="ada",
    )(c_all, w_ada, b_ada.reshape(depth, 1, n))


def _ffn_kernel(x_ref, mod_ref, g_ref, wa_ref, wb_ref, wo_ref, fg_ref, o_ref, h_scr, acc_scr, *,
                n_f, final_norm):
    f = pl.program_id(2)

    @pl.when(f == 0)
    def _():
        h_scr[...] = _modulated_norm(x_ref[...], g_ref[...], mod_ref).astype(BF16)
        acc_scr[...] = jnp.zeros_like(acc_scr)

    h = h_scr[...]
    a = _mm(h, wa_ref[...])
    b = _mm(h, wb_ref[...])
    acc_scr[...] += _mm(a * jax.nn.sigmoid(a) * b, wo_ref[...])

    @pl.when(f == n_f - 1)
    def _():
        y = x_ref[...] + 0.5 * (1.0 + mod_ref[2]) * acc_scr[...]
        if final_norm:
            y = y * lax.rsqrt(jnp.mean(y * y, axis=-1, keepdims=True) + EPS) * fg_ref[...]
        o_ref[...] = y


def _ffn_call(x, mod, g, w_in, w_out, l, j, final_g, tm):
    nb, t, d = x.shape
    dff = w_out.shape[2]
    tf = dff // 2 if dff % (2 * LANE) == 0 else LANE
    n_f = dff // tf
    tmod = mod.shape[2]
    mod_spec = (pl.BlockSpec((None, 3, 1, d), lambda b, i, f: (b, 0, 0, 0)) if tmod == 1 else
                pl.BlockSpec((None, 3, tm, d), lambda b, i, f: (b, 0, i, 0)))
    final_norm = final_g is not None
    fg = (final_g if final_norm else g).reshape(1, d)
    return pl.pallas_call(
        functools.partial(_ffn_kernel, n_f=n_f, final_norm=final_norm),
        grid=(nb, t // tm, n_f),
        in_specs=[pl.BlockSpec((None, tm, d), lambda b, i, f: (b, i, 0)),
                  mod_spec,
                  pl.BlockSpec((1, d), lambda b, i, f: (0, 0)),
                  pl.BlockSpec((None, None, d, tf), lambda b, i, f: (l, j, 0, f)),
                  pl.BlockSpec((None, None, d, tf), lambda b, i, f: (l, j, 0, f + n_f)),
                  pl.BlockSpec((None, None, tf, d), lambda b, i, f: (l, j, f, 0)),
                  pl.BlockSpec((1, d), lambda b, i, f: (0, 0))],
        out_specs=pl.BlockSpec((None, tm, d), lambda b, i, f: (b, i, 0)),
        out_shape=jax.ShapeDtypeStruct(x.shape, F32),
        scratch_shapes=[pltpu.VMEM((tm, d), BF16), pltpu.VMEM((tm, d), F32)],
        compiler_params=_cparams(("arbitrary", "arbitrary", "arbitrary")),
        name="ffn",
    )(x, mod, g.reshape(1, d), w_in, w_in, w_out, fg)


def _kv_ext(y):
    e = jnp.where(lax.broadcasted_iota(I32, (y.shape[0], HEAD_DIM), 1) == 0, 1.0, 0.0).astype(BF16)
    y = y.astype(BF16)
    return jnp.concatenate([y[:, :LANE], y[:, LANE:LANE + HEAD_DIM], e, y[:, LANE + HEAD_DIM:], e], axis=1)


def _kv_ext_t(y):
    e = jnp.where(lax.broadcasted_iota(I32, (HEAD_DIM, y.shape[1]), 0) == 0, 1.0, 0.0).astype(BF16)
    y = y.astype(BF16)
    return jnp.concatenate([y[:LANE], y[LANE:LANE + HEAD_DIM], e, y[LANE + HEAD_DIM:], e], axis=0)


_PROJ_OUT = (("qa", P_QA, 512, "h"), ("qb", P_QB, 512, "h"), ("qi", P_QI, 256, "h"),
             ("kva", P_KVA, 256, "f"), ("kvc", P_KVC, 256, "f"), ("kvs", P_KVS, 256, "f"),
             ("kvw", P_KVW, 256, "f"), ("misc", P_MISC, 128, "f"),
             ("kva_x", P_KVA, KVX, "x"), ("kvs_x", P_KVS, KVX, "x"), ("kvw_x", P_KVW, KVX, "x"),
             ("misc_h", P_MISC, 128, "h"))


def _proj_kernel(x_ref, mod_ref, g_ref, w_ref, *o_refs):
    h = _modulated_norm(x_ref[...], g_ref[...], mod_ref).astype(BF16)
    done = {}
    for (name, off, width, kind), o_ref in zip(_PROJ_OUT, o_refs):
        if off not in done:
            y = _mm(h, w_ref[:, off:off + (KVW if kind == "x" else width)])
            if name in ("qa", "qb"):
                y = y * (HEAD_DIM ** -0.5)
            done[off] = y
        o_ref[...] = _kv_ext(done[off]) if kind == "x" else done[off].astype(o_ref.dtype)


def _proj_call(x, mod, g, wp, tm):
    nb, t, d = x.shape
    tmod = mod.shape[2]
    mod_spec = (pl.BlockSpec((None, 3, 1, d), lambda b, i: (b, 0, 0, 0)) if tmod == 1 else
                pl.BlockSpec((None, 3, tm, d), lambda b, i: (b, 0, i, 0)))
    outs = pl.pallas_call(
        _proj_kernel,
        grid=(nb, t // tm),
        in_specs=[pl.BlockSpec((None, tm, d), lambda b, i: (b, i, 0)),
                  mod_spec,
                  pl.BlockSpec((1, d), lambda b, i: (0, 0)),
                  pl.BlockSpec((d, P_WIDTH), lambda b, i: (0, 0))],
        out_specs=[pl.BlockSpec((None, tm, w), lambda b, i: (b, i, 0)) for _, _, w, _ in _PROJ_OUT],
        out_shape=[jax.ShapeDtypeStruct((nb, t, w), F32 if kind == "f" else BF16)
                   for _, _, w, kind in _PROJ_OUT],
        compiler_params=_cparams(("arbitrary", "arbitrary")),
        name="proj",
    )(x, mod, g.reshape(1, d), wp)
    return {name: o for (name, _, _, _), o in zip(_PROJ_OUT, outs)}


def _rearranged_proj_weight(w):
    d = w.shape[0]
    o = np.cumsum((0, 512, 256, 256, 64, 4, 512, 256, 256, 256, 24))
    seg = lambda i: w[:, o[i]:o[i + 1]]
    qa, kva, qi, ki, wi, qb, kvc, kvs, kvw, gates = [seg(i) for i in range(10)]
    pad = jnp.zeros((d, LANE - IDX_DIM - IDX_HEADS - 3 * B_HEADS), w.dtype)
    return jnp.concatenate([qa, qb, kva, qi, kvc, kvs, kvw, ki, wi, gates, pad], axis=1).astype(BF16)


def _t5_bucket(dist):
    n = jnp.maximum(dist, 0)
    max_exact = N_BUCKETS // 2
    nf = jnp.maximum(n, 1).astype(F32)
    large = max_exact + (jnp.log(nf / max_exact) / math.log(MAX_DISTANCE / max_exact)
                         * (N_BUCKETS - max_exact)).astype(I32)
    large = jnp.minimum(large, N_BUCKETS - 1)
    return jnp.where(n < max_exact, n, large)


def _bias_lookup(tab, dist):
    b = _t5_bucket(jnp.asarray(dist.reshape(-1), I32))
    onehot = (b[:, None] == jnp.arange(N_BUCKETS, dtype=I32)[None, :]).astype(F32)
    out = jnp.dot(onehot, tab, precision=lax.Precision.HIGHEST)
    return jnp.transpose(out).reshape((tab.shape[1],) + dist.shape)


NEAR = -(-(MAX_DISTANCE + LANE - 1) // LANE)


def _bias_strips(tab, tq):
    r = np.arange(tq)[None, :, None]
    c = np.arange(LANE)[None, None, :]
    bd = np.arange(NEAR + 1)[:, None, None]
    dist = np.where(bd == NEAR, MAX_DISTANCE, r - c + LANE * bd)
    return _bias_lookup(tab, dist)


def _cmp_near_count(delta):
    return -(-(-(-(MAX_DISTANCE - delta) // CMP_BLOCK)) // 8) * 8


def _cmp_delta(tq):
    return -(CMP_BLOCK - 1) - CMP_BLOCK * ((tq - CMP_BLOCK) // CMP_BLOCK)


def _cmp_strips(tab, tq):
    delta = _cmp_delta(tq)
    ncn = _cmp_near_count(delta)
    dist = np.arange(tq)[:, None] + delta + CMP_BLOCK * np.arange(ncn)[None, :]
    v = _bias_lookup(tab, dist)
    hi = v.astype(BF16)
    r1 = v - hi.astype(F32)
    mid = r1.astype(BF16)
    lo = (r1 - mid.astype(F32)).astype(BF16)
    return jnp.stack([hi, mid, lo], axis=1)


def _padded_heads(x, g):
    z = jnp.zeros((x.shape[0], HEAD_DIM), x.dtype)
    rows = []
    for r in range(GROUP):
        h = g * GROUP + r
        piece = x[:, h * HEAD_DIM:(h + 1) * HEAD_DIM]
        rows.append(jnp.concatenate([piece, z] if g == 0 else [z, piece], axis=1))
    return jnp.concatenate(rows, axis=0)


def _k_part(kv, kv_t):
    return kv[:LANE] if kv_t else kv[:, :LANE]


def _v_part(kv, g, kv_t):
    return kv[(1 + g) * LANE:(2 + g) * LANE] if kv_t else kv[:, (1 + g) * LANE:(2 + g) * LANE]


def _flash_init(scr, tq, in_values):
    if in_values:
        return tuple((jnp.full((GROUP, tq, LANE), NEG, F32), jnp.zeros((GROUP * tq, LANE), F32))
                     for _ in range(A_KV))
    scr[0][...] = jnp.full_like(scr[0], NEG)
    scr[1][...] = jnp.zeros_like(scr[1])
    return 0


def _flash_store(st, scr):
    if isinstance(st, tuple):
        for g, (m, acc) in enumerate(st):
            scr[0][g] = m
            scr[1][g] = acc


def _mask_add(mask):
    return jnp.where(mask, 0.0, NEG)


def _flash_update(st, scr, g, qp_g, kv, strips_ref, far_ref, bds, madd, tq, kv_t):
    m_all, acc = st[g] if isinstance(st, tuple) else (scr[0][g], scr[1][g])
    kb = _k_part(kv, kv_t)
    s_all = _mm(qp_g, kb) if kv_t else _mm_nt(qp_g, kb)
    nblk = madd.shape[1] // LANE
    ps, alphas, m_news = [], [], []
    for r in range(GROUP):
        h = g * GROUP + r
        blocks = []
        for j in range(nblk):
            sj = s_all[r * tq:(r + 1) * tq, j * LANE:(j + 1) * LANE] + madd[:, j * LANE:(j + 1) * LANE]
            blocks.append(sj if bds is None else sj + strips_ref[h, bds[j]])
        bmax = blocks[0]
        for sj in blocks[1:]:
            bmax = jnp.maximum(bmax, sj)
        row_max = jnp.max(bmax, axis=-1, keepdims=True)
        m_old = m_all[r]
        if bds is None:
            m_new = jnp.maximum(m_old, row_max + far_ref[h])
            shift = m_new - far_ref[h]
        else:
            m_new = jnp.maximum(m_old, row_max)
            shift = m_new
        m_news.append(m_new)
        alphas.append(jnp.exp(m_old - m_new))
        ps.append(jnp.concatenate([jnp.exp(sj - shift).astype(BF16) for sj in blocks], axis=1))
    p = jnp.concatenate(ps, axis=0)
    vx = _v_part(kv, g, kv_t)
    pv = _mm_nt(p, vx) if kv_t else _mm(p, vx)
    m_all = jnp.stack(m_news)
    acc = jnp.concatenate(alphas, axis=0) * acc + pv
    if isinstance(st, tuple):
        return st[:g] + ((m_all, acc),) + st[g + 1:]
    scr[0][g] = m_all
    scr[1][g] = acc
    return st


def _flash_result(scr, g):
    acc = scr[1][g]
    l = acc[:, HEAD_DIM:HEAD_DIM + 1]
    return jnp.where(l > 0.0, acc[:, :HEAD_DIM] / jnp.where(l > 0.0, l, 1.0), 0.0)


def _tile_bds(qb, k, tk):
    nblk = tk // LANE
    return [jnp.clip(qb - (k * nblk + j), 0, NEAR) for j in range(nblk)]


ZERO_KEY_BASE = 2 ** 16 - 1


def _ordered_key(x, pos):
    b = lax.bitcast_convert_type(x, I32)
    return jnp.where(x == 0.0, ZERO_KEY_BASE - pos, b ^ ((b >> 31) & 0x7FFFFFFF))


def _far_tiles(qb, tk):
    n = (qb - NEAR + 1) // (tk // LANE)
    return max(n, 0) if isinstance(n, int) else jnp.maximum(n, 0)


def _dsa_core(qi, wi, qa, get_kidx, get_kva, strips_ref, far_ref, t0, tq, tk, topk, kv_t, unroll,
              keys_scr, wib_scr, m_scr, acc_scr, tri_scr):
    n_tiles = (t0 + tq - 1) // tk + 1
    qb = t0 // LANE
    n_far = _far_tiles(qb, tk)
    row_pos = t0 + lax.broadcasted_iota(I32, (tq, tk), 0)
    col = lax.broadcasted_iota(I32, (tq, tk), 1)

    if kv_t:
        qi_st = jnp.concatenate([qi[:, h * IDX_DIM:(h + 1) * IDX_DIM] for h in range(IDX_HEADS)], axis=0)
    else:
        z = jnp.zeros((tq, LANE - IDX_DIM), qi.dtype)
        qi_st = jnp.concatenate(
            [jnp.concatenate([qi[:, h * IDX_DIM:(h + 1) * IDX_DIM], z], axis=1) for h in range(IDX_HEADS)], axis=0)
    for h in range(IDX_HEADS):
        wib_scr[h] = jnp.broadcast_to(wi[:, h:h + 1], (tq, tk))

    def score_tile(k, c):
        kt = get_kidx(k)
        sc = _mm(qi_st, kt) if kv_t else _mm_nt(qi_st, kt)
        score = jnp.maximum(sc[:tq], 0.0) * wib_scr[0]
        for h in range(1, IDX_HEADS):
            score = score + jnp.maximum(sc[h * tq:(h + 1) * tq], 0.0) * wib_scr[h]
        pos = k * tk + col
        keys_scr[k] = jnp.where(pos <= row_pos, _ordered_key(score, pos), KMIN)
        return c

    lax.fori_loop(0, n_tiles, score_tile, 0, unroll=unroll)

    def count(pred):
        def count_tile(k, cnt):
            key = keys_scr[k]
            for j in range(tk // LANE):
                cnt = cnt + jnp.where(pred(key[:, j * LANE:(j + 1) * LANE]), 1.0, 0.0)
            return cnt

        cnt = lax.fori_loop(0, n_tiles, count_tile, jnp.zeros((tq, LANE), F32))
        return jnp.sum(cnt, axis=-1, keepdims=True)

    tu = jnp.zeros((tq, 1), I32)
    n_ge = jnp.zeros((tq, 1), F32)
    for bit in range(31, -1, -1):
        cand_u = tu | np.int32(-2 ** 31 if bit == 31 else 2 ** bit)
        cand_s = jnp.broadcast_to(cand_u ^ np.int32(KMIN), (tq, LANE))
        tot = count(lambda key, cand_s=cand_s: key >= cand_s)
        tu = jnp.where(tot >= topk, cand_u, tu)
        n_ge = jnp.where(tot >= topk, tot, n_ge)
    thr = tu ^ np.int32(KMIN)
    found = thr > np.int32(KMIN)
    has_ties = jnp.max(jnp.where(found & (n_ge > topk), 1.0, 0.0)) > 0.5

    qp_g = [_padded_heads(qa, g) for g in range(A_KV)]
    scr = (m_scr, acc_scr)
    in_values = bool(unroll)

    def attend(k, st, mask, far):
        kv = get_kva(k)
        bds = None if far else _tile_bds(qb, k, tk)
        madd = _mask_add(mask)
        for g in range(A_KV):
            st = _flash_update(st, scr, g, qp_g[g], kv, strips_ref, far_ref, bds, madd, tq, kv_t)
        return st

    @pl.when(jnp.logical_not(has_ties))
    def _():
        thr_ge = jnp.where(found, thr, np.int32(KMIN + 1))

        def tile(k, st, far):
            return attend(k, st, keys_scr[k] >= thr_ge, far)

        st = _flash_init(scr, tq, in_values)
        st = lax.fori_loop(0, n_far, functools.partial(tile, far=True), st, unroll=unroll)
        st = lax.fori_loop(n_far, n_tiles, functools.partial(tile, far=False), st, unroll=unroll)
        _flash_store(st, scr)

    @pl.when(has_ties)
    def _():
        need = topk - count(lambda key: key > jnp.broadcast_to(thr, (tq, LANE)))
        tri_scr[...] = jnp.where(
            lax.broadcasted_iota(I32, (LANE, LANE), 0) <= lax.broadcasted_iota(I32, (LANE, LANE), 1),
            1.0, 0.0).astype(BF16)

        def tile(k, c):
            eq_seen, st = c
            key = keys_scr[k]
            eq = key == thr
            takes = []
            for j in range(tk // LANE):
                eq_j = eq[:, j * LANE:(j + 1) * LANE]
                rank = eq_seen + _mm(jnp.where(eq_j, 1.0, 0.0), tri_scr[...])
                takes.append(eq_j & (rank <= need))
                eq_seen = rank[:, LANE - 1:LANE]
            take = jnp.concatenate(takes, axis=1)
            st = attend(k, st, (key > thr) | (take & found), False)
            return eq_seen, st

        _, st = lax.fori_loop(0, n_tiles, tile, (jnp.zeros((tq, 1), F32), _flash_init(scr, tq, in_values)))
        _flash_store(st, scr)

    outs = []
    for g in range(A_KV):
        o = _flash_result(scr, g)
        outs += [o[r * tq:(r + 1) * tq] for r in range(GROUP)]
    return jnp.concatenate(outs, axis=1)


def _pick_blocks(v, n_pick, axis):
    lane = lax.broadcasted_iota(I32, v.shape, axis).astype(F32)
    sel = jnp.zeros(v.shape, jnp.bool_)
    for _ in range(n_pick):
        mx = jnp.max(v, axis=axis, keepdims=True)
        cand = (v == mx) & jnp.logical_not(sel)
        first = jnp.min(jnp.where(cand, lane, float(v.shape[axis])), axis=axis, keepdims=True)
        pick = lane == first
        sel = sel | pick
        v = jnp.where(pick, -jnp.inf, v)
    return sel


def _nsa_core(qb, gates, kcp, get_kvs, win_tiles, strips_ref, cstrips_ref, far_ref, t0, tq, tk, nj,
              n_pick, extra_block, kv_t, unroll, m_scr, acc_scr):
    n_tiles = (t0 + tq - 1) // tk + 1
    qblk = t0 // LANE
    delta = _cmp_delta(tq)
    ncn = cstrips_ref.shape[-1]
    qp_g = [_padded_heads(qb, g) for g in range(B_KV)]
    row_pos1 = t0 + lax.broadcasted_iota(I32, (tq, 1), 0)

    lam = lax.broadcasted_iota(I32, (tq, 2 * nj), 1)
    m_of = 2 * (lam % nj) + lam // nj
    cdist = row_pos1 - (CMP_BLOCK * m_of + CMP_BLOCK - 1)
    cmask = cdist >= 0
    m_hi = (t0 - (CMP_BLOCK - 1) - delta) // CMP_BLOCK
    lam_s = lax.broadcasted_iota(I32, (ncn, 2 * nj), 1)
    shift = jnp.where(2 * (lam_s % nj) + lam_s // nj == m_hi - lax.broadcasted_iota(I32, (ncn, 2 * nj), 0),
                      1.0, 0.0).astype(BF16)
    is_far = m_of <= m_hi - ncn
    kc_both = kcp[:, :LANE]
    o_c, imp = [], []
    for g in range(B_KV):
        bias = []
        for r in range(GROUP):
            h = g * GROUP + r
            b = (_mm(cstrips_ref[h, 0], shift) + _mm(cstrips_ref[h, 1], shift)) + _mm(cstrips_ref[h, 2], shift)
            bias.append(jnp.where(is_far, far_ref[h], b))
        s = _mm_nt(qp_g[g], kc_both).reshape(GROUP, tq, 2 * nj) + jnp.stack(bias)
        s = jnp.where(cmask[None], s, NEG)
        e = jnp.where(cmask[None], jnp.exp(s - jnp.max(s, axis=-1, keepdims=True)), 0.0)
        l = jnp.sum(e, axis=-1, keepdims=True)
        p = jnp.where(l > 0.0, e / jnp.where(l > 0.0, l, 1.0), 0.0)
        o_c.append(_mm(p.reshape(GROUP * tq, 2 * nj), kcp[:, (B_KV + g) * HEAD_DIM:(B_KV + g + 1) * HEAD_DIM]))
        ps = p[0]
        for r in range(1, GROUP):
            ps = ps + p[r]
        imp.append(ps[:, :nj] + ps[:, nj:])

    jl = lax.broadcasted_iota(I32, (tq, nj), 1)
    cur = row_pos1 // SEL_BLOCK
    forced = (jl == 0) | (jl == cur) | (jl == cur - 1)
    admissible = jl <= cur
    vs = [jnp.where(admissible, jnp.where(forced, jnp.inf, imp[g]), -jnp.inf) for g in range(B_KV)]
    if tq % LANE == 0:
        picked_t = _pick_blocks(jnp.concatenate([jnp.transpose(v) for v in vs], axis=1), n_pick, 0)
        picked = [jnp.transpose(jnp.where(picked_t[:, g * tq:(g + 1) * tq], 1.0, 0.0)) > 0.5
                  for g in range(B_KV)]
    else:
        picked = [_pick_blocks(v, n_pick, 1) for v in vs]
    bmask = [jnp.where(picked[g] & admissible, 1.0, 0.0).astype(BF16) for g in range(B_KV)]
    row_pos = t0 + lax.broadcasted_iota(I32, (tq, tk), 0)
    col = lax.broadcasted_iota(I32, (tq, tk), 1)
    ej = lax.broadcasted_iota(I32, (nj, tk), 0)
    ec = lax.broadcasted_iota(I32, (nj, tk), 1)
    scr = (m_scr, acc_scr)
    in_values = bool(unroll)

    def sel_tile(k, st, far):
        kv = get_kvs(k)
        expand = jnp.where(ej == (k * tk + ec) // SEL_BLOCK, 1.0, 0.0).astype(BF16)
        for g in range(B_KV):
            mask = _mm(bmask[g], expand) > 0.5
            if far:
                bds = None
            else:
                bds = _tile_bds(qblk, k, tk)
                key_pos = k * tk + col
                if extra_block:
                    mask = mask | (key_pos >= nj * SEL_BLOCK)
                mask = mask & (key_pos <= row_pos)
            st = _flash_update(st, scr, g, qp_g[g], kv, strips_ref, far_ref, bds, _mask_add(mask), tq, kv_t)
        return st

    n_far = _far_tiles(qblk, tk)
    st = _flash_init(scr, tq, in_values)
    st = lax.fori_loop(0, n_far, functools.partial(sel_tile, far=True), st, unroll=unroll)
    st = lax.fori_loop(n_far, n_tiles, functools.partial(sel_tile, far=False), st, unroll=unroll)
    _flash_store(st, scr)
    o_s = [_flash_result(scr, g) for g in range(B_KV)]

    st = _flash_init(scr, tq, in_values)
    for get_tile, off, guard in win_tiles:
        def win_tile(st, get_tile=get_tile, off=off):
            kv = get_tile()
            wt = kv.shape[1] if kv_t else kv.shape[0]
            dist = (lax.broadcasted_iota(I32, (tq, wt), 0) - lax.broadcasted_iota(I32, (tq, wt), 1)
                    - LANE * off)
            wadd = _mask_add((dist >= 0) & (dist <= WINDOW))
            bds = [min(max(-off - j, 0), NEAR) for j in range(wt // LANE)]
            for g in range(B_KV):
                st = _flash_update(st, scr, g, qp_g[g], kv, strips_ref, far_ref, bds, wadd, tq, kv_t)
            return st

        if guard is None:
            st = win_tile(st)
        else:
            assert not in_values

            def guarded(win_tile=win_tile):
                win_tile(0)

            pl.when(guard)(guarded)
    _flash_store(st, scr)
    o_w = [_flash_result(scr, g) for g in range(B_KV)]

    gs = jax.nn.sigmoid(gates)
    outs = []
    for g in range(B_KV):
        gcol = lambda j: jnp.concatenate(
            [gs[:, (g * GROUP + r) * 3 + j:(g * GROUP + r) * 3 + j + 1] for r in range(GROUP)], axis=0)
        o = gcol(0) * o_c[g] + gcol(1) * o_s[g] + gcol(2) * o_w[g]
        outs += [o[r * tq:(r + 1) * tq] for r in range(GROUP)]
    return jnp.concatenate(outs, axis=1)


def _compress(load_rows, n_pairs, pe_ref, wbd_ref):
    acc = [jnp.zeros((n_pairs, KVW), F32), jnp.zeros((n_pairs, KVW), F32)]
    for c in range(2 * CMP_BLOCK):
        ci = c % CMP_BLOCK
        acc[c // CMP_BLOCK] = acc[c // CMP_BLOCK] + _mm(load_rows(c) + pe_ref[ci], wbd_ref[ci])
    return jnp.concatenate(acc, axis=0)


def _compress_operands(pe, w):
    pe_rows = jnp.transpose(jnp.broadcast_to(pe[:, None], (2, B_KV, CMP_BLOCK, HEAD_DIM)), (2, 0, 1, 3))
    pe_rows = pe_rows.reshape(CMP_BLOCK, 1, KVW)
    eye = jnp.eye(2 * B_KV, dtype=w.dtype).reshape(2, B_KV, 2, B_KV)
    wbd = jnp.einsum("icde,igjh->cigdjhe", w, eye).reshape(CMP_BLOCK, KVW, KVW)
    return pe_rows, wbd.astype(BF16)


def _strided_rows(lo_ref, hi_ref, c, n):
    rows = pl.ds(c, n, stride=2 * CMP_BLOCK)
    return jnp.concatenate([lo_ref[rows, :], hi_ref[rows, :]], axis=1)


def _flash_scratch(tq):
    return [pltpu.VMEM((A_KV, GROUP, tq, LANE), F32), pltpu.VMEM((A_KV, GROUP * tq, LANE), F32)]


TQ_P = 128
TK_P = 512


def _prompt_tile(ref, k, tk):
    return ref[pl.ds(pl.multiple_of(k * tk, tk), tk), :]


def _pdsa_kernel(qi_ref, misc_ref, qa_ref, kidx_ref, kva_ref, strips_ref, far_ref, o_ref,
                 keys_scr, wib_scr, m_scr, acc_scr, tri_scr, *, topk, tk):
    i = pl.program_id(1)
    o = _dsa_core(qi_ref[...], misc_ref[:, M_WI:M_WI + IDX_HEADS], qa_ref[...],
                  lambda k: _prompt_tile(kidx_ref, k, tk), lambda k: _prompt_tile(kva_ref, k, tk),
                  strips_ref, far_ref, i * TQ_P, TQ_P, tk, topk, False, None,
                  keys_scr, wib_scr, m_scr, acc_scr, tri_scr)
    o_ref[...] = o.astype(o_ref.dtype)


def _pdsa_call(pr, strips, far, tk):
    nb, s, _ = pr["qa"].shape
    assert s < ZERO_KEY_BASE
    tq = TQ_P
    row = lambda w: pl.BlockSpec((None, tq, w), lambda b, i: (b, i, 0))
    full = lambda w: pl.BlockSpec((None, s, w), lambda b, i: (b, 0, 0))
    return pl.pallas_call(
        functools.partial(_pdsa_kernel, topk=min(IDX_TOPK, s // 4), tk=tk),
        grid=(nb, s // tq),
        in_specs=[row(256), row(128), row(512), full(128), full(KVX),
                  pl.BlockSpec(strips.shape, lambda b, i: (0, 0, 0, 0)),
                  pl.BlockSpec(memory_space=pltpu.SMEM)],
        out_specs=row(512),
        out_shape=jax.ShapeDtypeStruct((nb, s, A_HEADS * HEAD_DIM), BF16),
        scratch_shapes=[pltpu.VMEM((s // tk, tq, tk), I32), pltpu.VMEM((IDX_HEADS, tq, tk), F32)]
        + _flash_scratch(tq) + [pltpu.VMEM((LANE, LANE), BF16)],
        compiler_params=_cparams(("arbitrary", "arbitrary")),
        name="dsa_prompt",
    )(pr["qi"], pr["misc"], pr["qa"], pr["misc_h"], pr["kva_x"], strips, far)


def _pcmp_kernel(lo_ref, hi_ref, pe_ref, wbd_ref, o_ref, *, n_pairs):
    o_ref[...] = _compress(lambda c: _strided_rows(lo_ref, hi_ref, c, n_pairs), n_pairs, pe_ref, wbd_ref)


def _pcmp_call(kvc, pe_rows, wbd):
    nb, s, _ = kvc.shape
    n_pairs = s // (2 * CMP_BLOCK)
    return pl.pallas_call(
        functools.partial(_pcmp_kernel, n_pairs=n_pairs),
        grid=(nb,),
        in_specs=[pl.BlockSpec((None, s, LANE), lambda b: (b, 0, 0)),
                  pl.BlockSpec((None, s, LANE), lambda b: (b, 0, 1)),
                  pl.BlockSpec((CMP_BLOCK, 1, KVW), lambda b: (0, 0, 0)),
                  pl.BlockSpec((CMP_BLOCK, KVW, KVW), lambda b: (0, 0, 0))],
        out_specs=pl.BlockSpec((None, 2 * n_pairs, KVW), lambda b: (b, 0, 0)),
        out_shape=jax.ShapeDtypeStruct((nb, 2 * n_pairs, KVW), F32),
        compiler_params=_cparams(("arbitrary",)),
        name="compress_prompt",
    )(kvc, kvc, pe_rows, wbd)


def _pnsa_kernel(qb_ref, misc_ref, kcp_ref, kvs_ref, kvw_ref, strips_ref, cstrips_ref, far_ref, o_ref,
                 m_scr, acc_scr, *, nj, n_pick, tk):
    i = pl.program_id(1)
    nw = WINDOW // LANE
    win_tiles = [(lambda: kvw_ref[pl.ds(pl.multiple_of((i - nw) * LANE, LANE), WINDOW + TQ_P), :], -nw, i >= nw)]
    win_tiles += [(lambda wd=wd: _prompt_tile(kvw_ref, i - wd, LANE), -wd, (i < nw) & (i - wd >= 0))
                  for wd in range(nw - 1, -1, -1)]
    o = _nsa_core(qb_ref[...], misc_ref[:, M_GATE:M_GATE + 3 * B_HEADS], kcp_ref[...],
                  lambda k: _prompt_tile(kvs_ref, k, tk), win_tiles,
                  strips_ref, cstrips_ref, far_ref, i * TQ_P, TQ_P, tk, nj, n_pick, False, False, None,
                  m_scr, acc_scr)
    o_ref[...] = o.astype(o_ref.dtype)


def _pnsa_call(pr, kcp, strips, cstrips, far, tk):
    nb, s, _ = pr["qb"].shape
    tq = TQ_P
    nj = s // SEL_BLOCK
    row = lambda w: pl.BlockSpec((None, tq, w), lambda b, i: (b, i, 0))
    full = lambda n, w: pl.BlockSpec((None, n, w), lambda b, i: (b, 0, 0))
    const = lambda a: pl.BlockSpec(a.shape, lambda b, i: (0,) * a.ndim)
    return pl.pallas_call(
        functools.partial(_pnsa_kernel, nj=nj, n_pick=min(SEL_TOPN, nj), tk=tk),
        grid=(nb, s // tq),
        in_specs=[row(512), row(128), full(2 * nj, KVW), full(s, KVX), full(s, KVX),
                  const(strips), const(cstrips),
                  pl.BlockSpec(memory_space=pltpu.SMEM)],
        out_specs=row(512),
        out_shape=jax.ShapeDtypeStruct((nb, s, B_HEADS * HEAD_DIM), BF16),
        scratch_shapes=_flash_scratch(tq),
        compiler_params=_cparams(("arbitrary", "arbitrary")),
        name="nsa_prompt",
    )(pr["qb"], pr["misc"], kcp, pr["kvs_x"], pr["kvw_x"], strips, cstrips, far)


TQ_S = 8
TK_S = 2048
PPT = TK_S // PAGE


def _pages_per_step(n_pages):
    return max(g for g in (8, 4, 2, 1) if n_pages % g == 0)


def _new_token_block(col):
    return jnp.where(lax.broadcasted_iota(I32, (col.shape[0], PAGE), 1) == 0, col, 0.0)


def _slot_tile(buf, k):
    blk = buf[pl.ds(k * PPT, PPT)]
    return jnp.concatenate([blk[j] for j in range(PPT)], axis=1)


def _sdsa_kernel(*refs, n_pages, pps, topk):
    idx_refs = refs[1:1 + pps]
    akv_refs = refs[1 + pps:1 + 2 * pps]
    (qi_ref, wi_ref, qa_ref, new_idx_ref, new_kv_ref, strips_ref, far_ref, o_ref,
     idx_buf, akv_buf, keys_scr, wib_scr, m_scr, acc_scr, tri_scr) = refs[1 + 2 * pps:]
    p = pl.program_id(1)
    for j in range(pps):
        idx_buf[p * pps + j] = idx_refs[j][...].astype(BF16)
        akv_buf[p * pps + j] = _kv_ext_t(akv_refs[j][...])

    @pl.when(p == n_pages // pps - 1)
    def _():
        idx_buf[n_pages] = _new_token_block(new_idx_ref[...]).astype(BF16)
        akv_buf[n_pages] = _kv_ext_t(_new_token_block(new_kv_ref[...]))
        for j in range(1, PPT):
            idx_buf[n_pages + j] = jnp.zeros((IDX_DIM, PAGE), BF16)
            akv_buf[n_pages + j] = _kv_ext_t(jnp.zeros((KVW, PAGE), BF16))
        o = _dsa_core(qi_ref[...], wi_ref[...], qa_ref[...],
                      lambda k: _slot_tile(idx_buf, k), lambda k: _slot_tile(akv_buf, k),
                      strips_ref, far_ref, n_pages * PAGE, TQ_S, TK_S, topk, True, True,
                      keys_scr, wib_scr, m_scr, acc_scr, tri_scr)
        o_ref[...] = o.astype(o_ref.dtype)


def _sample_rows(x):
    return jnp.broadcast_to(x[:, None, :], (x.shape[0], TQ_S, x.shape[1]))


def _pages_t(cache):
    n, p = cache.shape[:2]
    nd = cache.ndim
    return jnp.transpose(cache, (0, 1) + tuple(range(3, nd)) + (2,)).reshape(n, p, -1, cache.shape[2])


def _sdsa_call(a, page_table, idx_t, akv_t, pr, strips, far):
    bd, n_pages = page_table.shape
    assert idx_t.shape[3] == PAGE and (n_pages * PAGE) % TK_S == 0 and n_pages * PAGE + TK_S < ZERO_KEY_BASE
    pps = _pages_per_step(n_pages)
    slots = n_pages + PPT
    misc = pr["misc"][0]
    rows = lambda w: pl.BlockSpec((None, TQ_S, w), lambda b, p, pt: (b, 0, 0))
    col = lambda w: pl.BlockSpec((None, w, 1), lambda b, p, pt: (b, 0, 0))
    page = lambda w, j: pl.BlockSpec((None, None, w, PAGE),
                                     lambda b, p, pt: (a, pt[b * n_pages + p * pps + j], 0, 0))
    grid_spec = pltpu.PrefetchScalarGridSpec(
        num_scalar_prefetch=1,
        grid=(bd, n_pages // pps),
        in_specs=[page(IDX_DIM, j) for j in range(pps)] + [page(KVW, j) for j in range(pps)]
        + [rows(256), rows(IDX_HEADS), rows(512), col(IDX_DIM), col(KVW),
           pl.BlockSpec(strips.shape, lambda b, p, pt: (0, 0, 0, 0)),
           pl.BlockSpec(memory_space=pltpu.SMEM)],
        out_specs=rows(512),
        scratch_shapes=[pltpu.VMEM((slots, IDX_DIM, PAGE), BF16), pltpu.VMEM((slots, KVX, PAGE), BF16),
                        pltpu.VMEM((slots // PPT, TQ_S, TK_S), I32), pltpu.VMEM((IDX_HEADS, TQ_S, TK_S), F32)]
        + _flash_scratch(TQ_S) + [pltpu.VMEM((LANE, LANE), BF16)])
    out = pl.pallas_call(
        functools.partial(_sdsa_kernel, n_pages=n_pages, pps=pps, topk=min(IDX_TOPK, (n_pages * PAGE + 1) // 4)),
        grid_spec=grid_spec,
        out_shape=jax.ShapeDtypeStruct((bd, TQ_S, A_HEADS * HEAD_DIM), BF16),
        compiler_params=_cparams(("arbitrary", "arbitrary")),
        name="dsa_sample",
    )(page_table.reshape(-1), *([idx_t] * pps), *([akv_t] * pps),
      _sample_rows(pr["qi"][0]), _sample_rows(misc[:, M_WI:M_WI + IDX_HEADS]), _sample_rows(pr["qa"][0]),
      misc[:, M_KI:M_KI + IDX_DIM, None], pr["kva"][0][:, :, None], strips, far)
    return out[:, 0]


def _snsa_kernel(*refs, n_pages, pps):
    cmp_refs = refs[1:1 + pps]
    sel_refs = refs[1 + pps:1 + 2 * pps]
    (qb_ref, gates_ref, new_sel_ref, win_ref, new_win_ref, pe_ref, wbd_ref, strips_ref, cstrips_ref, far_ref,
     o_ref, cmp_lo, cmp_hi, sel_buf, m_scr, acc_scr) = refs[1 + 2 * pps:]
    p = pl.program_id(1)
    past = n_pages * PAGE
    for j in range(pps):
        rows = pl.ds(pl.multiple_of((p * pps + j) * PAGE, PAGE), PAGE)
        cmp_lo[rows, :] = jnp.transpose(cmp_refs[j][:LANE])
        cmp_hi[rows, :] = jnp.transpose(cmp_refs[j][LANE:])
        sel_buf[p * pps + j] = _kv_ext_t(sel_refs[j][...])

    @pl.when(p == n_pages // pps - 1)
    def _():
        sel_buf[n_pages] = _kv_ext_t(_new_token_block(new_sel_ref[...]))
        for j in range(1, PPT):
            sel_buf[n_pages + j] = _kv_ext_t(jnp.zeros((KVW, PAGE), BF16))
        nj = past // SEL_BLOCK
        kcp = _compress(lambda c: _strided_rows(cmp_lo, cmp_hi, c, nj), nj, pe_ref, wbd_ref)
        win_tiles = [(lambda: _kv_ext_t(win_ref[...]), -(WINDOW // LANE), None),
                     (lambda: _kv_ext_t(_new_token_block(new_win_ref[...])), 0, None)]
        o = _nsa_core(qb_ref[...], gates_ref[...], kcp, lambda k: _slot_tile(sel_buf, k), win_tiles,
                      strips_ref, cstrips_ref, far_ref, past, TQ_S, TK_S, nj,
                      min(SEL_TOPN, nj + 1) - 1, True, True, True, m_scr, acc_scr)
        o_ref[...] = o.astype(o_ref.dtype)


def _snsa_call(a, page_table, cmp_t, sel_t, win_t, pr, pe_rows, wbd, strips, cstrips, far):
    bd, n_pages = page_table.shape
    past = n_pages * PAGE
    assert past % TK_S == 0 and win_t.shape[3] == WINDOW
    pps = _pages_per_step(n_pages)
    misc = pr["misc"][0]
    rows = lambda w: pl.BlockSpec((None, TQ_S, w), lambda b, p, pt: (b, 0, 0))
    col = lambda w: pl.BlockSpec((None, w, 1), lambda b, p, pt: (b, 0, 0))
    page = lambda j: pl.BlockSpec((None, None, KVW, PAGE),
                                  lambda b, p, pt: (a, pt[b * n_pages + p * pps + j], 0, 0))
    const = lambda x: pl.BlockSpec(x.shape, lambda b, p, pt: (0,) * x.ndim)
    grid_spec = pltpu.PrefetchScalarGridSpec(
        num_scalar_prefetch=1,
        grid=(bd, n_pages // pps),
        in_specs=[page(j) for j in range(pps)] + [page(j) for j in range(pps)]
        + [rows(512), rows(3 * B_HEADS), col(KVW),
           pl.BlockSpec((None, None, KVW, WINDOW), lambda b, p, pt: (a, b, 0, 0)), col(KVW),
           const(pe_rows), const(wbd), const(strips), const(cstrips),
           pl.BlockSpec(memory_space=pltpu.SMEM)],
        out_specs=rows(512),
        scratch_shapes=[pltpu.VMEM((past, LANE), F32), pltpu.VMEM((past, LANE), F32),
                        pltpu.VMEM((n_pages + PPT, KVX, PAGE), BF16)] + _flash_scratch(TQ_S))
    out = pl.pallas_call(
        functools.partial(_snsa_kernel, n_pages=n_pages, pps=pps),
        grid_spec=grid_spec,
        out_shape=jax.ShapeDtypeStruct((bd, TQ_S, B_HEADS * HEAD_DIM), BF16),
        compiler_params=_cparams(("arbitrary", "arbitrary")),
        name="nsa_sample",
    )(page_table.reshape(-1), *([cmp_t] * pps), *([sel_t] * pps),
      _sample_rows(pr["qb"][0]), _sample_rows(misc[:, M_GATE:M_GATE + 3 * B_HEADS]),
      pr["kvs"][0][:, :, None], win_t, pr["kvw"][0][:, :, None],
      pe_rows, wbd, strips, cstrips, far)
    return out[:, 0]


def _merge_kernel(x_ref, mod_ref, oa_ref, ob_ref, w_ref, o_ref):
    n = oa_ref.shape[-1]
    y = _mm(oa_ref[...], w_ref[:n]) + _mm(ob_ref[...], w_ref[n:])
    o_ref[...] = x_ref[...] + (1.0 + mod_ref[2]) * y


def _merge_call(x, mod, oa, ob, w, tm):
    nb, t, d = x.shape
    tmod = mod.shape[2]
    mod_spec = (pl.BlockSpec((None, 3, 1, d), lambda b, i: (b, 0, 0, 0)) if tmod == 1 else
                pl.BlockSpec((None, 3, tm, d), lambda b, i: (b, 0, i, 0)))
    row = lambda w_: pl.BlockSpec((None, tm, w_), lambda b, i: (b, i, 0))
    return pl.pallas_call(
        _merge_kernel,
        grid=(nb, t // tm),
        in_specs=[row(d), mod_spec, row(oa.shape[-1]), row(ob.shape[-1]),
                  pl.BlockSpec(w.shape, lambda b, i: (0, 0))],
        out_specs=row(d),
        out_shape=jax.ShapeDtypeStruct(x.shape, F32),
        compiler_params=_cparams(("arbitrary", "arbitrary")),
        name="merge",
    )(x, mod, oa, ob, w)


def _gelu_ln(h, w_in_ref, ln_ref, cw):
    uv = jax.nn.gelu(_mm(h, w_in_ref[...]))
    u, v = uv[:, :cw], uv[:, cw:]
    mu = jnp.mean(v, axis=-1, keepdims=True)
    var = jnp.mean(jnp.square(v - mu), axis=-1, keepdims=True)
    return u, (v - mu) * lax.rsqrt(var + EPS) * ln_ref[...]


def _pgmlp_kernel(x_ref, mod_ref, g_ref, w_in_ref, ln_ref, ws_ref, bs_ref, w_out_ref, o_ref, *, cw):
    x = x_ref[...]
    h = _modulated_norm(x, g_ref[...], mod_ref).astype(BF16)
    u, v = _gelu_ln(h, w_in_ref, ln_ref, cw)
    gw = cw // C_GROUPS
    rows = []
    for n in range(x.shape[0] // C_CHUNK):
        vb = v[n * C_CHUNK:(n + 1) * C_CHUNK].astype(BF16)
        sv = [_mm(ws_ref[g], vb[:, g * gw:(g + 1) * gw]) + bs_ref[:, g:g + 1] for g in range(C_GROUPS)]
        rows.append(jnp.concatenate(sv, axis=1))
    sv = rows[0] if len(rows) == 1 else jnp.concatenate(rows, axis=0)
    o_ref[...] = x + (1.0 + mod_ref[2]) * _mm(u * sv, w_out_ref[...])


def _pgmlp_call(x, mod, g, w_in, ln_g, ws, bs_t, w_out, tm):
    nb, t, d = x.shape
    cw = w_out.shape[0]
    const = lambda a: pl.BlockSpec(a.shape, lambda b, i: (0,) * a.ndim)
    return pl.pallas_call(
        functools.partial(_pgmlp_kernel, cw=cw),
        grid=(nb, t // tm),
        in_specs=[pl.BlockSpec((None, tm, d), lambda b, i: (b, i, 0)),
                  pl.BlockSpec((None, 3, 1, d), lambda b, i: (b, 0, 0, 0)),
                  pl.BlockSpec((1, d), lambda b, i: (0, 0)),
                  const(w_in), pl.BlockSpec((1, cw), lambda b, i: (0, 0)), const(ws), const(bs_t), const(w_out)],
        out_specs=pl.BlockSpec((None, tm, d), lambda b, i: (b, i, 0)),
        out_shape=jax.ShapeDtypeStruct(x.shape, F32),
        compiler_params=_cparams(("arbitrary", "arbitrary")),
        name="gmlp_prompt",
    )(x, mod, g.reshape(1, d), w_in, ln_g.reshape(1, cw), ws, bs_t, w_out)


def _sgmlp_kernel(x_ref, mod_ref, g_ref, w_in_ref, ln_ref, ws0_ref, bs0_ref, w_out_ref, o_ref, v_ref, *, cw):
    x = x_ref[...]
    h = _modulated_norm(x, g_ref[...], mod_ref).astype(BF16)
    u, v = _gelu_ln(h, w_in_ref, ln_ref, cw)
    v_ref[...] = v
    sv = v.astype(BF16).astype(F32) * ws0_ref[...].astype(F32) + bs0_ref[...]
    o_ref[...] = x + (1.0 + mod_ref[2]) * _mm(u * sv, w_out_ref[...])


def _sgmlp_call(x, mod, g, w_in, ln_g, ws, b_sp, w_out):
    nb, t, d = x.shape
    cw = w_out.shape[0]
    gw = cw // C_GROUPS
    ws0 = jnp.repeat(ws[:, 0, 0], gw).reshape(1, cw)
    bs0 = jnp.repeat(b_sp[:, 0], gw).reshape(1, cw)
    full = lambda a: pl.BlockSpec(a.shape, lambda: (0,) * a.ndim)
    x2, mod2 = x[0], mod[0]
    o, v = pl.pallas_call(
        functools.partial(_sgmlp_kernel, cw=cw),
        in_specs=[full(x2), full(mod2), pl.BlockSpec((1, d), lambda: (0, 0)), full(w_in),
                  pl.BlockSpec((1, cw), lambda: (0, 0)), full(ws0), full(bs0), full(w_out)],
        out_specs=[full(x2), pl.BlockSpec((t, cw), lambda: (0, 0))],
        out_shape=[jax.ShapeDtypeStruct(x2.shape, F32), jax.ShapeDtypeStruct((t, cw), F32)],
        compiler_params=pltpu.CompilerParams(vmem_limit_bytes=VMEM_LIMIT),
        name="gmlp_sample",
    )(x2, mod2, g.reshape(1, d), w_in, ln_g.reshape(1, cw), ws0, bs0, w_out)
    return o[None], v


def kernel(x_prompt, x_sample, cache_a_kv, cache_a_idx, cache_b_cmp_kv, cache_b_sel_kv, state_b_win_kv,
           page_table, c_prompt, c_sample, rel_bias, w_ada, b_ada, norm_g, w_ffn_in, w_ffn_out,
           w_in_att, w_out_att, cmp_pe, w_cmp, w_in_c, ln_c_g, w_sp, b_sp, w_out_c, final_g):
    nbp, s, d = x_prompt.shape
    nbs, tdec, _ = x_sample.shape
    depth = w_ada.shape[0]
    assert tdec == 1 and s % TQ_P == 0 and cache_a_kv.shape[2] == PAGE

    ada = _ada_call(jnp.concatenate([c_prompt, c_sample], axis=0), w_ada, b_ada)
    ada = ada.reshape(depth, nbp + nbs, 3, 3, d)
    mod_p = lambda l, j: ada[l, :nbp, j][:, :, None, :]
    mod_s = lambda l, j: jnp.transpose(ada[l, nbp:, j], (1, 0, 2))[None]

    w_ffn_in_h = w_ffn_in.astype(BF16)
    w_ffn_out_h = w_ffn_out.astype(BF16)
    tab_a, tab_b = rel_bias[:, :A_HEADS], rel_bias[:, A_HEADS:]
    strips_pa, strips_pb = _bias_strips(tab_a, TQ_P), _bias_strips(tab_b, TQ_P)
    strips_sa, strips_sb = _bias_strips(tab_a, TQ_S), _bias_strips(tab_b, TQ_S)
    cstrips_p, cstrips_s = _cmp_strips(tab_b, TQ_P), _cmp_strips(tab_b, TQ_S)
    far_a, far_b = tab_a[N_BUCKETS - 1], tab_b[N_BUCKETS - 1]
    idx_t, akv_t = _pages_t(cache_a_idx), _pages_t(cache_a_kv)
    cmp_t, sel_t, win_t = _pages_t(cache_b_cmp_kv), _pages_t(cache_b_sel_kv), _pages_t(state_b_win_kv)

    xp = x_prompt
    xs = jnp.transpose(x_sample, (1, 0, 2))
    tm_p = 512 if s % 512 == 0 else TQ_P
    tk_p = TK_P if s % TK_P == 0 else TQ_P
    st_p, st_s, st_c = [], [], []
    for l in range(depth):
        last = l == depth - 1
        xp = _ffn_call(xp, mod_p(l, 0), norm_g[l, 0], w_ffn_in_h, w_ffn_out_h, l, 0, None, tm_p)
        xs = _ffn_call(xs, mod_s(l, 0), norm_g[l, 0], w_ffn_in_h, w_ffn_out_h, l, 0, None, nbs)
        if l % 2 == 0:
            a = l // 2
            wp = _rearranged_proj_weight(w_in_att[a])
            w_out = w_out_att[a].astype(BF16)
            pe_rows, wbd = _compress_operands(cmp_pe[a], w_cmp[a])

            pr = _proj_call(xp, mod_p(l, 1), norm_g[l, 1], wp, tm_p)
            oa = _pdsa_call(pr, strips_pa, far_a, tk_p)
            kcp = _pcmp_call(pr["kvc"], pe_rows, wbd)
            ob = _pnsa_call(pr, kcp, strips_pb, cstrips_p, far_b, tk_p)
            xp = _merge_call(xp, mod_p(l, 1), oa, ob, w_out, tm_p)
            kv5 = lambda t: t.reshape(t.shape[0], t.shape[1], 2, A_KV, HEAD_DIM)
            st_p.append((kv5(pr["kva"]), pr["misc"][:, :, M_KI:M_KI + IDX_DIM], kv5(pr["kvc"]),
                         kv5(pr["kvs"]), kv5(pr["kvw"][:, s - min(WINDOW, s):])))

            ps = _proj_call(xs, mod_s(l, 1), norm_g[l, 1], wp, nbs)
            oa = _sdsa_call(a, page_table, idx_t, akv_t, ps, strips_sa, far_a)
            ob = _snsa_call(a, page_table, cmp_t, sel_t, win_t, ps, pe_rows, wbd, strips_sb, cstrips_s, far_b)
            xs = _merge_call(xs, mod_s(l, 1), oa[None], ob[None], w_out, nbs)
            tok = lambda t: t[0].reshape(nbs, 1, 2, A_KV, HEAD_DIM)
            st_s.append((tok(ps["kva"]), ps["misc"][0][:, None, M_KI:M_KI + IDX_DIM], tok(ps["kvc"]),
                         tok(ps["kvs"]),
                         jnp.concatenate([state_b_win_kv[a][:, tdec:], tok(ps["kvw"])], axis=1)))
        else:
            ci = l // 2
            w_in = w_in_c[ci].astype(BF16)
            w_out = w_out_c[ci].astype(BF16)
            ws = (w_sp[ci] * jnp.tril(jnp.ones((C_CHUNK, C_CHUNK), w_sp.dtype))).astype(BF16)
            xp = _pgmlp_call(xp, mod_p(l, 1), norm_g[l, 1], w_in, ln_c_g[ci], ws,
                             jnp.transpose(b_sp[ci]), w_out, 256 if s % 256 == 0 else C_CHUNK)
            xs, v = _sgmlp_call(xs, mod_s(l, 1), norm_g[l, 1], w_in, ln_c_g[ci], ws, b_sp[ci], w_out)
            st_c.append(v[:, None, :])
        fg = final_g if last else None
        xp = _ffn_call(xp, mod_p(l, 2), norm_g[l, 2], w_ffn_in_h, w_ffn_out_h, l, 1, fg, tm_p)
        xs = _ffn_call(xs, mod_s(l, 2), norm_g[l, 2], w_ffn_in_h, w_ffn_out_h, l, 1, fg, nbs)

    stk = lambda sts, i: jnp.stack([st[i] for st in sts])
    return (xp, jnp.transpose(xs, (1, 0, 2)),
            stk(st_p, 0), stk(st_p, 1), stk(st_p, 2), stk(st_p, 3), stk(st_p, 4),
            stk(st_s, 0), stk(st_s, 1), stk(st_s, 2), stk(st_s, 3), stk(st_s, 4),
            jnp.stack(st_c))
```

```python
import functools
import math

import numpy as np
import jax
import jax.numpy as jnp
from jax import lax
from jax.experimental import pallas as pl
from jax.experimental.pallas import tpu as pltpu

F32 = jnp.float32
BF16 = jnp.bfloat16
I32 = jnp.int32

HEAD_DIM = 64
A_HEADS = 8
A_KV = 2
IDX_HEADS = 4
IDX_DIM = 64
IDX_TOPK = 256
B_HEADS = 8
B_KV = 2
CMP_BLOCK = 32
SEL_BLOCK = 64
SEL_TOPN = 16
WINDOW = 512
N_BUCKETS = 32
MAX_DISTANCE = 1024
C_CHUNK = 128
C_GROUPS = 8
EPS = 1e-6
PAGE = 128
GROUP = A_HEADS // A_KV
KVW = 2 * A_KV * HEAD_DIM

LANE = 128
VMEM_LIMIT = 56 * 1024 * 1024
KVX = 3 * LANE

NEG = -1e30
KMIN = -2 ** 31

P_QA, P_QB, P_KVA, P_QI, P_KVC, P_KVS, P_KVW, P_MISC = 0, 512, 1024, 1280, 1536, 1792, 2048, 2304
P_WIDTH = 2432
M_KI, M_WI, M_GATE = 0, IDX_DIM, IDX_DIM + IDX_HEADS

assert A_KV == 2 and B_KV == 2 and A_KV * HEAD_DIM == LANE and IDX_DIM == HEAD_DIM


def _cparams(sem):
    return pltpu.CompilerParams(dimension_semantics=sem, vmem_limit_bytes=VMEM_LIMIT)


def _mm(a, b):
    return jnp.dot(a.astype(BF16), b.astype(BF16), preferred_element_type=F32)


def _mm_nt(a, b):
    return lax.dot_general(a.astype(BF16), b.astype(BF16), (((1,), (1,)), ((), ())),
                           preferred_element_type=F32)


def _modulated_norm(x, g, mod_ref):
    y = x * lax.rsqrt(jnp.mean(x * x, axis=-1, keepdims=True) + EPS) * g
    return y * (1.0 + mod_ref[1]) + mod_ref[0]


def _ada_kernel(c_ref, w_ref, b_ref, o_ref):
    c = c_ref[...]
    o_ref[...] = _mm(c * jax.nn.sigmoid(c), w_ref[...]) + b_ref[...]


def _ada_call(c_all, w_ada, b_ada):
    depth, d, n = w_ada.shape
    r = c_all.shape[0]
    tn = n // 8
    return pl.pallas_call(
        _ada_kernel,
        grid=(depth, n // tn),
        in_specs=[pl.BlockSpec((r, d), lambda l, j: (0, 0)),
                  pl.BlockSpec((None, d, tn), lambda l, j: (l, 0, j)),
                  pl.BlockSpec((None, 1, tn), lambda l, j: (l, 0, j))],
        out_specs=pl.BlockSpec((None, r, tn), lambda l, j: (l, 0, j)),
        out_shape=jax.ShapeDtypeStruct((depth, r, n), F32),
        compiler_params=_cparams(("arbitrary", "arbitrary")),
        name="ada",
    )(c_all, w_ada, b_ada.reshape(depth, 1, n))


def _ffn_kernel(x_ref, mod_ref, g_ref, wa_ref, wb_ref, wo_ref, fg_ref, o_ref, h_scr, acc_scr, *,
                n_f, final_norm):
    f = pl.program_id(2)

    @pl.when(f == 0)
    def _():
        h_scr[...] = _modulated_norm(x_ref[...], g_ref[...], mod_ref).astype(BF16)
        acc_scr[...] = jnp.zeros_like(acc_scr)

    h = h_scr[...]
    a = _mm(h, wa_ref[...])
    b = _mm(h, wb_ref[...])
    acc_scr[...] += _mm(a * jax.nn.sigmoid(a) * b, wo_ref[...])

    @pl.when(f == n_f - 1)
    def _():
        y = x_ref[...] + 0.5 * (1.0 + mod_ref[2]) * acc_scr[...]
        if final_norm:
            y = y * lax.rsqrt(jnp.mean(y * y, axis=-1, keepdims=True) + EPS) * fg_ref[...]
        o_ref[...] = y


def _ffn_call(x, mod, g, w_in, w_out, l, j, final_g, tm):
    nb, t, d = x.shape
    dff = w_out.shape[2]
    tf = dff // 2 if dff % (2 * LANE) == 0 else LANE
    n_f = dff // tf
    tmod = mod.shape[2]
    mod_spec = (pl.BlockSpec((None, 3, 1, d), lambda b, i, f: (b, 0, 0, 0)) if tmod == 1 else
                pl.BlockSpec((None, 3, tm, d), lambda b, i, f: (b, 0, i, 0)))
    final_norm = final_g is not None
    fg = (final_g if final_norm else g).reshape(1, d)
    return pl.pallas_call(
        functools.partial(_ffn_kernel, n_f=n_f, final_norm=final_norm),
        grid=(nb, t // tm, n_f),
        in_specs=[pl.BlockSpec((None, tm, d), lambda b, i, f: (b, i, 0)),
                  mod_spec,
                  pl.BlockSpec((1, d), lambda b, i, f: (0, 0)),
                  pl.BlockSpec((None, None, d, tf), lambda b, i, f: (l, j, 0, f)),
                  pl.BlockSpec((None, None, d, tf), lambda b, i, f: (l, j, 0, f + n_f)),
                  pl.BlockSpec((None, None, tf, d), lambda b, i, f: (l, j, f, 0)),
                  pl.BlockSpec((1, d), lambda b, i, f: (0, 0))],
        out_specs=pl.BlockSpec((None, tm, d), lambda b, i, f: (b, i, 0)),
        out_shape=jax.ShapeDtypeStruct(x.shape, F32),
        scratch_shapes=[pltpu.VMEM((tm, d), BF16), pltpu.VMEM((tm, d), F32)],
        compiler_params=_cparams(("arbitrary", "arbitrary", "arbitrary")),
        name="ffn",
    )(x, mod, g.reshape(1, d), w_in, w_in, w_out, fg)


def _kv_ext(y):
    e = jnp.where(lax.broadcasted_iota(I32, (y.shape[0], HEAD_DIM), 1) == 0, 1.0, 0.0).astype(BF16)
    y = y.astype(BF16)
    return jnp.concatenate([y[:, :LANE], y[:, LANE:LANE + HEAD_DIM], e, y[:, LANE + HEAD_DIM:], e], axis=1)


def _kv_ext_t(y):
    e = jnp.where(lax.broadcasted_iota(I32, (HEAD_DIM, y.shape[1]), 0) == 0, 1.0, 0.0).astype(BF16)
    y = y.astype(BF16)
    return jnp.concatenate([y[:LANE], y[LANE:LANE + HEAD_DIM], e, y[LANE + HEAD_DIM:], e], axis=0)


_PROJ_OUT = (("qa", P_QA, 512, "h"), ("qb", P_QB, 512, "h"), ("qi", P_QI, 256, "h"),
             ("kva", P_KVA, 256, "f"), ("kvc", P_KVC, 256, "f"), ("kvs", P_KVS, 256, "f"),
             ("kvw", P_KVW, 256, "f"), ("misc", P_MISC, 128, "f"),
             ("kva_x", P_KVA, KVX, "x"), ("kvs_x", P_KVS, KVX, "x"), ("kvw_x", P_KVW, KVX, "x"),
             ("misc_h", P_MISC, 128, "h"))


def _proj_kernel(x_ref, mod_ref, g_ref, w_ref, *o_refs):
    h = _modulated_norm(x_ref[...], g_ref[...], mod_ref).astype(BF16)
    done = {}
    for (name, off, width, kind), o_ref in zip(_PROJ_OUT, o_refs):
        if off not in done:
            y = _mm(h, w_ref[:, off:off + (KVW if kind == "x" else width)])
            if name in ("qa", "qb"):
                y = y * (HEAD_DIM ** -0.5)
            done[off] = y
        o_ref[...] = _kv_ext(done[off]) if kind == "x" else done[off].astype(o_ref.dtype)


def _proj_call(x, mod, g, wp, tm):
    nb, t, d = x.shape
    tmod = mod.shape[2]
    mod_spec = (pl.BlockSpec((None, 3, 1, d), lambda b, i: (b, 0, 0, 0)) if tmod == 1 else
                pl.BlockSpec((None, 3, tm, d), lambda b, i: (b, 0, i, 0)))
    outs = pl.pallas_call(
        _proj_kernel,
        grid=(nb, t // tm),
        in_specs=[pl.BlockSpec((None, tm, d), lambda b, i: (b, i, 0)),
                  mod_spec,
                  pl.BlockSpec((1, d), lambda b, i: (0, 0)),
                  pl.BlockSpec((d, P_WIDTH), lambda b, i: (0, 0))],
        out_specs=[pl.BlockSpec((None, tm, w), lambda b, i: (b, i, 0)) for _, _, w, _ in _PROJ_OUT],
        out_shape=[jax.ShapeDtypeStruct((nb, t, w), F32 if kind == "f" else BF16)
                   for _, _, w, kind in _PROJ_OUT],
        compiler_params=_cparams(("arbitrary", "arbitrary")),
        name="proj",
    )(x, mod, g.reshape(1, d), wp)
    return {name: o for (name, _, _, _), o in zip(_PROJ_OUT, outs)}


def _rearranged_proj_weight(w):
    d = w.shape[0]
    o = np.cumsum((0, 512, 256, 256, 64, 4, 512, 256, 256, 256, 24))
    seg = lambda i: w[:, o[i]:o[i + 1]]
    qa, kva, qi, ki, wi, qb, kvc, kvs, kvw, gates = [seg(i) for i in range(10)]
    pad = jnp.zeros((d, LANE - IDX_DIM - IDX_HEADS - 3 * B_HEADS), w.dtype)
    return jnp.concatenate([qa, qb, kva, qi, kvc, kvs, kvw, ki, wi, gates, pad], axis=1).astype(BF16)


def _t5_bucket(dist):
    n = jnp.maximum(dist, 0)
    max_exact = N_BUCKETS // 2
    nf = jnp.maximum(n, 1).astype(F32)
    large = max_exact + (jnp.log(nf / max_exact) / math.log(MAX_DISTANCE / max_exact)
                         * (N_BUCKETS - max_exact)).astype(I32)
    large = jnp.minimum(large, N_BUCKETS - 1)
    return jnp.where(n < max_exact, n, large)


def _bias_lookup(tab, dist):
    b = _t5_bucket(jnp.asarray(dist.reshape(-1), I32))
    onehot = (b[:, None] == jnp.arange(N_BUCKETS, dtype=I32)[None, :]).astype(F32)
    out = jnp.dot(onehot, tab, precision=lax.Precision.HIGHEST)
    return jnp.transpose(out).reshape((tab.shape[1],) + dist.shape)


NEAR = -(-(MAX_DISTANCE + LANE - 1) // LANE)


def _bias_strips(tab, tq):
    r = np.arange(tq)[None, :, None]
    c = np.arange(LANE)[None, None, :]
    bd = np.arange(NEAR + 1)[:, None, None]
    dist = np.where(bd == NEAR, MAX_DISTANCE, r - c + LANE * bd)
    return _bias_lookup(tab, dist)


def _cmp_near_count(delta):
    return -(-(-(-(MAX_DISTANCE - delta) // CMP_BLOCK)) // 8) * 8


def _cmp_delta(tq):
    return -(CMP_BLOCK - 1) - CMP_BLOCK * ((tq - CMP_BLOCK) // CMP_BLOCK)


def _cmp_strips(tab, tq):
    delta = _cmp_delta(tq)
    ncn = _cmp_near_count(delta)
    dist = np.arange(tq)[:, None] + delta + CMP_BLOCK * np.arange(ncn)[None, :]
    v = _bias_lookup(tab, dist)
    hi = v.astype(BF16)
    r1 = v - hi.astype(F32)
    mid = r1.astype(BF16)
    lo = (r1 - mid.astype(F32)).astype(BF16)
    return jnp.stack([hi, mid, lo], axis=1)


def _padded_heads(x, g):
    z = jnp.zeros((x.shape[0], HEAD_DIM), x.dtype)
    rows = []
    for r in range(GROUP):
        h = g * GROUP + r
        piece = x[:, h * HEAD_DIM:(h + 1) * HEAD_DIM]
        rows.append(jnp.concatenate([piece, z] if g == 0 else [z, piece], axis=1))
    return jnp.concatenate(rows, axis=0)


def _k_part(kv, kv_t):
    return kv[:LANE] if kv_t else kv[:, :LANE]


def _v_part(kv, g, kv_t):
    return kv[(1 + g) * LANE:(2 + g) * LANE] if kv_t else kv[:, (1 + g) * LANE:(2 + g) * LANE]


def _flash_init(scr, tq, in_values):
    if in_values:
        return tuple((jnp.full((GROUP, tq, LANE), NEG, F32), jnp.zeros((GROUP * tq, LANE), F32))
                     for _ in range(A_KV))
    scr[0][...] = jnp.full_like(scr[0], NEG)
    scr[1][...] = jnp.zeros_like(scr[1])
    return 0


def _flash_store(st, scr):
    if isinstance(st, tuple):
        for g, (m, acc) in enumerate(st):
            scr[0][g] = m
            scr[1][g] = acc


def _mask_add(mask):
    return jnp.where(mask, 0.0, NEG)


def _flash_update(st, scr, g, qp_g, kv, strips_ref, far_ref, bds, madd, tq, kv_t):
    m_all, acc = st[g] if isinstance(st, tuple) else (scr[0][g], scr[1][g])
    kb = _k_part(kv, kv_t)
    s_all = _mm(qp_g, kb) if kv_t else _mm_nt(qp_g, kb)
    nblk = madd.shape[1] // LANE
    ps, alphas, m_news = [], [], []
    for r in range(GROUP):
        h = g * GROUP + r
        blocks = []
        for j in range(nblk):
            sj = s_all[r * tq:(r + 1) * tq, j * LANE:(j + 1) * LANE] + madd[:, j * LANE:(j + 1) * LANE]
            blocks.append(sj if bds is None else sj + strips_ref[h, bds[j]])
        bmax = blocks[0]
        for sj in blocks[1:]:
            bmax = jnp.maximum(bmax, sj)
        row_max = jnp.max(bmax, axis=-1, keepdims=True)
        m_old = m_all[r]
        if bds is None:
            m_new = jnp.maximum(m_old, row_max + far_ref[h])
            shift = m_new - far_ref[h]
        else:
            m_new = jnp.maximum(m_old, row_max)
            shift = m_new
        m_news.append(m_new)
        alphas.append(jnp.exp(m_old - m_new))
        ps.append(jnp.concatenate([jnp.exp(sj - shift).astype(BF16) for sj in blocks], axis=1))
    p = jnp.concatenate(ps, axis=0)
    vx = _v_part(kv, g, kv_t)
    pv = _mm_nt(p, vx) if kv_t else _mm(p, vx)
    m_all = jnp.stack(m_news)
    acc = jnp.concatenate(alphas, axis=0) * acc + pv
    if isinstance(st, tuple):
        return st[:g] + ((m_all, acc),) + st[g + 1:]
    scr[0][g] = m_all
    scr[1][g] = acc
    return st


def _flash_result(scr, g):
    acc = scr[1][g]
    l = acc[:, HEAD_DIM:HEAD_DIM + 1]
    return jnp.where(l > 0.0, acc[:, :HEAD_DIM] / jnp.where(l > 0.0, l, 1.0), 0.0)


def _tile_bds(qb, k, tk):
    nblk = tk // LANE
    return [jnp.clip(qb - (k * nblk + j), 0, NEAR) for j in range(nblk)]


ZERO_KEY_BASE = 2 ** 16 - 1


def _ordered_key(x, pos):
    b = lax.bitcast_convert_type(x, I32)
    return jnp.where(x == 0.0, ZERO_KEY_BASE - pos, b ^ ((b >> 31) & 0x7FFFFFFF))


def _far_tiles(qb, tk):
    n = (qb - NEAR + 1) // (tk // LANE)
    return max(n, 0) if isinstance(n, int) else jnp.maximum(n, 0)


def _dsa_core(qi, wi, qa, get_kidx, get_kva, strips_ref, far_ref, t0, tq, tk, topk, kv_t, unroll,
              keys_scr, wib_scr, m_scr, acc_scr, tri_scr):
    n_tiles = (t0 + tq - 1) // tk + 1
    qb = t0 // LANE
    n_far = _far_tiles(qb, tk)
    row_pos = t0 + lax.broadcasted_iota(I32, (tq, tk), 0)
    col = lax.broadcasted_iota(I32, (tq, tk), 1)

    if kv_t:
        qi_st = jnp.concatenate([qi[:, h * IDX_DIM:(h + 1) * IDX_DIM] for h in range(IDX_HEADS)], axis=0)
    else:
        z = jnp.zeros((tq, LANE - IDX_DIM), qi.dtype)
        qi_st = jnp.concatenate(
            [jnp.concatenate([qi[:, h * IDX_DIM:(h + 1) * IDX_DIM], z], axis=1) for h in range(IDX_HEADS)], axis=0)
    for h in range(IDX_HEADS):
        wib_scr[h] = jnp.broadcast_to(wi[:, h:h + 1], (tq, tk))

    def score_tile(k, c):
        kt = get_kidx(k)
        sc = _mm(qi_st, kt) if kv_t else _mm_nt(qi_st, kt)
        score = jnp.maximum(sc[:tq], 0.0) * wib_scr[0]
        for h in range(1, IDX_HEADS):
            score = score + jnp.maximum(sc[h * tq:(h + 1) * tq], 0.0) * wib_scr[h]
        pos = k * tk + col
        keys_scr[k] = jnp.where(pos <= row_pos, _ordered_key(score, pos), KMIN)
        return c

    lax.fori_loop(0, n_tiles, score_tile, 0, unroll=unroll)

    def count(pred):
        def count_tile(k, cnt):
            key = keys_scr[k]
            for j in range(tk // LANE):
                cnt = cnt + jnp.where(pred(key[:, j * LANE:(j + 1) * LANE]), 1.0, 0.0)
            return cnt

        cnt = lax.fori_loop(0, n_tiles, count_tile, jnp.zeros((tq, LANE), F32))
        return jnp.sum(cnt, axis=-1, keepdims=True)

    tu = jnp.zeros((tq, 1), I32)
    n_ge = jnp.zeros((tq, 1), F32)
    for bit in range(31, -1, -1):
        cand_u = tu | np.int32(-2 ** 31 if bit == 31 else 2 ** bit)
        cand_s = jnp.broadcast_to(cand_u ^ np.int32(KMIN), (tq, LANE))
        tot = count(lambda key, cand_s=cand_s: key >= cand_s)
        tu = jnp.where(tot >= topk, cand_u, tu)
        n_ge = jnp.where(tot >= topk, tot, n_ge)
    thr = tu ^ np.int32(KMIN)
    found = thr > np.int32(KMIN)
    has_ties = jnp.max(jnp.where(found & (n_ge > topk), 1.0, 0.0)) > 0.5

    qp_g = [_padded_heads(qa, g) for g in range(A_KV)]
    scr = (m_scr, acc_scr)
    in_values = bool(unroll)

    def attend(k, st, mask, far):
        kv = get_kva(k)
        bds = None if far else _tile_bds(qb, k, tk)
        madd = _mask_add(mask)
        for g in range(A_KV):
            st = _flash_update(st, scr, g, qp_g[g], kv, strips_ref, far_ref, bds, madd, tq, kv_t)
        return st

    @pl.when(jnp.logical_not(has_ties))
    def _():
        thr_ge = jnp.where(found, thr, np.int32(KMIN + 1))

        def tile(k, st, far):
            return attend(k, st, keys_scr[k] >= thr_ge, far)

        st = _flash_init(scr, tq, in_values)
        st = lax.fori_loop(0, n_far, functools.partial(tile, far=True), st, unroll=unroll)
        st = lax.fori_loop(n_far, n_tiles, functools.partial(tile, far=False), st, unroll=unroll)
        _flash_store(st, scr)

    @pl.when(has_ties)
    def _():
        need = topk - count(lambda key: key > jnp.broadcast_to(thr, (tq, LANE)))
        tri_scr[...] = jnp.where(
            lax.broadcasted_iota(I32, (LANE, LANE), 0) <= lax.broadcasted_iota(I32, (LANE, LANE), 1),
            1.0, 0.0).astype(BF16)

        def tile(k, c):
            eq_seen, st = c
            key = keys_scr[k]
            eq = key == thr
            takes = []
            for j in range(tk // LANE):
                eq_j = eq[:, j * LANE:(j + 1) * LANE]
                rank = eq_seen + _mm(jnp.where(eq_j, 1.0, 0.0), tri_scr[...])
                takes.append(eq_j & (rank <= need))
                eq_seen = rank[:, LANE - 1:LANE]
            take = jnp.concatenate(takes, axis=1)
            st = attend(k, st, (key > thr) | (take & found), False)
            return eq_seen, st

        _, st = lax.fori_loop(0, n_tiles, tile, (jnp.zeros((tq, 1), F32), _flash_init(scr, tq, in_values)))
        _flash_store(st, scr)

    outs = []
    for g in range(A_KV):
        o = _flash_result(scr, g)
        outs += [o[r * tq:(r + 1) * tq] for r in range(GROUP)]
    return jnp.concatenate(outs, axis=1)


def _pick_blocks(v, n_pick, axis):
    lane = lax.broadcasted_iota(I32, v.shape, axis).astype(F32)
    sel = jnp.zeros(v.shape, jnp.bool_)
    for _ in range(n_pick):
        mx = jnp.max(v, axis=axis, keepdims=True)
        cand = (v == mx) & jnp.logical_not(sel)
        first = jnp.min(jnp.where(cand, lane, float(v.shape[axis])), axis=axis, keepdims=True)
        pick = lane == first
        sel = sel | pick
        v = jnp.where(pick, -jnp.inf, v)
    return sel


def _nsa_core(qb, gates, kcp, get_kvs, win_tiles, strips_ref, cstrips_ref, far_ref, t0, tq, tk, nj,
              n_pick, extra_block, kv_t, unroll, m_scr, acc_scr):
    n_tiles = (t0 + tq - 1) // tk + 1
    qblk = t0 // LANE
    delta = _cmp_delta(tq)
    ncn = cstrips_ref.shape[-1]
    qp_g = [_padded_heads(qb, g) for g in range(B_KV)]
    row_pos1 = t0 + lax.broadcasted_iota(I32, (tq, 1), 0)

    lam = lax.broadcasted_iota(I32, (tq, 2 * nj), 1)
    m_of = 2 * (lam % nj) + lam // nj
    cdist = row_pos1 - (CMP_BLOCK * m_of + CMP_BLOCK - 1)
    cmask = cdist >= 0
    m_hi = (t0 - (CMP_BLOCK - 1) - delta) // CMP_BLOCK
    lam_s = lax.broadcasted_iota(I32, (ncn, 2 * nj), 1)
    shift = jnp.where(2 * (lam_s % nj) + lam_s // nj == m_hi - lax.broadcasted_iota(I32, (ncn, 2 * nj), 0),
                      1.0, 0.0).astype(BF16)
    is_far = m_of <= m_hi - ncn
    kc_both = kcp[:, :LANE]
    o_c, imp = [], []
    for g in range(B_KV):
        bias = []
        for r in range(GROUP):
            h = g * GROUP + r
            b = (_mm(cstrips_ref[h, 0], shift) + _mm(cstrips_ref[h, 1], shift)) + _mm(cstrips_ref[h, 2], shift)
            bias.append(jnp.where(is_far, far_ref[h], b))
        s = _mm_nt(qp_g[g], kc_both).reshape(GROUP, tq, 2 * nj) + jnp.stack(bias)
        s = jnp.where(cmask[None], s, NEG)
        e = jnp.where(cmask[None], jnp.exp(s - jnp.max(s, axis=-1, keepdims=True)), 0.0)
        l = jnp.sum(e, axis=-1, keepdims=True)
        p = jnp.where(l > 0.0, e / jnp.where(l > 0.0, l, 1.0), 0.0)
        o_c.append(_mm(p.reshape(GROUP * tq, 2 * nj), kcp[:, (B_KV + g) * HEAD_DIM:(B_KV + g + 1) * HEAD_DIM]))
        ps = p[0]
        for r in range(1, GROUP):
            ps = ps + p[r]
        imp.append(ps[:, :nj] + ps[:, nj:])

    jl = lax.broadcasted_iota(I32, (tq, nj), 1)
    cur = row_pos1 // SEL_BLOCK
    forced = (jl == 0) | (jl == cur) | (jl == cur - 1)
    admissible = jl <= cur
    vs = [jnp.where(admissible, jnp.where(forced, jnp.inf, imp[g]), -jnp.inf) for g in range(B_KV)]
    if tq % LANE == 0:
        picked_t = _pick_blocks(jnp.concatenate([jnp.transpose(v) for v in vs], axis=1), n_pick, 0)
        picked = [jnp.transpose(jnp.where(picked_t[:, g * tq:(g + 1) * tq], 1.0, 0.0)) > 0.5
                  for g in range(B_KV)]
    else:
        picked = [_pick_blocks(v, n_pick, 1) for v in vs]
    bmask = [jnp.where(picked[g] & admissible, 1.0, 0.0).astype(BF16) for g in range(B_KV)]
    row_pos = t0 + lax.broadcasted_iota(I32, (tq, tk), 0)
    col = lax.broadcasted_iota(I32, (tq, tk), 1)
    ej = lax.broadcasted_iota(I32, (nj, tk), 0)
    ec = lax.broadcasted_iota(I32, (nj, tk), 1)
    scr = (m_scr, acc_scr)
    in_values = bool(unroll)

    def sel_tile(k, st, far):
        kv = get_kvs(k)
        expand = jnp.where(ej == (k * tk + ec) // SEL_BLOCK, 1.0, 0.0).astype(BF16)
        for g in range(B_KV):
            mask = _mm(bmask[g], expand) > 0.5
            if far:
                bds = None
            else:
                bds = _tile_bds(qblk, k, tk)
                key_pos = k * tk + col
                if extra_block:
                    mask = mask | (key_pos >= nj * SEL_BLOCK)
                mask = mask & (key_pos <= row_pos)
            st = _flash_update(st, scr, g, qp_g[g], kv, strips_ref, far_ref, bds, _mask_add(mask), tq, kv_t)
        return st

    n_far = _far_tiles(qblk, tk)
    st = _flash_init(scr, tq, in_values)
    st = lax.fori_loop(0, n_far, functools.partial(sel_tile, far=True), st, unroll=unroll)
    st = lax.fori_loop(n_far, n_tiles, functools.partial(sel_tile, far=False), st, unroll=unroll)
    _flash_store(st, scr)
    o_s = [_flash_result(scr, g) for g in range(B_KV)]

    st = _flash_init(scr, tq, in_values)
    for get_tile, off, guard in win_tiles:
        def win_tile(st, get_tile=get_tile, off=off):
            kv = get_tile()
            wt = kv.shape[1] if kv_t else kv.shape[0]
            dist = (lax.broadcasted_iota(I32, (tq, wt), 0) - lax.broadcasted_iota(I32, (tq, wt), 1)
                    - LANE * off)
            wadd = _mask_add((dist >= 0) & (dist <= WINDOW))
            bds = [min(max(-off - j, 0), NEAR) for j in range(wt // LANE)]
            for g in range(B_KV):
                st = _flash_update(st, scr, g, qp_g[g], kv, strips_ref, far_ref, bds, wadd, tq, kv_t)
            return st

        if guard is None:
            st = win_tile(st)
        else:
            assert not in_values

            def guarded(win_tile=win_tile):
                win_tile(0)

            pl.when(guard)(guarded)
    _flash_store(st, scr)
    o_w = [_flash_result(scr, g) for g in range(B_KV)]

    gs = jax.nn.sigmoid(gates)
    outs = []
    for g in range(B_KV):
        gcol = lambda j: jnp.concatenate(
            [gs[:, (g * GROUP + r) * 3 + j:(g * GROUP + r) * 3 + j + 1] for r in range(GROUP)], axis=0)
        o = gcol(0) * o_c[g] + gcol(1) * o_s[g] + gcol(2) * o_w[g]
        outs += [o[r * tq:(r + 1) * tq] for r in range(GROUP)]
    return jnp.concatenate(outs, axis=1)


def _compress(load_rows, n_pairs, pe_ref, wbd_ref):
    acc = [jnp.zeros((n_pairs, KVW), F32), jnp.zeros((n_pairs, KVW), F32)]
    for c in range(2 * CMP_BLOCK):
        ci = c % CMP_BLOCK
        acc[c // CMP_BLOCK] = acc[c // CMP_BLOCK] + _mm(load_rows(c) + pe_ref[ci], wbd_ref[ci])
    return jnp.concatenate(acc, axis=0)


def _compress_operands(pe, w):
    pe_rows = jnp.transpose(jnp.broadcast_to(pe[:, None], (2, B_KV, CMP_BLOCK, HEAD_DIM)), (2, 0, 1, 3))
    pe_rows = pe_rows.reshape(CMP_BLOCK, 1, KVW)
    eye = jnp.eye(2 * B_KV, dtype=w.dtype).reshape(2, B_KV, 2, B_KV)
    wbd = jnp.einsum("icde,igjh->cigdjhe", w, eye).reshape(CMP_BLOCK, KVW, KVW)
    return pe_rows, wbd.astype(BF16)


def _strided_rows(lo_ref, hi_ref, c, n):
    rows = pl.ds(c, n, stride=2 * CMP_BLOCK)
    return jnp.concatenate([lo_ref[rows, :], hi_ref[rows, :]], axis=1)


def _flash_scratch(tq):
    return [pltpu.VMEM((A_KV, GROUP, tq, LANE), F32), pltpu.VMEM((A_KV, GROUP * tq, LANE), F32)]


TQ_P = 128
TK_P = 1024


def _prompt_tile(ref, k, tk):
    return ref[pl.ds(pl.multiple_of(k * tk, tk), tk), :]


def _pdsa_kernel(qi_ref, misc_ref, qa_ref, kidx_ref, kva_ref, strips_ref, far_ref, o_ref,
                 keys_scr, wib_scr, m_scr, acc_scr, tri_scr, *, topk, tk):
    i = pl.program_id(1)
    o = _dsa_core(qi_ref[...], misc_ref[:, M_WI:M_WI + IDX_HEADS], qa_ref[...],
                  lambda k: _prompt_tile(kidx_ref, k, tk), lambda k: _prompt_tile(kva_ref, k, tk),
                  strips_ref, far_ref, i * TQ_P, TQ_P, tk, topk, False, None,
                  keys_scr, wib_scr, m_scr, acc_scr, tri_scr)
    o_ref[...] = o.astype(o_ref.dtype)


def _pdsa_call(pr, strips, far, tk):
    nb, s, _ = pr["qa"].shape
    assert s < ZERO_KEY_BASE
    tq = TQ_P
    row = lambda w: pl.BlockSpec((None, tq, w), lambda b, i: (b, i, 0))
    full = lambda w: pl.BlockSpec((None, s, w), lambda b, i: (b, 0, 0))
    return pl.pallas_call(
        functools.partial(_pdsa_kernel, topk=min(IDX_TOPK, s // 4), tk=tk),
        grid=(nb, s // tq),
        in_specs=[row(256), row(128), row(512), full(128), full(KVX),
                  pl.BlockSpec(strips.shape, lambda b, i: (0, 0, 0, 0)),
                  pl.BlockSpec(memory_space=pltpu.SMEM)],
        out_specs=row(512),
        out_shape=jax.ShapeDtypeStruct((nb, s, A_HEADS * HEAD_DIM), BF16),
        scratch_shapes=[pltpu.VMEM((s // tk, tq, tk), I32), pltpu.VMEM((IDX_HEADS, tq, tk), F32)]
        + _flash_scratch(tq) + [pltpu.VMEM((LANE, LANE), BF16)],
        compiler_params=_cparams(("arbitrary", "arbitrary")),
        name="dsa_prompt",
    )(pr["qi"], pr["misc"], pr["qa"], pr["misc_h"], pr["kva_x"], strips, far)


def _pcmp_kernel(lo_ref, hi_ref, pe_ref, wbd_ref, o_ref, *, n_pairs):
    o_ref[...] = _compress(lambda c: _strided_rows(lo_ref, hi_ref, c, n_pairs), n_pairs, pe_ref, wbd_ref)


def _pcmp_call(kvc, pe_rows, wbd):
    nb, s, _ = kvc.shape
    n_pairs = s // (2 * CMP_BLOCK)
    return pl.pallas_call(
        functools.partial(_pcmp_kernel, n_pairs=n_pairs),
        grid=(nb,),
        in_specs=[pl.BlockSpec((None, s, LANE), lambda b: (b, 0, 0)),
                  pl.BlockSpec((None, s, LANE), lambda b: (b, 0, 1)),
                  pl.BlockSpec((CMP_BLOCK, 1, KVW), lambda b: (0, 0, 0)),
                  pl.BlockSpec((CMP_BLOCK, KVW, KVW), lambda b: (0, 0, 0))],
        out_specs=pl.BlockSpec((None, 2 * n_pairs, KVW), lambda b: (b, 0, 0)),
        out_shape=jax.ShapeDtypeStruct((nb, 2 * n_pairs, KVW), F32),
        compiler_params=_cparams(("arbitrary",)),
        name="compress_prompt",
    )(kvc, kvc, pe_rows, wbd)


def _pnsa_kernel(qb_ref, misc_ref, kcp_ref, kvs_ref, kvw_ref, strips_ref, cstrips_ref, far_ref, o_ref,
                 m_scr, acc_scr, *, nj, n_pick, tk):
    i = pl.program_id(1)
    nw = WINDOW // LANE
    win_tiles = [(lambda: kvw_ref[pl.ds(pl.multiple_of((i - nw) * LANE, LANE), WINDOW + TQ_P), :], -nw, i >= nw)]
    win_tiles += [(lambda wd=wd: _prompt_tile(kvw_ref, i - wd, LANE), -wd, (i < nw) & (i - wd >= 0))
                  for wd in range(nw - 1, -1, -1)]
    o = _nsa_core(qb_ref[...], misc_ref[:, M_GATE:M_GATE + 3 * B_HEADS], kcp_ref[...],
                  lambda k: _prompt_tile(kvs_ref, k, tk), win_tiles,
                  strips_ref, cstrips_ref, far_ref, i * TQ_P, TQ_P, tk, nj, n_pick, False, False, None,
                  m_scr, acc_scr)
    o_ref[...] = o.astype(o_ref.dtype)


def _pnsa_call(pr, kcp, strips, cstrips, far, tk):
    nb, s, _ = pr["qb"].shape
    tq = TQ_P
    nj = s // SEL_BLOCK
    row = lambda w: pl.BlockSpec((None, tq, w), lambda b, i: (b, i, 0))
    full = lambda n, w: pl.BlockSpec((None, n, w), lambda b, i: (b, 0, 0))
    const = lambda a: pl.BlockSpec(a.shape, lambda b, i: (0,) * a.ndim)
    return pl.pallas_call(
        functools.partial(_pnsa_kernel, nj=nj, n_pick=min(SEL_TOPN, nj), tk=tk),
        grid=(nb, s // tq),
        in_specs=[row(512), row(128), full(2 * nj, KVW), full(s, KVX), full(s, KVX),
                  const(strips), const(cstrips),
                  pl.BlockSpec(memory_space=pltpu.SMEM)],
        out_specs=row(512),
        out_shape=jax.ShapeDtypeStruct((nb, s, B_HEADS * HEAD_DIM), BF16),
        scratch_shapes=_flash_scratch(tq),
        compiler_params=_cparams(("arbitrary", "arbitrary")),
        name="nsa_prompt",
    )(pr["qb"], pr["misc"], kcp, pr["kvs_x"], pr["kvw_x"], strips, cstrips, far)


TQ_S = 8
TK_S = 2048
PPT = TK_S // PAGE


def _pages_per_step(n_pages, cap):
    return max(g for g in (64, 32, 16, 8, 4, 2, 1) if g <= cap and n_pages % g == 0)


def _new_token_block(col):
    return jnp.where(lax.broadcasted_iota(I32, (col.shape[0], PAGE), 1) == 0, col, 0.0)


def _slot_tile(buf, k):
    blk = buf[pl.ds(k * PPT, PPT)]
    return jnp.concatenate([blk[j] for j in range(PPT)], axis=1)


def _sdsa_kernel(*refs, n_pages, pps, topk):
    idx_refs = refs[1:1 + pps]
    akv_refs = refs[1 + pps:1 + 2 * pps]
    (qi_ref, wi_ref, qa_ref, new_idx_ref, new_kv_ref, strips_ref, far_ref, o_ref,
     idx_buf, akv_buf, keys_scr, wib_scr, m_scr, acc_scr, tri_scr) = refs[1 + 2 * pps:]
    p = pl.program_id(1)
    for j in range(pps):
        idx_buf[p * pps + j] = idx_refs[j][...].astype(BF16)
        akv_buf[p * pps + j] = _kv_ext_t(akv_refs[j][...])

    @pl.when(p == n_pages // pps - 1)
    def _():
        idx_buf[n_pages] = _new_token_block(new_idx_ref[...]).astype(BF16)
        akv_buf[n_pages] = _kv_ext_t(_new_token_block(new_kv_ref[...]))
        for j in range(1, PPT):
            idx_buf[n_pages + j] = jnp.zeros((IDX_DIM, PAGE), BF16)
            akv_buf[n_pages + j] = _kv_ext_t(jnp.zeros((KVW, PAGE), BF16))
        o = _dsa_core(qi_ref[...], wi_ref[...], qa_ref[...],
                      lambda k: _slot_tile(idx_buf, k), lambda k: _slot_tile(akv_buf, k),
                      strips_ref, far_ref, n_pages * PAGE, TQ_S, TK_S, topk, True, True,
                      keys_scr, wib_scr, m_scr, acc_scr, tri_scr)
        o_ref[...] = o.astype(o_ref.dtype)


def _sample_rows(x):
    return jnp.broadcast_to(x[:, None, :], (x.shape[0], TQ_S, x.shape[1]))


def _pages_t(cache):
    n, p = cache.shape[:2]
    nd = cache.ndim
    return jnp.transpose(cache, (0, 1) + tuple(range(3, nd)) + (2,)).reshape(n, p, -1, cache.shape[2])


def _sdsa_call(a, page_table, idx_t, akv_t, pr, strips, far):
    bd, n_pages = page_table.shape
    assert idx_t.shape[3] == PAGE and (n_pages * PAGE) % TK_S == 0 and n_pages * PAGE + TK_S < ZERO_KEY_BASE
    pps = _pages_per_step(n_pages, 64)
    slots = n_pages + PPT
    misc = pr["misc"][0]
    rows = lambda w: pl.BlockSpec((None, TQ_S, w), lambda b, p, pt: (b, 0, 0))
    col = lambda w: pl.BlockSpec((None, w, 1), lambda b, p, pt: (b, 0, 0))
    page = lambda w, j: pl.BlockSpec((None, None, w, PAGE),
                                     lambda b, p, pt: (a, pt[b * n_pages + p * pps + j], 0, 0))
    grid_spec = pltpu.PrefetchScalarGridSpec(
        num_scalar_prefetch=1,
        grid=(bd, n_pages // pps),
        in_specs=[page(IDX_DIM, j) for j in range(pps)] + [page(KVW, j) for j in range(pps)]
        + [rows(256), rows(IDX_HEADS), rows(512), col(IDX_DIM), col(KVW),
           pl.BlockSpec(strips.shape, lambda b, p, pt: (0, 0, 0, 0)),
           pl.BlockSpec(memory_space=pltpu.SMEM)],
        out_specs=rows(512),
        scratch_shapes=[pltpu.VMEM((slots, IDX_DIM, PAGE), BF16), pltpu.VMEM((slots, KVX, PAGE), BF16),
                        pltpu.VMEM((slots // PPT, TQ_S, TK_S), I32), pltpu.VMEM((IDX_HEADS, TQ_S, TK_S), F32)]
        + _flash_scratch(TQ_S) + [pltpu.VMEM((LANE, LANE), BF16)])
    out = pl.pallas_call(
        functools.partial(_sdsa_kernel, n_pages=n_pages, pps=pps, topk=min(IDX_TOPK, (n_pages * PAGE + 1) // 4)),
        grid_spec=grid_spec,
        out_shape=jax.ShapeDtypeStruct((bd, TQ_S, A_HEADS * HEAD_DIM), BF16),
        compiler_params=_cparams(("arbitrary", "arbitrary")),
        name="dsa_sample",
    )(page_table.reshape(-1), *([idx_t] * pps), *([akv_t] * pps),
      _sample_rows(pr["qi"][0]), _sample_rows(misc[:, M_WI:M_WI + IDX_HEADS]), _sample_rows(pr["qa"][0]),
      misc[:, M_KI:M_KI + IDX_DIM, None], pr["kva"][0][:, :, None], strips, far)
    return out[:, 0]


def _snsa_kernel(*refs, n_pages, pps):
    cmp_refs = refs[1:1 + pps]
    sel_refs = refs[1 + pps:1 + 2 * pps]
    (qb_ref, gates_ref, new_sel_ref, win_ref, new_win_ref, pe_ref, wbd_ref, strips_ref, cstrips_ref, far_ref,
     o_ref, cmp_lo, cmp_hi, sel_buf, m_scr, acc_scr) = refs[1 + 2 * pps:]
    p = pl.program_id(1)
    past = n_pages * PAGE
    for j in range(pps):
        rows = pl.ds(pl.multiple_of((p * pps + j) * PAGE, PAGE), PAGE)
        cmp_lo[rows, :] = jnp.transpose(cmp_refs[j][:LANE])
        cmp_hi[rows, :] = jnp.transpose(cmp_refs[j][LANE:])
        sel_buf[p * pps + j] = _kv_ext_t(sel_refs[j][...])

    @pl.when(p == n_pages // pps - 1)
    def _():
        sel_buf[n_pages] = _kv_ext_t(_new_token_block(new_sel_ref[...]))
        for j in range(1, PPT):
            sel_buf[n_pages + j] = _kv_ext_t(jnp.zeros((KVW, PAGE), BF16))
        nj = past // SEL_BLOCK
        kcp = _compress(lambda c: _strided_rows(cmp_lo, cmp_hi, c, nj), nj, pe_ref, wbd_ref)
        win_tiles = [(lambda: _kv_ext_t(win_ref[...]), -(WINDOW // LANE), None),
                     (lambda: _kv_ext_t(_new_token_block(new_win_ref[...])), 0, None)]
        o = _nsa_core(qb_ref[...], gates_ref[...], kcp, lambda k: _slot_tile(sel_buf, k), win_tiles,
                      strips_ref, cstrips_ref, far_ref, past, TQ_S, TK_S, nj,
                      min(SEL_TOPN, nj + 1) - 1, True, True, True, m_scr, acc_scr)
        o_ref[...] = o.astype(o_ref.dtype)


def _snsa_call(a, page_table, cmp_t, sel_t, win_t, pr, pe_rows, wbd, strips, cstrips, far):
    bd, n_pages = page_table.shape
    past = n_pages * PAGE
    assert past % TK_S == 0 and win_t.shape[3] == WINDOW
    pps = _pages_per_step(n_pages, 16)
    misc = pr["misc"][0]
    rows = lambda w: pl.BlockSpec((None, TQ_S, w), lambda b, p, pt: (b, 0, 0))
    col = lambda w: pl.BlockSpec((None, w, 1), lambda b, p, pt: (b, 0, 0))
    page = lambda j: pl.BlockSpec((None, None, KVW, PAGE),
                                  lambda b, p, pt: (a, pt[b * n_pages + p * pps + j], 0, 0))
    const = lambda x: pl.BlockSpec(x.shape, lambda b, p, pt: (0,) * x.ndim)
    grid_spec = pltpu.PrefetchScalarGridSpec(
        num_scalar_prefetch=1,
        grid=(bd, n_pages // pps),
        in_specs=[page(j) for j in range(pps)] + [page(j) for j in range(pps)]
        + [rows(512), rows(3 * B_HEADS), col(KVW),
           pl.BlockSpec((None, None, KVW, WINDOW), lambda b, p, pt: (a, b, 0, 0)), col(KVW),
           const(pe_rows), const(wbd), const(strips), const(cstrips),
           pl.BlockSpec(memory_space=pltpu.SMEM)],
        out_specs=rows(512),
        scratch_shapes=[pltpu.VMEM((past, LANE), F32), pltpu.VMEM((past, LANE), F32),
                        pltpu.VMEM((n_pages + PPT, KVX, PAGE), BF16)] + _flash_scratch(TQ_S))
    out = pl.pallas_call(
        functools.partial(_snsa_kernel, n_pages=n_pages, pps=pps),
        grid_spec=grid_spec,
        out_shape=jax.ShapeDtypeStruct((bd, TQ_S, B_HEADS * HEAD_DIM), BF16),
        compiler_params=_cparams(("arbitrary", "arbitrary")),
        name="nsa_sample",
    )(page_table.reshape(-1), *([cmp_t] * pps), *([sel_t] * pps),
      _sample_rows(pr["qb"][0]), _sample_rows(misc[:, M_GATE:M_GATE + 3 * B_HEADS]),
      pr["kvs"][0][:, :, None], win_t, pr["kvw"][0][:, :, None],
      pe_rows, wbd, strips, cstrips, far)
    return out[:, 0]


def _merge_kernel(x_ref, mod_ref, oa_ref, ob_ref, w_ref, o_ref):
    n = oa_ref.shape[-1]
    y = _mm(oa_ref[...], w_ref[:n]) + _mm(ob_ref[...], w_ref[n:])
    o_ref[...] = x_ref[...] + (1.0 + mod_ref[2]) * y


def _merge_call(x, mod, oa, ob, w, tm):
    nb, t, d = x.shape
    tmod = mod.shape[2]
    mod_spec = (pl.BlockSpec((None, 3, 1, d), lambda b, i: (b, 0, 0, 0)) if tmod == 1 else
                pl.BlockSpec((None, 3, tm, d), lambda b, i: (b, 0, i, 0)))
    row = lambda w_: pl.BlockSpec((None, tm, w_), lambda b, i: (b, i, 0))
    return pl.pallas_call(
        _merge_kernel,
        grid=(nb, t // tm),
        in_specs=[row(d), mod_spec, row(oa.shape[-1]), row(ob.shape[-1]),
                  pl.BlockSpec(w.shape, lambda b, i: (0, 0))],
        out_specs=row(d),
        out_shape=jax.ShapeDtypeStruct(x.shape, F32),
        compiler_params=_cparams(("arbitrary", "arbitrary")),
        name="merge",
    )(x, mod, oa, ob, w)


def _gelu_ln(h, w_in_ref, ln_ref, cw):
    uv = jax.nn.gelu(_mm(h, w_in_ref[...]))
    u, v = uv[:, :cw], uv[:, cw:]
    mu = jnp.mean(v, axis=-1, keepdims=True)
    var = jnp.mean(jnp.square(v - mu), axis=-1, keepdims=True)
    return u, (v - mu) * lax.rsqrt(var + EPS) * ln_ref[...]


def _pgmlp_kernel(x_ref, mod_ref, g_ref, w_in_ref, ln_ref, ws_ref, bs_ref, w_out_ref, o_ref, *, cw):
    x = x_ref[...]
    h = _modulated_norm(x, g_ref[...], mod_ref).astype(BF16)
    u, v = _gelu_ln(h, w_in_ref, ln_ref, cw)
    gw = cw // C_GROUPS
    rows = []
    for n in range(x.shape[0] // C_CHUNK):
        vb = v[n * C_CHUNK:(n + 1) * C_CHUNK].astype(BF16)
        sv = [_mm(ws_ref[g], vb[:, g * gw:(g + 1) * gw]) + bs_ref[:, g:g + 1] for g in range(C_GROUPS)]
        rows.append(jnp.concatenate(sv, axis=1))
    sv = rows[0] if len(rows) == 1 else jnp.concatenate(rows, axis=0)
    o_ref[...] = x + (1.0 + mod_ref[2]) * _mm(u * sv, w_out_ref[...])


def _pgmlp_call(x, mod, g, w_in, ln_g, ws, bs_t, w_out, tm):
    nb, t, d = x.shape
    cw = w_out.shape[0]
    const = lambda a: pl.BlockSpec(a.shape, lambda b, i: (0,) * a.ndim)
    return pl.pallas_call(
        functools.partial(_pgmlp_kernel, cw=cw),
        grid=(nb, t // tm),
        in_specs=[pl.BlockSpec((None, tm, d), lambda b, i: (b, i, 0)),
                  pl.BlockSpec((None, 3, 1, d), lambda b, i: (b, 0, 0, 0)),
                  pl.BlockSpec((1, d), lambda b, i: (0, 0)),
                  const(w_in), pl.BlockSpec((1, cw), lambda b, i: (0, 0)), const(ws), const(bs_t), const(w_out)],
        out_specs=pl.BlockSpec((None, tm, d), lambda b, i: (b, i, 0)),
        out_shape=jax.ShapeDtypeStruct(x.shape, F32),
        compiler_params=_cparams(("arbitrary", "arbitrary")),
        name="gmlp_prompt",
    )(x, mod, g.reshape(1, d), w_in, ln_g.reshape(1, cw), ws, bs_t, w_out)


def _sgmlp_kernel(x_ref, mod_ref, g_ref, w_in_ref, ln_ref, ws0_ref, bs0_ref, w_out_ref, o_ref, v_ref, *, cw):
    x = x_ref[...]
    h = _modulated_norm(x, g_ref[...], mod_ref).astype(BF16)
    u, v = _gelu_ln(h, w_in_ref, ln_ref, cw)
    v_ref[...] = v
    sv = v.astype(BF16).astype(F32) * ws0_ref[...].astype(F32) + bs0_ref[...]
    o_ref[...] = x + (1.0 + mod_ref[2]) * _mm(u * sv, w_out_ref[...])


def _sgmlp_call(x, mod, g, w_in, ln_g, ws, b_sp, w_out):
    nb, t, d = x.shape
    cw = w_out.shape[0]
    gw = cw // C_GROUPS
    ws0 = jnp.repeat(ws[:, 0, 0], gw).reshape(1, cw)
    bs0 = jnp.repeat(b_sp[:, 0], gw).reshape(1, cw)
    full = lambda a: pl.BlockSpec(a.shape, lambda: (0,) * a.ndim)
    x2, mod2 = x[0], mod[0]
    o, v = pl.pallas_call(
        functools.partial(_sgmlp_kernel, cw=cw),
        in_specs=[full(x2), full(mod2), pl.BlockSpec((1, d), lambda: (0, 0)), full(w_in),
                  pl.BlockSpec((1, cw), lambda: (0, 0)), full(ws0), full(bs0), full(w_out)],
        out_specs=[full(x2), pl.BlockSpec((t, cw), lambda: (0, 0))],
        out_shape=[jax.ShapeDtypeStruct(x2.shape, F32), jax.ShapeDtypeStruct((t, cw), F32)],
        compiler_params=pltpu.CompilerParams(vmem_limit_bytes=VMEM_LIMIT),
        name="gmlp_sample",
    )(x2, mod2, g.reshape(1, d), w_in, ln_g.reshape(1, cw), ws0, bs0, w_out)
    return o[None], v


def kernel(x_prompt, x_sample, cache_a_kv, cache_a_idx, cache_b_cmp_kv, cache_b_sel_kv, state_b_win_kv,
           page_table, c_prompt, c_sample, rel_bias, w_ada, b_ada, norm_g, w_ffn_in, w_ffn_out,
           w_in_att, w_out_att, cmp_pe, w_cmp, w_in_c, ln_c_g, w_sp, b_sp, w_out_c, final_g):
    nbp, s, d = x_prompt.shape
    nbs, tdec, _ = x_sample.shape
    depth = w_ada.shape[0]
    assert tdec == 1 and s % TQ_P == 0 and cache_a_kv.shape[2] == PAGE

    ada = _ada_call(jnp.concatenate([c_prompt, c_sample], axis=0), w_ada, b_ada)
    ada = ada.reshape(depth, nbp + nbs, 3, 3, d)
    mod_p = lambda l, j: ada[l, :nbp, j][:, :, None, :]
    mod_s = lambda l, j: jnp.transpose(ada[l, nbp:, j], (1, 0, 2))[None]

    w_ffn_in_h = w_ffn_in.astype(BF16)
    w_ffn_out_h = w_ffn_out.astype(BF16)
    tab_a, tab_b = rel_bias[:, :A_HEADS], rel_bias[:, A_HEADS:]
    strips_pa, strips_pb = _bias_strips(tab_a, TQ_P), _bias_strips(tab_b, TQ_P)
    strips_sa, strips_sb = _bias_strips(tab_a, TQ_S), _bias_strips(tab_b, TQ_S)
    cstrips_p, cstrips_s = _cmp_strips(tab_b, TQ_P), _cmp_strips(tab_b, TQ_S)
    far_a, far_b = tab_a[N_BUCKETS - 1], tab_b[N_BUCKETS - 1]
    idx_t, akv_t = _pages_t(cache_a_idx), _pages_t(cache_a_kv)
    cmp_t, sel_t, win_t = _pages_t(cache_b_cmp_kv), _pages_t(cache_b_sel_kv), _pages_t(state_b_win_kv)

    xp = x_prompt
    xs = jnp.transpose(x_sample, (1, 0, 2))
    tm_p = 512 if s % 512 == 0 else TQ_P
    tk_p = TK_P if s % TK_P == 0 else TQ_P
    st_p, st_s, st_c = [], [], []
    for l in range(depth):
        last = l == depth - 1
        xp = _ffn_call(xp, mod_p(l, 0), norm_g[l, 0], w_ffn_in_h, w_ffn_out_h, l, 0, None, tm_p)
        xs = _ffn_call(xs, mod_s(l, 0), norm_g[l, 0], w_ffn_in_h, w_ffn_out_h, l, 0, None, nbs)
        if l % 2 == 0:
            a = l // 2
            wp = _rearranged_proj_weight(w_in_att[a])
            w_out = w_out_att[a].astype(BF16)
            pe_rows, wbd = _compress_operands(cmp_pe[a], w_cmp[a])

            pr = _proj_call(xp, mod_p(l, 1), norm_g[l, 1], wp, tm_p)
            oa = _pdsa_call(pr, strips_pa, far_a, tk_p)
            kcp = _pcmp_call(pr["kvc"], pe_rows, wbd)
            ob = _pnsa_call(pr, kcp, strips_pb, cstrips_p, far_b, tk_p)
            xp = _merge_call(xp, mod_p(l, 1), oa, ob, w_out, tm_p)
            kv5 = lambda t: t.reshape(t.shape[0], t.shape[1], 2, A_KV, HEAD_DIM)
            st_p.append((kv5(pr["kva"]), pr["misc"][:, :, M_KI:M_KI + IDX_DIM], kv5(pr["kvc"]),
                         kv5(pr["kvs"]), kv5(pr["kvw"][:, s - min(WINDOW, s):])))

            ps = _proj_call(xs, mod_s(l, 1), norm_g[l, 1], wp, nbs)
            oa = _sdsa_call(a, page_table, idx_t, akv_t, ps, strips_sa, far_a)
            ob = _snsa_call(a, page_table, cmp_t, sel_t, win_t, ps, pe_rows, wbd, strips_sb, cstrips_s, far_b)
            xs = _merge_call(xs, mod_s(l, 1), oa[None], ob[None], w_out, nbs)
            tok = lambda t: t[0].reshape(nbs, 1, 2, A_KV, HEAD_DIM)
            st_s.append((tok(ps["kva"]), ps["misc"][0][:, None, M_KI:M_KI + IDX_DIM], tok(ps["kvc"]),
                         tok(ps["kvs"]),
                         jnp.concatenate([state_b_win_kv[a][:, tdec:], tok(ps["kvw"])], axis=1)))
        else:
            ci = l // 2
            w_in = w_in_c[ci].astype(BF16)
            w_out = w_out_c[ci].astype(BF16)
            ws = (w_sp[ci] * jnp.tril(jnp.ones((C_CHUNK, C_CHUNK), w_sp.dtype))).astype(BF16)
            xp = _pgmlp_call(xp, mod_p(l, 1), norm_g[l, 1], w_in, ln_c_g[ci], ws,
                             jnp.transpose(b_sp[ci]), w_out, 256 if s % 256 == 0 else C_CHUNK)
            xs, v = _sgmlp_call(xs, mod_s(l, 1), norm_g[l, 1], w_in, ln_c_g[ci], ws, b_sp[ci], w_out)
            st_c.append(v[:, None, :])
        fg = final_g if last else None
        xp = _ffn_call(xp, mod_p(l, 2), norm_g[l, 2], w_ffn_in_h, w_ffn_out_h, l, 1, fg, tm_p)
        xs = _ffn_call(xs, mod_s(l, 2), norm_g[l, 2], w_ffn_in_h, w_ffn_out_h, l, 1, fg, nbs)

    stk = lambda sts, i: jnp.stack([st[i] for st in sts])
    return (xp, jnp.transpose(xs, (1, 0, 2)),
            stk(st_p, 0), stk(st_p, 1), stk(st_p, 2), stk(st_p, 3), stk(st_p, 4),
            stk(st_s, 0), stk(st_s, 1), stk(st_s, 2), stk(st_s, 3), stk(st_s, 4),
            jnp.stack(st_c))
```

```python
import functools
import math

import numpy as np
import jax
import jax.numpy as jnp
from jax import lax
from jax.experimental import pallas as pl
from jax.experimental.pallas import tpu as pltpu

F32 = jnp.float32
BF16 = jnp.bfloat16
I32 = jnp.int32

HEAD_DIM = 64
A_HEADS = 8
A_KV = 2
IDX_HEADS = 4
IDX_DIM = 64
IDX_TOPK = 256
B_HEADS = 8
B_KV = 2
CMP_BLOCK = 32
SEL_BLOCK = 64
SEL_TOPN = 16
WINDOW = 512
N_BUCKETS = 32
MAX_DISTANCE = 1024
C_CHUNK = 128
C_GROUPS = 8
EPS = 1e-6
PAGE = 128
GROUP = A_HEADS // A_KV
KVW = 2 * A_KV * HEAD_DIM

LANE = 128
VMEM_LIMIT = 56 * 1024 * 1024
KVX = 3 * LANE

NEG = -1e30
KMIN = -2 ** 31

P_QA, P_QB, P_KVA, P_QI, P_KVC, P_KVS, P_KVW, P_MISC = 0, 512, 1024, 1280, 1536, 1792, 2048, 2304
P_WIDTH = 2432
M_KI, M_WI, M_GATE = 0, IDX_DIM, IDX_DIM + IDX_HEADS

assert A_KV == 2 and B_KV == 2 and A_KV * HEAD_DIM == LANE and IDX_DIM == HEAD_DIM


def _cparams(sem):
    return pltpu.CompilerParams(dimension_semantics=sem, vmem_limit_bytes=VMEM_LIMIT)


def _mm(a, b):
    return jnp.dot(a.astype(BF16), b.astype(BF16), preferred_element_type=F32)


def _mm_nt(a, b):
    return lax.dot_general(a.astype(BF16), b.astype(BF16), (((1,), (1,)), ((), ())),
                           preferred_element_type=F32)


def _modulated_norm(x, g, mod_ref):
    y = x * lax.rsqrt(jnp.mean(x * x, axis=-1, keepdims=True) + EPS) * g
    return y * (1.0 + mod_ref[1]) + mod_ref[0]


def _ada_kernel(c_ref, w_ref, b_ref, o_ref):
    c = c_ref[...]
    o_ref[...] = _mm(c * jax.nn.sigmoid(c), w_ref[...]) + b_ref[...]


def _ada_call(c_all, w_ada, b_ada):
    depth, d, n = w_ada.shape
    r = c_all.shape[0]
    tn = n // 8
    return pl.pallas_call(
        _ada_kernel,
        grid=(depth, n // tn),
        in_specs=[pl.BlockSpec((r, d), lambda l, j: (0, 0)),
                  pl.BlockSpec((None, d, tn), lambda l, j: (l, 0, j)),
                  pl.BlockSpec((None, 1, tn), lambda l, j: (l, 0, j))],
        out_specs=pl.BlockSpec((None, r, tn), lambda l, j: (l, 0, j)),
        out_shape=jax.ShapeDtypeStruct((depth, r, n), F32),
        compiler_params=_cparams(("arbitrary", "arbitrary")),
        name="ada",
    )(c_all, w_ada, b_ada.reshape(depth, 1, n))


def _ffn_kernel(x_ref, mod_ref, g_ref, wa_ref, wb_ref, wo_ref, fg_ref, o_ref, h_scr, acc_scr, *,
                n_f, final_norm):
    f = pl.program_id(2)

    @pl.when(f == 0)
    def _():
        h_scr[...] = _modulated_norm(x_ref[...], g_ref[...], mod_ref).astype(BF16)
        acc_scr[...] = jnp.zeros_like(acc_scr)

    h = h_scr[...]
    a = _mm(h, wa_ref[...])
    b = _mm(h, wb_ref[...])
    acc_scr[...] += _mm(a * jax.nn.sigmoid(a) * b, wo_ref[...])

    @pl.when(f == n_f - 1)
    def _():
        y = x_ref[...] + 0.5 * (1.0 + mod_ref[2]) * acc_scr[...]
        if final_norm:
            y = y * lax.rsqrt(jnp.mean(y * y, axis=-1, keepdims=True) + EPS) * fg_ref[...]
        o_ref[...] = y


def _ffn_call(x, mod, g, w_in, w_out, l, j, final_g, tm):
    nb, t, d = x.shape
    dff = w_out.shape[2]
    tf = dff // 2 if dff % (2 * LANE) == 0 else LANE
    n_f = dff // tf
    tmod = mod.shape[2]
    mod_spec = (pl.BlockSpec((None, 3, 1, d), lambda b, i, f: (b, 0, 0, 0)) if tmod == 1 else
                pl.BlockSpec((None, 3, tm, d), lambda b, i, f: (b, 0, i, 0)))
    final_norm = final_g is not None
    fg = (final_g if final_norm else g).reshape(1, d)
    return pl.pallas_call(
        functools.partial(_ffn_kernel, n_f=n_f, final_norm=final_norm),
        grid=(nb, t // tm, n_f),
        in_specs=[pl.BlockSpec((None, tm, d), lambda b, i, f: (b, i, 0)),
                  mod_spec,
                  pl.BlockSpec((1, d), lambda b, i, f: (0, 0)),
                  pl.BlockSpec((None, None, d, tf), lambda b, i, f: (l, j, 0, f)),
                  pl.BlockSpec((None, None, d, tf), lambda b, i, f: (l, j, 0, f + n_f)),
                  pl.BlockSpec((None, None, tf, d), lambda b, i, f: (l, j, f, 0)),
                  pl.BlockSpec((1, d), lambda b, i, f: (0, 0))],
        out_specs=pl.BlockSpec((None, tm, d), lambda b, i, f: (b, i, 0)),
        out_shape=jax.ShapeDtypeStruct(x.shape, F32),
        scratch_shapes=[pltpu.VMEM((tm, d), BF16), pltpu.VMEM((tm, d), F32)],
        compiler_params=_cparams(("arbitrary", "arbitrary", "arbitrary")),
        name="ffn",
    )(x, mod, g.reshape(1, d), w_in, w_in, w_out, fg)


def _kv_ext(y):
    e = jnp.where(lax.broadcasted_iota(I32, (y.shape[0], HEAD_DIM), 1) == 0, 1.0, 0.0).astype(BF16)
    y = y.astype(BF16)
    return jnp.concatenate([y[:, :LANE], y[:, LANE:LANE + HEAD_DIM], e, y[:, LANE + HEAD_DIM:], e], axis=1)


def _kv_ext_t(y):
    e = jnp.where(lax.broadcasted_iota(I32, (HEAD_DIM, y.shape[1]), 0) == 0, 1.0, 0.0).astype(BF16)
    y = y.astype(BF16)
    return jnp.concatenate([y[:LANE], y[LANE:LANE + HEAD_DIM], e, y[LANE + HEAD_DIM:], e], axis=0)


_PROJ_OUT = (("qa", P_QA, 512, "h"), ("qb", P_QB, 512, "h"), ("qi", P_QI, 256, "h"),
             ("kva", P_KVA, 256, "f"), ("kvc", P_KVC, 256, "f"), ("kvs", P_KVS, 256, "f"),
             ("kvw", P_KVW, 256, "f"), ("misc", P_MISC, 128, "f"),
             ("kva_x", P_KVA, KVX, "x"), ("kvs_x", P_KVS, KVX, "x"), ("kvw_x", P_KVW, KVX, "x"),
             ("misc_h", P_MISC, 128, "h"))


def _proj_kernel(x_ref, mod_ref, g_ref, w_ref, *o_refs):
    h = _modulated_norm(x_ref[...], g_ref[...], mod_ref).astype(BF16)
    done = {}
    for (name, off, width, kind), o_ref in zip(_PROJ_OUT, o_refs):
        if off not in done:
            y = _mm(h, w_ref[:, off:off + (KVW if kind == "x" else width)])
            if name in ("qa", "qb"):
                y = y * (HEAD_DIM ** -0.5)
            done[off] = y
        o_ref[...] = _kv_ext(done[off]) if kind == "x" else done[off].astype(o_ref.dtype)


def _proj_call(x, mod, g, wp, tm):
    nb, t, d = x.shape
    tmod = mod.shape[2]
    mod_spec = (pl.BlockSpec((None, 3, 1, d), lambda b, i: (b, 0, 0, 0)) if tmod == 1 else
                pl.BlockSpec((None, 3, tm, d), lambda b, i: (b, 0, i, 0)))
    outs = pl.pallas_call(
        _proj_kernel,
        grid=(nb, t // tm),
        in_specs=[pl.BlockSpec((None, tm, d), lambda b, i: (b, i, 0)),
                  mod_spec,
                  pl.BlockSpec((1, d), lambda b, i: (0, 0)),
                  pl.BlockSpec((d, P_WIDTH), lambda b, i: (0, 0))],
        out_specs=[pl.BlockSpec((None, tm, w), lambda b, i: (b, i, 0)) for _, _, w, _ in _PROJ_OUT],
        out_shape=[jax.ShapeDtypeStruct((nb, t, w), F32 if kind == "f" else BF16)
                   for _, _, w, kind in _PROJ_OUT],
        compiler_params=_cparams(("arbitrary", "arbitrary")),
        name="proj",
    )(x, mod, g.reshape(1, d), wp)
    return {name: o for (name, _, _, _), o in zip(_PROJ_OUT, outs)}


def _rearranged_proj_weight(w):
    d = w.shape[0]
    o = np.cumsum((0, 512, 256, 256, 64, 4, 512, 256, 256, 256, 24))
    seg = lambda i: w[:, o[i]:o[i + 1]]
    qa, kva, qi, ki, wi, qb, kvc, kvs, kvw, gates = [seg(i) for i in range(10)]
    pad = jnp.zeros((d, LANE - IDX_DIM - IDX_HEADS - 3 * B_HEADS), w.dtype)
    return jnp.concatenate([qa, qb, kva, qi, kvc, kvs, kvw, ki, wi, gates, pad], axis=1).astype(BF16)


def _t5_bucket(dist):
    n = jnp.maximum(dist, 0)
    max_exact = N_BUCKETS // 2
    nf = jnp.maximum(n, 1).astype(F32)
    large = max_exact + (jnp.log(nf / max_exact) / math.log(MAX_DISTANCE / max_exact)
                         * (N_BUCKETS - max_exact)).astype(I32)
    large = jnp.minimum(large, N_BUCKETS - 1)
    return jnp.where(n < max_exact, n, large)


def _bias_lookup(tab, dist):
    b = _t5_bucket(jnp.asarray(dist.reshape(-1), I32))
    onehot = (b[:, None] == jnp.arange(N_BUCKETS, dtype=I32)[None, :]).astype(F32)
    out = jnp.dot(onehot, tab, precision=lax.Precision.HIGHEST)
    return jnp.transpose(out).reshape((tab.shape[1],) + dist.shape)


NEAR = -(-(MAX_DISTANCE + LANE - 1) // LANE)


def _bias_strips(tab, tq):
    r = np.arange(tq)[None, :, None]
    c = np.arange(LANE)[None, None, :]
    bd = np.arange(NEAR + 1)[:, None, None]
    dist = np.where(bd == NEAR, MAX_DISTANCE, r - c + LANE * bd)
    return _bias_lookup(tab, dist)


def _cmp_near_count(delta):
    return -(-(-(-(MAX_DISTANCE - delta) // CMP_BLOCK)) // 8) * 8


def _cmp_delta(tq):
    return -(CMP_BLOCK - 1) - CMP_BLOCK * ((tq - CMP_BLOCK) // CMP_BLOCK)


def _cmp_strips(tab, tq):
    delta = _cmp_delta(tq)
    ncn = _cmp_near_count(delta)
    dist = np.arange(tq)[:, None] + delta + CMP_BLOCK * np.arange(ncn)[None, :]
    v = _bias_lookup(tab, dist)
    hi = v.astype(BF16)
    r1 = v - hi.astype(F32)
    mid = r1.astype(BF16)
    lo = (r1 - mid.astype(F32)).astype(BF16)
    return jnp.stack([hi, mid, lo], axis=1)


def _padded_heads(x, g):
    z = jnp.zeros((x.shape[0], HEAD_DIM), x.dtype)
    rows = []
    for r in range(GROUP):
        h = g * GROUP + r
        piece = x[:, h * HEAD_DIM:(h + 1) * HEAD_DIM]
        rows.append(jnp.concatenate([piece, z] if g == 0 else [z, piece], axis=1))
    return jnp.concatenate(rows, axis=0)


def _k_part(kv, kv_t):
    return kv[:LANE] if kv_t else kv[:, :LANE]


def _v_part(kv, g, kv_t):
    return kv[(1 + g) * LANE:(2 + g) * LANE] if kv_t else kv[:, (1 + g) * LANE:(2 + g) * LANE]


def _flash_init(scr, tq, in_values):
    if in_values:
        return tuple((jnp.full((GROUP, tq, LANE), NEG, F32), jnp.zeros((GROUP * tq, LANE), F32))
                     for _ in range(A_KV))
    scr[0][...] = jnp.full_like(scr[0], NEG)
    scr[1][...] = jnp.zeros_like(scr[1])
    return 0


def _flash_store(st, scr):
    if isinstance(st, tuple):
        for g, (m, acc) in enumerate(st):
            scr[0][g] = m
            scr[1][g] = acc


def _mask_add(mask):
    return jnp.where(mask, 0.0, NEG)


def _flash_update(st, scr, g, qp_g, kv, strips_ref, far_ref, bds, madd, tq, kv_t, s_all=None):
    m_all, acc = st[g] if isinstance(st, tuple) else (scr[0][g], scr[1][g])
    if s_all is None:
        kb = _k_part(kv, kv_t)
        s_all = _mm(qp_g, kb) if kv_t else _mm_nt(qp_g, kb)
    nblk = s_all.shape[1] // LANE
    ps, alphas, m_news = [], [], []
    for r in range(GROUP):
        h = g * GROUP + r
        blocks = []
        for j in range(nblk):
            sj = s_all[r * tq:(r + 1) * tq, j * LANE:(j + 1) * LANE]
            if madd is not None:
                sj = sj + madd[:, j * LANE:(j + 1) * LANE]
            blocks.append(sj if bds is None else sj + strips_ref[h, bds[j]])
        bmax = blocks[0]
        for sj in blocks[1:]:
            bmax = jnp.maximum(bmax, sj)
        row_max = jnp.max(bmax, axis=-1, keepdims=True)
        m_old = m_all[r]
        if bds is None:
            m_new = jnp.maximum(m_old, row_max + far_ref[h])
            shift = m_new - far_ref[h]
        else:
            m_new = jnp.maximum(m_old, row_max)
            shift = m_new
        m_news.append(m_new)
        alphas.append(jnp.exp(m_old - m_new))
        ps.append(jnp.concatenate([jnp.exp(sj - shift).astype(BF16) for sj in blocks], axis=1))
    p = jnp.concatenate(ps, axis=0)
    vx = _v_part(kv, g, kv_t)
    pv = _mm_nt(p, vx) if kv_t else _mm(p, vx)
    m_all = jnp.stack(m_news)
    acc = jnp.concatenate(alphas, axis=0) * acc + pv
    if isinstance(st, tuple):
        return st[:g] + ((m_all, acc),) + st[g + 1:]
    scr[0][g] = m_all
    scr[1][g] = acc
    return st


def _flash_result(scr, g):
    acc = scr[1][g]
    l = acc[:, HEAD_DIM:HEAD_DIM + 1]
    return jnp.where(l > 0.0, acc[:, :HEAD_DIM] / jnp.where(l > 0.0, l, 1.0), 0.0)


def _tile_bds(qb, k, tk):
    nblk = tk // LANE
    return [jnp.clip(qb - (k * nblk + j), 0, NEAR) for j in range(nblk)]


ZERO_KEY_BASE = 2 ** 16 - 1
LOW_BITS = 7


def _ordered_key(x, pos):
    b = lax.bitcast_convert_type(x, I32)
    return jnp.where(x == 0.0, (ZERO_KEY_BASE - pos) << LOW_BITS, b ^ ((b >> 31) & 0x7FFFFFFF))


def _far_tiles(qb, tk):
    n = (qb - NEAR + 1) // (tk // LANE)
    return max(n, 0) if isinstance(n, int) else jnp.maximum(n, 0)


def _dsa_core(qi, wi, qa, get_kidx, get_kva, strips_ref, far_ref, t0, tq, tk, topk, kv_t, unroll,
              keys_scr, wib_scr, m_scr, acc_scr, tri_scr):
    n_tiles = (t0 + tq - 1) // tk + 1
    qb = t0 // LANE
    n_far = _far_tiles(qb, tk)
    row_pos = t0 + lax.broadcasted_iota(I32, (tq, tk), 0)
    col = lax.broadcasted_iota(I32, (tq, tk), 1)

    if kv_t:
        qi_st = jnp.concatenate([qi[:, h * IDX_DIM:(h + 1) * IDX_DIM] for h in range(IDX_HEADS)], axis=0)
    else:
        z = jnp.zeros((tq, LANE - IDX_DIM), qi.dtype)
        qi_st = jnp.concatenate(
            [jnp.concatenate([qi[:, h * IDX_DIM:(h + 1) * IDX_DIM], z], axis=1) for h in range(IDX_HEADS)], axis=0)
    for h in range(IDX_HEADS):
        wib_scr[h] = jnp.broadcast_to(wi[:, h:h + 1], (tq, tk))

    def score_tile(k, c):
        kt = get_kidx(k)
        sc = _mm(qi_st, kt) if kv_t else _mm_nt(qi_st, kt)
        score = jnp.maximum(sc[:tq], 0.0) * wib_scr[0]
        for h in range(1, IDX_HEADS):
            score = score + jnp.maximum(sc[h * tq:(h + 1) * tq], 0.0) * wib_scr[h]
        pos = k * tk + col
        keys_scr[k] = jnp.where(pos <= row_pos, _ordered_key(score, pos), KMIN)
        return c

    lax.fori_loop(0, n_tiles, score_tile, 0, unroll=unroll)

    def count(pred):
        def count_tile(k, cnt):
            key = keys_scr[k]
            for j in range(tk // LANE):
                cnt = cnt + jnp.where(pred(key[:, j * LANE:(j + 1) * LANE]), 1.0, 0.0)
            return cnt

        cnt = lax.fori_loop(0, n_tiles, count_tile, jnp.zeros((tq, LANE), F32))
        return jnp.sum(cnt, axis=-1, keepdims=True)

    def search(bits, tu, n_ge):
        for bit in bits:
            cand_u = tu | np.int32(-2 ** 31 if bit == 31 else 2 ** bit)
            cand_s = jnp.broadcast_to(cand_u ^ np.int32(KMIN), (tq, LANE))
            tot = count(lambda key, cand_s=cand_s: key >= cand_s)
            tu = jnp.where(tot >= topk, cand_u, tu)
            n_ge = jnp.where(tot >= topk, tot, n_ge)
        return tu, n_ge

    tu, n_ge = search(range(31, LOW_BITS - 1, -1), jnp.zeros((tq, 1), I32), jnp.zeros((tq, 1), F32))
    settled = jnp.min(jnp.where((tu == 0) | (n_ge == topk), 1.0, 0.0)) > 0.5
    tu, n_ge = lax.cond(settled, lambda: (tu, n_ge), lambda: search(range(LOW_BITS - 1, -1, -1), tu, n_ge))
    thr = tu ^ np.int32(KMIN)
    found = thr > np.int32(KMIN)
    has_ties = jnp.max(jnp.where(found & (n_ge > topk), 1.0, 0.0)) > 0.5

    qp_g = [_padded_heads(qa, g) for g in range(A_KV)]
    scr = (m_scr, acc_scr)
    in_values = bool(unroll)

    def attend(k, st, mask, far):
        kv = get_kva(k)
        bds = None if far else _tile_bds(qb, k, tk)
        madd = _mask_add(mask)
        for g in range(A_KV):
            st = _flash_update(st, scr, g, qp_g[g], kv, strips_ref, far_ref, bds, madd, tq, kv_t)
        return st

    @pl.when(jnp.logical_not(has_ties))
    def _():
        thr_ge = jnp.where(found, thr, np.int32(KMIN + 1))

        def tile(k, st, far):
            return attend(k, st, keys_scr[k] >= thr_ge, far)

        st = _flash_init(scr, tq, in_values)
        st = lax.fori_loop(0, n_far, functools.partial(tile, far=True), st, unroll=unroll)
        st = lax.fori_loop(n_far, n_tiles, functools.partial(tile, far=False), st, unroll=unroll)
        _flash_store(st, scr)

    @pl.when(has_ties)
    def _():
        need = topk - count(lambda key: key > jnp.broadcast_to(thr, (tq, LANE)))
        tri_scr[...] = jnp.where(
            lax.broadcasted_iota(I32, (LANE, LANE), 0) <= lax.broadcasted_iota(I32, (LANE, LANE), 1),
            1.0, 0.0).astype(BF16)

        def tile(k, c):
            eq_seen, st = c
            key = keys_scr[k]
            eq = key == thr
            takes = []
            for j in range(tk // LANE):
                eq_j = eq[:, j * LANE:(j + 1) * LANE]
                rank = eq_seen + _mm(jnp.where(eq_j, 1.0, 0.0), tri_scr[...])
                takes.append(eq_j & (rank <= need))
                eq_seen = rank[:, LANE - 1:LANE]
            take = jnp.concatenate(takes, axis=1)
            st = attend(k, st, (key > thr) | (take & found), False)
            return eq_seen, st

        _, st = lax.fori_loop(0, n_tiles, tile, (jnp.zeros((tq, 1), F32), _flash_init(scr, tq, in_values)))
        _flash_store(st, scr)

    outs = []
    for g in range(A_KV):
        o = _flash_result(scr, g)
        outs += [o[r * tq:(r + 1) * tq] for r in range(GROUP)]
    return jnp.concatenate(outs, axis=1)


def _pick_blocks(v, n_pick, axis):
    lane = lax.broadcasted_iota(I32, v.shape, axis).astype(F32)
    sel = jnp.zeros(v.shape, jnp.bool_)
    for _ in range(n_pick):
        mx = jnp.max(v, axis=axis, keepdims=True)
        cand = (v == mx) & jnp.logical_not(sel)
        first = jnp.min(jnp.where(cand, lane, float(v.shape[axis])), axis=axis, keepdims=True)
        pick = lane == first
        sel = sel | pick
        v = jnp.where(pick, -jnp.inf, v)
    return sel


def _nsa_core(qb, gates, kcp, get_kvs, win_tiles, strips_ref, cstrips_ref, far_ref, t0, tq, tk, nj,
              n_pick, kv_t, unroll, m_scr, acc_scr):
    n_tiles = (t0 + tq - 1) // tk + 1
    qblk = t0 // LANE
    delta = _cmp_delta(tq)
    ncn = cstrips_ref.shape[-1]
    qp_g = [_padded_heads(qb, g) for g in range(B_KV)]
    row_pos1 = t0 + lax.broadcasted_iota(I32, (tq, 1), 0)

    lam = lax.broadcasted_iota(I32, (tq, 2 * nj), 1)
    m_of = 2 * (lam % nj) + lam // nj
    cdist = row_pos1 - (CMP_BLOCK * m_of + CMP_BLOCK - 1)
    cmask = cdist >= 0
    m_hi = (t0 - (CMP_BLOCK - 1) - delta) // CMP_BLOCK
    lam_s = lax.broadcasted_iota(I32, (ncn, 2 * nj), 1)
    shift = jnp.where(2 * (lam_s % nj) + lam_s // nj == m_hi - lax.broadcasted_iota(I32, (ncn, 2 * nj), 0),
                      1.0, 0.0).astype(BF16)
    is_far = m_of <= m_hi - ncn
    kc_both = kcp[:, :LANE]
    o_c, imp = [], []
    for g in range(B_KV):
        bias = []
        for r in range(GROUP):
            h = g * GROUP + r
            b = (_mm(cstrips_ref[h, 0], shift) + _mm(cstrips_ref[h, 1], shift)) + _mm(cstrips_ref[h, 2], shift)
            bias.append(jnp.where(is_far, far_ref[h], b))
        s = _mm_nt(qp_g[g], kc_both).reshape(GROUP, tq, 2 * nj) + jnp.stack(bias)
        s = jnp.where(cmask[None], s, NEG)
        e = jnp.where(cmask[None], jnp.exp(s - jnp.max(s, axis=-1, keepdims=True)), 0.0)
        l = jnp.sum(e, axis=-1, keepdims=True)
        p = jnp.where(l > 0.0, e / jnp.where(l > 0.0, l, 1.0), 0.0)
        o_c.append(_mm(p.reshape(GROUP * tq, 2 * nj), kcp[:, (B_KV + g) * HEAD_DIM:(B_KV + g + 1) * HEAD_DIM]))
        ps = p[0]
        for r in range(1, GROUP):
            ps = ps + p[r]
        imp.append(ps[:, :nj] + ps[:, nj:])

    jl = lax.broadcasted_iota(I32, (tq, nj), 1)
    cur = row_pos1 // SEL_BLOCK
    forced = (jl == 0) | (jl == cur) | (jl == cur - 1)
    admissible = jl <= cur
    vs = [jnp.where(admissible, jnp.where(forced, jnp.inf, imp[g]), -jnp.inf) for g in range(B_KV)]
    if tq % LANE == 0:
        picked_t = _pick_blocks(jnp.concatenate([jnp.transpose(v) for v in vs], axis=1), n_pick, 0)
        picked = [jnp.transpose(jnp.where(picked_t[:, g * tq:(g + 1) * tq], 1.0, 0.0)) > 0.5
                  for g in range(B_KV)]
    else:
        picked = [_pick_blocks(v, n_pick, 1) for v in vs]
    lhs = []
    for g in range(B_KV):
        bneg = jnp.where(picked[g] & admissible, 0.0, NEG).astype(BF16)
        lhs.append(jnp.concatenate([qp_g[g].astype(BF16), jnp.concatenate([bneg] * GROUP, axis=0)], axis=1))
    row_pos = t0 + lax.broadcasted_iota(I32, (tq, tk), 0)
    col = lax.broadcasted_iota(I32, (tq, tk), 1)
    ind_shape, j_axis = ((nj, tk), 0) if kv_t else ((tk, nj), 1)
    ej = lax.broadcasted_iota(I32, ind_shape, j_axis)
    ec = lax.broadcasted_iota(I32, ind_shape, 1 - j_axis)
    scr = (m_scr, acc_scr)
    in_values = bool(unroll)

    def sel_tile(k, st, far):
        kv = get_kvs(k)
        ind = jnp.where(ej == (k * tk + ec) // SEL_BLOCK, 1.0, 0.0).astype(BF16)
        rhs = jnp.concatenate([_k_part(kv, kv_t).astype(BF16), ind], axis=j_axis)
        if far:
            bds, madd = None, None
        else:
            bds, madd = _tile_bds(qblk, k, tk), _mask_add(k * tk + col <= row_pos)
        for g in range(B_KV):
            s_all = _mm(lhs[g], rhs) if kv_t else _mm_nt(lhs[g], rhs)
            st = _flash_update(st, scr, g, None, kv, strips_ref, far_ref, bds, madd, tq, kv_t, s_all)
        return st

    n_far = _far_tiles(qblk, tk)
    st = _flash_init(scr, tq, in_values)
    st = lax.fori_loop(0, n_far, functools.partial(sel_tile, far=True), st, unroll=unroll)
    st = lax.fori_loop(n_far, n_tiles, functools.partial(sel_tile, far=False), st, unroll=unroll)
    _flash_store(st, scr)
    o_s = [_flash_result(scr, g) for g in range(B_KV)]

    st = _flash_init(scr, tq, in_values)
    for get_tile, off, guard in win_tiles:
        def win_tile(st, get_tile=get_tile, off=off):
            kv = get_tile()
            wt = kv.shape[1] if kv_t else kv.shape[0]
            dist = (lax.broadcasted_iota(I32, (tq, wt), 0) - lax.broadcasted_iota(I32, (tq, wt), 1)
                    - LANE * off)
            wadd = _mask_add((dist >= 0) & (dist <= WINDOW))
            bds = [min(max(-off - j, 0), NEAR) for j in range(wt // LANE)]
            for g in range(B_KV):
                st = _flash_update(st, scr, g, qp_g[g], kv, strips_ref, far_ref, bds, wadd, tq, kv_t)
            return st

        if guard is None:
            st = win_tile(st)
        else:
            assert not in_values

            def guarded(win_tile=win_tile):
                win_tile(0)

            pl.when(guard)(guarded)
    _flash_store(st, scr)
    o_w = [_flash_result(scr, g) for g in range(B_KV)]

    gs = jax.nn.sigmoid(gates)
    outs = []
    for g in range(B_KV):
        gcol = lambda j: jnp.concatenate(
            [gs[:, (g * GROUP + r) * 3 + j:(g * GROUP + r) * 3 + j + 1] for r in range(GROUP)], axis=0)
        o = gcol(0) * o_c[g] + gcol(1) * o_s[g] + gcol(2) * o_w[g]
        outs += [o[r * tq:(r + 1) * tq] for r in range(GROUP)]
    return jnp.concatenate(outs, axis=1)


def _compress(load_rows, n_pairs, pe_ref, wbd_ref):
    acc = [jnp.zeros((n_pairs, KVW), F32), jnp.zeros((n_pairs, KVW), F32)]
    for c in range(2 * CMP_BLOCK):
        ci = c % CMP_BLOCK
        acc[c // CMP_BLOCK] = acc[c // CMP_BLOCK] + _mm(load_rows(c) + pe_ref[ci], wbd_ref[ci])
    return jnp.concatenate(acc, axis=0)


def _compress_operands(pe, w):
    pe_rows = jnp.transpose(jnp.broadcast_to(pe[:, None], (2, B_KV, CMP_BLOCK, HEAD_DIM)), (2, 0, 1, 3))
    pe_rows = pe_rows.reshape(CMP_BLOCK, 1, KVW)
    eye = jnp.eye(2 * B_KV, dtype=w.dtype).reshape(2, B_KV, 2, B_KV)
    wbd = jnp.einsum("icde,igjh->cigdjhe", w, eye).reshape(CMP_BLOCK, KVW, KVW)
    return pe_rows, wbd.astype(BF16)


def _strided_rows(lo_ref, hi_ref, c, n):
    rows = pl.ds(c, n, stride=2 * CMP_BLOCK)
    return jnp.concatenate([lo_ref[rows, :], hi_ref[rows, :]], axis=1)


def _flash_scratch(tq):
    return [pltpu.VMEM((A_KV, GROUP, tq, LANE), F32), pltpu.VMEM((A_KV, GROUP * tq, LANE), F32)]


TQ_P = 128
TK_P = 1024


def _prompt_tile(ref, k, tk):
    return ref[pl.ds(pl.multiple_of(k * tk, tk), tk), :]


def _pdsa_kernel(qi_ref, misc_ref, qa_ref, kidx_ref, kva_ref, strips_ref, far_ref, o_ref,
                 keys_scr, wib_scr, m_scr, acc_scr, tri_scr, *, topk, tk):
    i = pl.program_id(1)
    o = _dsa_core(qi_ref[...], misc_ref[:, M_WI:M_WI + IDX_HEADS], qa_ref[...],
                  lambda k: _prompt_tile(kidx_ref, k, tk), lambda k: _prompt_tile(kva_ref, k, tk),
                  strips_ref, far_ref, i * TQ_P, TQ_P, tk, topk, False, None,
                  keys_scr, wib_scr, m_scr, acc_scr, tri_scr)
    o_ref[...] = o.astype(o_ref.dtype)


def _pdsa_call(pr, strips, far, tk):
    nb, s, _ = pr["qa"].shape
    assert s < ZERO_KEY_BASE
    tq = TQ_P
    row = lambda w: pl.BlockSpec((None, tq, w), lambda b, i: (b, i, 0))
    full = lambda w: pl.BlockSpec((None, s, w), lambda b, i: (b, 0, 0))
    return pl.pallas_call(
        functools.partial(_pdsa_kernel, topk=min(IDX_TOPK, s // 4), tk=tk),
        grid=(nb, s // tq),
        in_specs=[row(256), row(128), row(512), full(128), full(KVX),
                  pl.BlockSpec(strips.shape, lambda b, i: (0, 0, 0, 0)),
                  pl.BlockSpec(memory_space=pltpu.SMEM)],
        out_specs=row(512),
        out_shape=jax.ShapeDtypeStruct((nb, s, A_HEADS * HEAD_DIM), BF16),
        scratch_shapes=[pltpu.VMEM((s // tk, tq, tk), I32), pltpu.VMEM((IDX_HEADS, tq, tk), F32)]
        + _flash_scratch(tq) + [pltpu.VMEM((LANE, LANE), BF16)],
        compiler_params=_cparams(("arbitrary", "arbitrary")),
        name="dsa_prompt",
    )(pr["qi"], pr["misc"], pr["qa"], pr["misc_h"], pr["kva_x"], strips, far)


def _pcmp_kernel(lo_ref, hi_ref, pe_ref, wbd_ref, o_ref, *, n_pairs):
    o_ref[...] = _compress(lambda c: _strided_rows(lo_ref, hi_ref, c, n_pairs), n_pairs, pe_ref, wbd_ref)


def _pcmp_call(kvc, pe_rows, wbd):
    nb, s, _ = kvc.shape
    n_pairs = s // (2 * CMP_BLOCK)
    return pl.pallas_call(
        functools.partial(_pcmp_kernel, n_pairs=n_pairs),
        grid=(nb,),
        in_specs=[pl.BlockSpec((None, s, LANE), lambda b: (b, 0, 0)),
                  pl.BlockSpec((None, s, LANE), lambda b: (b, 0, 1)),
                  pl.BlockSpec((CMP_BLOCK, 1, KVW), lambda b: (0, 0, 0)),
                  pl.BlockSpec((CMP_BLOCK, KVW, KVW), lambda b: (0, 0, 0))],
        out_specs=pl.BlockSpec((None, 2 * n_pairs, KVW), lambda b: (b, 0, 0)),
        out_shape=jax.ShapeDtypeStruct((nb, 2 * n_pairs, KVW), F32),
        compiler_params=_cparams(("arbitrary",)),
        name="compress_prompt",
    )(kvc, kvc, pe_rows, wbd)


def _pnsa_kernel(qb_ref, misc_ref, kcp_ref, kvs_ref, kvw_ref, strips_ref, cstrips_ref, far_ref, o_ref,
                 m_scr, acc_scr, *, nj, n_pick, tk):
    i = pl.program_id(1)
    nw = WINDOW // LANE
    win_tiles = [(lambda: kvw_ref[pl.ds(pl.multiple_of((i - nw) * LANE, LANE), WINDOW + TQ_P), :], -nw, i >= nw)]
    win_tiles += [(lambda wd=wd: _prompt_tile(kvw_ref, i - wd, LANE), -wd, (i < nw) & (i - wd >= 0))
                  for wd in range(nw - 1, -1, -1)]
    o = _nsa_core(qb_ref[...], misc_ref[:, M_GATE:M_GATE + 3 * B_HEADS], kcp_ref[...],
                  lambda k: _prompt_tile(kvs_ref, k, tk), win_tiles,
                  strips_ref, cstrips_ref, far_ref, i * TQ_P, TQ_P, tk, nj, n_pick, False, None,
                  m_scr, acc_scr)
    o_ref[...] = o.astype(o_ref.dtype)


def _pnsa_call(pr, kcp, strips, cstrips, far, tk):
    nb, s, _ = pr["qb"].shape
    tq = TQ_P
    nj = s // SEL_BLOCK
    row = lambda w: pl.BlockSpec((None, tq, w), lambda b, i: (b, i, 0))
    full = lambda n, w: pl.BlockSpec((None, n, w), lambda b, i: (b, 0, 0))
    const = lambda a: pl.BlockSpec(a.shape, lambda b, i: (0,) * a.ndim)
    return pl.pallas_call(
        functools.partial(_pnsa_kernel, nj=nj, n_pick=min(SEL_TOPN, nj), tk=tk),
        grid=(nb, s // tq),
        in_specs=[row(512), row(128), full(2 * nj, KVW), full(s, KVX), full(s, KVX),
                  const(strips), const(cstrips),
                  pl.BlockSpec(memory_space=pltpu.SMEM)],
        out_specs=row(512),
        out_shape=jax.ShapeDtypeStruct((nb, s, B_HEADS * HEAD_DIM), BF16),
        scratch_shapes=_flash_scratch(tq),
        compiler_params=_cparams(("arbitrary", "arbitrary")),
        name="nsa_prompt",
    )(pr["qb"], pr["misc"], kcp, pr["kvs_x"], pr["kvw_x"], strips, cstrips, far)


TQ_S = 8
TK_S = 2048
PPT = TK_S // PAGE


def _pages_per_step(n_pages, cap):
    return max(g for g in (64, 32, 16, 8, 4, 2, 1) if g <= cap and n_pages % g == 0)


def _new_token_block(col):
    return jnp.where(lax.broadcasted_iota(I32, (col.shape[0], PAGE), 1) == 0, col, 0.0)


def _slot_tile(buf, k):
    blk = buf[pl.ds(k * PPT, PPT)]
    return jnp.concatenate([blk[j] for j in range(PPT)], axis=1)


def _sdsa_kernel(*refs, n_pages, pps, topk):
    idx_refs = refs[1:1 + pps]
    akv_refs = refs[1 + pps:1 + 2 * pps]
    (qi_ref, wi_ref, qa_ref, new_idx_ref, new_kv_ref, strips_ref, far_ref, o_ref,
     idx_buf, akv_buf, keys_scr, wib_scr, m_scr, acc_scr, tri_scr) = refs[1 + 2 * pps:]
    p = pl.program_id(1)
    for j in range(pps):
        idx_buf[p * pps + j] = idx_refs[j][...].astype(BF16)
        akv_buf[p * pps + j] = _kv_ext_t(akv_refs[j][...])

    @pl.when(p == n_pages // pps - 1)
    def _():
        idx_buf[n_pages] = _new_token_block(new_idx_ref[...]).astype(BF16)
        akv_buf[n_pages] = _kv_ext_t(_new_token_block(new_kv_ref[...]))
        for j in range(1, PPT):
            idx_buf[n_pages + j] = jnp.zeros((IDX_DIM, PAGE), BF16)
            akv_buf[n_pages + j] = _kv_ext_t(jnp.zeros((KVW, PAGE), BF16))
        o = _dsa_core(qi_ref[...], wi_ref[...], qa_ref[...],
                      lambda k: _slot_tile(idx_buf, k), lambda k: _slot_tile(akv_buf, k),
                      strips_ref, far_ref, n_pages * PAGE, TQ_S, TK_S, topk, True, True,
                      keys_scr, wib_scr, m_scr, acc_scr, tri_scr)
        o_ref[...] = o.astype(o_ref.dtype)


def _sample_rows(x):
    return jnp.broadcast_to(x[:, None, :], (x.shape[0], TQ_S, x.shape[1]))


def _pages_t(cache):
    n, p = cache.shape[:2]
    nd = cache.ndim
    return jnp.transpose(cache, (0, 1) + tuple(range(3, nd)) + (2,)).reshape(n, p, -1, cache.shape[2])


def _sdsa_call(a, page_table, idx_t, akv_t, pr, strips, far):
    bd, n_pages = page_table.shape
    assert idx_t.shape[3] == PAGE and (n_pages * PAGE) % TK_S == 0 and n_pages * PAGE + TK_S < ZERO_KEY_BASE
    pps = _pages_per_step(n_pages, 64)
    slots = n_pages + PPT
    misc = pr["misc"][0]
    rows = lambda w: pl.BlockSpec((None, TQ_S, w), lambda b, p, pt: (b, 0, 0))
    col = lambda w: pl.BlockSpec((None, w, 1), lambda b, p, pt: (b, 0, 0))
    page = lambda w, j: pl.BlockSpec((None, None, w, PAGE),
                                     lambda b, p, pt: (a, pt[b * n_pages + p * pps + j], 0, 0))
    grid_spec = pltpu.PrefetchScalarGridSpec(
        num_scalar_prefetch=1,
        grid=(bd, n_pages // pps),
        in_specs=[page(IDX_DIM, j) for j in range(pps)] + [page(KVW, j) for j in range(pps)]
        + [rows(256), rows(IDX_HEADS), rows(512), col(IDX_DIM), col(KVW),
           pl.BlockSpec(strips.shape, lambda b, p, pt: (0, 0, 0, 0)),
           pl.BlockSpec(memory_space=pltpu.SMEM)],
        out_specs=rows(512),
        scratch_shapes=[pltpu.VMEM((slots, IDX_DIM, PAGE), BF16), pltpu.VMEM((slots, KVX, PAGE), BF16),
                        pltpu.VMEM((slots // PPT, TQ_S, TK_S), I32), pltpu.VMEM((IDX_HEADS, TQ_S, TK_S), F32)]
        + _flash_scratch(TQ_S) + [pltpu.VMEM((LANE, LANE), BF16)])
    out = pl.pallas_call(
        functools.partial(_sdsa_kernel, n_pages=n_pages, pps=pps, topk=min(IDX_TOPK, (n_pages * PAGE + 1) // 4)),
        grid_spec=grid_spec,
        out_shape=jax.ShapeDtypeStruct((bd, TQ_S, A_HEADS * HEAD_DIM), BF16),
        compiler_params=_cparams(("arbitrary", "arbitrary")),
        name="dsa_sample",
    )(page_table.reshape(-1), *([idx_t] * pps), *([akv_t] * pps),
      _sample_rows(pr["qi"][0]), _sample_rows(misc[:, M_WI:M_WI + IDX_HEADS]), _sample_rows(pr["qa"][0]),
      misc[:, M_KI:M_KI + IDX_DIM, None], pr["kva"][0][:, :, None], strips, far)
    return out[:, 0]


def _snsa_kernel(*refs, n_pages, pps):
    cmp_refs = refs[1:1 + pps]
    sel_refs = refs[1 + pps:1 + 2 * pps]
    (qb_ref, gates_ref, new_sel_ref, win_ref, new_win_ref, pe_ref, wbd_ref, strips_ref, cstrips_ref, far_ref,
     o_ref, cmp_lo, cmp_hi, sel_buf, m_scr, acc_scr) = refs[1 + 2 * pps:]
    p = pl.program_id(1)
    past = n_pages * PAGE
    for j in range(pps):
        rows = pl.ds(pl.multiple_of((p * pps + j) * PAGE, PAGE), PAGE)
        cmp_lo[rows, :] = jnp.transpose(cmp_refs[j][:LANE])
        cmp_hi[rows, :] = jnp.transpose(cmp_refs[j][LANE:])
        sel_buf[p * pps + j] = _kv_ext_t(sel_refs[j][...])

    @pl.when(p == n_pages // pps - 1)
    def _():
        sel_buf[n_pages] = _kv_ext_t(_new_token_block(new_sel_ref[...]))
        for j in range(1, PPT):
            sel_buf[n_pages + j] = _kv_ext_t(jnp.zeros((KVW, PAGE), BF16))
        nj = past // SEL_BLOCK
        kcp = _compress(lambda c: _strided_rows(cmp_lo, cmp_hi, c, nj), nj, pe_ref, wbd_ref)
        win_tiles = [(lambda: _kv_ext_t(win_ref[...]), -(WINDOW // LANE), None),
                     (lambda: _kv_ext_t(_new_token_block(new_win_ref[...])), 0, None)]
        o = _nsa_core(qb_ref[...], gates_ref[...], kcp, lambda k: _slot_tile(sel_buf, k), win_tiles,
                      strips_ref, cstrips_ref, far_ref, past, TQ_S, TK_S, nj,
                      min(SEL_TOPN, nj + 1) - 1, True, True, m_scr, acc_scr)
        o_ref[...] = o.astype(o_ref.dtype)


def _snsa_call(a, page_table, cmp_t, sel_t, win_t, pr, pe_rows, wbd, strips, cstrips, far):
    bd, n_pages = page_table.shape
    past = n_pages * PAGE
    assert past % TK_S == 0 and win_t.shape[3] == WINDOW
    pps = _pages_per_step(n_pages, 16)
    misc = pr["misc"][0]
    rows = lambda w: pl.BlockSpec((None, TQ_S, w), lambda b, p, pt: (b, 0, 0))
    col = lambda w: pl.BlockSpec((None, w, 1), lambda b, p, pt: (b, 0, 0))
    page = lambda j: pl.BlockSpec((None, None, KVW, PAGE),
                                  lambda b, p, pt: (a, pt[b * n_pages + p * pps + j], 0, 0))
    const = lambda x: pl.BlockSpec(x.shape, lambda b, p, pt: (0,) * x.ndim)
    grid_spec = pltpu.PrefetchScalarGridSpec(
        num_scalar_prefetch=1,
        grid=(bd, n_pages // pps),
        in_specs=[page(j) for j in range(pps)] + [page(j) for j in range(pps)]
        + [rows(512), rows(3 * B_HEADS), col(KVW),
           pl.BlockSpec((None, None, KVW, WINDOW), lambda b, p, pt: (a, b, 0, 0)), col(KVW),
           const(pe_rows), const(wbd), const(strips), const(cstrips),
           pl.BlockSpec(memory_space=pltpu.SMEM)],
        out_specs=rows(512),
        scratch_shapes=[pltpu.VMEM((past, LANE), F32), pltpu.VMEM((past, LANE), F32),
                        pltpu.VMEM((n_pages + PPT, KVX, PAGE), BF16)] + _flash_scratch(TQ_S))
    out = pl.pallas_call(
        functools.partial(_snsa_kernel, n_pages=n_pages, pps=pps),
        grid_spec=grid_spec,
        out_shape=jax.ShapeDtypeStruct((bd, TQ_S, B_HEADS * HEAD_DIM), BF16),
        compiler_params=_cparams(("arbitrary", "arbitrary")),
        name="nsa_sample",
    )(page_table.reshape(-1), *([cmp_t] * pps), *([sel_t] * pps),
      _sample_rows(pr["qb"][0]), _sample_rows(misc[:, M_GATE:M_GATE + 3 * B_HEADS]),
      pr["kvs"][0][:, :, None], win_t, pr["kvw"][0][:, :, None],
      pe_rows, wbd, strips, cstrips, far)
    return out[:, 0]


def _merge_kernel(x_ref, mod_ref, oa_ref, ob_ref, w_ref, o_ref):
    n = oa_ref.shape[-1]
    y = _mm(oa_ref[...], w_ref[:n]) + _mm(ob_ref[...], w_ref[n:])
    o_ref[...] = x_ref[...] + (1.0 + mod_ref[2]) * y


def _merge_call(x, mod, oa, ob, w, tm):
    nb, t, d = x.shape
    tmod = mod.shape[2]
    mod_spec = (pl.BlockSpec((None, 3, 1, d), lambda b, i: (b, 0, 0, 0)) if tmod == 1 else
                pl.BlockSpec((None, 3, tm, d), lambda b, i: (b, 0, i, 0)))
    row = lambda w_: pl.BlockSpec((None, tm, w_), lambda b, i: (b, i, 0))
    return pl.pallas_call(
        _merge_kernel,
        grid=(nb, t // tm),
        in_specs=[row(d), mod_spec, row(oa.shape[-1]), row(ob.shape[-1]),
                  pl.BlockSpec(w.shape, lambda b, i: (0, 0))],
        out_specs=row(d),
        out_shape=jax.ShapeDtypeStruct(x.shape, F32),
        compiler_params=_cparams(("arbitrary", "arbitrary")),
        name="merge",
    )(x, mod, oa, ob, w)


def _gelu_ln(h, w_in_ref, ln_ref, cw):
    uv = jax.nn.gelu(_mm(h, w_in_ref[...]))
    u, v = uv[:, :cw], uv[:, cw:]
    mu = jnp.mean(v, axis=-1, keepdims=True)
    var = jnp.mean(jnp.square(v - mu), axis=-1, keepdims=True)
    return u, (v - mu) * lax.rsqrt(var + EPS) * ln_ref[...]


def _pgmlp_kernel(x_ref, mod_ref, g_ref, w_in_ref, ln_ref, ws_ref, bs_ref, w_out_ref, o_ref, *, cw):
    x = x_ref[...]
    h = _modulated_norm(x, g_ref[...], mod_ref).astype(BF16)
    u, v = _gelu_ln(h, w_in_ref, ln_ref, cw)
    gw = cw // C_GROUPS
    rows = []
    for n in range(x.shape[0] // C_CHUNK):
        vb = v[n * C_CHUNK:(n + 1) * C_CHUNK].astype(BF16)
        sv = [_mm(ws_ref[g], vb[:, g * gw:(g + 1) * gw]) + bs_ref[:, g:g + 1] for g in range(C_GROUPS)]
        rows.append(jnp.concatenate(sv, axis=1))
    sv = rows[0] if len(rows) == 1 else jnp.concatenate(rows, axis=0)
    o_ref[...] = x + (1.0 + mod_ref[2]) * _mm(u * sv, w_out_ref[...])


def _pgmlp_call(x, mod, g, w_in, ln_g, ws, bs_t, w_out, tm):
    nb, t, d = x.shape
    cw = w_out.shape[0]
    const = lambda a: pl.BlockSpec(a.shape, lambda b, i: (0,) * a.ndim)
    return pl.pallas_call(
        functools.partial(_pgmlp_kernel, cw=cw),
        grid=(nb, t // tm),
        in_specs=[pl.BlockSpec((None, tm, d), lambda b, i: (b, i, 0)),
                  pl.BlockSpec((None, 3, 1, d), lambda b, i: (b, 0, 0, 0)),
                  pl.BlockSpec((1, d), lambda b, i: (0, 0)),
                  const(w_in), pl.BlockSpec((1, cw), lambda b, i: (0, 0)), const(ws), const(bs_t), const(w_out)],
        out_specs=pl.BlockSpec((None, tm, d), lambda b, i: (b, i, 0)),
        out_shape=jax.ShapeDtypeStruct(x.shape, F32),
        compiler_params=_cparams(("arbitrary", "arbitrary")),
        name="gmlp_prompt",
    )(x, mod, g.reshape(1, d), w_in, ln_g.reshape(1, cw), ws, bs_t, w_out)


def _sgmlp_kernel(x_ref, mod_ref, g_ref, w_in_ref, ln_ref, ws0_ref, bs0_ref, w_out_ref, o_ref, v_ref, *, cw):
    x = x_ref[...]
    h = _modulated_norm(x, g_ref[...], mod_ref).astype(BF16)
    u, v = _gelu_ln(h, w_in_ref, ln_ref, cw)
    v_ref[...] = v
    sv = v.astype(BF16).astype(F32) * ws0_ref[...].astype(F32) + bs0_ref[...]
    o_ref[...] = x + (1.0 + mod_ref[2]) * _mm(u * sv, w_out_ref[...])


def _sgmlp_call(x, mod, g, w_in, ln_g, ws, b_sp, w_out):
    nb, t, d = x.shape
    cw = w_out.shape[0]
    gw = cw // C_GROUPS
    ws0 = jnp.repeat(ws[:, 0, 0], gw).reshape(1, cw)
    bs0 = jnp.repeat(b_sp[:, 0], gw).reshape(1, cw)
    full = lambda a: pl.BlockSpec(a.shape, lambda: (0,) * a.ndim)
    x2, mod2 = x[0], mod[0]
    o, v = pl.pallas_call(
        functools.partial(_sgmlp_kernel, cw=cw),
        in_specs=[full(x2), full(mod2), pl.BlockSpec((1, d), lambda: (0, 0)), full(w_in),
                  pl.BlockSpec((1, cw), lambda: (0, 0)), full(ws0), full(bs0), full(w_out)],
        out_specs=[full(x2), pl.BlockSpec((t, cw), lambda: (0, 0))],
        out_shape=[jax.ShapeDtypeStruct(x2.shape, F32), jax.ShapeDtypeStruct((t, cw), F32)],
        compiler_params=pltpu.CompilerParams(vmem_limit_bytes=VMEM_LIMIT),
        name="gmlp_sample",
    )(x2, mod2, g.reshape(1, d), w_in, ln_g.reshape(1, cw), ws0, bs0, w_out)
    return o[None], v


def kernel(x_prompt, x_sample, cache_a_kv, cache_a_idx, cache_b_cmp_kv, cache_b_sel_kv, state_b_win_kv,
           page_table, c_prompt, c_sample, rel_bias, w_ada, b_ada, norm_g, w_ffn_in, w_ffn_out,
           w_in_att, w_out_att, cmp_pe, w_cmp, w_in_c, ln_c_g, w_sp, b_sp, w_out_c, final_g):
    nbp, s, d = x_prompt.shape
    nbs, tdec, _ = x_sample.shape
    depth = w_ada.shape[0]
    assert tdec == 1 and s % TQ_P == 0 and cache_a_kv.shape[2] == PAGE

    ada = _ada_call(jnp.concatenate([c_prompt, c_sample], axis=0), w_ada, b_ada)
    ada = ada.reshape(depth, nbp + nbs, 3, 3, d)
    mod_p = lambda l, j: ada[l, :nbp, j][:, :, None, :]
    mod_s = lambda l, j: jnp.transpose(ada[l, nbp:, j], (1, 0, 2))[None]

    w_ffn_in_h = w_ffn_in.astype(BF16)
    w_ffn_out_h = w_ffn_out.astype(BF16)
    tab_a, tab_b = rel_bias[:, :A_HEADS], rel_bias[:, A_HEADS:]
    strips_pa, strips_pb = _bias_strips(tab_a, TQ_P), _bias_strips(tab_b, TQ_P)
    strips_sa, strips_sb = _bias_strips(tab_a, TQ_S), _bias_strips(tab_b, TQ_S)
    cstrips_p, cstrips_s = _cmp_strips(tab_b, TQ_P), _cmp_strips(tab_b, TQ_S)
    far_a, far_b = tab_a[N_BUCKETS - 1], tab_b[N_BUCKETS - 1]
    idx_t, akv_t = _pages_t(cache_a_idx), _pages_t(cache_a_kv)
    cmp_t, sel_t, win_t = _pages_t(cache_b_cmp_kv), _pages_t(cache_b_sel_kv), _pages_t(state_b_win_kv)

    xp = x_prompt
    xs = jnp.transpose(x_sample, (1, 0, 2))
    tm_p = 512 if s % 512 == 0 else TQ_P
    tk_p = TK_P if s % TK_P == 0 else TQ_P
    st_p, st_s, st_c = [], [], []
    for l in range(depth):
        last = l == depth - 1
        xp = _ffn_call(xp, mod_p(l, 0), norm_g[l, 0], w_ffn_in_h, w_ffn_out_h, l, 0, None, tm_p)
        xs = _ffn_call(xs, mod_s(l, 0), norm_g[l, 0], w_ffn_in_h, w_ffn_out_h, l, 0, None, nbs)
        if l % 2 == 0:
            a = l // 2
            wp = _rearranged_proj_weight(w_in_att[a])
            w_out = w_out_att[a].astype(BF16)
            pe_rows, wbd = _compress_operands(cmp_pe[a], w_cmp[a])

            pr = _proj_call(xp, mod_p(l, 1), norm_g[l, 1], wp, tm_p)
            oa = _pdsa_call(pr, strips_pa, far_a, tk_p)
            kcp = _pcmp_call(pr["kvc"], pe_rows, wbd)
            ob = _pnsa_call(pr, kcp, strips_pb, cstrips_p, far_b, tk_p)
            xp = _merge_call(xp, mod_p(l, 1), oa, ob, w_out, tm_p)
            kv5 = lambda t: t.reshape(t.shape[0], t.shape[1], 2, A_KV, HEAD_DIM)
            st_p.append((kv5(pr["kva"]), pr["misc"][:, :, M_KI:M_KI + IDX_DIM], kv5(pr["kvc"]),
                         kv5(pr["kvs"]), kv5(pr["kvw"][:, s - min(WINDOW, s):])))

            ps = _proj_call(xs, mod_s(l, 1), norm_g[l, 1], wp, nbs)
            oa = _sdsa_call(a, page_table, idx_t, akv_t, ps, strips_sa, far_a)
            ob = _snsa_call(a, page_table, cmp_t, sel_t, win_t, ps, pe_rows, wbd, strips_sb, cstrips_s, far_b)
            xs = _merge_call(xs, mod_s(l, 1), oa[None], ob[None], w_out, nbs)
            tok = lambda t: t[0].reshape(nbs, 1, 2, A_KV, HEAD_DIM)
            st_s.append((tok(ps["kva"]), ps["misc"][0][:, None, M_KI:M_KI + IDX_DIM], tok(ps["kvc"]),
                         tok(ps["kvs"]),
                         jnp.concatenate([state_b_win_kv[a][:, tdec:], tok(ps["kvw"])], axis=1)))
        else:
            ci = l // 2
            w_in = w_in_c[ci].astype(BF16)
            w_out = w_out_c[ci].astype(BF16)
            ws = (w_sp[ci] * jnp.tril(jnp.ones((C_CHUNK, C_CHUNK), w_sp.dtype))).astype(BF16)
            xp = _pgmlp_call(xp, mod_p(l, 1), norm_g[l, 1], w_in, ln_c_g[ci], ws,
                             jnp.transpose(b_sp[ci]), w_out, 256 if s % 256 == 0 else C_CHUNK)
            xs, v = _sgmlp_call(xs, mod_s(l, 1), norm_g[l, 1], w_in, ln_c_g[ci], ws, b_sp[ci], w_out)
            st_c.append(v[:, None, :])
        fg = final_g if last else None
        xp = _ffn_call(xp, mod_p(l, 2), norm_g[l, 2], w_ffn_in_h, w_ffn_out_h, l, 1, fg, tm_p)
        xs = _ffn_call(xs, mod_s(l, 2), norm_g[l, 2], w_ffn_in_h, w_ffn_out_h, l, 1, fg, nbs)

    stk = lambda sts, i: jnp.stack([st[i] for st in sts])
    return (xp, jnp.transpose(xs, (1, 0, 2)),
            stk(st_p, 0), stk(st_p, 1), stk(st_p, 2), stk(st_p, 3), stk(st_p, 4),
            stk(st_s, 0), stk(st_s, 1), stk(st_s, 2), stk(st_s, 3), stk(st_s, 4),
            jnp.stack(st_c))
```

```python
import functools
import math

import numpy as np
import jax
import jax.numpy as jnp
from jax import lax
from jax.experimental import pallas as pl
from jax.experimental.pallas import tpu as pltpu

F32 = jnp.float32
BF16 = jnp.bfloat16
I32 = jnp.int32
I16 = jnp.int16

HEAD_DIM = 64
A_HEADS = 8
A_KV = 2
IDX_HEADS = 4
IDX_DIM = 64
IDX_TOPK = 256
B_HEADS = 8
B_KV = 2
CMP_BLOCK = 32
SEL_BLOCK = 64
SEL_TOPN = 16
WINDOW = 512
N_BUCKETS = 32
MAX_DISTANCE = 1024
C_CHUNK = 128
C_GROUPS = 8
EPS = 1e-6
PAGE = 128
GROUP = A_HEADS // A_KV
KVW = 2 * A_KV * HEAD_DIM

LANE = 128
VMEM_LIMIT = 56 * 1024 * 1024
KVX = 3 * LANE

NEG = -1e30
KMIN = -2 ** 31

P_QA, P_QB, P_KVA, P_QI, P_KVC, P_KVS, P_KVW, P_MISC = 0, 512, 1024, 1280, 1536, 1792, 2048, 2304
P_WIDTH = 2432
M_KI, M_WI, M_GATE = 0, IDX_DIM, IDX_DIM + IDX_HEADS

assert A_KV == 2 and B_KV == 2 and A_KV * HEAD_DIM == LANE and IDX_DIM == HEAD_DIM


def _cparams(sem):
    return pltpu.CompilerParams(dimension_semantics=sem, vmem_limit_bytes=VMEM_LIMIT)


def _mm(a, b):
    return jnp.dot(a.astype(BF16), b.astype(BF16), preferred_element_type=F32)


def _mm_nt(a, b):
    return lax.dot_general(a.astype(BF16), b.astype(BF16), (((1,), (1,)), ((), ())),
                           preferred_element_type=F32)


def _modulated_norm(x, g, mod_ref):
    y = x * lax.rsqrt(jnp.mean(x * x, axis=-1, keepdims=True) + EPS) * g
    return y * (1.0 + mod_ref[1]) + mod_ref[0]


def _ada_kernel(c_ref, w_ref, b_ref, o_ref):
    c = c_ref[...]
    o_ref[...] = _mm(c * jax.nn.sigmoid(c), w_ref[...]) + b_ref[...]


def _ada_call(c_all, w_ada, b_ada):
    depth, d, n = w_ada.shape
    r = c_all.shape[0]
    tn = n // 8
    return pl.pallas_call(
        _ada_kernel,
        grid=(depth, n // tn),
        in_specs=[pl.BlockSpec((r, d), lambda l, j: (0, 0)),
                  pl.BlockSpec((None, d, tn), lambda l, j: (l, 0, j)),
                  pl.BlockSpec((None, 1, tn), lambda l, j: (l, 0, j))],
        out_specs=pl.BlockSpec((None, r, tn), lambda l, j: (l, 0, j)),
        out_shape=jax.ShapeDtypeStruct((depth, r, n), F32),
        compiler_params=_cparams(("arbitrary", "arbitrary")),
        name="ada",
    )(c_all, w_ada, b_ada.reshape(depth, 1, n))


def _ffn_kernel(x_ref, mod_ref, g_ref, wa_ref, wb_ref, wo_ref, fg_ref, o_ref, h_scr, acc_scr, *,
                n_f, final_norm):
    f = pl.program_id(2)

    @pl.when(f == 0)
    def _():
        h_scr[...] = _modulated_norm(x_ref[...], g_ref[...], mod_ref).astype(BF16)
        acc_scr[...] = jnp.zeros_like(acc_scr)

    h = h_scr[...]
    a = _mm(h, wa_ref[...])
    b = _mm(h, wb_ref[...])
    acc_scr[...] += _mm(a * jax.nn.sigmoid(a) * b, wo_ref[...])

    @pl.when(f == n_f - 1)
    def _():
        y = x_ref[...] + 0.5 * (1.0 + mod_ref[2]) * acc_scr[...]
        if final_norm:
            y = y * lax.rsqrt(jnp.mean(y * y, axis=-1, keepdims=True) + EPS) * fg_ref[...]
        o_ref[...] = y


def _ffn_call(x, mod, g, w_in, w_out, l, j, final_g, tm):
    nb, t, d = x.shape
    dff = w_out.shape[2]
    tf = dff // 2 if dff % (2 * LANE) == 0 else LANE
    n_f = dff // tf
    tmod = mod.shape[2]
    mod_spec = (pl.BlockSpec((None, 3, 1, d), lambda b, i, f: (b, 0, 0, 0)) if tmod == 1 else
                pl.BlockSpec((None, 3, tm, d), lambda b, i, f: (b, 0, i, 0)))
    final_norm = final_g is not None
    fg = (final_g if final_norm else g).reshape(1, d)
    return pl.pallas_call(
        functools.partial(_ffn_kernel, n_f=n_f, final_norm=final_norm),
        grid=(nb, t // tm, n_f),
        in_specs=[pl.BlockSpec((None, tm, d), lambda b, i, f: (b, i, 0)),
                  mod_spec,
                  pl.BlockSpec((1, d), lambda b, i, f: (0, 0)),
                  pl.BlockSpec((None, None, d, tf), lambda b, i, f: (l, j, 0, f)),
                  pl.BlockSpec((None, None, d, tf), lambda b, i, f: (l, j, 0, f + n_f)),
                  pl.BlockSpec((None, None, tf, d), lambda b, i, f: (l, j, f, 0)),
                  pl.BlockSpec((1, d), lambda b, i, f: (0, 0))],
        out_specs=pl.BlockSpec((None, tm, d), lambda b, i, f: (b, i, 0)),
        out_shape=jax.ShapeDtypeStruct(x.shape, F32),
        scratch_shapes=[pltpu.VMEM((tm, d), BF16), pltpu.VMEM((tm, d), F32)],
        compiler_params=_cparams(("arbitrary", "arbitrary", "arbitrary")),
        name="ffn",
    )(x, mod, g.reshape(1, d), w_in, w_in, w_out, fg)


def _kv_ext(y):
    e = jnp.where(lax.broadcasted_iota(I32, (y.shape[0], HEAD_DIM), 1) == 0, 1.0, 0.0).astype(BF16)
    y = y.astype(BF16)
    return jnp.concatenate([y[:, :LANE], y[:, LANE:LANE + HEAD_DIM], e, y[:, LANE + HEAD_DIM:], e], axis=1)


def _kv_ext_t(y):
    e = jnp.where(lax.broadcasted_iota(I32, (HEAD_DIM, y.shape[1]), 0) == 0, 1.0, 0.0).astype(BF16)
    y = y.astype(BF16)
    return jnp.concatenate([y[:LANE], y[LANE:LANE + HEAD_DIM], e, y[LANE + HEAD_DIM:], e], axis=0)


_PROJ_OUT = (("qa", P_QA, 512, "h"), ("qb", P_QB, 512, "h"), ("qi", P_QI, 256, "h"),
             ("kva", P_KVA, 256, "f"), ("kvc", P_KVC, 256, "f"), ("kvs", P_KVS, 256, "f"),
             ("kvw", P_KVW, 256, "f"), ("misc", P_MISC, 128, "f"),
             ("kva_x", P_KVA, KVX, "x"), ("kvs_x", P_KVS, KVX, "x"), ("kvw_x", P_KVW, KVX, "x"),
             ("misc_h", P_MISC, 128, "h"))


def _proj_kernel(x_ref, mod_ref, g_ref, w_ref, *o_refs):
    h = _modulated_norm(x_ref[...], g_ref[...], mod_ref).astype(BF16)
    done = {}
    for (name, off, width, kind), o_ref in zip(_PROJ_OUT, o_refs):
        if off not in done:
            y = _mm(h, w_ref[:, off:off + (KVW if kind == "x" else width)])
            if name in ("qa", "qb"):
                y = y * (HEAD_DIM ** -0.5)
            done[off] = y
        o_ref[...] = _kv_ext(done[off]) if kind == "x" else done[off].astype(o_ref.dtype)


def _proj_call(x, mod, g, wp, tm):
    nb, t, d = x.shape
    tmod = mod.shape[2]
    mod_spec = (pl.BlockSpec((None, 3, 1, d), lambda b, i: (b, 0, 0, 0)) if tmod == 1 else
                pl.BlockSpec((None, 3, tm, d), lambda b, i: (b, 0, i, 0)))
    outs = pl.pallas_call(
        _proj_kernel,
        grid=(nb, t // tm),
        in_specs=[pl.BlockSpec((None, tm, d), lambda b, i: (b, i, 0)),
                  mod_spec,
                  pl.BlockSpec((1, d), lambda b, i: (0, 0)),
                  pl.BlockSpec((d, P_WIDTH), lambda b, i: (0, 0))],
        out_specs=[pl.BlockSpec((None, tm, w), lambda b, i: (b, i, 0)) for _, _, w, _ in _PROJ_OUT],
        out_shape=[jax.ShapeDtypeStruct((nb, t, w), F32 if kind == "f" else BF16)
                   for _, _, w, kind in _PROJ_OUT],
        compiler_params=_cparams(("arbitrary", "arbitrary")),
        name="proj",
    )(x, mod, g.reshape(1, d), wp)
    return {name: o for (name, _, _, _), o in zip(_PROJ_OUT, outs)}


def _rearranged_proj_weight(w):
    d = w.shape[0]
    o = np.cumsum((0, 512, 256, 256, 64, 4, 512, 256, 256, 256, 24))
    seg = lambda i: w[:, o[i]:o[i + 1]]
    qa, kva, qi, ki, wi, qb, kvc, kvs, kvw, gates = [seg(i) for i in range(10)]
    pad = jnp.zeros((d, LANE - IDX_DIM - IDX_HEADS - 3 * B_HEADS), w.dtype)
    return jnp.concatenate([qa, qb, kva, qi, kvc, kvs, kvw, ki, wi, gates, pad], axis=1).astype(BF16)


def _t5_bucket(dist):
    n = jnp.maximum(dist, 0)
    max_exact = N_BUCKETS // 2
    nf = jnp.maximum(n, 1).astype(F32)
    large = max_exact + (jnp.log(nf / max_exact) / math.log(MAX_DISTANCE / max_exact)
                         * (N_BUCKETS - max_exact)).astype(I32)
    large = jnp.minimum(large, N_BUCKETS - 1)
    return jnp.where(n < max_exact, n, large)


def _bias_lookup(tab, dist):
    b = _t5_bucket(jnp.asarray(dist.reshape(-1), I32))
    onehot = (b[:, None] == jnp.arange(N_BUCKETS, dtype=I32)[None, :]).astype(F32)
    out = jnp.dot(onehot, tab, precision=lax.Precision.HIGHEST)
    return jnp.transpose(out).reshape((tab.shape[1],) + dist.shape)


NEAR = -(-(MAX_DISTANCE + LANE - 1) // LANE)


def _bias_strips(tab, tq):
    r = np.arange(tq)[None, :, None]
    c = np.arange(LANE)[None, None, :]
    bd = np.arange(NEAR + 1)[:, None, None]
    dist = np.where(bd == NEAR, MAX_DISTANCE, r - c + LANE * bd)
    return _bias_lookup(tab, dist)


def _cmp_near_count(delta):
    return -(-(-(-(MAX_DISTANCE - delta) // CMP_BLOCK)) // 8) * 8


def _cmp_delta(tq):
    return -(CMP_BLOCK - 1) - CMP_BLOCK * ((tq - CMP_BLOCK) // CMP_BLOCK)


def _cmp_strips(tab, tq):
    delta = _cmp_delta(tq)
    ncn = _cmp_near_count(delta)
    dist = np.arange(tq)[:, None] + delta + CMP_BLOCK * np.arange(ncn)[None, :]
    v = _bias_lookup(tab, dist)
    hi = v.astype(BF16)
    r1 = v - hi.astype(F32)
    mid = r1.astype(BF16)
    lo = (r1 - mid.astype(F32)).astype(BF16)
    return jnp.stack([hi, mid, lo], axis=1)


def _padded_heads(x, g):
    z = jnp.zeros((x.shape[0], HEAD_DIM), x.dtype)
    rows = []
    for r in range(GROUP):
        h = g * GROUP + r
        piece = x[:, h * HEAD_DIM:(h + 1) * HEAD_DIM]
        rows.append(jnp.concatenate([piece, z] if g == 0 else [z, piece], axis=1))
    return jnp.concatenate(rows, axis=0)


def _k_part(kv, kv_t):
    return kv[:LANE] if kv_t else kv[:, :LANE]


def _v_part(kv, g, kv_t):
    return kv[(1 + g) * LANE:(2 + g) * LANE] if kv_t else kv[:, (1 + g) * LANE:(2 + g) * LANE]


def _flash_init(scr, tq, in_values):
    if in_values:
        return tuple((jnp.full((GROUP, tq, LANE), NEG, F32), jnp.zeros((GROUP * tq, LANE), F32))
                     for _ in range(A_KV))
    scr[0][...] = jnp.full_like(scr[0], NEG)
    scr[1][...] = jnp.zeros_like(scr[1])
    return 0


def _flash_store(st, scr):
    if isinstance(st, tuple):
        for g, (m, acc) in enumerate(st):
            scr[0][g] = m
            scr[1][g] = acc


def _mask_add(mask):
    return jnp.where(mask, 0.0, NEG)


def _flash_update(st, scr, g, qp_g, kv, strips_ref, far_ref, bds, madd, tq, kv_t, s_all=None):
    m_all, acc = st[g] if isinstance(st, tuple) else (scr[0][g], scr[1][g])
    if s_all is None:
        kb = _k_part(kv, kv_t)
        s_all = _mm(qp_g, kb) if kv_t else _mm_nt(qp_g, kb)
    nblk = s_all.shape[1] // LANE
    ps, alphas, m_news = [], [], []
    for r in range(GROUP):
        h = g * GROUP + r
        blocks = []
        for j in range(nblk):
            sj = s_all[r * tq:(r + 1) * tq, j * LANE:(j + 1) * LANE]
            if madd is not None:
                sj = sj + madd[:, j * LANE:(j + 1) * LANE]
            blocks.append(sj if bds is None else sj + strips_ref[h, bds[j]])
        bmax = blocks[0]
        for sj in blocks[1:]:
            bmax = jnp.maximum(bmax, sj)
        row_max = jnp.max(bmax, axis=-1, keepdims=True)
        m_old = m_all[r]
        if bds is None:
            m_new = jnp.maximum(m_old, row_max + far_ref[h])
            shift = m_new - far_ref[h]
        else:
            m_new = jnp.maximum(m_old, row_max)
            shift = m_new
        m_news.append(m_new)
        alphas.append(jnp.exp(m_old - m_new))
        ps.append(jnp.concatenate([jnp.exp(sj - shift).astype(BF16) for sj in blocks], axis=1))
    p = jnp.concatenate(ps, axis=0)
    vx = _v_part(kv, g, kv_t)
    pv = _mm_nt(p, vx) if kv_t else _mm(p, vx)
    m_all = jnp.stack(m_news)
    acc = jnp.concatenate(alphas, axis=0) * acc + pv
    if isinstance(st, tuple):
        return st[:g] + ((m_all, acc),) + st[g + 1:]
    scr[0][g] = m_all
    scr[1][g] = acc
    return st


def _flash_result(scr, g):
    acc = scr[1][g]
    l = acc[:, HEAD_DIM:HEAD_DIM + 1]
    return jnp.where(l > 0.0, acc[:, :HEAD_DIM] / jnp.where(l > 0.0, l, 1.0), 0.0)


def _tile_bds(qb, k, tk):
    nblk = tk // LANE
    return [jnp.clip(qb - (k * nblk + j), 0, NEAR) for j in range(nblk)]


ZERO_KEY_BASE = 2 ** 16 - 1
LOW_BITS = 7


def _ordered_key(x, pos):
    b = lax.bitcast_convert_type(x, I32)
    return jnp.where(x == 0.0, (ZERO_KEY_BASE - pos) << LOW_BITS, b ^ ((b >> 31) & 0x7FFFFFFF))


def _far_tiles(qb, tk):
    n = (qb - NEAR + 1) // (tk // LANE)
    return max(n, 0) if isinstance(n, int) else jnp.maximum(n, 0)


def _dsa_core(qi, wi, qa, get_kidx, get_kva, strips_ref, far_ref, t0, tq, tk, topk, kv_t, unroll,
              keys_scr, khi_scr, klo_scr, wib_scr, m_scr, acc_scr, tri_scr):
    n_tiles = (t0 + tq - 1) // tk + 1
    qb = t0 // LANE
    n_far = _far_tiles(qb, tk)
    row_pos = t0 + lax.broadcasted_iota(I32, (tq, tk), 0)
    col = lax.broadcasted_iota(I32, (tq, tk), 1)

    if kv_t:
        qi_st = jnp.concatenate([qi[:, h * IDX_DIM:(h + 1) * IDX_DIM] for h in range(IDX_HEADS)], axis=0)
    else:
        z = jnp.zeros((tq, LANE - IDX_DIM), qi.dtype)
        qi_st = jnp.concatenate(
            [jnp.concatenate([qi[:, h * IDX_DIM:(h + 1) * IDX_DIM], z], axis=1) for h in range(IDX_HEADS)], axis=0)
    for h in range(IDX_HEADS):
        wib_scr[h] = jnp.broadcast_to(wi[:, h:h + 1], (tq, tk))

    def score_tile(k, c):
        kt = get_kidx(k)
        sc = _mm(qi_st, kt) if kv_t else _mm_nt(qi_st, kt)
        score = jnp.maximum(sc[:tq], 0.0) * wib_scr[0]
        for h in range(1, IDX_HEADS):
            score = score + jnp.maximum(sc[h * tq:(h + 1) * tq], 0.0) * wib_scr[h]
        pos = k * tk + col
        key = jnp.where(pos <= row_pos, _ordered_key(score, pos), KMIN)
        keys_scr[k] = key
        khi_scr[k] = (key >> 16).astype(I16)
        klo_scr[k] = ((key & 0xFFFF) - 2 ** 15).astype(I16)
        return c

    lax.fori_loop(0, n_tiles, score_tile, 0, unroll=unroll)

    def count16(src, pred):
        def count_tile(k, cnt):
            half = src[k]
            for j in range(tk // LANE):
                cnt = cnt + jnp.where(pred(half[:, j * LANE:(j + 1) * LANE]), jnp.int16(1), jnp.int16(0))
            return cnt

        cnt = lax.fori_loop(0, n_tiles, count_tile, jnp.zeros((tq, LANE), I16))
        return jnp.sum(cnt.astype(F32), axis=-1, keepdims=True)

    def half_cand(u):
        return jnp.broadcast_to((u - 2 ** 15).astype(I16), (tq, LANE))

    def search16(src, bits, tu, n_ge, base):
        for bit in bits:
            cand_u = tu | np.int32(2 ** bit)
            cand = half_cand(cand_u)
            tot = base + count16(src, lambda h, cand=cand: h >= cand)
            tu = jnp.where(tot >= topk, cand_u, tu)
            n_ge = jnp.where(tot >= topk, tot, n_ge)
        return tu, n_ge

    zero_i, zero_f = jnp.zeros((tq, 1), I32), jnp.zeros((tq, 1), F32)
    tu_hi, n_ge = search16(khi_scr, range(15, -1, -1), zero_i, zero_f, zero_f)
    t_hi = half_cand(tu_hi)
    n_above = count16(khi_scr, lambda h: h > t_hi)

    def keep_low(k, c):
        same_hi = khi_scr[k] == jnp.concatenate([t_hi] * (tk // LANE), axis=1)
        klo_scr[k] = jnp.where(same_hi, klo_scr[k], jnp.int16(-2 ** 15))
        return c

    lax.fori_loop(0, n_tiles, keep_low, 0)
    tu_lo, n_ge = search16(klo_scr, range(15, LOW_BITS - 1, -1), zero_i, n_ge, n_above)
    settled = jnp.min(jnp.where((tu_hi == 0) | (n_ge == topk), 1.0, 0.0)) > 0.5
    tu_lo, n_ge = lax.cond(settled, lambda: (tu_lo, n_ge),
                           lambda: search16(klo_scr, range(LOW_BITS - 1, -1, -1), tu_lo, n_ge, n_above))
    thr = ((tu_hi << 16) | tu_lo) ^ np.int32(KMIN)
    found = thr > np.int32(KMIN)
    has_ties = jnp.max(jnp.where(found & (n_ge > topk), 1.0, 0.0)) > 0.5

    qp_g = [_padded_heads(qa, g) for g in range(A_KV)]
    scr = (m_scr, acc_scr)
    in_values = bool(unroll)

    def attend(k, st, mask, far):
        kv = get_kva(k)
        bds = None if far else _tile_bds(qb, k, tk)
        madd = _mask_add(mask)
        for g in range(A_KV):
            st = _flash_update(st, scr, g, qp_g[g], kv, strips_ref, far_ref, bds, madd, tq, kv_t)
        return st

    @pl.when(jnp.logical_not(has_ties))
    def _():
        thr_ge = jnp.where(found, thr, np.int32(KMIN + 1))

        def tile(k, st, far):
            return attend(k, st, keys_scr[k] >= thr_ge, far)

        st = _flash_init(scr, tq, in_values)
        st = lax.fori_loop(0, n_far, functools.partial(tile, far=True), st, unroll=unroll)
        st = lax.fori_loop(n_far, n_tiles, functools.partial(tile, far=False), st, unroll=unroll)
        _flash_store(st, scr)

    @pl.when(has_ties)
    def _():
        def count_gt(k, cnt):
            key = keys_scr[k]
            for j in range(tk // LANE):
                cnt = cnt + jnp.where(key[:, j * LANE:(j + 1) * LANE] > thr, 1.0, 0.0)
            return cnt

        n_gt = jnp.sum(lax.fori_loop(0, n_tiles, count_gt, jnp.zeros((tq, LANE), F32)), axis=-1, keepdims=True)
        need = topk - n_gt
        tri_scr[...] = jnp.where(
            lax.broadcasted_iota(I32, (LANE, LANE), 0) <= lax.broadcasted_iota(I32, (LANE, LANE), 1),
            1.0, 0.0).astype(BF16)

        def tile(k, c):
            eq_seen, st = c
            key = keys_scr[k]
            eq = key == thr
            takes = []
            for j in range(tk // LANE):
                eq_j = eq[:, j * LANE:(j + 1) * LANE]
                rank = eq_seen + _mm(jnp.where(eq_j, 1.0, 0.0), tri_scr[...])
                takes.append(eq_j & (rank <= need))
                eq_seen = rank[:, LANE - 1:LANE]
            take = jnp.concatenate(takes, axis=1)
            st = attend(k, st, (key > thr) | (take & found), False)
            return eq_seen, st

        _, st = lax.fori_loop(0, n_tiles, tile, (jnp.zeros((tq, 1), F32), _flash_init(scr, tq, in_values)))
        _flash_store(st, scr)

    outs = []
    for g in range(A_KV):
        o = _flash_result(scr, g)
        outs += [o[r * tq:(r + 1) * tq] for r in range(GROUP)]
    return jnp.concatenate(outs, axis=1)


def _pick_blocks(v, n_pick, axis):
    lane = lax.broadcasted_iota(I32, v.shape, axis).astype(F32)
    sel = jnp.zeros(v.shape, jnp.bool_)
    for _ in range(n_pick):
        mx = jnp.max(v, axis=axis, keepdims=True)
        cand = (v == mx) & jnp.logical_not(sel)
        first = jnp.min(jnp.where(cand, lane, float(v.shape[axis])), axis=axis, keepdims=True)
        pick = lane == first
        sel = sel | pick
        v = jnp.where(pick, -jnp.inf, v)
    return sel


def _nsa_core(qb, gates, kcp, get_kvs, win_tiles, strips_ref, cstrips_ref, far_ref, t0, tq, tk, nj,
              n_pick, kv_t, unroll, m_scr, acc_scr):
    n_tiles = (t0 + tq - 1) // tk + 1
    qblk = t0 // LANE
    delta = _cmp_delta(tq)
    ncn = cstrips_ref.shape[-1]
    qp_g = [_padded_heads(qb, g) for g in range(B_KV)]
    row_pos1 = t0 + lax.broadcasted_iota(I32, (tq, 1), 0)

    lam = lax.broadcasted_iota(I32, (tq, 2 * nj), 1)
    m_of = 2 * (lam % nj) + lam // nj
    cdist = row_pos1 - (CMP_BLOCK * m_of + CMP_BLOCK - 1)
    cmask = cdist >= 0
    m_hi = (t0 - (CMP_BLOCK - 1) - delta) // CMP_BLOCK
    lam_s = lax.broadcasted_iota(I32, (ncn, 2 * nj), 1)
    shift = jnp.where(2 * (lam_s % nj) + lam_s // nj == m_hi - lax.broadcasted_iota(I32, (ncn, 2 * nj), 0),
                      1.0, 0.0).astype(BF16)
    is_far = m_of <= m_hi - ncn
    kc_both = kcp[:, :LANE]
    o_c, imp = [], []
    for g in range(B_KV):
        bias = []
        for r in range(GROUP):
            h = g * GROUP + r
            b = (_mm(cstrips_ref[h, 0], shift) + _mm(cstrips_ref[h, 1], shift)) + _mm(cstrips_ref[h, 2], shift)
            bias.append(jnp.where(is_far, far_ref[h], b))
        s = _mm_nt(qp_g[g], kc_both).reshape(GROUP, tq, 2 * nj) + jnp.stack(bias)
        s = jnp.where(cmask[None], s, NEG)
        e = jnp.where(cmask[None], jnp.exp(s - jnp.max(s, axis=-1, keepdims=True)), 0.0)
        l = jnp.sum(e, axis=-1, keepdims=True)
        p = jnp.where(l > 0.0, e / jnp.where(l > 0.0, l, 1.0), 0.0)
        o_c.append(_mm(p.reshape(GROUP * tq, 2 * nj), kcp[:, (B_KV + g) * HEAD_DIM:(B_KV + g + 1) * HEAD_DIM]))
        ps = p[0]
        for r in range(1, GROUP):
            ps = ps + p[r]
        imp.append(ps[:, :nj] + ps[:, nj:])

    jl = lax.broadcasted_iota(I32, (tq, nj), 1)
    cur = row_pos1 // SEL_BLOCK
    forced = (jl == 0) | (jl == cur) | (jl == cur - 1)
    admissible = jl <= cur
    vs = [jnp.where(admissible, jnp.where(forced, jnp.inf, imp[g]), -jnp.inf) for g in range(B_KV)]
    if tq % LANE == 0:
        picked_t = _pick_blocks(jnp.concatenate([jnp.transpose(v) for v in vs], axis=1), n_pick, 0)
        picked = [jnp.transpose(jnp.where(picked_t[:, g * tq:(g + 1) * tq], 1.0, 0.0)) > 0.5
                  for g in range(B_KV)]
    else:
        picked = [_pick_blocks(v, n_pick, 1) for v in vs]
    lhs = []
    for g in range(B_KV):
        bneg = jnp.where(picked[g] & admissible, 0.0, NEG).astype(BF16)
        lhs.append(jnp.concatenate([qp_g[g].astype(BF16), jnp.concatenate([bneg] * GROUP, axis=0)], axis=1))
    row_pos = t0 + lax.broadcasted_iota(I32, (tq, tk), 0)
    col = lax.broadcasted_iota(I32, (tq, tk), 1)
    ind_shape, j_axis = ((nj, tk), 0) if kv_t else ((tk, nj), 1)
    ej = lax.broadcasted_iota(I32, ind_shape, j_axis)
    ec = lax.broadcasted_iota(I32, ind_shape, 1 - j_axis)
    scr = (m_scr, acc_scr)
    in_values = bool(unroll)

    def sel_tile(k, st, far):
        kv = get_kvs(k)
        ind = jnp.where(ej == (k * tk + ec) // SEL_BLOCK, 1.0, 0.0).astype(BF16)
        rhs = jnp.concatenate([_k_part(kv, kv_t).astype(BF16), ind], axis=j_axis)
        if far:
            bds, madd = None, None
        else:
            bds, madd = _tile_bds(qblk, k, tk), _mask_add(k * tk + col <= row_pos)
        for g in range(B_KV):
            s_all = _mm(lhs[g], rhs) if kv_t else _mm_nt(lhs[g], rhs)
            st = _flash_update(st, scr, g, None, kv, strips_ref, far_ref, bds, madd, tq, kv_t, s_all)
        return st

    n_far = _far_tiles(qblk, tk)
    st = _flash_init(scr, tq, in_values)
    st = lax.fori_loop(0, n_far, functools.partial(sel_tile, far=True), st, unroll=unroll)
    st = lax.fori_loop(n_far, n_tiles, functools.partial(sel_tile, far=False), st, unroll=unroll)
    _flash_store(st, scr)
    o_s = [_flash_result(scr, g) for g in range(B_KV)]

    st = _flash_init(scr, tq, in_values)
    for get_tile, off, guard in win_tiles:
        def win_tile(st, get_tile=get_tile, off=off):
            kv = get_tile()
            wt = kv.shape[1] if kv_t else kv.shape[0]
            dist = (lax.broadcasted_iota(I32, (tq, wt), 0) - lax.broadcasted_iota(I32, (tq, wt), 1)
                    - LANE * off)
            wadd = _mask_add((dist >= 0) & (dist <= WINDOW))
            bds = [min(max(-off - j, 0), NEAR) for j in range(wt // LANE)]
            for g in range(B_KV):
                st = _flash_update(st, scr, g, qp_g[g], kv, strips_ref, far_ref, bds, wadd, tq, kv_t)
            return st

        if guard is None:
            st = win_tile(st)
        else:
            assert not in_values

            def guarded(win_tile=win_tile):
                win_tile(0)

            pl.when(guard)(guarded)
    _flash_store(st, scr)
    o_w = [_flash_result(scr, g) for g in range(B_KV)]

    gs = jax.nn.sigmoid(gates)
    outs = []
    for g in range(B_KV):
        gcol = lambda j: jnp.concatenate(
            [gs[:, (g * GROUP + r) * 3 + j:(g * GROUP + r) * 3 + j + 1] for r in range(GROUP)], axis=0)
        o = gcol(0) * o_c[g] + gcol(1) * o_s[g] + gcol(2) * o_w[g]
        outs += [o[r * tq:(r + 1) * tq] for r in range(GROUP)]
    return jnp.concatenate(outs, axis=1)


def _compress(load_rows, n_pairs, pe_ref, wbd_ref):
    acc = [jnp.zeros((n_pairs, KVW), F32), jnp.zeros((n_pairs, KVW), F32)]
    for c in range(2 * CMP_BLOCK):
        ci = c % CMP_BLOCK
        acc[c // CMP_BLOCK] = acc[c // CMP_BLOCK] + _mm(load_rows(c) + pe_ref[ci], wbd_ref[ci])
    return jnp.concatenate(acc, axis=0)


def _compress_operands(pe, w):
    pe_rows = jnp.transpose(jnp.broadcast_to(pe[:, None], (2, B_KV, CMP_BLOCK, HEAD_DIM)), (2, 0, 1, 3))
    pe_rows = pe_rows.reshape(CMP_BLOCK, 1, KVW)
    eye = jnp.eye(2 * B_KV, dtype=w.dtype).reshape(2, B_KV, 2, B_KV)
    wbd = jnp.einsum("icde,igjh->cigdjhe", w, eye).reshape(CMP_BLOCK, KVW, KVW)
    return pe_rows, wbd.astype(BF16)


def _strided_rows(lo_ref, hi_ref, c, n):
    rows = pl.ds(c, n, stride=2 * CMP_BLOCK)
    return jnp.concatenate([lo_ref[rows, :], hi_ref[rows, :]], axis=1)


def _flash_scratch(tq):
    return [pltpu.VMEM((A_KV, GROUP, tq, LANE), F32), pltpu.VMEM((A_KV, GROUP * tq, LANE), F32)]


TQ_P = 128
TK_P = 1024


def _prompt_tile(ref, k, tk):
    return ref[pl.ds(pl.multiple_of(k * tk, tk), tk), :]


def _pdsa_kernel(qi_ref, misc_ref, qa_ref, kidx_ref, kva_ref, strips_ref, far_ref, o_ref,
                 keys_scr, khi_scr, klo_scr, wib_scr, m_scr, acc_scr, tri_scr, *, topk, tk):
    i = pl.program_id(1)
    o = _dsa_core(qi_ref[...], misc_ref[:, M_WI:M_WI + IDX_HEADS], qa_ref[...],
                  lambda k: _prompt_tile(kidx_ref, k, tk), lambda k: _prompt_tile(kva_ref, k, tk),
                  strips_ref, far_ref, i * TQ_P, TQ_P, tk, topk, False, None,
                  keys_scr, khi_scr, klo_scr, wib_scr, m_scr, acc_scr, tri_scr)
    o_ref[...] = o.astype(o_ref.dtype)


def _pdsa_call(pr, strips, far, tk):
    nb, s, _ = pr["qa"].shape
    assert s < ZERO_KEY_BASE
    tq = TQ_P
    row = lambda w: pl.BlockSpec((None, tq, w), lambda b, i: (b, i, 0))
    full = lambda w: pl.BlockSpec((None, s, w), lambda b, i: (b, 0, 0))
    return pl.pallas_call(
        functools.partial(_pdsa_kernel, topk=min(IDX_TOPK, s // 4), tk=tk),
        grid=(nb, s // tq),
        in_specs=[row(256), row(128), row(512), full(128), full(KVX),
                  pl.BlockSpec(strips.shape, lambda b, i: (0, 0, 0, 0)),
                  pl.BlockSpec(memory_space=pltpu.SMEM)],
        out_specs=row(512),
        out_shape=jax.ShapeDtypeStruct((nb, s, A_HEADS * HEAD_DIM), BF16),
        scratch_shapes=[pltpu.VMEM((s // tk, tq, tk), I32), pltpu.VMEM((s // tk, tq, tk), I16),
                        pltpu.VMEM((s // tk, tq, tk), I16), pltpu.VMEM((IDX_HEADS, tq, tk), F32)]
        + _flash_scratch(tq) + [pltpu.VMEM((LANE, LANE), BF16)],
        compiler_params=_cparams(("arbitrary", "arbitrary")),
        name="dsa_prompt",
    )(pr["qi"], pr["misc"], pr["qa"], pr["misc_h"], pr["kva_x"], strips, far)


def _pcmp_kernel(lo_ref, hi_ref, pe_ref, wbd_ref, o_ref, *, n_pairs):
    o_ref[...] = _compress(lambda c: _strided_rows(lo_ref, hi_ref, c, n_pairs), n_pairs, pe_ref, wbd_ref)


def _pcmp_call(kvc, pe_rows, wbd):
    nb, s, _ = kvc.shape
    n_pairs = s // (2 * CMP_BLOCK)
    return pl.pallas_call(
        functools.partial(_pcmp_kernel, n_pairs=n_pairs),
        grid=(nb,),
        in_specs=[pl.BlockSpec((None, s, LANE), lambda b: (b, 0, 0)),
                  pl.BlockSpec((None, s, LANE), lambda b: (b, 0, 1)),
                  pl.BlockSpec((CMP_BLOCK, 1, KVW), lambda b: (0, 0, 0)),
                  pl.BlockSpec((CMP_BLOCK, KVW, KVW), lambda b: (0, 0, 0))],
        out_specs=pl.BlockSpec((None, 2 * n_pairs, KVW), lambda b: (b, 0, 0)),
        out_shape=jax.ShapeDtypeStruct((nb, 2 * n_pairs, KVW), F32),
        compiler_params=_cparams(("arbitrary",)),
        name="compress_prompt",
    )(kvc, kvc, pe_rows, wbd)


def _pnsa_kernel(qb_ref, misc_ref, kcp_ref, kvs_ref, kvw_ref, strips_ref, cstrips_ref, far_ref, o_ref,
                 m_scr, acc_scr, *, nj, n_pick, tk):
    i = pl.program_id(1)
    nw = WINDOW // LANE
    win_tiles = [(lambda: kvw_ref[pl.ds(pl.multiple_of((i - nw) * LANE, LANE), WINDOW + TQ_P), :], -nw, i >= nw)]
    win_tiles += [(lambda wd=wd: _prompt_tile(kvw_ref, i - wd, LANE), -wd, (i < nw) & (i - wd >= 0))
                  for wd in range(nw - 1, -1, -1)]
    o = _nsa_core(qb_ref[...], misc_ref[:, M_GATE:M_GATE + 3 * B_HEADS], kcp_ref[...],
                  lambda k: _prompt_tile(kvs_ref, k, tk), win_tiles,
                  strips_ref, cstrips_ref, far_ref, i * TQ_P, TQ_P, tk, nj, n_pick, False, None,
                  m_scr, acc_scr)
    o_ref[...] = o.astype(o_ref.dtype)


def _pnsa_call(pr, kcp, strips, cstrips, far, tk):
    nb, s, _ = pr["qb"].shape
    tq = TQ_P
    nj = s // SEL_BLOCK
    row = lambda w: pl.BlockSpec((None, tq, w), lambda b, i: (b, i, 0))
    full = lambda n, w: pl.BlockSpec((None, n, w), lambda b, i: (b, 0, 0))
    const = lambda a: pl.BlockSpec(a.shape, lambda b, i: (0,) * a.ndim)
    return pl.pallas_call(
        functools.partial(_pnsa_kernel, nj=nj, n_pick=min(SEL_TOPN, nj), tk=tk),
        grid=(nb, s // tq),
        in_specs=[row(512), row(128), full(2 * nj, KVW), full(s, KVX), full(s, KVX),
                  const(strips), const(cstrips),
                  pl.BlockSpec(memory_space=pltpu.SMEM)],
        out_specs=row(512),
        out_shape=jax.ShapeDtypeStruct((nb, s, B_HEADS * HEAD_DIM), BF16),
        scratch_shapes=_flash_scratch(tq),
        compiler_params=_cparams(("arbitrary", "arbitrary")),
        name="nsa_prompt",
    )(pr["qb"], pr["misc"], kcp, pr["kvs_x"], pr["kvw_x"], strips, cstrips, far)


TQ_S = 8
TK_S = 2048
PPT = TK_S // PAGE


def _pages_per_step(n_pages, cap):
    return max(g for g in (64, 32, 16, 8, 4, 2, 1) if g <= cap and n_pages % g == 0)


def _new_token_block(col):
    return jnp.where(lax.broadcasted_iota(I32, (col.shape[0], PAGE), 1) == 0, col, 0.0)


def _slot_tile(buf, k):
    blk = buf[pl.ds(k * PPT, PPT)]
    return jnp.concatenate([blk[j] for j in range(PPT)], axis=1)


def _sdsa_kernel(*refs, n_pages, pps, topk):
    idx_refs = refs[1:1 + pps]
    akv_refs = refs[1 + pps:1 + 2 * pps]
    (qi_ref, wi_ref, qa_ref, new_idx_ref, new_kv_ref, strips_ref, far_ref, o_ref,
     idx_buf, akv_buf, keys_scr, khi_scr, klo_scr, wib_scr, m_scr, acc_scr, tri_scr) = refs[1 + 2 * pps:]
    p = pl.program_id(1)
    for j in range(pps):
        idx_buf[p * pps + j] = idx_refs[j][...].astype(BF16)
        akv_buf[p * pps + j] = _kv_ext_t(akv_refs[j][...])

    @pl.when(p == n_pages // pps - 1)
    def _():
        idx_buf[n_pages] = _new_token_block(new_idx_ref[...]).astype(BF16)
        akv_buf[n_pages] = _kv_ext_t(_new_token_block(new_kv_ref[...]))
        for j in range(1, PPT):
            idx_buf[n_pages + j] = jnp.zeros((IDX_DIM, PAGE), BF16)
            akv_buf[n_pages + j] = _kv_ext_t(jnp.zeros((KVW, PAGE), BF16))
        o = _dsa_core(qi_ref[...], wi_ref[...], qa_ref[...],
                      lambda k: _slot_tile(idx_buf, k), lambda k: _slot_tile(akv_buf, k),
                      strips_ref, far_ref, n_pages * PAGE, TQ_S, TK_S, topk, True, True,
                      keys_scr, khi_scr, klo_scr, wib_scr, m_scr, acc_scr, tri_scr)
        o_ref[...] = o.astype(o_ref.dtype)


def _sample_rows(x):
    return jnp.broadcast_to(x[:, None, :], (x.shape[0], TQ_S, x.shape[1]))


def _pages_t(cache):
    n, p = cache.shape[:2]
    nd = cache.ndim
    return jnp.transpose(cache, (0, 1) + tuple(range(3, nd)) + (2,)).reshape(n, p, -1, cache.shape[2])


def _sdsa_call(a, page_table, idx_t, akv_t, pr, strips, far):
    bd, n_pages = page_table.shape
    assert idx_t.shape[3] == PAGE and (n_pages * PAGE) % TK_S == 0 and n_pages * PAGE + TK_S < ZERO_KEY_BASE
    pps = _pages_per_step(n_pages, 64)
    slots = n_pages + PPT
    misc = pr["misc"][0]
    rows = lambda w: pl.BlockSpec((None, TQ_S, w), lambda b, p, pt: (b, 0, 0))
    col = lambda w: pl.BlockSpec((None, w, 1), lambda b, p, pt: (b, 0, 0))
    page = lambda w, j: pl.BlockSpec((None, None, w, PAGE),
                                     lambda b, p, pt: (a, pt[b * n_pages + p * pps + j], 0, 0))
    grid_spec = pltpu.PrefetchScalarGridSpec(
        num_scalar_prefetch=1,
        grid=(bd, n_pages // pps),
        in_specs=[page(IDX_DIM, j) for j in range(pps)] + [page(KVW, j) for j in range(pps)]
        + [rows(256), rows(IDX_HEADS), rows(512), col(IDX_DIM), col(KVW),
           pl.BlockSpec(strips.shape, lambda b, p, pt: (0, 0, 0, 0)),
           pl.BlockSpec(memory_space=pltpu.SMEM)],
        out_specs=rows(512),
        scratch_shapes=[pltpu.VMEM((slots, IDX_DIM, PAGE), BF16), pltpu.VMEM((slots, KVX, PAGE), BF16),
                        pltpu.VMEM((slots // PPT, TQ_S, TK_S), I32), pltpu.VMEM((slots // PPT, TQ_S, TK_S), I16),
                        pltpu.VMEM((slots // PPT, TQ_S, TK_S), I16), pltpu.VMEM((IDX_HEADS, TQ_S, TK_S), F32)]
        + _flash_scratch(TQ_S) + [pltpu.VMEM((LANE, LANE), BF16)])
    out = pl.pallas_call(
        functools.partial(_sdsa_kernel, n_pages=n_pages, pps=pps, topk=min(IDX_TOPK, (n_pages * PAGE + 1) // 4)),
        grid_spec=grid_spec,
        out_shape=jax.ShapeDtypeStruct((bd, TQ_S, A_HEADS * HEAD_DIM), BF16),
        compiler_params=_cparams(("arbitrary", "arbitrary")),
        name="dsa_sample",
    )(page_table.reshape(-1), *([idx_t] * pps), *([akv_t] * pps),
      _sample_rows(pr["qi"][0]), _sample_rows(misc[:, M_WI:M_WI + IDX_HEADS]), _sample_rows(pr["qa"][0]),
      misc[:, M_KI:M_KI + IDX_DIM, None], pr["kva"][0][:, :, None], strips, far)
    return out[:, 0]


def _snsa_kernel(*refs, n_pages, pps):
    cmp_refs = refs[1:1 + pps]
    sel_refs = refs[1 + pps:1 + 2 * pps]
    (qb_ref, gates_ref, new_sel_ref, win_ref, new_win_ref, pe_ref, wbd_ref, strips_ref, cstrips_ref, far_ref,
     o_ref, cmp_lo, cmp_hi, sel_buf, m_scr, acc_scr) = refs[1 + 2 * pps:]
    p = pl.program_id(1)
    past = n_pages * PAGE
    for j in range(pps):
        rows = pl.ds(pl.multiple_of((p * pps + j) * PAGE, PAGE), PAGE)
        cmp_lo[rows, :] = jnp.transpose(cmp_refs[j][:LANE])
        cmp_hi[rows, :] = jnp.transpose(cmp_refs[j][LANE:])
        sel_buf[p * pps + j] = _kv_ext_t(sel_refs[j][...])

    @pl.when(p == n_pages // pps - 1)
    def _():
        sel_buf[n_pages] = _kv_ext_t(_new_token_block(new_sel_ref[...]))
        for j in range(1, PPT):
            sel_buf[n_pages + j] = _kv_ext_t(jnp.zeros((KVW, PAGE), BF16))
        nj = past // SEL_BLOCK
        kcp = _compress(lambda c: _strided_rows(cmp_lo, cmp_hi, c, nj), nj, pe_ref, wbd_ref)
        win_tiles = [(lambda: _kv_ext_t(win_ref[...]), -(WINDOW // LANE), None),
                     (lambda: _kv_ext_t(_new_token_block(new_win_ref[...])), 0, None)]
        o = _nsa_core(qb_ref[...], gates_ref[...], kcp, lambda k: _slot_tile(sel_buf, k), win_tiles,
                      strips_ref, cstrips_ref, far_ref, past, TQ_S, TK_S, nj,
                      min(SEL_TOPN, nj + 1) - 1, True, True, m_scr, acc_scr)
        o_ref[...] = o.astype(o_ref.dtype)


def _snsa_call(a, page_table, cmp_t, sel_t, win_t, pr, pe_rows, wbd, strips, cstrips, far):
    bd, n_pages = page_table.shape
    past = n_pages * PAGE
    assert past % TK_S == 0 and win_t.shape[3] == WINDOW
    pps = _pages_per_step(n_pages, 16)
    misc = pr["misc"][0]
    rows = lambda w: pl.BlockSpec((None, TQ_S, w), lambda b, p, pt: (b, 0, 0))
    col = lambda w: pl.BlockSpec((None, w, 1), lambda b, p, pt: (b, 0, 0))
    page = lambda j: pl.BlockSpec((None, None, KVW, PAGE),
                                  lambda b, p, pt: (a, pt[b * n_pages + p * pps + j], 0, 0))
    const = lambda x: pl.BlockSpec(x.shape, lambda b, p, pt: (0,) * x.ndim)
    grid_spec = pltpu.PrefetchScalarGridSpec(
        num_scalar_prefetch=1,
        grid=(bd, n_pages // pps),
        in_specs=[page(j) for j in range(pps)] + [page(j) for j in range(pps)]
        + [rows(512), rows(3 * B_HEADS), col(KVW),
           pl.BlockSpec((None, None, KVW, WINDOW), lambda b, p, pt: (a, b, 0, 0)), col(KVW),
           const(pe_rows), const(wbd), const(strips), const(cstrips),
           pl.BlockSpec(memory_space=pltpu.SMEM)],
        out_specs=rows(512),
        scratch_shapes=[pltpu.VMEM((past, LANE), F32), pltpu.VMEM((past, LANE), F32),
                        pltpu.VMEM((n_pages + PPT, KVX, PAGE), BF16)] + _flash_scratch(TQ_S))
    out = pl.pallas_call(
        functools.partial(_snsa_kernel, n_pages=n_pages, pps=pps),
        grid_spec=grid_spec,
        out_shape=jax.ShapeDtypeStruct((bd, TQ_S, B_HEADS * HEAD_DIM), BF16),
        compiler_params=_cparams(("arbitrary", "arbitrary")),
        name="nsa_sample",
    )(page_table.reshape(-1), *([cmp_t] * pps), *([sel_t] * pps),
      _sample_rows(pr["qb"][0]), _sample_rows(misc[:, M_GATE:M_GATE + 3 * B_HEADS]),
      pr["kvs"][0][:, :, None], win_t, pr["kvw"][0][:, :, None],
      pe_rows, wbd, strips, cstrips, far)
    return out[:, 0]


def _merge_kernel(x_ref, mod_ref, oa_ref, ob_ref, w_ref, o_ref):
    n = oa_ref.shape[-1]
    y = _mm(oa_ref[...], w_ref[:n]) + _mm(ob_ref[...], w_ref[n:])
    o_ref[...] = x_ref[...] + (1.0 + mod_ref[2]) * y


def _merge_call(x, mod, oa, ob, w, tm):
    nb, t, d = x.shape
    tmod = mod.shape[2]
    mod_spec = (pl.BlockSpec((None, 3, 1, d), lambda b, i: (b, 0, 0, 0)) if tmod == 1 else
                pl.BlockSpec((None, 3, tm, d), lambda b, i: (b, 0, i, 0)))
    row = lambda w_: pl.BlockSpec((None, tm, w_), lambda b, i: (b, i, 0))
    return pl.pallas_call(
        _merge_kernel,
        grid=(nb, t // tm),
        in_specs=[row(d), mod_spec, row(oa.shape[-1]), row(ob.shape[-1]),
                  pl.BlockSpec(w.shape, lambda b, i: (0, 0))],
        out_specs=row(d),
        out_shape=jax.ShapeDtypeStruct(x.shape, F32),
        compiler_params=_cparams(("arbitrary", "arbitrary")),
        name="merge",
    )(x, mod, oa, ob, w)


def _gelu_ln(h, w_in_ref, ln_ref, cw):
    uv = jax.nn.gelu(_mm(h, w_in_ref[...]))
    u, v = uv[:, :cw], uv[:, cw:]
    mu = jnp.mean(v, axis=-1, keepdims=True)
    var = jnp.mean(jnp.square(v - mu), axis=-1, keepdims=True)
    return u, (v - mu) * lax.rsqrt(var + EPS) * ln_ref[...]


def _pgmlp_kernel(x_ref, mod_ref, g_ref, w_in_ref, ln_ref, ws_ref, bs_ref, w_out_ref, o_ref, *, cw):
    x = x_ref[...]
    h = _modulated_norm(x, g_ref[...], mod_ref).astype(BF16)
    u, v = _gelu_ln(h, w_in_ref, ln_ref, cw)
    gw = cw // C_GROUPS
    rows = []
    for n in range(x.shape[0] // C_CHUNK):
        vb = v[n * C_CHUNK:(n + 1) * C_CHUNK].astype(BF16)
        sv = [_mm(ws_ref[g], vb[:, g * gw:(g + 1) * gw]) + bs_ref[:, g:g + 1] for g in range(C_GROUPS)]
        rows.append(jnp.concatenate(sv, axis=1))
    sv = rows[0] if len(rows) == 1 else jnp.concatenate(rows, axis=0)
    o_ref[...] = x + (1.0 + mod_ref[2]) * _mm(u * sv, w_out_ref[...])


def _pgmlp_call(x, mod, g, w_in, ln_g, ws, bs_t, w_out, tm):
    nb, t, d = x.shape
    cw = w_out.shape[0]
    const = lambda a: pl.BlockSpec(a.shape, lambda b, i: (0,) * a.ndim)
    return pl.pallas_call(
        functools.partial(_pgmlp_kernel, cw=cw),
        grid=(nb, t // tm),
        in_specs=[pl.BlockSpec((None, tm, d), lambda b, i: (b, i, 0)),
                  pl.BlockSpec((None, 3, 1, d), lambda b, i: (b, 0, 0, 0)),
                  pl.BlockSpec((1, d), lambda b, i: (0, 0)),
                  const(w_in), pl.BlockSpec((1, cw), lambda b, i: (0, 0)), const(ws), const(bs_t), const(w_out)],
        out_specs=pl.BlockSpec((None, tm, d), lambda b, i: (b, i, 0)),
        out_shape=jax.ShapeDtypeStruct(x.shape, F32),
        compiler_params=_cparams(("arbitrary", "arbitrary")),
        name="gmlp_prompt",
    )(x, mod, g.reshape(1, d), w_in, ln_g.reshape(1, cw), ws, bs_t, w_out)


def _sgmlp_kernel(x_ref, mod_ref, g_ref, w_in_ref, ln_ref, ws0_ref, bs0_ref, w_out_ref, o_ref, v_ref, *, cw):
    x = x_ref[...]
    h = _modulated_norm(x, g_ref[...], mod_ref).astype(BF16)
    u, v = _gelu_ln(h, w_in_ref, ln_ref, cw)
    v_ref[...] = v
    sv = v.astype(BF16).astype(F32) * ws0_ref[...].astype(F32) + bs0_ref[...]
    o_ref[...] = x + (1.0 + mod_ref[2]) * _mm(u * sv, w_out_ref[...])


def _sgmlp_call(x, mod, g, w_in, ln_g, ws, b_sp, w_out):
    nb, t, d = x.shape
    cw = w_out.shape[0]
    gw = cw // C_GROUPS
    ws0 = jnp.repeat(ws[:, 0, 0], gw).reshape(1, cw)
    bs0 = jnp.repeat(b_sp[:, 0], gw).reshape(1, cw)
    full = lambda a: pl.BlockSpec(a.shape, lambda: (0,) * a.ndim)
    x2, mod2 = x[0], mod[0]
    o, v = pl.pallas_call(
        functools.partial(_sgmlp_kernel, cw=cw),
        in_specs=[full(x2), full(mod2), pl.BlockSpec((1, d), lambda: (0, 0)), full(w_in),
                  pl.BlockSpec((1, cw), lambda: (0, 0)), full(ws0), full(bs0), full(w_out)],
        out_specs=[full(x2), pl.BlockSpec((t, cw), lambda: (0, 0))],
        out_shape=[jax.ShapeDtypeStruct(x2.shape, F32), jax.ShapeDtypeStruct((t, cw), F32)],
        compiler_params=pltpu.CompilerParams(vmem_limit_bytes=VMEM_LIMIT),
        name="gmlp_sample",
    )(x2, mod2, g.reshape(1, d), w_in, ln_g.reshape(1, cw), ws0, bs0, w_out)
    return o[None], v


def kernel(x_prompt, x_sample, cache_a_kv, cache_a_idx, cache_b_cmp_kv, cache_b_sel_kv, state_b_win_kv,
           page_table, c_prompt, c_sample, rel_bias, w_ada, b_ada, norm_g, w_ffn_in, w_ffn_out,
           w_in_att, w_out_att, cmp_pe, w_cmp, w_in_c, ln_c_g, w_sp, b_sp, w_out_c, final_g):
    nbp, s, d = x_prompt.shape
    nbs, tdec, _ = x_sample.shape
    depth = w_ada.shape[0]
    assert tdec == 1 and s % TQ_P == 0 and cache_a_kv.shape[2] == PAGE

    ada = _ada_call(jnp.concatenate([c_prompt, c_sample], axis=0), w_ada, b_ada)
    ada = ada.reshape(depth, nbp + nbs, 3, 3, d)
    mod_p = lambda l, j: ada[l, :nbp, j][:, :, None, :]
    mod_s = lambda l, j: jnp.transpose(ada[l, nbp:, j], (1, 0, 2))[None]

    w_ffn_in_h = w_ffn_in.astype(BF16)
    w_ffn_out_h = w_ffn_out.astype(BF16)
    tab_a, tab_b = rel_bias[:, :A_HEADS], rel_bias[:, A_HEADS:]
    strips_pa, strips_pb = _bias_strips(tab_a, TQ_P), _bias_strips(tab_b, TQ_P)
    strips_sa, strips_sb = _bias_strips(tab_a, TQ_S), _bias_strips(tab_b, TQ_S)
    cstrips_p, cstrips_s = _cmp_strips(tab_b, TQ_P), _cmp_strips(tab_b, TQ_S)
    far_a, far_b = tab_a[N_BUCKETS - 1], tab_b[N_BUCKETS - 1]
    idx_t, akv_t = _pages_t(cache_a_idx), _pages_t(cache_a_kv)
    cmp_t, sel_t, win_t = _pages_t(cache_b_cmp_kv), _pages_t(cache_b_sel_kv), _pages_t(state_b_win_kv)

    xp = x_prompt
    xs = jnp.transpose(x_sample, (1, 0, 2))
    tm_p = 512 if s % 512 == 0 else TQ_P
    tk_p = TK_P if s % TK_P == 0 else TQ_P
    st_p, st_s, st_c = [], [], []
    for l in range(depth):
        last = l == depth - 1
        xp = _ffn_call(xp, mod_p(l, 0), norm_g[l, 0], w_ffn_in_h, w_ffn_out_h, l, 0, None, tm_p)
        xs = _ffn_call(xs, mod_s(l, 0), norm_g[l, 0], w_ffn_in_h, w_ffn_out_h, l, 0, None, nbs)
        if l % 2 == 0:
            a = l // 2
            wp = _rearranged_proj_weight(w_in_att[a])
            w_out = w_out_att[a].astype(BF16)
            pe_rows, wbd = _compress_operands(cmp_pe[a], w_cmp[a])

            pr = _proj_call(xp, mod_p(l, 1), norm_g[l, 1], wp, tm_p)
            oa = _pdsa_call(pr, strips_pa, far_a, tk_p)
            kcp = _pcmp_call(pr["kvc"], pe_rows, wbd)
            ob = _pnsa_call(pr, kcp, strips_pb, cstrips_p, far_b, tk_p)
            xp = _merge_call(xp, mod_p(l, 1), oa, ob, w_out, tm_p)
            kv5 = lambda t: t.reshape(t.shape[0], t.shape[1], 2, A_KV, HEAD_DIM)
            st_p.append((kv5(pr["kva"]), pr["misc"][:, :, M_KI:M_KI + IDX_DIM], kv5(pr["kvc"]),
                         kv5(pr["kvs"]), kv5(pr["kvw"][:, s - min(WINDOW, s):])))

            ps = _proj_call(xs, mod_s(l, 1), norm_g[l, 1], wp, nbs)
            oa = _sdsa_call(a, page_table, idx_t, akv_t, ps, strips_sa, far_a)
            ob = _snsa_call(a, page_table, cmp_t, sel_t, win_t, ps, pe_rows, wbd, strips_sb, cstrips_s, far_b)
            xs = _merge_call(xs, mod_s(l, 1), oa[None], ob[None], w_out, nbs)
            tok = lambda t: t[0].reshape(nbs, 1, 2, A_KV, HEAD_DIM)
            st_s.append((tok(ps["kva"]), ps["misc"][0][:, None, M_KI:M_KI + IDX_DIM], tok(ps["kvc"]),
                         tok(ps["kvs"]),
                         jnp.concatenate([state_b_win_kv[a][:, tdec:], tok(ps["kvw"])], axis=1)))
        else:
            ci = l // 2
            w_in = w_in_c[ci].astype(BF16)
            w_out = w_out_c[ci].astype(BF16)
            ws = (w_sp[ci] * jnp.tril(jnp.ones((C_CHUNK, C_CHUNK), w_sp.dtype))).astype(BF16)
            xp = _pgmlp_call(xp, mod_p(l, 1), norm_g[l, 1], w_in, ln_c_g[ci], ws,
                             jnp.transpose(b_sp[ci]), w_out, 256 if s % 256 == 0 else C_CHUNK)
            xs, v = _sgmlp_call(xs, mod_s(l, 1), norm_g[l, 1], w_in, ln_c_g[ci], ws, b_sp[ci], w_out)
            st_c.append(v[:, None, :])
        fg = final_g if last else None
        xp = _ffn_call(xp, mod_p(l, 2), norm_g[l, 2], w_ffn_in_h, w_ffn_out_h, l, 1, fg, tm_p)
        xs = _ffn_call(xs, mod_s(l, 2), norm_g[l, 2], w_ffn_in_h, w_ffn_out_h, l, 1, fg, nbs)

    stk = lambda sts, i: jnp.stack([st[i] for st in sts])
    return (xp, jnp.transpose(xs, (1, 0, 2)),
            stk(st_p, 0), stk(st_p, 1), stk(st_p, 2), stk(st_p, 3), stk(st_p, 4),
            stk(st_s, 0), stk(st_s, 1), stk(st_s, 2), stk(st_s, 3), stk(st_s, 4),
            jnp.stack(st_c))
```

```python
import functools
import math

import numpy as np
import jax
import jax.numpy as jnp
from jax import lax
from jax.experimental import pallas as pl
from jax.experimental.pallas import tpu as pltpu

F32 = jnp.float32
BF16 = jnp.bfloat16
I32 = jnp.int32

HEAD_DIM = 64
A_HEADS = 8
A_KV = 2
IDX_HEADS = 4
IDX_DIM = 64
IDX_TOPK = 256
B_HEADS = 8
B_KV = 2
CMP_BLOCK = 32
SEL_BLOCK = 64
SEL_TOPN = 16
WINDOW = 512
N_BUCKETS = 32
MAX_DISTANCE = 1024
C_CHUNK = 128
C_GROUPS = 8
EPS = 1e-6
PAGE = 128
GROUP = A_HEADS // A_KV
KVW = 2 * A_KV * HEAD_DIM

LANE = 128
VMEM_LIMIT = 56 * 1024 * 1024
KVX = 3 * LANE

NEG = -1e30
KMIN = -2 ** 31

P_QA, P_QB, P_KVA, P_QI, P_KVC, P_KVS, P_KVW, P_MISC = 0, 512, 1024, 1280, 1536, 1792, 2048, 2304
P_WIDTH = 2432
M_KI, M_WI, M_GATE = 0, IDX_DIM, IDX_DIM + IDX_HEADS

assert A_KV == 2 and B_KV == 2 and A_KV * HEAD_DIM == LANE and IDX_DIM == HEAD_DIM


def _cparams(sem):
    return pltpu.CompilerParams(dimension_semantics=sem, vmem_limit_bytes=VMEM_LIMIT)


def _mm(a, b):
    return jnp.dot(a.astype(BF16), b.astype(BF16), preferred_element_type=F32)


def _mm_nt(a, b):
    return lax.dot_general(a.astype(BF16), b.astype(BF16), (((1,), (1,)), ((), ())),
                           preferred_element_type=F32)


def _modulated_norm(x, g, mod_ref):
    y = x * lax.rsqrt(jnp.mean(x * x, axis=-1, keepdims=True) + EPS) * g
    return y * (1.0 + mod_ref[1]) + mod_ref[0]


def _ada_kernel(c_ref, w_ref, b_ref, o_ref):
    c = c_ref[...]
    o_ref[...] = _mm(c * jax.nn.sigmoid(c), w_ref[...]) + b_ref[...]


def _ada_call(c_all, w_ada, b_ada):
    depth, d, n = w_ada.shape
    r = c_all.shape[0]
    tn = n // 8
    return pl.pallas_call(
        _ada_kernel,
        grid=(depth, n // tn),
        in_specs=[pl.BlockSpec((r, d), lambda l, j: (0, 0)),
                  pl.BlockSpec((None, d, tn), lambda l, j: (l, 0, j)),
                  pl.BlockSpec((None, 1, tn), lambda l, j: (l, 0, j))],
        out_specs=pl.BlockSpec((None, r, tn), lambda l, j: (l, 0, j)),
        out_shape=jax.ShapeDtypeStruct((depth, r, n), F32),
        compiler_params=_cparams(("arbitrary", "arbitrary")),
        name="ada",
    )(c_all, w_ada, b_ada.reshape(depth, 1, n))


def _ffn_kernel(x_ref, mod_ref, g_ref, wa_ref, wb_ref, wo_ref, fg_ref, o_ref, h_scr, acc_scr, *,
                n_f, final_norm):
    f = pl.program_id(2)

    @pl.when(f == 0)
    def _():
        h_scr[...] = _modulated_norm(x_ref[...], g_ref[...], mod_ref).astype(BF16)
        acc_scr[...] = jnp.zeros_like(acc_scr)

    h = h_scr[...]
    a = _mm(h, wa_ref[...])
    b = _mm(h, wb_ref[...])
    acc_scr[...] += _mm(a * jax.nn.sigmoid(a) * b, wo_ref[...])

    @pl.when(f == n_f - 1)
    def _():
        y = x_ref[...] + 0.5 * (1.0 + mod_ref[2]) * acc_scr[...]
        if final_norm:
            y = y * lax.rsqrt(jnp.mean(y * y, axis=-1, keepdims=True) + EPS) * fg_ref[...]
        o_ref[...] = y


def _ffn_call(x, mod, g, w_in, w_out, l, j, final_g, tm):
    nb, t, d = x.shape
    dff = w_out.shape[2]
    tf = dff // 2 if dff % (2 * LANE) == 0 else LANE
    n_f = dff // tf
    tmod = mod.shape[2]
    mod_spec = (pl.BlockSpec((None, 3, 1, d), lambda b, i, f: (b, 0, 0, 0)) if tmod == 1 else
                pl.BlockSpec((None, 3, tm, d), lambda b, i, f: (b, 0, i, 0)))
    final_norm = final_g is not None
    fg = (final_g if final_norm else g).reshape(1, d)
    return pl.pallas_call(
        functools.partial(_ffn_kernel, n_f=n_f, final_norm=final_norm),
        grid=(nb, t // tm, n_f),
        in_specs=[pl.BlockSpec((None, tm, d), lambda b, i, f: (b, i, 0)),
                  mod_spec,
                  pl.BlockSpec((1, d), lambda b, i, f: (0, 0)),
                  pl.BlockSpec((None, None, d, tf), lambda b, i, f: (l, j, 0, f)),
                  pl.BlockSpec((None, None, d, tf), lambda b, i, f: (l, j, 0, f + n_f)),
                  pl.BlockSpec((None, None, tf, d), lambda b, i, f: (l, j, f, 0)),
                  pl.BlockSpec((1, d), lambda b, i, f: (0, 0))],
        out_specs=pl.BlockSpec((None, tm, d), lambda b, i, f: (b, i, 0)),
        out_shape=jax.ShapeDtypeStruct(x.shape, F32),
        scratch_shapes=[pltpu.VMEM((tm, d), BF16), pltpu.VMEM((tm, d), F32)],
        compiler_params=_cparams(("arbitrary", "arbitrary", "arbitrary")),
        name="ffn",
    )(x, mod, g.reshape(1, d), w_in, w_in, w_out, fg)


def _kv_ext(y):
    e = jnp.where(lax.broadcasted_iota(I32, (y.shape[0], HEAD_DIM), 1) == 0, 1.0, 0.0).astype(BF16)
    y = y.astype(BF16)
    return jnp.concatenate([y[:, :LANE], y[:, LANE:LANE + HEAD_DIM], e, y[:, LANE + HEAD_DIM:], e], axis=1)


def _kv_ext_t(y):
    e = jnp.where(lax.broadcasted_iota(I32, (HEAD_DIM, y.shape[1]), 0) == 0, 1.0, 0.0).astype(BF16)
    y = y.astype(BF16)
    return jnp.concatenate([y[:LANE], y[LANE:LANE + HEAD_DIM], e, y[LANE + HEAD_DIM:], e], axis=0)


_PROJ_OUT = (("qa", P_QA, 512, "h"), ("qb", P_QB, 512, "h"), ("qi", P_QI, 256, "h"),
             ("kva", P_KVA, 256, "f"), ("kvc", P_KVC, 256, "f"), ("kvs", P_KVS, 256, "f"),
             ("kvw", P_KVW, 256, "f"), ("misc", P_MISC, 128, "f"),
             ("kva_x", P_KVA, KVX, "x"), ("kvs_x", P_KVS, KVX, "x"), ("kvw_x", P_KVW, KVX, "x"),
             ("misc_h", P_MISC, 128, "h"))


def _proj_kernel(x_ref, mod_ref, g_ref, w_ref, *o_refs):
    h = _modulated_norm(x_ref[...], g_ref[...], mod_ref).astype(BF16)
    done = {}
    for (name, off, width, kind), o_ref in zip(_PROJ_OUT, o_refs):
        if off not in done:
            y = _mm(h, w_ref[:, off:off + (KVW if kind == "x" else width)])
            if name in ("qa", "qb"):
                y = y * (HEAD_DIM ** -0.5)
            done[off] = y
        o_ref[...] = _kv_ext(done[off]) if kind == "x" else done[off].astype(o_ref.dtype)


def _proj_call(x, mod, g, wp, tm):
    nb, t, d = x.shape
    tmod = mod.shape[2]
    mod_spec = (pl.BlockSpec((None, 3, 1, d), lambda b, i: (b, 0, 0, 0)) if tmod == 1 else
                pl.BlockSpec((None, 3, tm, d), lambda b, i: (b, 0, i, 0)))
    outs = pl.pallas_call(
        _proj_kernel,
        grid=(nb, t // tm),
        in_specs=[pl.BlockSpec((None, tm, d), lambda b, i: (b, i, 0)),
                  mod_spec,
                  pl.BlockSpec((1, d), lambda b, i: (0, 0)),
                  pl.BlockSpec((d, P_WIDTH), lambda b, i: (0, 0))],
        out_specs=[pl.BlockSpec((None, tm, w), lambda b, i: (b, i, 0)) for _, _, w, _ in _PROJ_OUT],
        out_shape=[jax.ShapeDtypeStruct((nb, t, w), F32 if kind == "f" else BF16)
                   for _, _, w, kind in _PROJ_OUT],
        compiler_params=_cparams(("arbitrary", "arbitrary")),
        name="proj",
    )(x, mod, g.reshape(1, d), wp)
    return {name: o for (name, _, _, _), o in zip(_PROJ_OUT, outs)}


def _rearranged_proj_weight(w):
    d = w.shape[0]
    o = np.cumsum((0, 512, 256, 256, 64, 4, 512, 256, 256, 256, 24))
    seg = lambda i: w[:, o[i]:o[i + 1]]
    qa, kva, qi, ki, wi, qb, kvc, kvs, kvw, gates = [seg(i) for i in range(10)]
    pad = jnp.zeros((d, LANE - IDX_DIM - IDX_HEADS - 3 * B_HEADS), w.dtype)
    return jnp.concatenate([qa, qb, kva, qi, kvc, kvs, kvw, ki, wi, gates, pad], axis=1).astype(BF16)


def _t5_bucket(dist):
    n = jnp.maximum(dist, 0)
    max_exact = N_BUCKETS // 2
    nf = jnp.maximum(n, 1).astype(F32)
    large = max_exact + (jnp.log(nf / max_exact) / math.log(MAX_DISTANCE / max_exact)
                         * (N_BUCKETS - max_exact)).astype(I32)
    large = jnp.minimum(large, N_BUCKETS - 1)
    return jnp.where(n < max_exact, n, large)


def _bias_lookup(tab, dist):
    b = _t5_bucket(jnp.asarray(dist.reshape(-1), I32))
    onehot = (b[:, None] == jnp.arange(N_BUCKETS, dtype=I32)[None, :]).astype(F32)
    out = jnp.dot(onehot, tab, precision=lax.Precision.HIGHEST)
    return jnp.transpose(out).reshape((tab.shape[1],) + dist.shape)


NEAR = -(-(MAX_DISTANCE + LANE - 1) // LANE)


def _bias_strips(tab, tq):
    r = np.arange(tq)[None, :, None]
    c = np.arange(LANE)[None, None, :]
    bd = np.arange(NEAR + 1)[:, None, None]
    dist = np.where(bd == NEAR, MAX_DISTANCE, r - c + LANE * bd)
    return _bias_lookup(tab, dist)


def _cmp_near_count(delta):
    return -(-(-(-(MAX_DISTANCE - delta) // CMP_BLOCK)) // 8) * 8


def _cmp_delta(tq):
    return -(CMP_BLOCK - 1) - CMP_BLOCK * ((tq - CMP_BLOCK) // CMP_BLOCK)


def _cmp_strips(tab, tq):
    delta = _cmp_delta(tq)
    ncn = _cmp_near_count(delta)
    dist = np.arange(tq)[:, None] + delta + CMP_BLOCK * np.arange(ncn)[None, :]
    v = _bias_lookup(tab, dist)
    hi = v.astype(BF16)
    r1 = v - hi.astype(F32)
    mid = r1.astype(BF16)
    lo = (r1 - mid.astype(F32)).astype(BF16)
    return jnp.stack([hi, mid, lo], axis=1)


def _padded_heads(x, g):
    z = jnp.zeros((x.shape[0], HEAD_DIM), x.dtype)
    rows = []
    for r in range(GROUP):
        h = g * GROUP + r
        piece = x[:, h * HEAD_DIM:(h + 1) * HEAD_DIM]
        rows.append(jnp.concatenate([piece, z] if g == 0 else [z, piece], axis=1))
    return jnp.concatenate(rows, axis=0)


def _k_part(kv, kv_t):
    return kv[:LANE] if kv_t else kv[:, :LANE]


def _v_part(kv, g, kv_t):
    return kv[(1 + g) * LANE:(2 + g) * LANE] if kv_t else kv[:, (1 + g) * LANE:(2 + g) * LANE]


def _flash_init(scr, tq, in_values):
    if in_values:
        return tuple((jnp.full((GROUP, tq, LANE), NEG, F32), jnp.zeros((GROUP * tq, LANE), F32))
                     for _ in range(A_KV))
    scr[0][...] = jnp.full_like(scr[0], NEG)
    scr[1][...] = jnp.zeros_like(scr[1])
    return 0


def _flash_store(st, scr):
    if isinstance(st, tuple):
        for g, (m, acc) in enumerate(st):
            scr[0][g] = m
            scr[1][g] = acc


def _mask_add(mask):
    return jnp.where(mask, 0.0, NEG)


def _flash_update(st, scr, g, qp_g, kv, strips_ref, far_ref, bds, madd, tq, kv_t, s_all=None):
    m_all, acc = st[g] if isinstance(st, tuple) else (scr[0][g], scr[1][g])
    if s_all is None:
        kb = _k_part(kv, kv_t)
        s_all = _mm(qp_g, kb) if kv_t else _mm_nt(qp_g, kb)
    nblk = s_all.shape[1] // LANE
    ps, alphas, m_news = [], [], []
    for r in range(GROUP):
        h = g * GROUP + r
        blocks = []
        for j in range(nblk):
            sj = s_all[r * tq:(r + 1) * tq, j * LANE:(j + 1) * LANE]
            if madd is not None:
                sj = sj + madd[:, j * LANE:(j + 1) * LANE]
            blocks.append(sj if bds is None else sj + strips_ref[h, bds[j]])
        bmax = blocks[0]
        for sj in blocks[1:]:
            bmax = jnp.maximum(bmax, sj)
        row_max = jnp.max(bmax, axis=-1, keepdims=True)
        m_old = m_all[r]
        if bds is None:
            m_new = jnp.maximum(m_old, row_max + far_ref[h])
            shift = m_new - far_ref[h]
        else:
            m_new = jnp.maximum(m_old, row_max)
            shift = m_new
        m_news.append(m_new)
        alphas.append(jnp.exp(m_old - m_new))
        ps.append(jnp.concatenate([jnp.exp(sj - shift).astype(BF16) for sj in blocks], axis=1))
    p = jnp.concatenate(ps, axis=0)
    vx = _v_part(kv, g, kv_t)
    pv = _mm_nt(p, vx) if kv_t else _mm(p, vx)
    m_all = jnp.stack(m_news)
    acc = jnp.concatenate(alphas, axis=0) * acc + pv
    if isinstance(st, tuple):
        return st[:g] + ((m_all, acc),) + st[g + 1:]
    scr[0][g] = m_all
    scr[1][g] = acc
    return st


def _flash_result(scr, g):
    acc = scr[1][g]
    l = acc[:, HEAD_DIM:HEAD_DIM + 1]
    return jnp.where(l > 0.0, acc[:, :HEAD_DIM] / jnp.where(l > 0.0, l, 1.0), 0.0)


def _tile_bds(qb, k, tk):
    nblk = tk // LANE
    return [jnp.clip(qb - (k * nblk + j), 0, NEAR) for j in range(nblk)]


ZERO_KEY_BASE = 2 ** 16 - 1
LOW_BITS = 7


def _ordered_key(x, pos):
    b = lax.bitcast_convert_type(x, I32)
    return jnp.where(x == 0.0, (ZERO_KEY_BASE - pos) << LOW_BITS, b ^ ((b >> 31) & 0x7FFFFFFF))


def _far_tiles(qb, tk):
    n = (qb - NEAR + 1) // (tk // LANE)
    return max(n, 0) if isinstance(n, int) else jnp.maximum(n, 0)


def _dsa_core(qi, wi, qa, get_kidx, get_kva, strips_ref, far_ref, t0, tq, tk, topk, kv_t, unroll,
              keys_scr, wib_scr, m_scr, acc_scr, tri_scr):
    n_tiles = (t0 + tq - 1) // tk + 1
    qb = t0 // LANE
    n_far = _far_tiles(qb, tk)
    row_pos = t0 + lax.broadcasted_iota(I32, (tq, tk), 0)
    col = lax.broadcasted_iota(I32, (tq, tk), 1)

    if kv_t:
        qi_st = jnp.concatenate([qi[:, h * IDX_DIM:(h + 1) * IDX_DIM] for h in range(IDX_HEADS)], axis=0)
    else:
        z = jnp.zeros((tq, LANE - IDX_DIM), qi.dtype)
        qi_st = jnp.concatenate(
            [jnp.concatenate([qi[:, h * IDX_DIM:(h + 1) * IDX_DIM], z], axis=1) for h in range(IDX_HEADS)], axis=0)
    for h in range(IDX_HEADS):
        wib_scr[h] = jnp.broadcast_to(wi[:, h:h + 1], (tq, tk))

    def score_tile(k, c):
        kt = get_kidx(k)
        sc = _mm(qi_st, kt) if kv_t else _mm_nt(qi_st, kt)
        score = jnp.maximum(sc[:tq], 0.0) * wib_scr[0]
        for h in range(1, IDX_HEADS):
            score = score + jnp.maximum(sc[h * tq:(h + 1) * tq], 0.0) * wib_scr[h]
        pos = k * tk + col
        keys_scr[k] = jnp.where(pos <= row_pos, _ordered_key(score, pos), KMIN)
        return c

    lax.fori_loop(0, n_tiles, score_tile, 0, unroll=unroll)

    def count(pred):
        def count_tile(k, cnt):
            key = keys_scr[k]
            for j in range(tk // LANE):
                cnt = cnt + jnp.where(pred(key[:, j * LANE:(j + 1) * LANE]), 1.0, 0.0)
            return cnt

        cnt = lax.fori_loop(0, n_tiles, count_tile, jnp.zeros((tq, LANE), F32))
        return jnp.sum(cnt, axis=-1, keepdims=True)

    def search(bits, tu, n_ge):
        for bit in bits:
            cand_u = tu | np.int32(-2 ** 31 if bit == 31 else 2 ** bit)
            cand_s = jnp.broadcast_to(cand_u ^ np.int32(KMIN), (tq, LANE))
            tot = count(lambda key, cand_s=cand_s: key >= cand_s)
            tu = jnp.where(tot >= topk, cand_u, tu)
            n_ge = jnp.where(tot >= topk, tot, n_ge)
        return tu, n_ge

    tu, n_ge = search(range(31, LOW_BITS - 1, -1), jnp.zeros((tq, 1), I32), jnp.zeros((tq, 1), F32))
    settled = jnp.min(jnp.where((tu == 0) | (n_ge == topk), 1.0, 0.0)) > 0.5
    tu, n_ge = lax.cond(settled, lambda: (tu, n_ge), lambda: search(range(LOW_BITS - 1, -1, -1), tu, n_ge))
    thr = tu ^ np.int32(KMIN)
    found = thr > np.int32(KMIN)
    has_ties = jnp.max(jnp.where(found & (n_ge > topk), 1.0, 0.0)) > 0.5

    qp_g = [_padded_heads(qa, g) for g in range(A_KV)]
    scr = (m_scr, acc_scr)
    in_values = bool(unroll)

    def attend(k, st, mask, far):
        kv = get_kva(k)
        bds = None if far else _tile_bds(qb, k, tk)
        madd = _mask_add(mask)
        for g in range(A_KV):
            st = _flash_update(st, scr, g, qp_g[g], kv, strips_ref, far_ref, bds, madd, tq, kv_t)
        return st

    @pl.when(jnp.logical_not(has_ties))
    def _():
        thr_ge = jnp.where(found, thr, np.int32(KMIN + 1))

        def tile(k, st, far):
            return attend(k, st, keys_scr[k] >= thr_ge, far)

        st = _flash_init(scr, tq, in_values)
        st = lax.fori_loop(0, n_far, functools.partial(tile, far=True), st, unroll=unroll)
        st = lax.fori_loop(n_far, n_tiles, functools.partial(tile, far=False), st, unroll=unroll)
        _flash_store(st, scr)

    @pl.when(has_ties)
    def _():
        need = topk - count(lambda key: key > jnp.broadcast_to(thr, (tq, LANE)))
        tri_scr[...] = jnp.where(
            lax.broadcasted_iota(I32, (LANE, LANE), 0) <= lax.broadcasted_iota(I32, (LANE, LANE), 1),
            1.0, 0.0).astype(BF16)

        def tile(k, c):
            eq_seen, st = c
            key = keys_scr[k]
            eq = key == thr
            takes = []
            for j in range(tk // LANE):
                eq_j = eq[:, j * LANE:(j + 1) * LANE]
                rank = eq_seen + _mm(jnp.where(eq_j, 1.0, 0.0), tri_scr[...])
                takes.append(eq_j & (rank <= need))
                eq_seen = rank[:, LANE - 1:LANE]
            take = jnp.concatenate(takes, axis=1)
            st = attend(k, st, (key > thr) | (take & found), False)
            return eq_seen, st

        _, st = lax.fori_loop(0, n_tiles, tile, (jnp.zeros((tq, 1), F32), _flash_init(scr, tq, in_values)))
        _flash_store(st, scr)

    outs = []
    for g in range(A_KV):
        o = _flash_result(scr, g)
        outs += [o[r * tq:(r + 1) * tq] for r in range(GROUP)]
    return jnp.concatenate(outs, axis=1)


def _pick_blocks(v, n_pick, axis):
    lane = lax.broadcasted_iota(I32, v.shape, axis).astype(F32)
    sel = jnp.zeros(v.shape, jnp.bool_)
    for _ in range(n_pick):
        mx = jnp.max(v, axis=axis, keepdims=True)
        cand = (v == mx) & jnp.logical_not(sel)
        first = jnp.min(jnp.where(cand, lane, float(v.shape[axis])), axis=axis, keepdims=True)
        pick = lane == first
        sel = sel | pick
        v = jnp.where(pick, -jnp.inf, v)
    return sel


def _nsa_core(qb, gates, kcp, get_kvs, win_tiles, strips_ref, cstrips_ref, far_ref, t0, tq, tk, nj,
              n_pick, kv_t, unroll, m_scr, acc_scr):
    n_tiles = (t0 + tq - 1) // tk + 1
    qblk = t0 // LANE
    delta = _cmp_delta(tq)
    ncn = cstrips_ref.shape[-1]
    qp_g = [_padded_heads(qb, g) for g in range(B_KV)]
    row_pos1 = t0 + lax.broadcasted_iota(I32, (tq, 1), 0)

    lam = lax.broadcasted_iota(I32, (tq, 2 * nj), 1)
    m_of = 2 * (lam % nj) + lam // nj
    cdist = row_pos1 - (CMP_BLOCK * m_of + CMP_BLOCK - 1)
    cmask = cdist >= 0
    m_hi = (t0 - (CMP_BLOCK - 1) - delta) // CMP_BLOCK
    lam_s = lax.broadcasted_iota(I32, (ncn, 2 * nj), 1)
    shift = jnp.where(2 * (lam_s % nj) + lam_s // nj == m_hi - lax.broadcasted_iota(I32, (ncn, 2 * nj), 0),
                      1.0, 0.0).astype(BF16)
    is_far = m_of <= m_hi - ncn
    kc_both = kcp[:, :LANE]
    o_c, imp = [], []
    for g in range(B_KV):
        bias = []
        for r in range(GROUP):
            h = g * GROUP + r
            b = (_mm(cstrips_ref[h, 0], shift) + _mm(cstrips_ref[h, 1], shift)) + _mm(cstrips_ref[h, 2], shift)
            bias.append(jnp.where(is_far, far_ref[h], b))
        s = _mm_nt(qp_g[g], kc_both).reshape(GROUP, tq, 2 * nj) + jnp.stack(bias)
        s = jnp.where(cmask[None], s, NEG)
        e = jnp.where(cmask[None], jnp.exp(s - jnp.max(s, axis=-1, keepdims=True)), 0.0)
        l = jnp.sum(e, axis=-1, keepdims=True)
        p = jnp.where(l > 0.0, e / jnp.where(l > 0.0, l, 1.0), 0.0)
        o_c.append(_mm(p.reshape(GROUP * tq, 2 * nj), kcp[:, (B_KV + g) * HEAD_DIM:(B_KV + g + 1) * HEAD_DIM]))
        ps = p[0]
        for r in range(1, GROUP):
            ps = ps + p[r]
        imp.append(ps[:, :nj] + ps[:, nj:])

    jl = lax.broadcasted_iota(I32, (tq, nj), 1)
    cur = row_pos1 // SEL_BLOCK
    forced = (jl == 0) | (jl == cur) | (jl == cur - 1)
    admissible = jl <= cur
    vs = [jnp.where(admissible, jnp.where(forced, jnp.inf, imp[g]), -jnp.inf) for g in range(B_KV)]
    if tq % LANE == 0:
        picked_t = _pick_blocks(jnp.concatenate([jnp.transpose(v) for v in vs], axis=1), n_pick, 0)
        picked = [jnp.transpose(jnp.where(picked_t[:, g * tq:(g + 1) * tq], 1.0, 0.0)) > 0.5
                  for g in range(B_KV)]
    else:
        picked = [_pick_blocks(v, n_pick, 1) for v in vs]
    lhs = []
    for g in range(B_KV):
        bneg = jnp.where(picked[g] & admissible, 0.0, NEG).astype(BF16)
        lhs.append(jnp.concatenate([qp_g[g].astype(BF16), jnp.concatenate([bneg] * GROUP, axis=0)], axis=1))
    row_pos = t0 + lax.broadcasted_iota(I32, (tq, tk), 0)
    col = lax.broadcasted_iota(I32, (tq, tk), 1)
    ind_shape, j_axis = ((nj, tk), 0) if kv_t else ((tk, nj), 1)
    ej = lax.broadcasted_iota(I32, ind_shape, j_axis)
    ec = lax.broadcasted_iota(I32, ind_shape, 1 - j_axis)
    scr = (m_scr, acc_scr)
    in_values = bool(unroll)

    def sel_tile(k, st, far):
        kv = get_kvs(k)
        ind = jnp.where(ej == (k * tk + ec) // SEL_BLOCK, 1.0, 0.0).astype(BF16)
        rhs = jnp.concatenate([_k_part(kv, kv_t).astype(BF16), ind], axis=j_axis)
        if far:
            bds, madd = None, None
        else:
            bds, madd = _tile_bds(qblk, k, tk), _mask_add(k * tk + col <= row_pos)
        for g in range(B_KV):
            s_all = _mm(lhs[g], rhs) if kv_t else _mm_nt(lhs[g], rhs)
            st = _flash_update(st, scr, g, None, kv, strips_ref, far_ref, bds, madd, tq, kv_t, s_all)
        return st

    n_far = _far_tiles(qblk, tk)
    st = _flash_init(scr, tq, in_values)
    st = lax.fori_loop(0, n_far, functools.partial(sel_tile, far=True), st, unroll=unroll)
    st = lax.fori_loop(n_far, n_tiles, functools.partial(sel_tile, far=False), st, unroll=unroll)
    _flash_store(st, scr)
    o_s = [_flash_result(scr, g) for g in range(B_KV)]

    st = _flash_init(scr, tq, in_values)
    for get_tile, off, guard in win_tiles:
        def win_tile(st, get_tile=get_tile, off=off):
            kv = get_tile()
            wt = kv.shape[1] if kv_t else kv.shape[0]
            dist = (lax.broadcasted_iota(I32, (tq, wt), 0) - lax.broadcasted_iota(I32, (tq, wt), 1)
                    - LANE * off)
            wadd = _mask_add((dist >= 0) & (dist <= WINDOW))
            bds = [min(max(-off - j, 0), NEAR) for j in range(wt // LANE)]
            for g in range(B_KV):
                st = _flash_update(st, scr, g, qp_g[g], kv, strips_ref, far_ref, bds, wadd, tq, kv_t)
            return st

        if guard is None:
            st = win_tile(st)
        else:
            assert not in_values

            def guarded(win_tile=win_tile):
                win_tile(0)

            pl.when(guard)(guarded)
    _flash_store(st, scr)
    o_w = [_flash_result(scr, g) for g in range(B_KV)]

    gs = jax.nn.sigmoid(gates)
    outs = []
    for g in range(B_KV):
        gcol = lambda j: jnp.concatenate(
            [gs[:, (g * GROUP + r) * 3 + j:(g * GROUP + r) * 3 + j + 1] for r in range(GROUP)], axis=0)
        o = gcol(0) * o_c[g] + gcol(1) * o_s[g] + gcol(2) * o_w[g]
        outs += [o[r * tq:(r + 1) * tq] for r in range(GROUP)]
    return jnp.concatenate(outs, axis=1)


def _compress(load_rows, n_pairs, pe_ref, wbd_ref):
    acc = [jnp.zeros((n_pairs, KVW), F32), jnp.zeros((n_pairs, KVW), F32)]
    for c in range(2 * CMP_BLOCK):
        ci = c % CMP_BLOCK
        acc[c // CMP_BLOCK] = acc[c // CMP_BLOCK] + _mm(load_rows(c) + pe_ref[ci], wbd_ref[ci])
    return jnp.concatenate(acc, axis=0)


def _compress_operands(pe, w):
    pe_rows = jnp.transpose(jnp.broadcast_to(pe[:, None], (2, B_KV, CMP_BLOCK, HEAD_DIM)), (2, 0, 1, 3))
    pe_rows = pe_rows.reshape(CMP_BLOCK, 1, KVW)
    eye = jnp.eye(2 * B_KV, dtype=w.dtype).reshape(2, B_KV, 2, B_KV)
    wbd = jnp.einsum("icde,igjh->cigdjhe", w, eye).reshape(CMP_BLOCK, KVW, KVW)
    return pe_rows, wbd.astype(BF16)


def _strided_rows(lo_ref, hi_ref, c, n):
    rows = pl.ds(c, n, stride=2 * CMP_BLOCK)
    return jnp.concatenate([lo_ref[rows, :], hi_ref[rows, :]], axis=1)


def _flash_scratch(tq):
    return [pltpu.VMEM((A_KV, GROUP, tq, LANE), F32), pltpu.VMEM((A_KV, GROUP * tq, LANE), F32)]


TQ_P = 128
TK_P = 1024


def _prompt_tile(ref, k, tk):
    return ref[pl.ds(pl.multiple_of(k * tk, tk), tk), :]


def _pdsa_kernel(qi_ref, misc_ref, qa_ref, kidx_ref, kva_ref, strips_ref, far_ref, o_ref,
                 keys_scr, wib_scr, m_scr, acc_scr, tri_scr, *, topk, tk):
    i = pl.program_id(1)
    o = _dsa_core(qi_ref[...], misc_ref[:, M_WI:M_WI + IDX_HEADS], qa_ref[...],
                  lambda k: _prompt_tile(kidx_ref, k, tk), lambda k: _prompt_tile(kva_ref, k, tk),
                  strips_ref, far_ref, i * TQ_P, TQ_P, tk, topk, False, None,
                  keys_scr, wib_scr, m_scr, acc_scr, tri_scr)
    o_ref[...] = o.astype(o_ref.dtype)


def _pdsa_call(pr, strips, far, tk):
    nb, s, _ = pr["qa"].shape
    assert s < ZERO_KEY_BASE
    tq = TQ_P
    row = lambda w: pl.BlockSpec((None, tq, w), lambda b, i: (b, i, 0))
    full = lambda w: pl.BlockSpec((None, s, w), lambda b, i: (b, 0, 0))
    return pl.pallas_call(
        functools.partial(_pdsa_kernel, topk=min(IDX_TOPK, s // 4), tk=tk),
        grid=(nb, s // tq),
        in_specs=[row(256), row(128), row(512), full(128), full(KVX),
                  pl.BlockSpec(strips.shape, lambda b, i: (0, 0, 0, 0)),
                  pl.BlockSpec(memory_space=pltpu.SMEM)],
        out_specs=row(512),
        out_shape=jax.ShapeDtypeStruct((nb, s, A_HEADS * HEAD_DIM), BF16),
        scratch_shapes=[pltpu.VMEM((s // tk, tq, tk), I32), pltpu.VMEM((IDX_HEADS, tq, tk), F32)]
        + _flash_scratch(tq) + [pltpu.VMEM((LANE, LANE), BF16)],
        compiler_params=_cparams(("arbitrary", "arbitrary")),
        name="dsa_prompt",
    )(pr["qi"], pr["misc"], pr["qa"], pr["misc_h"], pr["kva_x"], strips, far)


def _pcmp_kernel(lo_ref, hi_ref, pe_ref, wbd_ref, o_ref, *, n_pairs):
    o_ref[...] = _compress(lambda c: _strided_rows(lo_ref, hi_ref, c, n_pairs), n_pairs, pe_ref, wbd_ref)


def _pcmp_call(kvc, pe_rows, wbd):
    nb, s, _ = kvc.shape
    n_pairs = s // (2 * CMP_BLOCK)
    return pl.pallas_call(
        functools.partial(_pcmp_kernel, n_pairs=n_pairs),
        grid=(nb,),
        in_specs=[pl.BlockSpec((None, s, LANE), lambda b: (b, 0, 0)),
                  pl.BlockSpec((None, s, LANE), lambda b: (b, 0, 1)),
                  pl.BlockSpec((CMP_BLOCK, 1, KVW), lambda b: (0, 0, 0)),
                  pl.BlockSpec((CMP_BLOCK, KVW, KVW), lambda b: (0, 0, 0))],
        out_specs=pl.BlockSpec((None, 2 * n_pairs, KVW), lambda b: (b, 0, 0)),
        out_shape=jax.ShapeDtypeStruct((nb, 2 * n_pairs, KVW), F32),
        compiler_params=_cparams(("arbitrary",)),
        name="compress_prompt",
    )(kvc, kvc, pe_rows, wbd)


def _pnsa_kernel(qb_ref, misc_ref, kcp_ref, kvs_ref, kvw_ref, strips_ref, cstrips_ref, far_ref, o_ref,
                 m_scr, acc_scr, *, nj, n_pick, tk):
    i = pl.program_id(1)
    nw = WINDOW // LANE
    win_tiles = [(lambda: kvw_ref[pl.ds(pl.multiple_of((i - nw) * LANE, LANE), WINDOW + TQ_P), :], -nw, i >= nw)]
    win_tiles += [(lambda wd=wd: _prompt_tile(kvw_ref, i - wd, LANE), -wd, (i < nw) & (i - wd >= 0))
                  for wd in range(nw - 1, -1, -1)]
    o = _nsa_core(qb_ref[...], misc_ref[:, M_GATE:M_GATE + 3 * B_HEADS], kcp_ref[...],
                  lambda k: _prompt_tile(kvs_ref, k, tk), win_tiles,
                  strips_ref, cstrips_ref, far_ref, i * TQ_P, TQ_P, tk, nj, n_pick, False, None,
                  m_scr, acc_scr)
    o_ref[...] = o.astype(o_ref.dtype)


def _pnsa_call(pr, kcp, strips, cstrips, far, tk):
    nb, s, _ = pr["qb"].shape
    tq = TQ_P
    nj = s // SEL_BLOCK
    row = lambda w: pl.BlockSpec((None, tq, w), lambda b, i: (b, i, 0))
    full = lambda n, w: pl.BlockSpec((None, n, w), lambda b, i: (b, 0, 0))
    const = lambda a: pl.BlockSpec(a.shape, lambda b, i: (0,) * a.ndim)
    return pl.pallas_call(
        functools.partial(_pnsa_kernel, nj=nj, n_pick=min(SEL_TOPN, nj), tk=tk),
        grid=(nb, s // tq),
        in_specs=[row(512), row(128), full(2 * nj, KVW), full(s, KVX), full(s, KVX),
                  const(strips), const(cstrips),
                  pl.BlockSpec(memory_space=pltpu.SMEM)],
        out_specs=row(512),
        out_shape=jax.ShapeDtypeStruct((nb, s, B_HEADS * HEAD_DIM), BF16),
        scratch_shapes=_flash_scratch(tq),
        compiler_params=_cparams(("arbitrary", "arbitrary")),
        name="nsa_prompt",
    )(pr["qb"], pr["misc"], kcp, pr["kvs_x"], pr["kvw_x"], strips, cstrips, far)


TQ_S = 8
TK_S = 2048
PPT = TK_S // PAGE


def _pages_per_step(n_pages, cap):
    return max(g for g in (64, 32, 16, 8, 4, 2, 1) if g <= cap and n_pages % g == 0)


def _new_token_block(col):
    return jnp.where(lax.broadcasted_iota(I32, (col.shape[0], PAGE), 1) == 0, col, 0.0)


def _slot_tile(buf, k):
    blk = buf[pl.ds(k * PPT, PPT)]
    return jnp.concatenate([blk[j] for j in range(PPT)], axis=1)


def _sdsa_kernel(*refs, n_pages, pps, topk):
    idx_refs = refs[1:1 + pps]
    akv_refs = refs[1 + pps:1 + 2 * pps]
    (qi_ref, wi_ref, qa_ref, new_idx_ref, new_kv_ref, strips_ref, far_ref, o_ref,
     idx_buf, akv_buf, keys_scr, wib_scr, m_scr, acc_scr, tri_scr) = refs[1 + 2 * pps:]
    p = pl.program_id(1)
    for j in range(pps):
        idx_buf[p * pps + j] = idx_refs[j][...].astype(BF16)
        akv_buf[p * pps + j] = _kv_ext_t(akv_refs[j][...])

    @pl.when(p == n_pages // pps - 1)
    def _():
        idx_buf[n_pages] = _new_token_block(new_idx_ref[...]).astype(BF16)
        akv_buf[n_pages] = _kv_ext_t(_new_token_block(new_kv_ref[...]))
        for j in range(1, PPT):
            idx_buf[n_pages + j] = jnp.zeros((IDX_DIM, PAGE), BF16)
            akv_buf[n_pages + j] = _kv_ext_t(jnp.zeros((KVW, PAGE), BF16))
        o = _dsa_core(qi_ref[...], wi_ref[...], qa_ref[...],
                      lambda k: _slot_tile(idx_buf, k), lambda k: _slot_tile(akv_buf, k),
                      strips_ref, far_ref, n_pages * PAGE, TQ_S, TK_S, topk, True, True,
                      keys_scr, wib_scr, m_scr, acc_scr, tri_scr)
        o_ref[...] = o.astype(o_ref.dtype)


def _sample_rows(x):
    return jnp.broadcast_to(x[:, None, :], (x.shape[0], TQ_S, x.shape[1]))


def _pages_t(cache):
    n, p = cache.shape[:2]
    nd = cache.ndim
    return jnp.transpose(cache, (0, 1) + tuple(range(3, nd)) + (2,)).reshape(n, p, -1, cache.shape[2])


def _sdsa_call(a, page_table, idx_t, akv_t, pr, strips, far):
    bd, n_pages = page_table.shape
    assert idx_t.shape[3] == PAGE and (n_pages * PAGE) % TK_S == 0 and n_pages * PAGE + TK_S < ZERO_KEY_BASE
    pps = _pages_per_step(n_pages, 64)
    slots = n_pages + PPT
    misc = pr["misc"][0]
    rows = lambda w: pl.BlockSpec((None, TQ_S, w), lambda b, p, pt: (b, 0, 0))
    col = lambda w: pl.BlockSpec((None, w, 1), lambda b, p, pt: (b, 0, 0))
    page = lambda w, j: pl.BlockSpec((None, None, w, PAGE),
                                     lambda b, p, pt: (a, pt[b * n_pages + p * pps + j], 0, 0))
    grid_spec = pltpu.PrefetchScalarGridSpec(
        num_scalar_prefetch=1,
        grid=(bd, n_pages // pps),
        in_specs=[page(IDX_DIM, j) for j in range(pps)] + [page(KVW, j) for j in range(pps)]
        + [rows(256), rows(IDX_HEADS), rows(512), col(IDX_DIM), col(KVW),
           pl.BlockSpec(strips.shape, lambda b, p, pt: (0, 0, 0, 0)),
           pl.BlockSpec(memory_space=pltpu.SMEM)],
        out_specs=rows(512),
        scratch_shapes=[pltpu.VMEM((slots, IDX_DIM, PAGE), BF16), pltpu.VMEM((slots, KVX, PAGE), BF16),
                        pltpu.VMEM((slots // PPT, TQ_S, TK_S), I32), pltpu.VMEM((IDX_HEADS, TQ_S, TK_S), F32)]
        + _flash_scratch(TQ_S) + [pltpu.VMEM((LANE, LANE), BF16)])
    out = pl.pallas_call(
        functools.partial(_sdsa_kernel, n_pages=n_pages, pps=pps, topk=min(IDX_TOPK, (n_pages * PAGE + 1) // 4)),
        grid_spec=grid_spec,
        out_shape=jax.ShapeDtypeStruct((bd, TQ_S, A_HEADS * HEAD_DIM), BF16),
        compiler_params=_cparams(("arbitrary", "arbitrary")),
        name="dsa_sample",
    )(page_table.reshape(-1), *([idx_t] * pps), *([akv_t] * pps),
      _sample_rows(pr["qi"][0]), _sample_rows(misc[:, M_WI:M_WI + IDX_HEADS]), _sample_rows(pr["qa"][0]),
      misc[:, M_KI:M_KI + IDX_DIM, None], pr["kva"][0][:, :, None], strips, far)
    return out[:, 0]


def _snsa_kernel(*refs, n_pages, pps):
    cmp_refs = refs[1:1 + pps]
    sel_refs = refs[1 + pps:1 + 2 * pps]
    (qb_ref, gates_ref, new_sel_ref, win_ref, new_win_ref, pe_ref, wbd_ref, strips_ref, cstrips_ref, far_ref,
     o_ref, cmp_lo, cmp_hi, sel_buf, m_scr, acc_scr) = refs[1 + 2 * pps:]
    p = pl.program_id(1)
    past = n_pages * PAGE
    for j in range(pps):
        rows = pl.ds(pl.multiple_of((p * pps + j) * PAGE, PAGE), PAGE)
        cmp_lo[rows, :] = jnp.transpose(cmp_refs[j][:LANE])
        cmp_hi[rows, :] = jnp.transpose(cmp_refs[j][LANE:])
        sel_buf[p * pps + j] = _kv_ext_t(sel_refs[j][...])

    @pl.when(p == n_pages // pps - 1)
    def _():
        sel_buf[n_pages] = _kv_ext_t(_new_token_block(new_sel_ref[...]))
        for j in range(1, PPT):
            sel_buf[n_pages + j] = _kv_ext_t(jnp.zeros((KVW, PAGE), BF16))
        nj = past // SEL_BLOCK
        kcp = _compress(lambda c: _strided_rows(cmp_lo, cmp_hi, c, nj), nj, pe_ref, wbd_ref)
        win_tiles = [(lambda: _kv_ext_t(win_ref[...]), -(WINDOW // LANE), None),
                     (lambda: _kv_ext_t(_new_token_block(new_win_ref[...])), 0, None)]
        o = _nsa_core(qb_ref[...], gates_ref[...], kcp, lambda k: _slot_tile(sel_buf, k), win_tiles,
                      strips_ref, cstrips_ref, far_ref, past, TQ_S, TK_S, nj,
                      min(SEL_TOPN, nj + 1) - 1, True, True, m_scr, acc_scr)
        o_ref[...] = o.astype(o_ref.dtype)


def _snsa_call(a, page_table, cmp_t, sel_t, win_t, pr, pe_rows, wbd, strips, cstrips, far):
    bd, n_pages = page_table.shape
    past = n_pages * PAGE
    assert past % TK_S == 0 and win_t.shape[3] == WINDOW
    pps = _pages_per_step(n_pages, 16)
    misc = pr["misc"][0]
    rows = lambda w: pl.BlockSpec((None, TQ_S, w), lambda b, p, pt: (b, 0, 0))
    col = lambda w: pl.BlockSpec((None, w, 1), lambda b, p, pt: (b, 0, 0))
    page = lambda j: pl.BlockSpec((None, None, KVW, PAGE),
                                  lambda b, p, pt: (a, pt[b * n_pages + p * pps + j], 0, 0))
    const = lambda x: pl.BlockSpec(x.shape, lambda b, p, pt: (0,) * x.ndim)
    grid_spec = pltpu.PrefetchScalarGridSpec(
        num_scalar_prefetch=1,
        grid=(bd, n_pages // pps),
        in_specs=[page(j) for j in range(pps)] + [page(j) for j in range(pps)]
        + [rows(512), rows(3 * B_HEADS), col(KVW),
           pl.BlockSpec((None, None, KVW, WINDOW), lambda b, p, pt: (a, b, 0, 0)), col(KVW),
           const(pe_rows), const(wbd), const(strips), const(cstrips),
           pl.BlockSpec(memory_space=pltpu.SMEM)],
        out_specs=rows(512),
        scratch_shapes=[pltpu.VMEM((past, LANE), F32), pltpu.VMEM((past, LANE), F32),
                        pltpu.VMEM((n_pages + PPT, KVX, PAGE), BF16)] + _flash_scratch(TQ_S))
    out = pl.pallas_call(
        functools.partial(_snsa_kernel, n_pages=n_pages, pps=pps),
        grid_spec=grid_spec,
        out_shape=jax.ShapeDtypeStruct((bd, TQ_S, B_HEADS * HEAD_DIM), BF16),
        compiler_params=_cparams(("arbitrary", "arbitrary")),
        name="nsa_sample",
    )(page_table.reshape(-1), *([cmp_t] * pps), *([sel_t] * pps),
      _sample_rows(pr["qb"][0]), _sample_rows(misc[:, M_GATE:M_GATE + 3 * B_HEADS]),
      pr["kvs"][0][:, :, None], win_t, pr["kvw"][0][:, :, None],
      pe_rows, wbd, strips, cstrips, far)
    return out[:, 0]


def _merge_kernel(x_ref, mod_ref, oa_ref, ob_ref, w_ref, o_ref):
    n = oa_ref.shape[-1]
    y = _mm(oa_ref[...], w_ref[:n]) + _mm(ob_ref[...], w_ref[n:])
    o_ref[...] = x_ref[...] + (1.0 + mod_ref[2]) * y


def _merge_call(x, mod, oa, ob, w, tm):
    nb, t, d = x.shape
    tmod = mod.shape[2]
    mod_spec = (pl.BlockSpec((None, 3, 1, d), lambda b, i: (b, 0, 0, 0)) if tmod == 1 else
                pl.BlockSpec((None, 3, tm, d), lambda b, i: (b, 0, i, 0)))
    row = lambda w_: pl.BlockSpec((None, tm, w_), lambda b, i: (b, i, 0))
    return pl.pallas_call(
        _merge_kernel,
        grid=(nb, t // tm),
        in_specs=[row(d), mod_spec, row(oa.shape[-1]), row(ob.shape[-1]),
                  pl.BlockSpec(w.shape, lambda b, i: (0, 0))],
        out_specs=row(d),
        out_shape=jax.ShapeDtypeStruct(x.shape, F32),
        compiler_params=_cparams(("arbitrary", "arbitrary")),
        name="merge",
    )(x, mod, oa, ob, w)


def _gelu_ln(h, w_in_ref, ln_ref, cw):
    uv = jax.nn.gelu(_mm(h, w_in_ref[...]))
    u, v = uv[:, :cw], uv[:, cw:]
    mu = jnp.mean(v, axis=-1, keepdims=True)
    var = jnp.mean(jnp.square(v - mu), axis=-1, keepdims=True)
    return u, (v - mu) * lax.rsqrt(var + EPS) * ln_ref[...]


def _pgmlp_kernel(x_ref, mod_ref, g_ref, w_in_ref, ln_ref, ws_ref, bs_ref, w_out_ref, o_ref, *, cw):
    x = x_ref[...]
    h = _modulated_norm(x, g_ref[...], mod_ref).astype(BF16)
    u, v = _gelu_ln(h, w_in_ref, ln_ref, cw)
    gw = cw // C_GROUPS
    rows = []
    for n in range(x.shape[0] // C_CHUNK):
        vb = v[n * C_CHUNK:(n + 1) * C_CHUNK].astype(BF16)
        sv = [_mm(ws_ref[g], vb[:, g * gw:(g + 1) * gw]) + bs_ref[:, g:g + 1] for g in range(C_GROUPS)]
        rows.append(jnp.concatenate(sv, axis=1))
    sv = rows[0] if len(rows) == 1 else jnp.concatenate(rows, axis=0)
    o_ref[...] = x + (1.0 + mod_ref[2]) * _mm(u * sv, w_out_ref[...])


def _pgmlp_call(x, mod, g, w_in, ln_g, ws, bs_t, w_out, tm):
    nb, t, d = x.shape
    cw = w_out.shape[0]
    const = lambda a: pl.BlockSpec(a.shape, lambda b, i: (0,) * a.ndim)
    return pl.pallas_call(
        functools.partial(_pgmlp_kernel, cw=cw),
        grid=(nb, t // tm),
        in_specs=[pl.BlockSpec((None, tm, d), lambda b, i: (b, i, 0)),
                  pl.BlockSpec((None, 3, 1, d), lambda b, i: (b, 0, 0, 0)),
                  pl.BlockSpec((1, d), lambda b, i: (0, 0)),
                  const(w_in), pl.BlockSpec((1, cw), lambda b, i: (0, 0)), const(ws), const(bs_t), const(w_out)],
        out_specs=pl.BlockSpec((None, tm, d), lambda b, i: (b, i, 0)),
        out_shape=jax.ShapeDtypeStruct(x.shape, F32),
        compiler_params=_cparams(("arbitrary", "arbitrary")),
        name="gmlp_prompt",
    )(x, mod, g.reshape(1, d), w_in, ln_g.reshape(1, cw), ws, bs_t, w_out)


def _sgmlp_kernel(x_ref, mod_ref, g_ref, w_in_ref, ln_ref, ws0_ref, bs0_ref, w_out_ref, o_ref, v_ref, *, cw):
    x = x_ref[...]
    h = _modulated_norm(x, g_ref[...], mod_ref).astype(BF16)
    u, v = _gelu_ln(h, w_in_ref, ln_ref, cw)
    v_ref[...] = v
    sv = v.astype(BF16).astype(F32) * ws0_ref[...].astype(F32) + bs0_ref[...]
    o_ref[...] = x + (1.0 + mod_ref[2]) * _mm(u * sv, w_out_ref[...])


def _sgmlp_call(x, mod, g, w_in, ln_g, ws, b_sp, w_out):
    nb, t, d = x.shape
    cw = w_out.shape[0]
    gw = cw // C_GROUPS
    ws0 = jnp.repeat(ws[:, 0, 0], gw).reshape(1, cw)
    bs0 = jnp.repeat(b_sp[:, 0], gw).reshape(1, cw)
    full = lambda a: pl.BlockSpec(a.shape, lambda: (0,) * a.ndim)
    x2, mod2 = x[0], mod[0]
    o, v = pl.pallas_call(
        functools.partial(_sgmlp_kernel, cw=cw),
        in_specs=[full(x2), full(mod2), pl.BlockSpec((1, d), lambda: (0, 0)), full(w_in),
                  pl.BlockSpec((1, cw), lambda: (0, 0)), full(ws0), full(bs0), full(w_out)],
        out_specs=[full(x2), pl.BlockSpec((t, cw), lambda: (0, 0))],
        out_shape=[jax.ShapeDtypeStruct(x2.shape, F32), jax.ShapeDtypeStruct((t, cw), F32)],
        compiler_params=pltpu.CompilerParams(vmem_limit_bytes=VMEM_LIMIT),
        name="gmlp_sample",
    )(x2, mod2, g.reshape(1, d), w_in, ln_g.reshape(1, cw), ws0, bs0, w_out)
    return o[None], v


def kernel(x_prompt, x_sample, cache_a_kv, cache_a_idx, cache_b_cmp_kv, cache_b_sel_kv, state_b_win_kv,
           page_table, c_prompt, c_sample, rel_bias, w_ada, b_ada, norm_g, w_ffn_in, w_ffn_out,
           w_in_att, w_out_att, cmp_pe, w_cmp, w_in_c, ln_c_g, w_sp, b_sp, w_out_c, final_g):
    nbp, s, d = x_prompt.shape
    nbs, tdec, _ = x_sample.shape
    depth = w_ada.shape[0]
    assert tdec == 1 and s % TQ_P == 0 and cache_a_kv.shape[2] == PAGE

    ada = _ada_call(jnp.concatenate([c_prompt, c_sample], axis=0), w_ada, b_ada)
    ada = ada.reshape(depth, nbp + nbs, 3, 3, d)
    mod_p = lambda l, j: ada[l, :nbp, j][:, :, None, :]
    mod_s = lambda l, j: jnp.transpose(ada[l, nbp:, j], (1, 0, 2))[None]

    w_ffn_in_h = w_ffn_in.astype(BF16)
    w_ffn_out_h = w_ffn_out.astype(BF16)
    tab_a, tab_b = rel_bias[:, :A_HEADS], rel_bias[:, A_HEADS:]
    strips_pa, strips_pb = _bias_strips(tab_a, TQ_P), _bias_strips(tab_b, TQ_P)
    strips_sa, strips_sb = _bias_strips(tab_a, TQ_S), _bias_strips(tab_b, TQ_S)
    cstrips_p, cstrips_s = _cmp_strips(tab_b, TQ_P), _cmp_strips(tab_b, TQ_S)
    far_a, far_b = tab_a[N_BUCKETS - 1], tab_b[N_BUCKETS - 1]
    idx_t, akv_t = _pages_t(cache_a_idx), _pages_t(cache_a_kv)
    cmp_t, sel_t, win_t = _pages_t(cache_b_cmp_kv), _pages_t(cache_b_sel_kv), _pages_t(state_b_win_kv)

    xp = x_prompt
    xs = jnp.transpose(x_sample, (1, 0, 2))
    tm_p = 512 if s % 512 == 0 else TQ_P
    tk_p = TK_P if s % TK_P == 0 else TQ_P
    st_p, st_s, st_c = [], [], []
    for l in range(depth):
        last = l == depth - 1
        xp = _ffn_call(xp, mod_p(l, 0), norm_g[l, 0], w_ffn_in_h, w_ffn_out_h, l, 0, None, tm_p)
        xs = _ffn_call(xs, mod_s(l, 0), norm_g[l, 0], w_ffn_in_h, w_ffn_out_h, l, 0, None, nbs)
        if l % 2 == 0:
            a = l // 2
            wp = _rearranged_proj_weight(w_in_att[a])
            w_out = w_out_att[a].astype(BF16)
            pe_rows, wbd = _compress_operands(cmp_pe[a], w_cmp[a])

            pr = _proj_call(xp, mod_p(l, 1), norm_g[l, 1], wp, tm_p)
            oa = _pdsa_call(pr, strips_pa, far_a, tk_p)
            kcp = _pcmp_call(pr["kvc"], pe_rows, wbd)
            ob = _pnsa_call(pr, kcp, strips_pb, cstrips_p, far_b, tk_p)
            xp = _merge_call(xp, mod_p(l, 1), oa, ob, w_out, tm_p)
            kv5 = lambda t: t.reshape(t.shape[0], t.shape[1], 2, A_KV, HEAD_DIM)
            st_p.append((kv5(pr["kva"]), pr["misc"][:, :, M_KI:M_KI + IDX_DIM], kv5(pr["kvc"]),
                         kv5(pr["kvs"]), kv5(pr["kvw"][:, s - min(WINDOW, s):])))

            ps = _proj_call(xs, mod_s(l, 1), norm_g[l, 1], wp, nbs)
            oa = _sdsa_call(a, page_table, idx_t, akv_t, ps, strips_sa, far_a)
            ob = _snsa_call(a, page_table, cmp_t, sel_t, win_t, ps, pe_rows, wbd, strips_sb, cstrips_s, far_b)
            xs = _merge_call(xs, mod_s(l, 1), oa[None], ob[None], w_out, nbs)
            tok = lambda t: t[0].reshape(nbs, 1, 2, A_KV, HEAD_DIM)
            st_s.append((tok(ps["kva"]), ps["misc"][0][:, None, M_KI:M_KI + IDX_DIM], tok(ps["kvc"]),
                         tok(ps["kvs"]),
                         jnp.concatenate([state_b_win_kv[a][:, tdec:], tok(ps["kvw"])], axis=1)))
        else:
            ci = l // 2
            w_in = w_in_c[ci].astype(BF16)
            w_out = w_out_c[ci].astype(BF16)
            ws = (w_sp[ci] * jnp.tril(jnp.ones((C_CHUNK, C_CHUNK), w_sp.dtype))).astype(BF16)
            xp = _pgmlp_call(xp, mod_p(l, 1), norm_g[l, 1], w_in, ln_c_g[ci], ws,
                             jnp.transpose(b_sp[ci]), w_out, tm_p if s % tm_p == 0 else C_CHUNK)
            xs, v = _sgmlp_call(xs, mod_s(l, 1), norm_g[l, 1], w_in, ln_c_g[ci], ws, b_sp[ci], w_out)
            st_c.append(v[:, None, :])
        fg = final_g if last else None
        xp = _ffn_call(xp, mod_p(l, 2), norm_g[l, 2], w_ffn_in_h, w_ffn_out_h, l, 1, fg, tm_p)
        xs = _ffn_call(xs, mod_s(l, 2), norm_g[l, 2], w_ffn_in_h, w_ffn_out_h, l, 1, fg, nbs)

    stk = lambda sts, i: jnp.stack([st[i] for st in sts])
    return (xp, jnp.transpose(xs, (1, 0, 2)),
            stk(st_p, 0), stk(st_p, 1), stk(st_p, 2), stk(st_p, 3), stk(st_p, 4),
            stk(st_s, 0), stk(st_s, 1), stk(st_s, 2), stk(st_s, 3), stk(st_s, 4),
            jnp.stack(st_c))
```

```python
import functools
import math

import numpy as np
import jax
import jax.numpy as jnp
from jax import lax
from jax.experimental import pallas as pl
from jax.experimental.pallas import tpu as pltpu

F32 = jnp.float32
BF16 = jnp.bfloat16
I32 = jnp.int32

HEAD_DIM = 64
A_HEADS = 8
A_KV = 2
IDX_HEADS = 4
IDX_DIM = 64
IDX_TOPK = 256
B_HEADS = 8
B_KV = 2
CMP_BLOCK = 32
SEL_BLOCK = 64
SEL_TOPN = 16
WINDOW = 512
N_BUCKETS = 32
MAX_DISTANCE = 1024
C_CHUNK = 128
C_GROUPS = 8
EPS = 1e-6
PAGE = 128
GROUP = A_HEADS // A_KV
KVW = 2 * A_KV * HEAD_DIM

LANE = 128
VMEM_LIMIT = 56 * 1024 * 1024
KVX = 3 * LANE

NEG = -1e30
KMIN = -2 ** 31

P_QA, P_QB, P_KVA, P_QI, P_KVC, P_KVS, P_KVW, P_MISC = 0, 512, 1024, 1280, 1536, 1792, 2048, 2304
P_WIDTH = 2432
M_KI, M_WI, M_GATE = 0, IDX_DIM, IDX_DIM + IDX_HEADS

assert A_KV == 2 and B_KV == 2 and A_KV * HEAD_DIM == LANE and IDX_DIM == HEAD_DIM


def _cparams(sem):
    return pltpu.CompilerParams(dimension_semantics=sem, vmem_limit_bytes=VMEM_LIMIT)


def _mm(a, b):
    return jnp.dot(a.astype(BF16), b.astype(BF16), preferred_element_type=F32)


def _mm_nt(a, b):
    return lax.dot_general(a.astype(BF16), b.astype(BF16), (((1,), (1,)), ((), ())),
                           preferred_element_type=F32)


def _modulated_norm(x, g, mod_ref):
    y = x * lax.rsqrt(jnp.mean(x * x, axis=-1, keepdims=True) + EPS) * g
    return y * (1.0 + mod_ref[1]) + mod_ref[0]


def _ada_kernel(c_ref, w_ref, b_ref, o_ref):
    c = c_ref[...]
    o_ref[...] = _mm(c * jax.nn.sigmoid(c), w_ref[...]) + b_ref[...]


def _ada_call(c_all, w_ada, b_ada):
    depth, d, n = w_ada.shape
    r = c_all.shape[0]
    tn = n // 8
    return pl.pallas_call(
        _ada_kernel,
        grid=(depth, n // tn),
        in_specs=[pl.BlockSpec((r, d), lambda l, j: (0, 0)),
                  pl.BlockSpec((None, d, tn), lambda l, j: (l, 0, j)),
                  pl.BlockSpec((None, 1, tn), lambda l, j: (l, 0, j))],
        out_specs=pl.BlockSpec((None, r, tn), lambda l, j: (l, 0, j)),
        out_shape=jax.ShapeDtypeStruct((depth, r, n), F32),
        compiler_params=_cparams(("arbitrary", "arbitrary")),
        name="ada",
    )(c_all, w_ada, b_ada.reshape(depth, 1, n))


def _ffn_kernel(x_ref, mod_ref, g_ref, wa_ref, wb_ref, wo_ref, fg_ref, o_ref, h_scr, acc_scr, *,
                n_f, final_norm):
    f = pl.program_id(2)

    @pl.when(f == 0)
    def _():
        h_scr[...] = _modulated_norm(x_ref[...], g_ref[...], mod_ref).astype(BF16)
        acc_scr[...] = jnp.zeros_like(acc_scr)

    h = h_scr[...]
    a = _mm(h, wa_ref[...])
    b = _mm(h, wb_ref[...])
    acc_scr[...] += _mm(a * jax.nn.sigmoid(a) * b, wo_ref[...])

    @pl.when(f == n_f - 1)
    def _():
        y = x_ref[...] + 0.5 * (1.0 + mod_ref[2]) * acc_scr[...]
        if final_norm:
            y = y * lax.rsqrt(jnp.mean(y * y, axis=-1, keepdims=True) + EPS) * fg_ref[...]
        o_ref[...] = y


def _ffn_call(x, mod, g, w_in, w_out, l, j, final_g, tm):
    nb, t, d = x.shape
    dff = w_out.shape[2]
    tf = dff // 2 if dff % (2 * LANE) == 0 else LANE
    n_f = dff // tf
    tmod = mod.shape[2]
    mod_spec = (pl.BlockSpec((None, 3, 1, d), lambda b, i, f: (b, 0, 0, 0)) if tmod == 1 else
                pl.BlockSpec((None, 3, tm, d), lambda b, i, f: (b, 0, i, 0)))
    final_norm = final_g is not None
    fg = (final_g if final_norm else g).reshape(1, d)
    return pl.pallas_call(
        functools.partial(_ffn_kernel, n_f=n_f, final_norm=final_norm),
        grid=(nb, t // tm, n_f),
        in_specs=[pl.BlockSpec((None, tm, d), lambda b, i, f: (b, i, 0)),
                  mod_spec,
                  pl.BlockSpec((1, d), lambda b, i, f: (0, 0)),
                  pl.BlockSpec((None, None, d, tf), lambda b, i, f: (l, j, 0, f)),
                  pl.BlockSpec((None, None, d, tf), lambda b, i, f: (l, j, 0, f + n_f)),
                  pl.BlockSpec((None, None, tf, d), lambda b, i, f: (l, j, f, 0)),
                  pl.BlockSpec((1, d), lambda b, i, f: (0, 0))],
        out_specs=pl.BlockSpec((None, tm, d), lambda b, i, f: (b, i, 0)),
        out_shape=jax.ShapeDtypeStruct(x.shape, F32),
        scratch_shapes=[pltpu.VMEM((tm, d), BF16), pltpu.VMEM((tm, d), F32)],
        compiler_params=_cparams(("arbitrary", "arbitrary", "arbitrary")),
        name="ffn",
    )(x, mod, g.reshape(1, d), w_in, w_in, w_out, fg)


def _kv_ext(y):
    e = jnp.where(lax.broadcasted_iota(I32, (y.shape[0], HEAD_DIM), 1) == 0, 1.0, 0.0).astype(BF16)
    y = y.astype(BF16)
    return jnp.concatenate([y[:, :LANE], y[:, LANE:LANE + HEAD_DIM], e, y[:, LANE + HEAD_DIM:], e], axis=1)


def _kv_ext_t(y):
    e = jnp.where(lax.broadcasted_iota(I32, (HEAD_DIM, y.shape[1]), 0) == 0, 1.0, 0.0).astype(BF16)
    y = y.astype(BF16)
    return jnp.concatenate([y[:LANE], y[LANE:LANE + HEAD_DIM], e, y[LANE + HEAD_DIM:], e], axis=0)


_PROJ_OUT = (("qa", P_QA, 512, "h"), ("qb", P_QB, 512, "h"), ("qi", P_QI, 256, "h"),
             ("kva", P_KVA, 256, "f"), ("kvc", P_KVC, 256, "f"), ("kvs", P_KVS, 256, "f"),
             ("kvw", P_KVW, 256, "f"), ("misc", P_MISC, 128, "f"),
             ("kva_x", P_KVA, KVX, "x"), ("kvs_x", P_KVS, KVX, "x"), ("kvw_x", P_KVW, KVX, "x"),
             ("misc_h", P_MISC, 128, "h"))


def _proj_kernel(x_ref, mod_ref, g_ref, w_ref, *o_refs):
    h = _modulated_norm(x_ref[...], g_ref[...], mod_ref).astype(BF16)
    done = {}
    for (name, off, width, kind), o_ref in zip(_PROJ_OUT, o_refs):
        if off not in done:
            y = _mm(h, w_ref[:, off:off + (KVW if kind == "x" else width)])
            if name in ("qa", "qb"):
                y = y * (HEAD_DIM ** -0.5)
            done[off] = y
        o_ref[...] = _kv_ext(done[off]) if kind == "x" else done[off].astype(o_ref.dtype)


def _proj_call(x, mod, g, wp, tm):
    nb, t, d = x.shape
    tmod = mod.shape[2]
    mod_spec = (pl.BlockSpec((None, 3, 1, d), lambda b, i: (b, 0, 0, 0)) if tmod == 1 else
                pl.BlockSpec((None, 3, tm, d), lambda b, i: (b, 0, i, 0)))
    outs = pl.pallas_call(
        _proj_kernel,
        grid=(nb, t // tm),
        in_specs=[pl.BlockSpec((None, tm, d), lambda b, i: (b, i, 0)),
                  mod_spec,
                  pl.BlockSpec((1, d), lambda b, i: (0, 0)),
                  pl.BlockSpec((d, P_WIDTH), lambda b, i: (0, 0))],
        out_specs=[pl.BlockSpec((None, tm, w), lambda b, i: (b, i, 0)) for _, _, w, _ in _PROJ_OUT],
        out_shape=[jax.ShapeDtypeStruct((nb, t, w), F32 if kind == "f" else BF16)
                   for _, _, w, kind in _PROJ_OUT],
        compiler_params=_cparams(("arbitrary", "arbitrary")),
        name="proj",
    )(x, mod, g.reshape(1, d), wp)
    return {name: o for (name, _, _, _), o in zip(_PROJ_OUT, outs)}


def _rearranged_proj_weight(w):
    d = w.shape[0]
    o = np.cumsum((0, 512, 256, 256, 64, 4, 512, 256, 256, 256, 24))
    seg = lambda i: w[:, o[i]:o[i + 1]]
    qa, kva, qi, ki, wi, qb, kvc, kvs, kvw, gates = [seg(i) for i in range(10)]
    pad = jnp.zeros((d, LANE - IDX_DIM - IDX_HEADS - 3 * B_HEADS), w.dtype)
    return jnp.concatenate([qa, qb, kva, qi, kvc, kvs, kvw, ki, wi, gates, pad], axis=1).astype(BF16)


def _t5_bucket(dist):
    n = jnp.maximum(dist, 0)
    max_exact = N_BUCKETS // 2
    nf = jnp.maximum(n, 1).astype(F32)
    large = max_exact + (jnp.log(nf / max_exact) / math.log(MAX_DISTANCE / max_exact)
                         * (N_BUCKETS - max_exact)).astype(I32)
    large = jnp.minimum(large, N_BUCKETS - 1)
    return jnp.where(n < max_exact, n, large)


def _bias_lookup(tab, dist):
    b = _t5_bucket(jnp.asarray(dist.reshape(-1), I32))
    onehot = (b[:, None] == jnp.arange(N_BUCKETS, dtype=I32)[None, :]).astype(F32)
    out = jnp.dot(onehot, tab, precision=lax.Precision.HIGHEST)
    return jnp.transpose(out).reshape((tab.shape[1],) + dist.shape)


NEAR = -(-(MAX_DISTANCE + LANE - 1) // LANE)


def _bias_strips(tab, tq):
    r = np.arange(tq)[None, :, None]
    c = np.arange(LANE)[None, None, :]
    bd = np.arange(NEAR + 1)[:, None, None]
    dist = np.where(bd == NEAR, MAX_DISTANCE, r - c + LANE * bd)
    return _bias_lookup(tab, dist)


def _cmp_near_count(delta):
    return -(-(-(-(MAX_DISTANCE - delta) // CMP_BLOCK)) // 8) * 8


def _cmp_delta(tq):
    return -(CMP_BLOCK - 1) - CMP_BLOCK * ((tq - CMP_BLOCK) // CMP_BLOCK)


def _cmp_strips(tab, tq):
    delta = _cmp_delta(tq)
    ncn = _cmp_near_count(delta)
    dist = np.arange(tq)[:, None] + delta + CMP_BLOCK * np.arange(ncn)[None, :]
    v = _bias_lookup(tab, dist)
    hi = v.astype(BF16)
    r1 = v - hi.astype(F32)
    mid = r1.astype(BF16)
    lo = (r1 - mid.astype(F32)).astype(BF16)
    return jnp.stack([hi, mid, lo], axis=1)


def _padded_heads(x, g):
    z = jnp.zeros((x.shape[0], HEAD_DIM), x.dtype)
    rows = []
    for r in range(GROUP):
        h = g * GROUP + r
        piece = x[:, h * HEAD_DIM:(h + 1) * HEAD_DIM]
        rows.append(jnp.concatenate([piece, z] if g == 0 else [z, piece], axis=1))
    return jnp.concatenate(rows, axis=0)


def _k_part(kv, kv_t):
    return kv[:LANE] if kv_t else kv[:, :LANE]


def _v_part(kv, g, kv_t):
    return kv[(1 + g) * LANE:(2 + g) * LANE] if kv_t else kv[:, (1 + g) * LANE:(2 + g) * LANE]


def _flash_init(scr, tq, in_values):
    if in_values:
        return tuple((jnp.full((GROUP, tq, LANE), NEG, F32), jnp.zeros((GROUP * tq, LANE), F32))
                     for _ in range(A_KV))
    scr[0][...] = jnp.full_like(scr[0], NEG)
    scr[1][...] = jnp.zeros_like(scr[1])
    return 0


def _flash_store(st, scr):
    if isinstance(st, tuple):
        for g, (m, acc) in enumerate(st):
            scr[0][g] = m
            scr[1][g] = acc


def _mask_add(mask):
    return jnp.where(mask, 0.0, NEG)


def _flash_update(st, scr, g, qp_g, kv, strips_ref, far_ref, bds, madd, tq, kv_t, s_all=None):
    m_all, acc = st[g] if isinstance(st, tuple) else (scr[0][g], scr[1][g])
    if s_all is None:
        kb = _k_part(kv, kv_t)
        s_all = _mm(qp_g, kb) if kv_t else _mm_nt(qp_g, kb)
    nblk = s_all.shape[1] // LANE
    ps, alphas, m_news = [], [], []
    for r in range(GROUP):
        h = g * GROUP + r
        blocks = []
        for j in range(nblk):
            sj = s_all[r * tq:(r + 1) * tq, j * LANE:(j + 1) * LANE]
            if madd is not None:
                sj = sj + madd[:, j * LANE:(j + 1) * LANE]
            blocks.append(sj if bds is None else sj + strips_ref[h, bds[j]])
        bmax = blocks[0]
        for sj in blocks[1:]:
            bmax = jnp.maximum(bmax, sj)
        row_max = jnp.max(bmax, axis=-1, keepdims=True)
        m_old = m_all[r]
        if bds is None:
            m_new = jnp.maximum(m_old, row_max + far_ref[h])
            shift = m_new - far_ref[h]
        else:
            m_new = jnp.maximum(m_old, row_max)
            shift = m_new
        m_news.append(m_new)
        alphas.append(jnp.exp(m_old - m_new))
        ps.append(jnp.concatenate([jnp.exp(sj - shift).astype(BF16) for sj in blocks], axis=1))
    p = jnp.concatenate(ps, axis=0)
    vx = _v_part(kv, g, kv_t)
    pv = _mm_nt(p, vx) if kv_t else _mm(p, vx)
    m_all = jnp.stack(m_news)
    acc = jnp.concatenate(alphas, axis=0) * acc + pv
    if isinstance(st, tuple):
        return st[:g] + ((m_all, acc),) + st[g + 1:]
    scr[0][g] = m_all
    scr[1][g] = acc
    return st


def _flash_result(scr, g):
    acc = scr[1][g]
    l = acc[:, HEAD_DIM:HEAD_DIM + 1]
    return jnp.where(l > 0.0, acc[:, :HEAD_DIM] / jnp.where(l > 0.0, l, 1.0), 0.0)


def _tile_bds(qb, k, tk):
    nblk = tk // LANE
    return [jnp.clip(qb - (k * nblk + j), 0, NEAR) for j in range(nblk)]


ZERO_KEY_BASE = 2 ** 16 - 1
LOW_BITS = 7


def _ordered_key(x, pos):
    b = lax.bitcast_convert_type(x, I32)
    return jnp.where(x == 0.0, (ZERO_KEY_BASE - pos) << LOW_BITS, b ^ ((b >> 31) & 0x7FFFFFFF))


def _far_tiles(qb, tk):
    n = (qb - NEAR + 1) // (tk // LANE)
    return max(n, 0) if isinstance(n, int) else jnp.maximum(n, 0)


def _dsa_core(qi, wi, qa, get_kidx, get_kva, strips_ref, far_ref, t0, tq, tk, topk, kv_t, unroll,
              keys_scr, wib_scr, m_scr, acc_scr, tri_scr):
    n_tiles = (t0 + tq - 1) // tk + 1
    qb = t0 // LANE
    n_far = _far_tiles(qb, tk)
    row_pos = t0 + lax.broadcasted_iota(I32, (tq, tk), 0)
    col = lax.broadcasted_iota(I32, (tq, tk), 1)

    if kv_t:
        qi_st = jnp.concatenate([qi[:, h * IDX_DIM:(h + 1) * IDX_DIM] for h in range(IDX_HEADS)], axis=0)
    else:
        z = jnp.zeros((tq, LANE - IDX_DIM), qi.dtype)
        qi_st = jnp.concatenate(
            [jnp.concatenate([qi[:, h * IDX_DIM:(h + 1) * IDX_DIM], z], axis=1) for h in range(IDX_HEADS)], axis=0)
    for h in range(IDX_HEADS):
        wib_scr[h] = jnp.broadcast_to(wi[:, h:h + 1], (tq, tk))

    def score_tile(k, c):
        kt = get_kidx(k)
        sc = _mm(qi_st, kt) if kv_t else _mm_nt(qi_st, kt)
        score = jnp.maximum(sc[:tq], 0.0) * wib_scr[0]
        for h in range(1, IDX_HEADS):
            score = score + jnp.maximum(sc[h * tq:(h + 1) * tq], 0.0) * wib_scr[h]
        pos = k * tk + col
        keys_scr[k] = jnp.where(pos <= row_pos, _ordered_key(score, pos), KMIN)
        return c

    lax.fori_loop(0, n_tiles, score_tile, 0, unroll=unroll)

    def count(pred):
        def count_tile(k, cnt):
            key = keys_scr[k]
            for j in range(tk // LANE):
                cnt = cnt + jnp.where(pred(key[:, j * LANE:(j + 1) * LANE]), 1.0, 0.0)
            return cnt

        cnt = lax.fori_loop(0, n_tiles, count_tile, jnp.zeros((tq, LANE), F32))
        return jnp.sum(cnt, axis=-1, keepdims=True)

    def search(bits, tu, n_ge):
        for bit in bits:
            cand_u = tu | np.int32(-2 ** 31 if bit == 31 else 2 ** bit)
            cand_s = jnp.broadcast_to(cand_u ^ np.int32(KMIN), (tq, LANE))
            tot = count(lambda key, cand_s=cand_s: key >= cand_s)
            tu = jnp.where(tot >= topk, cand_u, tu)
            n_ge = jnp.where(tot >= topk, tot, n_ge)
        return tu, n_ge

    tu, n_ge = search(range(31, LOW_BITS - 1, -1), jnp.zeros((tq, 1), I32), jnp.zeros((tq, 1), F32))
    settled = jnp.min(jnp.where((tu == 0) | (n_ge == topk), 1.0, 0.0)) > 0.5
    tu, n_ge = lax.cond(settled, lambda: (tu, n_ge), lambda: search(range(LOW_BITS - 1, -1, -1), tu, n_ge))
    thr = tu ^ np.int32(KMIN)
    found = thr > np.int32(KMIN)
    has_ties = jnp.max(jnp.where(found & (n_ge > topk), 1.0, 0.0)) > 0.5

    qp_g = [_padded_heads(qa, g) for g in range(A_KV)]
    scr = (m_scr, acc_scr)
    in_values = bool(unroll)

    def attend(k, st, mask, far):
        kv = get_kva(k)
        bds = None if far else _tile_bds(qb, k, tk)
        madd = _mask_add(mask)
        for g in range(A_KV):
            st = _flash_update(st, scr, g, qp_g[g], kv, strips_ref, far_ref, bds, madd, tq, kv_t)
        return st

    @pl.when(jnp.logical_not(has_ties))
    def _():
        thr_ge = jnp.where(found, thr, np.int32(KMIN + 1))

        def tile(k, st, far):
            return attend(k, st, keys_scr[k] >= thr_ge, far)

        st = _flash_init(scr, tq, in_values)
        st = lax.fori_loop(0, n_far, functools.partial(tile, far=True), st, unroll=unroll)
        st = lax.fori_loop(n_far, n_tiles, functools.partial(tile, far=False), st, unroll=unroll)
        _flash_store(st, scr)

    @pl.when(has_ties)
    def _():
        need = topk - count(lambda key: key > jnp.broadcast_to(thr, (tq, LANE)))
        tri_scr[...] = jnp.where(
            lax.broadcasted_iota(I32, (LANE, LANE), 0) <= lax.broadcasted_iota(I32, (LANE, LANE), 1),
            1.0, 0.0).astype(BF16)

        def tile(k, c):
            eq_seen, st = c
            key = keys_scr[k]
            eq = key == thr
            takes = []
            for j in range(tk // LANE):
                eq_j = eq[:, j * LANE:(j + 1) * LANE]
                rank = eq_seen + _mm(jnp.where(eq_j, 1.0, 0.0), tri_scr[...])
                takes.append(eq_j & (rank <= need))
                eq_seen = rank[:, LANE - 1:LANE]
            take = jnp.concatenate(takes, axis=1)
            st = attend(k, st, (key > thr) | (take & found), False)
            return eq_seen, st

        _, st = lax.fori_loop(0, n_tiles, tile, (jnp.zeros((tq, 1), F32), _flash_init(scr, tq, in_values)))
        _flash_store(st, scr)

    outs = []
    for g in range(A_KV):
        o = _flash_result(scr, g)
        outs += [o[r * tq:(r + 1) * tq] for r in range(GROUP)]
    return jnp.concatenate(outs, axis=1)


def _pick_blocks(v, n_pick, axis):
    lane = lax.broadcasted_iota(I32, v.shape, axis).astype(F32)
    sel = jnp.zeros(v.shape, jnp.bool_)
    for _ in range(n_pick):
        mx = jnp.max(v, axis=axis, keepdims=True)
        cand = (v == mx) & jnp.logical_not(sel)
        first = jnp.min(jnp.where(cand, lane, float(v.shape[axis])), axis=axis, keepdims=True)
        pick = lane == first
        sel = sel | pick
        v = jnp.where(pick, -jnp.inf, v)
    return sel


def _nsa_core(qb, gates, kcp, get_kvs, win_tiles, strips_ref, cstrips_ref, far_ref, t0, tq, tk, nj,
              n_pick, kv_t, unroll, m_scr, acc_scr):
    n_tiles = (t0 + tq - 1) // tk + 1
    qblk = t0 // LANE
    delta = _cmp_delta(tq)
    ncn = cstrips_ref.shape[-1]
    qp_g = [_padded_heads(qb, g) for g in range(B_KV)]
    row_pos1 = t0 + lax.broadcasted_iota(I32, (tq, 1), 0)

    lam = lax.broadcasted_iota(I32, (tq, 2 * nj), 1)
    m_of = 2 * (lam % nj) + lam // nj
    cdist = row_pos1 - (CMP_BLOCK * m_of + CMP_BLOCK - 1)
    cmask = cdist >= 0
    m_hi = (t0 - (CMP_BLOCK - 1) - delta) // CMP_BLOCK
    lam_s = lax.broadcasted_iota(I32, (ncn, 2 * nj), 1)
    shift = jnp.where(2 * (lam_s % nj) + lam_s // nj == m_hi - lax.broadcasted_iota(I32, (ncn, 2 * nj), 0),
                      1.0, 0.0).astype(BF16)
    is_far = m_of <= m_hi - ncn
    kc_both = kcp[:, :LANE]
    o_c, imp = [], []
    for g in range(B_KV):
        bias = []
        for r in range(GROUP):
            h = g * GROUP + r
            b = (_mm(cstrips_ref[h, 0], shift) + _mm(cstrips_ref[h, 1], shift)) + _mm(cstrips_ref[h, 2], shift)
            bias.append(jnp.where(is_far, far_ref[h], b))
        s = _mm_nt(qp_g[g], kc_both).reshape(GROUP, tq, 2 * nj) + jnp.stack(bias)
        s = jnp.where(cmask[None], s, NEG)
        e = jnp.where(cmask[None], jnp.exp(s - jnp.max(s, axis=-1, keepdims=True)), 0.0)
        l = jnp.sum(e, axis=-1, keepdims=True)
        p = jnp.where(l > 0.0, e / jnp.where(l > 0.0, l, 1.0), 0.0)
        o_c.append(_mm(p.reshape(GROUP * tq, 2 * nj), kcp[:, (B_KV + g) * HEAD_DIM:(B_KV + g + 1) * HEAD_DIM]))
        ps = p[0]
        for r in range(1, GROUP):
            ps = ps + p[r]
        imp.append(ps[:, :nj] + ps[:, nj:])

    jl = lax.broadcasted_iota(I32, (tq, nj), 1)
    cur = row_pos1 // SEL_BLOCK
    forced = (jl == 0) | (jl == cur) | (jl == cur - 1)
    admissible = jl <= cur
    vs = [jnp.where(admissible, jnp.where(forced, jnp.inf, imp[g]), -jnp.inf) for g in range(B_KV)]
    if tq % LANE == 0:
        picked_t = _pick_blocks(jnp.concatenate([jnp.transpose(v) for v in vs], axis=1), n_pick, 0)
        picked = [jnp.transpose(jnp.where(picked_t[:, g * tq:(g + 1) * tq], 1.0, 0.0)) > 0.5
                  for g in range(B_KV)]
    else:
        picked = [_pick_blocks(v, n_pick, 1) for v in vs]
    lhs = []
    for g in range(B_KV):
        bneg = jnp.where(picked[g] & admissible, 0.0, NEG).astype(BF16)
        lhs.append(jnp.concatenate([qp_g[g].astype(BF16), jnp.concatenate([bneg] * GROUP, axis=0)], axis=1))
    row_pos = t0 + lax.broadcasted_iota(I32, (tq, tk), 0)
    col = lax.broadcasted_iota(I32, (tq, tk), 1)
    ind_shape, j_axis = ((nj, tk), 0) if kv_t else ((tk, nj), 1)
    ej = lax.broadcasted_iota(I32, ind_shape, j_axis)
    ec = lax.broadcasted_iota(I32, ind_shape, 1 - j_axis)
    scr = (m_scr, acc_scr)
    in_values = bool(unroll)

    def sel_tile(k, st, far):
        kv = get_kvs(k)
        ind = jnp.where(ej == (k * tk + ec) // SEL_BLOCK, 1.0, 0.0).astype(BF16)
        rhs = jnp.concatenate([_k_part(kv, kv_t).astype(BF16), ind], axis=j_axis)
        if far:
            bds, madd = None, None
        else:
            bds, madd = _tile_bds(qblk, k, tk), _mask_add(k * tk + col <= row_pos)
        for g in range(B_KV):
            s_all = _mm(lhs[g], rhs) if kv_t else _mm_nt(lhs[g], rhs)
            st = _flash_update(st, scr, g, None, kv, strips_ref, far_ref, bds, madd, tq, kv_t, s_all)
        return st

    n_far = _far_tiles(qblk, tk)
    st = _flash_init(scr, tq, in_values)
    st = lax.fori_loop(0, n_far, functools.partial(sel_tile, far=True), st, unroll=unroll)
    st = lax.fori_loop(n_far, n_tiles, functools.partial(sel_tile, far=False), st, unroll=unroll)
    _flash_store(st, scr)
    o_s = [_flash_result(scr, g) for g in range(B_KV)]

    st = _flash_init(scr, tq, in_values)
    for get_tile, off, guard in win_tiles:
        def win_tile(st, get_tile=get_tile, off=off):
            kv = get_tile()
            wt = kv.shape[1] if kv_t else kv.shape[0]
            dist = (lax.broadcasted_iota(I32, (tq, wt), 0) - lax.broadcasted_iota(I32, (tq, wt), 1)
                    - LANE * off)
            wadd = _mask_add((dist >= 0) & (dist <= WINDOW))
            bds = [min(max(-off - j, 0), NEAR) for j in range(wt // LANE)]
            for g in range(B_KV):
                st = _flash_update(st, scr, g, qp_g[g], kv, strips_ref, far_ref, bds, wadd, tq, kv_t)
            return st

        if guard is None:
            st = win_tile(st)
        else:
            assert not in_values

            def guarded(win_tile=win_tile):
                win_tile(0)

            pl.when(guard)(guarded)
    _flash_store(st, scr)
    o_w = [_flash_result(scr, g) for g in range(B_KV)]

    gs = jax.nn.sigmoid(gates)
    outs = []
    for g in range(B_KV):
        gcol = lambda j: jnp.concatenate(
            [gs[:, (g * GROUP + r) * 3 + j:(g * GROUP + r) * 3 + j + 1] for r in range(GROUP)], axis=0)
        o = gcol(0) * o_c[g] + gcol(1) * o_s[g] + gcol(2) * o_w[g]
        outs += [o[r * tq:(r + 1) * tq] for r in range(GROUP)]
    return jnp.concatenate(outs, axis=1)


def _compress(load_rows, n_pairs, pe_ref, wbd_ref):
    acc = [jnp.zeros((n_pairs, KVW), F32), jnp.zeros((n_pairs, KVW), F32)]
    for c in range(2 * CMP_BLOCK):
        ci = c % CMP_BLOCK
        acc[c // CMP_BLOCK] = acc[c // CMP_BLOCK] + _mm(load_rows(c) + pe_ref[ci], wbd_ref[ci])
    return jnp.concatenate(acc, axis=0)


def _compress_operands(pe, w):
    pe_rows = jnp.transpose(jnp.broadcast_to(pe[:, None], (2, B_KV, CMP_BLOCK, HEAD_DIM)), (2, 0, 1, 3))
    pe_rows = pe_rows.reshape(CMP_BLOCK, 1, KVW)
    eye = jnp.eye(2 * B_KV, dtype=w.dtype).reshape(2, B_KV, 2, B_KV)
    wbd = jnp.einsum("icde,igjh->cigdjhe", w, eye).reshape(CMP_BLOCK, KVW, KVW)
    return pe_rows, wbd.astype(BF16)


def _strided_rows(lo_ref, hi_ref, c, n):
    rows = pl.ds(c, n, stride=2 * CMP_BLOCK)
    return jnp.concatenate([lo_ref[rows, :], hi_ref[rows, :]], axis=1)


def _flash_scratch(tq):
    return [pltpu.VMEM((A_KV, GROUP, tq, LANE), F32), pltpu.VMEM((A_KV, GROUP * tq, LANE), F32)]


TQ_P = 128
TK_P = 1024


def _prompt_tile(ref, k, tk):
    return ref[pl.ds(pl.multiple_of(k * tk, tk), tk), :]


def _pdsa_kernel(qi_ref, misc_ref, qa_ref, kidx_ref, kva_ref, strips_ref, far_ref, o_ref,
                 keys_scr, wib_scr, m_scr, acc_scr, tri_scr, *, topk, tk):
    i = pl.program_id(1)
    o = _dsa_core(qi_ref[...], misc_ref[:, M_WI:M_WI + IDX_HEADS], qa_ref[...],
                  lambda k: _prompt_tile(kidx_ref, k, tk), lambda k: _prompt_tile(kva_ref, k, tk),
                  strips_ref, far_ref, i * TQ_P, TQ_P, tk, topk, False, None,
                  keys_scr, wib_scr, m_scr, acc_scr, tri_scr)
    o_ref[...] = o.astype(o_ref.dtype)


def _pdsa_call(pr, strips, far, tk):
    nb, s, _ = pr["qa"].shape
    assert s < ZERO_KEY_BASE
    tq = TQ_P
    row = lambda w: pl.BlockSpec((None, tq, w), lambda b, i: (b, i, 0))
    full = lambda w: pl.BlockSpec((None, s, w), lambda b, i: (b, 0, 0))
    return pl.pallas_call(
        functools.partial(_pdsa_kernel, topk=min(IDX_TOPK, s // 4), tk=tk),
        grid=(nb, s // tq),
        in_specs=[row(256), row(128), row(512), full(128), full(KVX),
                  pl.BlockSpec(strips.shape, lambda b, i: (0, 0, 0, 0)),
                  pl.BlockSpec(memory_space=pltpu.SMEM)],
        out_specs=row(512),
        out_shape=jax.ShapeDtypeStruct((nb, s, A_HEADS * HEAD_DIM), BF16),
        scratch_shapes=[pltpu.VMEM((s // tk, tq, tk), I32), pltpu.VMEM((IDX_HEADS, tq, tk), F32)]
        + _flash_scratch(tq) + [pltpu.VMEM((LANE, LANE), BF16)],
        compiler_params=_cparams(("arbitrary", "arbitrary")),
        name="dsa_prompt",
    )(pr["qi"], pr["misc"], pr["qa"], pr["misc_h"], pr["kva_x"], strips, far)


def _pcmp_kernel(lo_ref, hi_ref, pe_ref, wbd_ref, o_ref, *, n_pairs):
    o_ref[...] = _compress(lambda c: _strided_rows(lo_ref, hi_ref, c, n_pairs), n_pairs, pe_ref, wbd_ref)


def _pcmp_call(kvc, pe_rows, wbd):
    nb, s, _ = kvc.shape
    n_pairs = s // (2 * CMP_BLOCK)
    return pl.pallas_call(
        functools.partial(_pcmp_kernel, n_pairs=n_pairs),
        grid=(nb,),
        in_specs=[pl.BlockSpec((None, s, LANE), lambda b: (b, 0, 0)),
                  pl.BlockSpec((None, s, LANE), lambda b: (b, 0, 1)),
                  pl.BlockSpec((CMP_BLOCK, 1, KVW), lambda b: (0, 0, 0)),
                  pl.BlockSpec((CMP_BLOCK, KVW, KVW), lambda b: (0, 0, 0))],
        out_specs=pl.BlockSpec((None, 2 * n_pairs, KVW), lambda b: (b, 0, 0)),
        out_shape=jax.ShapeDtypeStruct((nb, 2 * n_pairs, KVW), F32),
        compiler_params=_cparams(("arbitrary",)),
        name="compress_prompt",
    )(kvc, kvc, pe_rows, wbd)


def _pnsa_kernel(qb_ref, misc_ref, kcp_ref, kvs_ref, kvw_ref, strips_ref, cstrips_ref, far_ref, o_ref,
                 m_scr, acc_scr, *, nj, n_pick, tk):
    i = pl.program_id(1)
    nw = WINDOW // LANE
    win_tiles = [(lambda: kvw_ref[pl.ds(pl.multiple_of((i - nw) * LANE, LANE), WINDOW + TQ_P), :], -nw, i >= nw)]
    win_tiles += [(lambda wd=wd: _prompt_tile(kvw_ref, i - wd, LANE), -wd, (i < nw) & (i - wd >= 0))
                  for wd in range(nw - 1, -1, -1)]
    o = _nsa_core(qb_ref[...], misc_ref[:, M_GATE:M_GATE + 3 * B_HEADS], kcp_ref[...],
                  lambda k: _prompt_tile(kvs_ref, k, tk), win_tiles,
                  strips_ref, cstrips_ref, far_ref, i * TQ_P, TQ_P, tk, nj, n_pick, False, None,
                  m_scr, acc_scr)
    o_ref[...] = o.astype(o_ref.dtype)


def _pnsa_call(pr, kcp, strips, cstrips, far, tk):
    nb, s, _ = pr["qb"].shape
    tq = TQ_P
    nj = s // SEL_BLOCK
    row = lambda w: pl.BlockSpec((None, tq, w), lambda b, i: (b, i, 0))
    full = lambda n, w: pl.BlockSpec((None, n, w), lambda b, i: (b, 0, 0))
    const = lambda a: pl.BlockSpec(a.shape, lambda b, i: (0,) * a.ndim)
    return pl.pallas_call(
        functools.partial(_pnsa_kernel, nj=nj, n_pick=min(SEL_TOPN, nj), tk=tk),
        grid=(nb, s // tq),
        in_specs=[row(512), row(128), full(2 * nj, KVW), full(s, KVX), full(s, KVX),
                  const(strips), const(cstrips),
                  pl.BlockSpec(memory_space=pltpu.SMEM)],
        out_specs=row(512),
        out_shape=jax.ShapeDtypeStruct((nb, s, B_HEADS * HEAD_DIM), BF16),
        scratch_shapes=_flash_scratch(tq),
        compiler_params=_cparams(("arbitrary", "arbitrary")),
        name="nsa_prompt",
    )(pr["qb"], pr["misc"], kcp, pr["kvs_x"], pr["kvw_x"], strips, cstrips, far)


TQ_S = 8
TK_S = 2048
PPT = TK_S // PAGE


def _pages_per_step(n_pages, cap):
    return max(g for g in (64, 32, 16, 8, 4, 2, 1) if g <= cap and n_pages % g == 0)


def _new_token_block(col):
    return jnp.where(lax.broadcasted_iota(I32, (col.shape[0], PAGE), 1) == 0, col, 0.0)


def _slot_tile(buf, k):
    blk = buf[pl.ds(k * PPT, PPT)]
    return jnp.concatenate([blk[j] for j in range(PPT)], axis=1)


def _sdsa_kernel(*refs, n_pages, pps, topk):
    idx_refs = refs[1:1 + pps]
    akv_refs = refs[1 + pps:1 + 2 * pps]
    (qi_ref, wi_ref, qa_ref, new_idx_ref, new_kv_ref, strips_ref, far_ref, o_ref,
     idx_buf, akv_buf, keys_scr, wib_scr, m_scr, acc_scr, tri_scr) = refs[1 + 2 * pps:]
    p = pl.program_id(1)
    for j in range(pps):
        idx_buf[p * pps + j] = idx_refs[j][...].astype(BF16)
        akv_buf[p * pps + j] = _kv_ext_t(akv_refs[j][...])

    @pl.when(p == n_pages // pps - 1)
    def _():
        idx_buf[n_pages] = _new_token_block(new_idx_ref[...]).astype(BF16)
        akv_buf[n_pages] = _kv_ext_t(_new_token_block(new_kv_ref[...]))
        for j in range(1, PPT):
            idx_buf[n_pages + j] = jnp.zeros((IDX_DIM, PAGE), BF16)
            akv_buf[n_pages + j] = _kv_ext_t(jnp.zeros((KVW, PAGE), BF16))
        o = _dsa_core(qi_ref[...], wi_ref[...], qa_ref[...],
                      lambda k: _slot_tile(idx_buf, k), lambda k: _slot_tile(akv_buf, k),
                      strips_ref, far_ref, n_pages * PAGE, TQ_S, TK_S, topk, True, True,
                      keys_scr, wib_scr, m_scr, acc_scr, tri_scr)
        o_ref[...] = o.astype(o_ref.dtype)


def _sample_rows(x):
    return jnp.broadcast_to(x[:, None, :], (x.shape[0], TQ_S, x.shape[1]))


def _pages_t(cache):
    n, p = cache.shape[:2]
    nd = cache.ndim
    return jnp.transpose(cache, (0, 1) + tuple(range(3, nd)) + (2,)).reshape(n, p, -1, cache.shape[2])


def _sdsa_call(a, page_table, idx_t, akv_t, pr, strips, far):
    bd, n_pages = page_table.shape
    assert idx_t.shape[3] == PAGE and (n_pages * PAGE) % TK_S == 0 and n_pages * PAGE + TK_S < ZERO_KEY_BASE
    pps = _pages_per_step(n_pages, 64)
    slots = n_pages + PPT
    misc = pr["misc"][0]
    rows = lambda w: pl.BlockSpec((None, TQ_S, w), lambda b, p, pt: (b, 0, 0))
    col = lambda w: pl.BlockSpec((None, w, 1), lambda b, p, pt: (b, 0, 0))
    page = lambda w, j: pl.BlockSpec((None, None, w, PAGE),
                                     lambda b, p, pt: (a, pt[b * n_pages + p * pps + j], 0, 0))
    grid_spec = pltpu.PrefetchScalarGridSpec(
        num_scalar_prefetch=1,
        grid=(bd, n_pages // pps),
        in_specs=[page(IDX_DIM, j) for j in range(pps)] + [page(KVW, j) for j in range(pps)]
        + [rows(256), rows(IDX_HEADS), rows(512), col(IDX_DIM), col(KVW),
           pl.BlockSpec(strips.shape, lambda b, p, pt: (0, 0, 0, 0)),
           pl.BlockSpec(memory_space=pltpu.SMEM)],
        out_specs=rows(512),
        scratch_shapes=[pltpu.VMEM((slots, IDX_DIM, PAGE), BF16), pltpu.VMEM((slots, KVX, PAGE), BF16),
                        pltpu.VMEM((slots // PPT, TQ_S, TK_S), I32), pltpu.VMEM((IDX_HEADS, TQ_S, TK_S), F32)]
        + _flash_scratch(TQ_S) + [pltpu.VMEM((LANE, LANE), BF16)])
    out = pl.pallas_call(
        functools.partial(_sdsa_kernel, n_pages=n_pages, pps=pps, topk=min(IDX_TOPK, (n_pages * PAGE + 1) // 4)),
        grid_spec=grid_spec,
        out_shape=jax.ShapeDtypeStruct((bd, TQ_S, A_HEADS * HEAD_DIM), BF16),
        compiler_params=_cparams(("arbitrary", "arbitrary")),
        name="dsa_sample",
    )(page_table.reshape(-1), *([idx_t] * pps), *([akv_t] * pps),
      _sample_rows(pr["qi"][0]), _sample_rows(misc[:, M_WI:M_WI + IDX_HEADS]), _sample_rows(pr["qa"][0]),
      misc[:, M_KI:M_KI + IDX_DIM, None], pr["kva"][0][:, :, None], strips, far)
    return out[:, 0]


def _snsa_kernel(*refs, n_pages, pps):
    cmp_refs = refs[1:1 + pps]
    sel_refs = refs[1 + pps:1 + 2 * pps]
    (qb_ref, gates_ref, new_sel_ref, win_ref, new_win_ref, pe_ref, wbd_ref, strips_ref, cstrips_ref, far_ref,
     o_ref, cmp_lo, cmp_hi, sel_buf, m_scr, acc_scr) = refs[1 + 2 * pps:]
    p = pl.program_id(1)
    past = n_pages * PAGE
    for j in range(pps):
        rows = pl.ds(pl.multiple_of((p * pps + j) * PAGE, PAGE), PAGE)
        cmp_lo[rows, :] = jnp.transpose(cmp_refs[j][:LANE])
        cmp_hi[rows, :] = jnp.transpose(cmp_refs[j][LANE:])
        sel_buf[p * pps + j] = _kv_ext_t(sel_refs[j][...])

    @pl.when(p == n_pages // pps - 1)
    def _():
        sel_buf[n_pages] = _kv_ext_t(_new_token_block(new_sel_ref[...]))
        for j in range(1, PPT):
            sel_buf[n_pages + j] = _kv_ext_t(jnp.zeros((KVW, PAGE), BF16))
        nj = past // SEL_BLOCK
        kcp = _compress(lambda c: _strided_rows(cmp_lo, cmp_hi, c, nj), nj, pe_ref, wbd_ref)
        win_tiles = [(lambda: _kv_ext_t(win_ref[...]), -(WINDOW // LANE), None),
                     (lambda: _kv_ext_t(_new_token_block(new_win_ref[...])), 0, None)]
        o = _nsa_core(qb_ref[...], gates_ref[...], kcp, lambda k: _slot_tile(sel_buf, k), win_tiles,
                      strips_ref, cstrips_ref, far_ref, past, TQ_S, TK_S, nj,
                      min(SEL_TOPN, nj + 1) - 1, True, True, m_scr, acc_scr)
        o_ref[...] = o.astype(o_ref.dtype)


def _snsa_call(a, page_table, cmp_t, sel_t, win_t, pr, pe_rows, wbd, strips, cstrips, far):
    bd, n_pages = page_table.shape
    past = n_pages * PAGE
    assert past % TK_S == 0 and win_t.shape[3] == WINDOW
    pps = _pages_per_step(n_pages, 32)
    misc = pr["misc"][0]
    rows = lambda w: pl.BlockSpec((None, TQ_S, w), lambda b, p, pt: (b, 0, 0))
    col = lambda w: pl.BlockSpec((None, w, 1), lambda b, p, pt: (b, 0, 0))
    page = lambda j: pl.BlockSpec((None, None, KVW, PAGE),
                                  lambda b, p, pt: (a, pt[b * n_pages + p * pps + j], 0, 0))
    const = lambda x: pl.BlockSpec(x.shape, lambda b, p, pt: (0,) * x.ndim)
    grid_spec = pltpu.PrefetchScalarGridSpec(
        num_scalar_prefetch=1,
        grid=(bd, n_pages // pps),
        in_specs=[page(j) for j in range(pps)] + [page(j) for j in range(pps)]
        + [rows(512), rows(3 * B_HEADS), col(KVW),
           pl.BlockSpec((None, None, KVW, WINDOW), lambda b, p, pt: (a, b, 0, 0)), col(KVW),
           const(pe_rows), const(wbd), const(strips), const(cstrips),
           pl.BlockSpec(memory_space=pltpu.SMEM)],
        out_specs=rows(512),
        scratch_shapes=[pltpu.VMEM((past, LANE), F32), pltpu.VMEM((past, LANE), F32),
                        pltpu.VMEM((n_pages + PPT, KVX, PAGE), BF16)] + _flash_scratch(TQ_S))
    out = pl.pallas_call(
        functools.partial(_snsa_kernel, n_pages=n_pages, pps=pps),
        grid_spec=grid_spec,
        out_shape=jax.ShapeDtypeStruct((bd, TQ_S, B_HEADS * HEAD_DIM), BF16),
        compiler_params=_cparams(("arbitrary", "arbitrary")),
        name="nsa_sample",
    )(page_table.reshape(-1), *([cmp_t] * pps), *([sel_t] * pps),
      _sample_rows(pr["qb"][0]), _sample_rows(misc[:, M_GATE:M_GATE + 3 * B_HEADS]),
      pr["kvs"][0][:, :, None], win_t, pr["kvw"][0][:, :, None],
      pe_rows, wbd, strips, cstrips, far)
    return out[:, 0]


def _merge_kernel(x_ref, mod_ref, oa_ref, ob_ref, w_ref, o_ref):
    n = oa_ref.shape[-1]
    y = _mm(oa_ref[...], w_ref[:n]) + _mm(ob_ref[...], w_ref[n:])
    o_ref[...] = x_ref[...] + (1.0 + mod_ref[2]) * y


def _merge_call(x, mod, oa, ob, w, tm):
    nb, t, d = x.shape
    tmod = mod.shape[2]
    mod_spec = (pl.BlockSpec((None, 3, 1, d), lambda b, i: (b, 0, 0, 0)) if tmod == 1 else
                pl.BlockSpec((None, 3, tm, d), lambda b, i: (b, 0, i, 0)))
    row = lambda w_: pl.BlockSpec((None, tm, w_), lambda b, i: (b, i, 0))
    return pl.pallas_call(
        _merge_kernel,
        grid=(nb, t // tm),
        in_specs=[row(d), mod_spec, row(oa.shape[-1]), row(ob.shape[-1]),
                  pl.BlockSpec(w.shape, lambda b, i: (0, 0))],
        out_specs=row(d),
        out_shape=jax.ShapeDtypeStruct(x.shape, F32),
        compiler_params=_cparams(("arbitrary", "arbitrary")),
        name="merge",
    )(x, mod, oa, ob, w)


def _gelu_ln(h, w_in_ref, ln_ref, cw):
    uv = jax.nn.gelu(_mm(h, w_in_ref[...]))
    u, v = uv[:, :cw], uv[:, cw:]
    mu = jnp.mean(v, axis=-1, keepdims=True)
    var = jnp.mean(jnp.square(v - mu), axis=-1, keepdims=True)
    return u, (v - mu) * lax.rsqrt(var + EPS) * ln_ref[...]


def _pgmlp_kernel(x_ref, mod_ref, g_ref, w_in_ref, ln_ref, ws_ref, bs_ref, w_out_ref, o_ref, *, cw):
    x = x_ref[...]
    h = _modulated_norm(x, g_ref[...], mod_ref).astype(BF16)
    u, v = _gelu_ln(h, w_in_ref, ln_ref, cw)
    gw = cw // C_GROUPS
    rows = []
    for n in range(x.shape[0] // C_CHUNK):
        vb = v[n * C_CHUNK:(n + 1) * C_CHUNK].astype(BF16)
        sv = [_mm(ws_ref[g], vb[:, g * gw:(g + 1) * gw]) + bs_ref[:, g:g + 1] for g in range(C_GROUPS)]
        rows.append(jnp.concatenate(sv, axis=1))
    sv = rows[0] if len(rows) == 1 else jnp.concatenate(rows, axis=0)
    o_ref[...] = x + (1.0 + mod_ref[2]) * _mm(u * sv, w_out_ref[...])


def _pgmlp_call(x, mod, g, w_in, ln_g, ws, bs_t, w_out, tm):
    nb, t, d = x.shape
    cw = w_out.shape[0]
    const = lambda a: pl.BlockSpec(a.shape, lambda b, i: (0,) * a.ndim)
    return pl.pallas_call(
        functools.partial(_pgmlp_kernel, cw=cw),
        grid=(nb, t // tm),
        in_specs=[pl.BlockSpec((None, tm, d), lambda b, i: (b, i, 0)),
                  pl.BlockSpec((None, 3, 1, d), lambda b, i: (b, 0, 0, 0)),
                  pl.BlockSpec((1, d), lambda b, i: (0, 0)),
                  const(w_in), pl.BlockSpec((1, cw), lambda b, i: (0, 0)), const(ws), const(bs_t), const(w_out)],
        out_specs=pl.BlockSpec((None, tm, d), lambda b, i: (b, i, 0)),
        out_shape=jax.ShapeDtypeStruct(x.shape, F32),
        compiler_params=_cparams(("arbitrary", "arbitrary")),
        name="gmlp_prompt",
    )(x, mod, g.reshape(1, d), w_in, ln_g.reshape(1, cw), ws, bs_t, w_out)


def _sgmlp_kernel(x_ref, mod_ref, g_ref, w_in_ref, ln_ref, ws0_ref, bs0_ref, w_out_ref, o_ref, v_ref, *, cw):
    x = x_ref[...]
    h = _modulated_norm(x, g_ref[...], mod_ref).astype(BF16)
    u, v = _gelu_ln(h, w_in_ref, ln_ref, cw)
    v_ref[...] = v
    sv = v.astype(BF16).astype(F32) * ws0_ref[...].astype(F32) + bs0_ref[...]
    o_ref[...] = x + (1.0 + mod_ref[2]) * _mm(u * sv, w_out_ref[...])


def _sgmlp_call(x, mod, g, w_in, ln_g, ws, b_sp, w_out):
    nb, t, d = x.shape
    cw = w_out.shape[0]
    gw = cw // C_GROUPS
    ws0 = jnp.repeat(ws[:, 0, 0], gw).reshape(1, cw)
    bs0 = jnp.repeat(b_sp[:, 0], gw).reshape(1, cw)
    full = lambda a: pl.BlockSpec(a.shape, lambda: (0,) * a.ndim)
    x2, mod2 = x[0], mod[0]
    o, v = pl.pallas_call(
        functools.partial(_sgmlp_kernel, cw=cw),
        in_specs=[full(x2), full(mod2), pl.BlockSpec((1, d), lambda: (0, 0)), full(w_in),
                  pl.BlockSpec((1, cw), lambda: (0, 0)), full(ws0), full(bs0), full(w_out)],
        out_specs=[full(x2), pl.BlockSpec((t, cw), lambda: (0, 0))],
        out_shape=[jax.ShapeDtypeStruct(x2.shape, F32), jax.ShapeDtypeStruct((t, cw), F32)],
        compiler_params=pltpu.CompilerParams(vmem_limit_bytes=VMEM_LIMIT),
        name="gmlp_sample",
    )(x2, mod2, g.reshape(1, d), w_in, ln_g.reshape(1, cw), ws0, bs0, w_out)
    return o[None], v


def kernel(x_prompt, x_sample, cache_a_kv, cache_a_idx, cache_b_cmp_kv, cache_b_sel_kv, state_b_win_kv,
           page_table, c_prompt, c_sample, rel_bias, w_ada, b_ada, norm_g, w_ffn_in, w_ffn_out,
           w_in_att, w_out_att, cmp_pe, w_cmp, w_in_c, ln_c_g, w_sp, b_sp, w_out_c, final_g):
    nbp, s, d = x_prompt.shape
    nbs, tdec, _ = x_sample.shape
    depth = w_ada.shape[0]
    assert tdec == 1 and s % TQ_P == 0 and cache_a_kv.shape[2] == PAGE

    ada = _ada_call(jnp.concatenate([c_prompt, c_sample], axis=0), w_ada, b_ada)
    ada = ada.reshape(depth, nbp + nbs, 3, 3, d)
    mod_p = lambda l, j: ada[l, :nbp, j][:, :, None, :]
    mod_s = lambda l, j: jnp.transpose(ada[l, nbp:, j], (1, 0, 2))[None]

    w_ffn_in_h = w_ffn_in.astype(BF16)
    w_ffn_out_h = w_ffn_out.astype(BF16)
    tab_a, tab_b = rel_bias[:, :A_HEADS], rel_bias[:, A_HEADS:]
    strips_pa, strips_pb = _bias_strips(tab_a, TQ_P), _bias_strips(tab_b, TQ_P)
    strips_sa, strips_sb = _bias_strips(tab_a, TQ_S), _bias_strips(tab_b, TQ_S)
    cstrips_p, cstrips_s = _cmp_strips(tab_b, TQ_P), _cmp_strips(tab_b, TQ_S)
    far_a, far_b = tab_a[N_BUCKETS - 1], tab_b[N_BUCKETS - 1]
    idx_t, akv_t = _pages_t(cache_a_idx), _pages_t(cache_a_kv)
    cmp_t, sel_t, win_t = _pages_t(cache_b_cmp_kv), _pages_t(cache_b_sel_kv), _pages_t(state_b_win_kv)

    xp = x_prompt
    xs = jnp.transpose(x_sample, (1, 0, 2))
    tm_p = 512 if s % 512 == 0 else TQ_P
    tk_p = TK_P if s % TK_P == 0 else TQ_P
    st_p, st_s, st_c = [], [], []
    for l in range(depth):
        last = l == depth - 1
        xp = _ffn_call(xp, mod_p(l, 0), norm_g[l, 0], w_ffn_in_h, w_ffn_out_h, l, 0, None, tm_p)
        xs = _ffn_call(xs, mod_s(l, 0), norm_g[l, 0], w_ffn_in_h, w_ffn_out_h, l, 0, None, nbs)
        if l % 2 == 0:
            a = l // 2
            wp = _rearranged_proj_weight(w_in_att[a])
            w_out = w_out_att[a].astype(BF16)
            pe_rows, wbd = _compress_operands(cmp_pe[a], w_cmp[a])

            pr = _proj_call(xp, mod_p(l, 1), norm_g[l, 1], wp, tm_p)
            oa = _pdsa_call(pr, strips_pa, far_a, tk_p)
            kcp = _pcmp_call(pr["kvc"], pe_rows, wbd)
            ob = _pnsa_call(pr, kcp, strips_pb, cstrips_p, far_b, tk_p)
            xp = _merge_call(xp, mod_p(l, 1), oa, ob, w_out, tm_p)
            kv5 = lambda t: t.reshape(t.shape[0], t.shape[1], 2, A_KV, HEAD_DIM)
            st_p.append((kv5(pr["kva"]), pr["misc"][:, :, M_KI:M_KI + IDX_DIM], kv5(pr["kvc"]),
                         kv5(pr["kvs"]), kv5(pr["kvw"][:, s - min(WINDOW, s):])))

            ps = _proj_call(xs, mod_s(l, 1), norm_g[l, 1], wp, nbs)
            oa = _sdsa_call(a, page_table, idx_t, akv_t, ps, strips_sa, far_a)
            ob = _snsa_call(a, page_table, cmp_t, sel_t, win_t, ps, pe_rows, wbd, strips_sb, cstrips_s, far_b)
            xs = _merge_call(xs, mod_s(l, 1), oa[None], ob[None], w_out, nbs)
            tok = lambda t: t[0].reshape(nbs, 1, 2, A_KV, HEAD_DIM)
            st_s.append((tok(ps["kva"]), ps["misc"][0][:, None, M_KI:M_KI + IDX_DIM], tok(ps["kvc"]),
                         tok(ps["kvs"]),
                         jnp.concatenate([state_b_win_kv[a][:, tdec:], tok(ps["kvw"])], axis=1)))
        else:
            ci = l // 2
            w_in = w_in_c[ci].astype(BF16)
            w_out = w_out_c[ci].astype(BF16)
            ws = (w_sp[ci] * jnp.tril(jnp.ones((C_CHUNK, C_CHUNK), w_sp.dtype))).astype(BF16)
            xp = _pgmlp_call(xp, mod_p(l, 1), norm_g[l, 1], w_in, ln_c_g[ci], ws,
                             jnp.transpose(b_sp[ci]), w_out, tm_p if s % tm_p == 0 else C_CHUNK)
            xs, v = _sgmlp_call(xs, mod_s(l, 1), norm_g[l, 1], w_in, ln_c_g[ci], ws, b_sp[ci], w_out)
            st_c.append(v[:, None, :])
        fg = final_g if last else None
        xp = _ffn_call(xp, mod_p(l, 2), norm_g[l, 2], w_ffn_in_h, w_ffn_out_h, l, 1, fg, tm_p)
        xs = _ffn_call(xs, mod_s(l, 2), norm_g[l, 2], w_ffn_in_h, w_ffn_out_h, l, 1, fg, nbs)

    stk = lambda sts, i: jnp.stack([st[i] for st in sts])
    return (xp, jnp.transpose(xs, (1, 0, 2)),
            stk(st_p, 0), stk(st_p, 1), stk(st_p, 2), stk(st_p, 3), stk(st_p, 4),
            stk(st_s, 0), stk(st_s, 1), stk(st_s, 2), stk(st_s, 3), stk(st_s, 4),
            jnp.stack(st_c))
```
